```python
import math
import jax, jax.numpy as jnp
from jax import lax
import numpy as np

D_MODEL = 1024
BATCH = 2
SEQ = 8192
DEPTH = 1

HEAD_DIM = 64
MIX_WIDTH = D_MODEL
NSA_HEADS = (MIX_WIDTH // 2) // HEAD_DIM
NSA_KV_GROUPS = 2
NSA_GROUP_SIZE = NSA_HEADS // NSA_KV_GROUPS
NSA_WIDTH = NSA_HEADS * HEAD_DIM
NSA_KV_WIDTH = NSA_KV_GROUPS * HEAD_DIM
CMP_BLOCK = 32
CMP_STRIDE = 16
CMP_HIDDEN = 256
SEL_BLOCK = 64
SEL_TOPK = 16
SEL_FORCE = 1e4
WINDOW = 512
Q_BLOCK = 128
RWKV_HEADS = (MIX_WIDTH - NSA_WIDTH) // HEAD_DIM
RWKV_WIDTH = RWKV_HEADS * HEAD_DIM
DECAY_LORA = 64
AAA_LORA = 64
GATE_LORA = 128
RWKV_GN_EPS = 64e-5
D_FF = 2816
MEM_TOKENS = 256
MEM_HEADS = 4
MEM_HEAD_DIM = D_MODEL // MEM_HEADS
ROPE_THETA = 10000.0
NORM_EPS = 1e-6
NEG_INF = -1e30

NSA_SPLITS = (NSA_WIDTH,) + (NSA_KV_WIDTH,) * 6 + (3 * NSA_HEADS,)
RWKV_SPLITS = (RWKV_WIDTH,) * 3 + (DECAY_LORA, AAA_LORA, GATE_LORA)
NSA_COLS = sum(NSA_SPLITS)
RWKV_COLS = sum(RWKV_SPLITS)
IN_COLS = NSA_COLS + RWKV_COLS

kernel_name = 'hybrid_nsa_rwkv7_macaron_block'


def rms_norm(x, g):
    xf = x.astype(jnp.float32)
    y = xf * lax.rsqrt(jnp.mean(xf * xf, -1, keepdims=True) + NORM_EPS)
    return (y * g.astype(jnp.float32)).astype(x.dtype)


def swiglu(x, w_gate, w_up, w_down):
    return (jax.nn.silu(x @ w_gate) * (x @ w_up)) @ w_down


def rope(x, pos):
    half = x.shape[-1] // 2
    inv = ROPE_THETA ** (-jnp.arange(half, dtype=jnp.float32) / half)
    ang = pos.astype(jnp.float32)[:, None] * inv[None, :]
    cos, sin = jnp.cos(ang).astype(x.dtype), jnp.sin(ang).astype(x.dtype)
    x1, x2 = x[..., :half], x[..., half:]
    return jnp.concatenate([x1 * cos - x2 * sin, x1 * sin + x2 * cos], -1)


def masked_softmax(s, mask):
    s = jnp.where(mask, s.astype(jnp.float32), NEG_INF)
    m = jnp.max(s, -1, keepdims=True)
    e = jnp.where(mask, jnp.exp(s - m), 0.0)
    return e / jnp.maximum(jnp.sum(e, -1, keepdims=True), 1e-30)


def split_cols(z, sizes):
    return jnp.split(z, np.cumsum(sizes)[:-1].tolist(), axis=-1)


def compress_blocks(kv, pe, w1, w2):
    T = kv.shape[2]
    n_cmp = (T - CMP_BLOCK) // CMP_STRIDE + 1
    idx = np.arange(n_cmp)[:, None] * CMP_STRIDE + np.arange(CMP_BLOCK)[None, :]
    blocks = kv[:, :, idx, :] + pe
    flat = blocks.reshape(blocks.shape[:3] + (CMP_BLOCK * HEAD_DIM,))
    return jax.nn.silu(flat @ w1) @ w2


def nsa_mixer(q, k_cmp, v_cmp, k_slc, v_slc, k_win, v_win, gate_logits,
              pe_k, w1_k, w2_k, pe_v, w1_v, w2_v):
    B, T, _ = q.shape
    G, R, dh = NSA_KV_GROUPS, NSA_GROUP_SIZE, HEAD_DIM
    pos = jnp.arange(T)
    qh = rope(q.reshape(B, T, G, R, dh).transpose(0, 2, 3, 1, 4), pos)

    def kv_heads(z):
        return z.reshape(B, T, G, dh).transpose(0, 2, 1, 3)

    n_cmp = (T - CMP_BLOCK) // CMP_STRIDE + 1
    cmp_start = np.arange(n_cmp) * CMP_STRIDE
    cmp_end = jnp.asarray(cmp_start + CMP_BLOCK - 1)
    kc = rope(compress_blocks(kv_heads(k_cmp), pe_k, w1_k, w2_k), cmp_end)
    vc = compress_blocks(kv_heads(v_cmp), pe_v, w1_v, w2_v)

    n_sel = T // SEL_BLOCK
    n_top = min(SEL_TOPK, n_sel)
    sel_start = np.arange(n_sel) * SEL_BLOCK
    overlap = jnp.asarray(((cmp_start[:, None] < sel_start[None, :] + SEL_BLOCK)
                           & (cmp_start[:, None] + CMP_BLOCK > sel_start[None, :])).astype(np.float32))
    ks_blocks = rope(kv_heads(k_slc), pos).reshape(B, G, n_sel, SEL_BLOCK, dh)
    vs_blocks = kv_heads(v_slc).reshape(B, G, n_sel, SEL_BLOCK, dh)
    pad = ((0, 0), (0, 0), (WINDOW, 0), (0, 0))
    kw_pad = jnp.pad(rope(kv_heads(k_win), pos), pad)
    vw_pad = jnp.pad(kv_heads(v_win), pad)
    gates = jax.nn.sigmoid(gate_logits).reshape(B, T, G, R, 3).transpose(0, 2, 3, 1, 4)
    scale = HEAD_DIM ** -0.5
    gather_blocks = jax.vmap(jax.vmap(lambda blocks, ids: blocks[ids]))
    sel_ids = jnp.arange(n_sel)

    def query_block(i):
        q0 = i * Q_BLOCK
        t = q0 + jnp.arange(Q_BLOCK)
        qb = lax.dynamic_slice_in_dim(qh, q0, Q_BLOCK, axis=3)
        gb = lax.dynamic_slice_in_dim(gates, q0, Q_BLOCK, axis=3)
        s_c = jnp.einsum('bgrqd,bgcd->bgrqc', qb, kc) * scale
        p_c = masked_softmax(s_c, cmp_end[None, :] <= t[:, None])
        o_c = jnp.einsum('bgrqc,bgcd->bgrqd', p_c.astype(vc.dtype), vc)
        imp = jnp.einsum('bgrqc,cs->bgqs', p_c, overlap)
        cur = (t // SEL_BLOCK)[:, None]
        forced = (sel_ids == 0) | (sel_ids == cur) | (sel_ids == cur - 1)
        score = jnp.where(forced, SEL_FORCE, jnp.where(sel_ids <= cur, imp, -SEL_FORCE))
        _, top = lax.top_k(score, n_top)
        ks = gather_blocks(ks_blocks, top).reshape(B, G, Q_BLOCK, n_top * SEL_BLOCK, dh)
        vs = gather_blocks(vs_blocks, top).reshape(B, G, Q_BLOCK, n_top * SEL_BLOCK, dh)
        pos_s = (top[..., None] * SEL_BLOCK + jnp.arange(SEL_BLOCK)).reshape(B, G, 1, Q_BLOCK, n_top * SEL_BLOCK)
        s_s = jnp.einsum('bgrqd,bgqkd->bgrqk', qb, ks) * scale
        p_s = masked_softmax(s_s, pos_s <= t[:, None])
        o_s = jnp.einsum('bgrqk,bgqkd->bgrqd', p_s.astype(vs.dtype), vs)
        kw = lax.dynamic_slice_in_dim(kw_pad, q0, Q_BLOCK + WINDOW, axis=2)
        vw = lax.dynamic_slice_in_dim(vw_pad, q0, Q_BLOCK + WINDOW, axis=2)
        pos_w = q0 - WINDOW + jnp.arange(Q_BLOCK + WINDOW)
        rel = t[:, None] - pos_w[None, :]
        s_w = jnp.einsum('bgrqd,bgkd->bgrqk', qb, kw) * scale
        p_w = masked_softmax(s_w, (pos_w[None, :] >= 0) & (rel >= 0) & (rel < WINDOW))
        o_w = jnp.einsum('bgrqk,bgkd->bgrqd', p_w.astype(vw.dtype), vw)
        return gb[..., 0:1] * o_c + gb[..., 1:2] * o_s + gb[..., 2:3] * o_w

    out = lax.map(query_block, jnp.arange(T // Q_BLOCK))
    return out.transpose(1, 0, 4, 2, 3, 5).reshape(B, T, NSA_WIDTH)


def rwkv7_mixer(r, k, v, w_lo, a_lo, g_lo, w0, w2, a0, a2, g2, k_k, k_a, r_k, gn_g, gn_b):
    B, T, C = r.shape
    H, N = RWKV_HEADS, HEAD_DIM
    f32 = jnp.float32
    w = -jax.nn.softplus(-(w0 + jnp.tanh(w_lo) @ w2)) - 0.5
    decay = jnp.exp(-jnp.exp(w.astype(f32)))
    a = jax.nn.sigmoid(a0 + a_lo @ a2)
    g = jax.nn.sigmoid(g_lo) @ g2
    kk = (k * k_k).astype(f32).reshape(B, T, H, N)
    kk = kk / jnp.maximum(jnp.sqrt(jnp.sum(kk * kk, -1, keepdims=True)), 1e-12)
    k = k * (1 + (a - 1) * k_a)

    def heads(z):
        return z.astype(f32).reshape(B, T, H, N).transpose(1, 0, 2, 3)

    def step(state, inp):
        r_t, w_t, k_t, v_t, kk_t, a_t = inp
        sa = jnp.einsum('bhvk,bhk->bhv', state, -kk_t)
        state = (state * w_t[:, :, None, :] + sa[..., None] * (kk_t * a_t)[:, :, None, :]
                 + v_t[..., None] * k_t[:, :, None, :])
        return state, jnp.einsum('bhvk,bhk->bhv', state, r_t)

    state0 = jnp.zeros((B, H, N, N), f32)
    _, y = lax.scan(step, state0, (heads(r), heads(decay), heads(k), heads(v),
                                   kk.transpose(1, 0, 2, 3), heads(a)))
    y = y.transpose(1, 0, 2, 3)
    mu = jnp.mean(y, -1, keepdims=True)
    var = jnp.mean(jnp.square(y - mu), -1, keepdims=True)
    y = ((y - mu) * lax.rsqrt(var + RWKV_GN_EPS)).reshape(B, T, C)
    y = y * gn_g.astype(f32) + gn_b.astype(f32)
    bonus = (jnp.sum((r * k * r_k).astype(f32).reshape(B, T, H, N), -1, keepdims=True)
             * v.astype(f32).reshape(B, T, H, N)).reshape(B, T, C)
    return (y + bonus).astype(r.dtype) * g


def memory_cross_attention(h, m, wq, wk, wv, wo):
    B, T, _ = h.shape
    M = m.shape[1]
    q = (h @ wq).reshape(B, T, MEM_HEADS, MEM_HEAD_DIM)
    k = (m @ wk).reshape(B, M, MEM_HEADS, MEM_HEAD_DIM)
    v = (m @ wv).reshape(B, M, MEM_HEADS, MEM_HEAD_DIM)
    s = jnp.einsum('bqhd,bkhd->bhqk', q, k) * (MEM_HEAD_DIM ** -0.5)
    p = jax.nn.softmax(s.astype(jnp.float32), axis=-1).astype(v.dtype)
    o = jnp.einsum('bhqk,bkhd->bqhd', p, v).reshape(B, T, D_MODEL)
    return o @ wo


def setup_inputs(seed: int = 0) -> dict:
    key = jax.random.key(seed)
    L = DEPTH
    specs = [
        ('x', (BATCH, SEQ, D_MODEL), 'normal', 1.0),
        ('mem', (BATCH, MEM_TOKENS, D_MODEL), 'normal', 1.0),
        ('ffn1_pre_g', (L, D_MODEL), 'gain', 0.02),
        ('ffn1_w_gate', (L, D_MODEL, D_FF), 'normal', D_MODEL ** -0.5),
        ('ffn1_w_up', (L, D_MODEL, D_FF), 'normal', D_MODEL ** -0.5),
        ('ffn1_w_down', (L, D_FF, D_MODEL), 'normal', D_FF ** -0.5),
        ('ffn1_post_g', (L, D_MODEL), 'gain', 0.02),
        ('mix_pre_g', (L, D_MODEL), 'gain', 0.02),
        ('w_in', (L, D_MODEL, IN_COLS), 'normal', D_MODEL ** -0.5),
        ('cmp_pe_k', (L, CMP_BLOCK, HEAD_DIM), 'normal', 0.1),
        ('cmp_w1_k', (L, CMP_BLOCK * HEAD_DIM, CMP_HIDDEN), 'normal', (CMP_BLOCK * HEAD_DIM) ** -0.5),
        ('cmp_w2_k', (L, CMP_HIDDEN, HEAD_DIM), 'normal', CMP_HIDDEN ** -0.5),
        ('cmp_pe_v', (L, CMP_BLOCK, HEAD_DIM), 'normal', 0.1),
        ('cmp_w1_v', (L, CMP_BLOCK * HEAD_DIM, CMP_HIDDEN), 'normal', (CMP_BLOCK * HEAD_DIM) ** -0.5),
        ('cmp_w2_v', (L, CMP_HIDDEN, HEAD_DIM), 'normal', CMP_HIDDEN ** -0.5),
        ('nsa_out_g', (L, NSA_WIDTH), 'gain', 0.02),
        ('rwkv_mu', (L, RWKV_COLS), 'uniform', (0.0, 1.0)),
        ('rwkv_w0', (L, RWKV_WIDTH), 'uniform', (-5.0, 1.0)),
        ('rwkv_w2', (L, DECAY_LORA, RWKV_WIDTH), 'normal', DECAY_LORA ** -0.5),
        ('rwkv_a0', (L, RWKV_WIDTH), 'normal', 0.1),
        ('rwkv_a2', (L, AAA_LORA, RWKV_WIDTH), 'normal', AAA_LORA ** -0.5),
        ('rwkv_g2', (L, GATE_LORA, RWKV_WIDTH), 'normal', GATE_LORA ** -0.5),
        ('rwkv_k_k', (L, RWKV_WIDTH), 'uniform', (0.7, 1.0)),
        ('rwkv_k_a', (L, RWKV_WIDTH), 'uniform', (0.8, 1.2)),
        ('rwkv_r_k', (L, RWKV_WIDTH), 'normal', 0.1),
        ('rwkv_gn_g', (L, RWKV_WIDTH), 'gain', 0.02),
        ('rwkv_gn_b', (L, RWKV_WIDTH), 'normal', 0.02),
        ('w_out', (L, MIX_WIDTH, D_MODEL), 'normal', MIX_WIDTH ** -0.5),
        ('mix_post_g', (L, D_MODEL), 'gain', 0.02),
        ('mem_pre_g', (L, D_MODEL), 'gain', 0.02),
        ('mem_kv_g', (L, D_MODEL), 'gain', 0.02),
        ('mem_wq', (L, D_MODEL, D_MODEL), 'normal', D_MODEL ** -0.5),
        ('mem_wk', (L, D_MODEL, D_MODEL), 'normal', D_MODEL ** -0.5),
        ('mem_wv', (L, D_MODEL, D_MODEL), 'normal', D_MODEL ** -0.5),
        ('mem_wo', (L, D_MODEL, D_MODEL), 'normal', D_MODEL ** -0.5),
        ('mem_post_g', (L, D_MODEL), 'gain', 0.02),
        ('ffn2_pre_g', (L, D_MODEL), 'gain', 0.02),
        ('ffn2_w_gate', (L, D_MODEL, D_FF), 'normal', D_MODEL ** -0.5),
        ('ffn2_w_up', (L, D_MODEL, D_FF), 'normal', D_MODEL ** -0.5),
        ('ffn2_w_down', (L, D_FF, D_MODEL), 'normal', D_FF ** -0.5),
        ('ffn2_post_g', (L, D_MODEL), 'gain', 0.02),
    ]
    keys = jax.random.split(key, len(specs))
    out = {}
    for kk, (name, shape, kind, p) in zip(keys, specs):
        if kind == 'normal':
            out[name] = p * jax.random.normal(kk, shape, jnp.float32)
        elif kind == 'gain':
            out[name] = 1.0 + p * jax.random.normal(kk, shape, jnp.float32)
        else:
            out[name] = jax.random.uniform(kk, shape, jnp.float32, p[0], p[1])
    return out


def reference(x, mem,
              ffn1_pre_g, ffn1_w_gate, ffn1_w_up, ffn1_w_down, ffn1_post_g,
              mix_pre_g, w_in, cmp_pe_k, cmp_w1_k, cmp_w2_k, cmp_pe_v, cmp_w1_v, cmp_w2_v, nsa_out_g,
              rwkv_mu, rwkv_w0, rwkv_w2, rwkv_a0, rwkv_a2, rwkv_g2, rwkv_k_k, rwkv_k_a, rwkv_r_k,
              rwkv_gn_g, rwkv_gn_b, w_out, mix_post_g,
              mem_pre_g, mem_kv_g, mem_wq, mem_wk, mem_wv, mem_wo, mem_post_g,
              ffn2_pre_g, ffn2_w_gate, ffn2_w_up, ffn2_w_down, ffn2_post_g):
    for l in range(DEPTH):
        h = rms_norm(x, ffn1_pre_g[l])
        x = x + 0.5 * rms_norm(swiglu(h, ffn1_w_gate[l], ffn1_w_up[l], ffn1_w_down[l]), ffn1_post_g[l])
        h = rms_norm(x, mix_pre_g[l])
        proj = h @ w_in[l]
        nsa_cols, rwkv_cols = proj[..., :NSA_COLS], proj[..., NSA_COLS:]
        prev = jnp.pad(rwkv_cols, ((0, 0), (1, 0), (0, 0)))[:, :-1]
        rwkv_cols = rwkv_cols + (prev - rwkv_cols) * rwkv_mu[l]
        q, k_c, v_c, k_s, v_s, k_w, v_w, gate_logits = split_cols(nsa_cols, NSA_SPLITS)
        r, k, v, w_lo, a_lo, g_lo = split_cols(rwkv_cols, RWKV_SPLITS)
        o_nsa = nsa_mixer(q, k_c, v_c, k_s, v_s, k_w, v_w, gate_logits,
                          cmp_pe_k[l], cmp_w1_k[l], cmp_w2_k[l], cmp_pe_v[l], cmp_w1_v[l], cmp_w2_v[l])
        o_rwkv = rwkv7_mixer(r, k, v, w_lo, a_lo, g_lo, rwkv_w0[l], rwkv_w2[l], rwkv_a0[l], rwkv_a2[l],
                             rwkv_g2[l], rwkv_k_k[l], rwkv_k_a[l], rwkv_r_k[l], rwkv_gn_g[l], rwkv_gn_b[l])
        mixed = jnp.concatenate([rms_norm(o_nsa, nsa_out_g[l]), o_rwkv], -1) @ w_out[l]
        x = x + rms_norm(mixed, mix_post_g[l])
        h = rms_norm(x, mem_pre_g[l])
        m = rms_norm(mem, mem_kv_g[l])
        x = x + rms_norm(memory_cross_attention(h, m, mem_wq[l], mem_wk[l], mem_wv[l], mem_wo[l]), mem_post_g[l])
        h = rms_norm(x, ffn2_pre_g[l])
        x = x + 0.5 * rms_norm(swiglu(h, ffn2_w_gate[l], ffn2_w_up[l], ffn2_w_down[l]), ffn2_post_g[l])
    return x
```

```python
import functools

import numpy as np
import jax
import jax.numpy as jnp
from jax import lax
from jax.experimental import pallas as pl
from jax.experimental.pallas import tpu as pltpu

F32 = jnp.float32
BF16 = jnp.bfloat16

HEAD_DIM = 64
NSA_HEADS = 8
NSA_GROUPS = 2
NSA_GROUP_SIZE = 4
NSA_WIDTH = 512
NSA_KV_WIDTH = 128
CMP_BLOCK = 32
CMP_STRIDE = 16
CMP_HIDDEN = 256
SEL_BLOCK = 64
SEL_TOPK = 16
SEL_FORCE = 1e4
WINDOW = 512
RWKV_HEADS = 8
RWKV_WIDTH = 512
DECAY_LORA = 64
AAA_LORA = 64
GATE_LORA = 128
RWKV_GN_EPS = 64e-5
MEM_HEADS = 4
ROPE_THETA = 10000.0
NORM_EPS = 1e-6
NEG_INF = -1e30

LANES = 128
KEY_BLOCK = 128
Q_TILE = 128
RWKV_CHUNK = 64
HALF = 256
VMEM_LIMIT = 56 * 1024 * 1024


def _bdot(a, b):
    return jnp.dot(a.astype(BF16), b.astype(BF16), preferred_element_type=F32)


def _split3(x):
    h1 = x.astype(BF16)
    r1 = x - h1.astype(F32)
    h2 = r1.astype(BF16)
    r2 = r1 - h2.astype(F32)
    return h1, h2, r2.astype(BF16)


def _dot3_right(x, m):
    h1, h2, h3 = _split3(x)
    d = lambda h: jnp.dot(h, m, preferred_element_type=F32)
    return d(h1) + d(h2) + d(h3)


def _dot3_left(m, x):
    h1, h2, h3 = _split3(x)
    d = lambda h: jnp.dot(m, h, preferred_element_type=F32)
    return d(h1) + d(h2) + d(h3)


def _rms(x, g):
    return x * lax.rsqrt(jnp.mean(x * x, axis=-1, keepdims=True) + NORM_EPS) * g


def _silu(x):
    return x / (1.0 + jnp.exp(-x))


def _sigmoid(x):
    return 1.0 / (1.0 + jnp.exp(-x))


def _const_spec(shape):
    nd = len(shape)
    return pl.BlockSpec(shape, lambda *_: (0,) * nd)


def _params(sem):
    return pltpu.CompilerParams(dimension_semantics=sem, vmem_limit_bytes=VMEM_LIMIT)


def _ffn_kernel(x_ref, pre_ref, wg_ref, wu_ref, wd_ref, post_ref, o_ref, *, ff_chunk):
    x = x_ref[...]
    h = _rms(x, pre_ref[...]).astype(BF16)
    d_ff = wg_ref.shape[1]
    acc = jnp.zeros(x.shape, F32)
    for c0 in range(0, d_ff, ff_chunk):
        g = jnp.dot(h, wg_ref[:, c0:c0 + ff_chunk], preferred_element_type=F32)
        u = jnp.dot(h, wu_ref[:, c0:c0 + ff_chunk], preferred_element_type=F32)
        a = (_silu(g) * u).astype(BF16)
        acc = acc + jnp.dot(a, wd_ref[c0:c0 + ff_chunk, :], preferred_element_type=F32)
    o_ref[...] = x + 0.5 * _rms(acc, post_ref[...])


def _ffn_block(x2, pre_g, wg, wu, wd, post_g, *, tm=512, ff_chunk=256):
    m, d = x2.shape
    d_ff = wg.shape[1]
    return pl.pallas_call(
        functools.partial(_ffn_kernel, ff_chunk=ff_chunk),
        grid=(m // tm,),
        in_specs=[
            pl.BlockSpec((tm, d), lambda i: (i, 0)),
            _const_spec((1, d)),
            _const_spec((d, d_ff)),
            _const_spec((d, d_ff)),
            _const_spec((d_ff, d)),
            _const_spec((1, d)),
        ],
        out_specs=pl.BlockSpec((tm, d), lambda i: (i, 0)),
        out_shape=jax.ShapeDtypeStruct((m, d), F32),
        compiler_params=_params(("arbitrary",)),
        name="ffn_block",
    )(x2, pre_g.reshape(1, d), wg.astype(BF16), wu.astype(BF16), wd.astype(BF16),
      post_g.reshape(1, d))


def _swap_halves(x):
    n = x.shape[-1]
    lane = lax.broadcasted_iota(jnp.int32, x.shape, x.ndim - 1)
    fwd = pltpu.roll(x, n - HEAD_DIM // 2, x.ndim - 1)
    bwd = pltpu.roll(x, HEAD_DIM // 2, x.ndim - 1)
    return jnp.where((lane % HEAD_DIM) < HEAD_DIM // 2, fwd, bwd)


def _rope(x, cos, sin_signed):
    reps = x.shape[-1] // LANES
    c = jnp.concatenate([cos] * reps, axis=-1) if reps > 1 else cos
    s = jnp.concatenate([sin_signed] * reps, axis=-1) if reps > 1 else sin_signed
    return x * c + _swap_halves(x) * s


def _in_proj_kernel(x_ref, g_ref, w_ref, mu_ref, cos_ref, sin_ref,
                    q_ref, kc_ref, vc_ref, ks_ref, vst_ref, kw_ref, vwt_ref, gt_ref,
                    r_ref, k_ref, v_ref, lo_ref, carry_ref):
    @pl.when(pl.program_id(1) == 0)
    def _():
        carry_ref[...] = jnp.zeros_like(carry_ref)

    h = _rms(x_ref[0], g_ref[...]).astype(BF16)
    p = jnp.dot(h, w_ref[...], preferred_element_type=F32)
    cos = cos_ref[...]
    sin = sin_ref[...]
    tm = p.shape[0]

    o = 0
    q = _rope(p[:, o:o + NSA_WIDTH], cos, sin) * (HEAD_DIM ** -0.5)
    q_ref[0] = q.astype(BF16)
    o += NSA_WIDTH
    kc_ref[0] = p[:, o:o + LANES].astype(BF16); o += LANES
    vc_ref[0] = p[:, o:o + LANES].astype(BF16); o += LANES
    ks_ref[0] = _rope(p[:, o:o + LANES], cos, sin).astype(BF16); o += LANES
    vst_ref[0] = p[:, o:o + LANES].T.astype(BF16); o += LANES
    kw_ref[0] = _rope(p[:, o:o + LANES], cos, sin).astype(BF16); o += LANES
    vwt_ref[0] = p[:, o:o + LANES].T.astype(BF16); o += LANES
    gates_t = _sigmoid(p[:, o:o + LANES]).T
    gt_ref[0] = gates_t[:gt_ref.shape[1], :]
    o += LANES

    rw = p[:, o:]
    row = lax.broadcasted_iota(jnp.int32, rw.shape, 0)
    prev = jnp.where(row == 0, carry_ref[0:1, :], pltpu.roll(rw, 1, 0))
    carry_ref[...] = jnp.broadcast_to(rw[tm - 1:tm, :], carry_ref.shape)
    mixed = rw + (prev - rw) * mu_ref[...]
    r_ref[0] = mixed[:, 0:512]
    k_ref[0] = mixed[:, 512:1024]
    v_ref[0] = mixed[:, 1024:1536]
    lo_ref[0] = mixed[:, 1536:1792]


def _in_proj(x3, g, w_cols, mu_cols, cos_t, sin_t, *, tm=256):
    b, t, d = x3.shape
    n = w_cols.shape[1]
    row = lambda bi, ti: (bi, ti, 0)
    col = lambda bi, ti: (bi, 0, ti)
    out_shapes = [
        jax.ShapeDtypeStruct((b, t, NSA_WIDTH), BF16),
        jax.ShapeDtypeStruct((b, t, LANES), BF16),
        jax.ShapeDtypeStruct((b, t, LANES), BF16),
        jax.ShapeDtypeStruct((b, t, LANES), BF16),
        jax.ShapeDtypeStruct((b, LANES, t), BF16),
        jax.ShapeDtypeStruct((b, t, LANES), BF16),
        jax.ShapeDtypeStruct((b, LANES, t), BF16),
        jax.ShapeDtypeStruct((b, 32, t), F32),
        jax.ShapeDtypeStruct((b, t, RWKV_WIDTH), F32),
        jax.ShapeDtypeStruct((b, t, RWKV_WIDTH), F32),
        jax.ShapeDtypeStruct((b, t, RWKV_WIDTH), F32),
        jax.ShapeDtypeStruct((b, t, 256), F32),
    ]
    out_specs = [
        pl.BlockSpec((1, tm, NSA_WIDTH), row),
        pl.BlockSpec((1, tm, LANES), row),
        pl.BlockSpec((1, tm, LANES), row),
        pl.BlockSpec((1, tm, LANES), row),
        pl.BlockSpec((1, LANES, tm), col),
        pl.BlockSpec((1, tm, LANES), row),
        pl.BlockSpec((1, LANES, tm), col),
        pl.BlockSpec((1, 32, tm), col),
        pl.BlockSpec((1, tm, RWKV_WIDTH), row),
        pl.BlockSpec((1, tm, RWKV_WIDTH), row),
        pl.BlockSpec((1, tm, RWKV_WIDTH), row),
        pl.BlockSpec((1, tm, 256), row),
    ]
    return pl.pallas_call(
        _in_proj_kernel,
        grid=(b, t // tm),
        in_specs=[
            pl.BlockSpec((1, tm, d), row),
            _const_spec((1, d)),
            _const_spec((d, n)),
            _const_spec((1, 1792)),
            pl.BlockSpec((tm, LANES), lambda bi, ti: (ti, 0)),
            pl.BlockSpec((tm, LANES), lambda bi, ti: (ti, 0)),
        ],
        out_specs=out_specs,
        out_shape=out_shapes,
        scratch_shapes=[pltpu.VMEM((8, 1792), F32)],
        compiler_params=_params(("arbitrary", "arbitrary")),
        name="in_proj",
    )(x3, g.reshape(1, d), w_cols, mu_cols, cos_t, sin_t)


def _compress_kernel(kin_ref, vin_ref, pek_ref, w1k_ref, w1kc_ref, w2k_ref,
                     pev_ref, w1v_ref, w1vc_ref, w2v_ref, cos_ref, sin_ref,
                     kc_ref, vct_ref):
    def phi(rows, pe, w1, w1cat, w2bd):
        n = rows.shape[0]
        pr = jnp.dot(rows, w1cat, preferred_element_type=F32)
        bias = jnp.dot(pe, w1, preferred_element_type=F32)[0:1, :]
        hid = []
        for g in range(NSA_GROUPS):
            top = pr[:, g * CMP_HIDDEN:(g + 1) * CMP_HIDDEN]
            bot = pr[:, (NSA_GROUPS + g) * CMP_HIDDEN:(NSA_GROUPS + g + 1) * CMP_HIDDEN]
            hid.append(top + pltpu.roll(bot, n - 1, 0) + bias)
        act = _silu(jnp.concatenate(hid, axis=-1)).astype(BF16)
        return jnp.dot(act, w2bd, preferred_element_type=F32)

    kc = phi(kin_ref[0], pek_ref[...], w1k_ref[...], w1kc_ref[...], w2k_ref[...])
    kc_ref[0] = _rope(kc, cos_ref[...], sin_ref[...]).astype(BF16)
    vc = phi(vin_ref[0], pev_ref[...], w1v_ref[...], w1vc_ref[...], w2v_ref[...])
    vct_ref[0] = vc.T.astype(BF16)


def _compress(kin, vin, pek, w1k, w1kc, w2k, pev, w1v, w1vc, w2v, cos_c, sin_c):
    b, ncp, width = kin.shape
    blk = pl.BlockSpec((1, ncp, width), lambda bi: (bi, 0, 0))
    cs = lambda a: _const_spec(a.shape)
    return pl.pallas_call(
        _compress_kernel,
        grid=(b,),
        in_specs=[blk, blk, cs(pek), cs(w1k), cs(w1kc), cs(w2k),
                  cs(pev), cs(w1v), cs(w1vc), cs(w2v), cs(cos_c), cs(sin_c)],
        out_specs=[pl.BlockSpec((1, ncp, LANES), lambda bi: (bi, 0, 0)),
                   pl.BlockSpec((1, LANES, ncp), lambda bi: (bi, 0, 0))],
        out_shape=[jax.ShapeDtypeStruct((b, ncp, LANES), BF16),
                   jax.ShapeDtypeStruct((b, LANES, ncp), BF16)],
        compiler_params=_params(("arbitrary",)),
        name="nsa_compress",
    )(kin, vin, pek, w1k, w1kc, w2k, pev, w1v, w1vc, w2v, cos_c, sin_c)


def _nsa_kernel(q_ref, gt_ref, kc_ref, vct_ref, ks_ref, vst_ref, kw_ref, vwt_ref, ovt_ref,
                o_ref, sel_ref, acc_ref):
    i = pl.program_id(1)
    q0 = i * Q_TILE
    ncp = kc_ref.shape[1]
    ns = ovt_ref.shape[0]
    width = NSA_GROUP_SIZE * Q_TILE
    t_row = q0 + lax.broadcasted_iota(jnp.int32, (1, width), 1) % Q_TILE

    qf = q_ref[0].astype(F32)
    zeros_half = jnp.zeros((HEAD_DIM, Q_TILE), F32)

    def attend(k_ref, vt_ref, j_lo, j_hi, mask_fn, qt):
        acc_ref[...] = jnp.zeros_like(acc_ref)

        def body(j, carry):
            m_run, l_run = carry
            k0 = pl.multiple_of(j * KEY_BLOCK, KEY_BLOCK)
            s = jnp.dot(k_ref[0, pl.ds(k0, KEY_BLOCK), :], qt, preferred_element_type=F32)
            key = k0 + lax.broadcasted_iota(jnp.int32, (KEY_BLOCK, width), 0)
            s = jnp.where(mask_fn(j, key), s, NEG_INF)
            m_new = jnp.maximum(m_run, jnp.max(s, axis=0, keepdims=True))
            alpha = jnp.exp(m_run - m_new)
            p = jnp.exp(s - m_new)
            l_new = alpha * l_run + jnp.sum(p, axis=0, keepdims=True)
            pv = jnp.dot(vt_ref[0, :, pl.ds(k0, KEY_BLOCK)], p.astype(BF16),
                         preferred_element_type=F32)
            acc_ref[...] = acc_ref[...] * alpha + pv
            return m_new, l_new

        init = (jnp.full((1, width), NEG_INF, F32), jnp.zeros((1, width), F32))
        _, l_fin = lax.fori_loop(j_lo, j_hi, body, init)
        return acc_ref[...] / l_fin

    outs = []
    for g in range(NSA_GROUPS):
        parts = []
        for pair in range(NSA_GROUP_SIZE // 2):
            slab_t = qf[:, (2 * g + pair) * LANES:(2 * g + pair + 1) * LANES].T
            for half in range(2):
                feat = slab_t[half * HEAD_DIM:(half + 1) * HEAD_DIM, :]
                parts.append(jnp.concatenate([feat, zeros_half] if g == 0 else [zeros_half, feat], axis=0))
        qt = jnp.concatenate(parts, axis=1).astype(BF16)

        sc = jnp.dot(kc_ref[0], qt, preferred_element_type=F32)
        c_end = lax.broadcasted_iota(jnp.int32, (ncp, width), 0) * CMP_STRIDE + (CMP_BLOCK - 1)
        c_mask = c_end <= t_row
        sc = jnp.where(c_mask, sc, NEG_INF)
        m_c = jnp.max(sc, axis=0, keepdims=True)
        e_c = jnp.where(c_mask, jnp.exp(sc - m_c), 0.0)
        p_c = e_c / jnp.maximum(jnp.sum(e_c, axis=0, keepdims=True), 1e-30)
        o_c = jnp.dot(vct_ref[0], p_c.astype(BF16), preferred_element_type=F32)

        p_sum = p_c[:, 0:Q_TILE]
        for r in range(1, NSA_GROUP_SIZE):
            p_sum = p_sum + p_c[:, r * Q_TILE:(r + 1) * Q_TILE]
        imp = _dot3_left(ovt_ref[...], p_sum)
        s_id = lax.broadcasted_iota(jnp.int32, (ns, Q_TILE), 0)
        cur = t_row[:, 0:Q_TILE] // SEL_BLOCK
        forced = (s_id == 0) | (s_id == cur) | (s_id == cur - 1)
        score = jnp.where(forced, SEL_FORCE, jnp.where(s_id <= cur, imp, -SEL_FORCE))

        def pick_round(_, carry):
            score, sel = carry
            mx = jnp.max(score, axis=0, keepdims=True)
            first = jnp.min(jnp.where(score == mx, s_id, ns), axis=0, keepdims=True)
            hit = s_id == first
            return jnp.where(hit, -3e38, score), jnp.where(hit, 1.0, sel)

        _, sel = lax.fori_loop(0, min(SEL_TOPK, ns), pick_round,
                               (score, jnp.zeros((ns, Q_TILE), F32)))
        sel_ref[...] = sel

        def sel_mask(j, key):
            rows = []
            for h in range(KEY_BLOCK // SEL_BLOCK):
                srow = sel_ref[pl.ds(j * (KEY_BLOCK // SEL_BLOCK) + h, 1), :]
                srow = jnp.concatenate([srow] * NSA_GROUP_SIZE, axis=1)
                rows.append(jnp.broadcast_to(srow, (SEL_BLOCK, width)))
            return (jnp.concatenate(rows, axis=0) > 0.5) & (key <= t_row)

        o_s = attend(ks_ref, vst_ref, 0, i + 1, sel_mask, qt)

        def win_mask(j, key):
            rel = t_row - key
            return (rel >= 0) & (rel < WINDOW)

        o_w = attend(kw_ref, vwt_ref, jnp.maximum(i - WINDOW // KEY_BLOCK, 0), i + 1, win_mask, qt)

        lo = g * HEAD_DIM
        heads = []
        for r in range(NSA_GROUP_SIZE):
            base = (g * NSA_GROUP_SIZE + r) * 3
            cols = slice(r * Q_TILE, (r + 1) * Q_TILE)
            heads.append(gt_ref[0, base:base + 1, :] * o_c[lo:lo + HEAD_DIM, cols]
                         + gt_ref[0, base + 1:base + 2, :] * o_s[lo:lo + HEAD_DIM, cols]
                         + gt_ref[0, base + 2:base + 3, :] * o_w[lo:lo + HEAD_DIM, cols])
        for pair in range(NSA_GROUP_SIZE // 2):
            outs.append(jnp.concatenate(heads[2 * pair:2 * pair + 2], axis=0).T)
    o_ref[0] = jnp.concatenate(outs, axis=1)


def _nsa_attn(q, gates_t, kc, vct, ks, vst, kw, vwt, ovt):
    b, t, _ = q.shape
    ncp = kc.shape[1]
    ns = ovt.shape[0]
    full_rows = lambda bi, qi: (bi, 0, 0)
    return pl.pallas_call(
        _nsa_kernel,
        grid=(b, t // Q_TILE),
        in_specs=[
            pl.BlockSpec((1, Q_TILE, NSA_WIDTH), lambda bi, qi: (bi, qi, 0)),
            pl.BlockSpec((1, 32, Q_TILE), lambda bi, qi: (bi, 0, qi)),
            pl.BlockSpec((1, ncp, LANES), full_rows),
            pl.BlockSpec((1, LANES, ncp), full_rows),
            pl.BlockSpec((1, t, LANES), full_rows),
            pl.BlockSpec((1, LANES, t), full_rows),
            pl.BlockSpec((1, t, LANES), full_rows),
            pl.BlockSpec((1, LANES, t), full_rows),
            _const_spec((ns, ncp)),
        ],
        out_specs=pl.BlockSpec((1, Q_TILE, NSA_WIDTH), lambda bi, qi: (bi, qi, 0)),
        out_shape=jax.ShapeDtypeStruct((b, t, NSA_WIDTH), F32),
        scratch_shapes=[pltpu.VMEM((ns, Q_TILE), F32),
                        pltpu.VMEM((LANES, NSA_GROUP_SIZE * Q_TILE), F32)],
        compiler_params=_params(("arbitrary", "arbitrary")),
        name="nsa_attn",
    )(q, gates_t, kc, vct, ks, vst, kw, vwt, ovt)


def _rwkv_kernel(r_ref, k_ref, v_ref, lo_ref, w0_ref, w2_ref, a0_ref, a2_ref, g2_ref,
                 kk_ref, ka_ref, rk_ref, gng_ref, gnb_ref, o_ref, s_ref):
    @pl.when(pl.program_id(1) == 0)
    def _():
        s_ref[...] = jnp.zeros_like(s_ref)

    c = RWKV_CHUNK
    r = r_ref[0]
    k = k_ref[0]
    v = v_ref[0]
    lo = lo_ref[0]
    w_lo = lo[:, 0:DECAY_LORA]
    a_lo = lo[:, DECAY_LORA:DECAY_LORA + AAA_LORA]
    g_lo = lo[:, DECAY_LORA + AAA_LORA:]

    z = -(w0_ref[...] + _bdot(jnp.tanh(w_lo), w2_ref[...]))
    softplus = jnp.maximum(z, 0.0) + jnp.log(1.0 + jnp.exp(-jnp.abs(z)))
    log_decay = -jnp.exp(-softplus - 0.5)
    lr = _sigmoid(a0_ref[...] + _bdot(a_lo, a2_ref[...]))
    gate = _bdot(_sigmoid(g_lo), g2_ref[...])

    row_i = lax.broadcasted_iota(jnp.int32, (c, c), 0)
    col_i = lax.broadcasted_iota(jnp.int32, (c, c), 1)
    tril_incl = (row_i >= col_i).astype(BF16)
    cum = _dot3_left(tril_incl, log_decay)
    g_incl = jnp.exp(cum)
    g_excl = jnp.exp(cum - log_decay)
    g_inv = jnp.exp(-cum)

    bi = lax.broadcasted_iota(jnp.int32, (HALF, HALF), 0) // HEAD_DIM
    bj = lax.broadcasted_iota(jnp.int32, (HALF, HALF), 1) // HEAD_DIM
    same_head = bi == bj
    ones_bd = same_head.astype(BF16)
    t_id = lax.broadcasted_iota(jnp.int32, (c, HALF), 0)
    j_id = lax.broadcasted_iota(jnp.int32, (c, HALF), 1) % HEAD_DIM
    strict_lower = t_id > j_id
    incl_lower = t_id >= j_id
    eye_all = (t_id == j_id).astype(F32)

    def bd_rows(x):
        xb = x.astype(BF16)
        return jnp.where(same_head, jnp.concatenate([xb] * (HALF // c), axis=0), jnp.zeros((), BF16))

    def bd_cols(xt):
        xb = xt.astype(BF16)
        return jnp.where(same_head, jnp.concatenate([xb, xb], axis=1), jnp.zeros((), BF16))

    outs = []
    for hh in range(RWKV_WIDTH // HALF):
        ln = slice(hh * HALF, (hh + 1) * HALF)
        rh, kh, vh = r[:, ln], k[:, ln], v[:, ln]
        lrh = lr[:, ln]
        kk = kh * kk_ref[:, ln]
        ssq = _dot3_right(kk * kk, ones_bd)
        kk = kk / jnp.maximum(jnp.sqrt(ssq), 1e-12)
        k2 = kh * (1.0 + (lrh - 1.0) * ka_ref[:, ln])

        at = -kk * g_excl[:, ln]
        bt = kk * lrh * g_inv[:, ln]
        kt = k2 * g_inv[:, ln]
        rt = rh * g_incl[:, ln]

        bt_bd = bd_cols(jnp.concatenate([bt, bt], axis=0).T)
        kt_bd = bd_cols(jnp.concatenate([kt, kt], axis=0).T)
        ar = jnp.concatenate([at, rt], axis=0).astype(BF16)
        ab = jnp.dot(ar, bt_bd, preferred_element_type=F32)
        ak = jnp.dot(ar, kt_bd, preferred_element_type=F32)
        a_ab = jnp.where(strict_lower, ab[0:c], 0.0)
        a_rb = jnp.where(incl_lower, ab[c:], 0.0)
        a_ak = jnp.where(strict_lower, ak[0:c], 0.0)
        a_rk = jnp.where(incl_lower, ak[c:], 0.0)

        inv = eye_all + a_ab
        pw = a_ab
        for _ in range(5):
            pw = jnp.dot(pw.astype(BF16), bd_rows(pw), preferred_element_type=F32)
            inv = inv + jnp.dot(inv.astype(BF16), bd_rows(pw), preferred_element_type=F32)

        v_bd = bd_rows(vh)
        w_mat = jnp.dot(inv.astype(BF16), bd_rows(at), preferred_element_type=F32)
        akv = jnp.dot(a_ak.astype(BF16), v_bd, preferred_element_type=F32)
        z_mat = jnp.dot(inv.astype(BF16), bd_rows(akv), preferred_element_type=F32)
        rkv = jnp.dot(a_rk.astype(BF16), v_bd, preferred_element_type=F32)

        s0 = s_ref[hh]
        s0b = s0.astype(BF16)
        u = jnp.dot(w_mat.astype(BF16), s0b, preferred_element_type=F32) + z_mat
        y = (jnp.dot(rt.astype(BF16), s0b, preferred_element_type=F32)
             + jnp.dot(a_rb.astype(BF16), bd_rows(u), preferred_element_type=F32) + rkv)
        bk_t = jnp.concatenate([bt, kt], axis=0).T.astype(BF16)
        uv = jnp.concatenate([u, vh], axis=0).astype(BF16)
        upd = jnp.dot(bk_t, uv, preferred_element_type=F32)
        g_last = jnp.broadcast_to(g_incl[c - 1:c, ln], (LANES, HALF)).T
        g_col = jnp.concatenate([g_last, g_last], axis=1)
        s_ref[hh] = g_col * (s0 + jnp.where(same_head, upd, 0.0))

        mu = _dot3_right(y, ones_bd) * (1.0 / HEAD_DIM)
        yc = y - mu
        var = _dot3_right(yc * yc, ones_bd) * (1.0 / HEAD_DIM)
        yn = yc * lax.rsqrt(var + RWKV_GN_EPS) * gng_ref[:, ln] + gnb_ref[:, ln]
        bonus = _dot3_right(rh * k2 * rk_ref[:, ln], ones_bd) * vh
        outs.append((yn + bonus) * gate[:, ln])
    o_ref[0] = jnp.concatenate(outs, axis=1)


def _rwkv(r, k, v, lo, w0, w2, a0, a2, g2, k_k, k_a, r_k, gn_g, gn_b):
    b, t, width = r.shape
    c = RWKV_CHUNK
    row = lambda bi, ci: (bi, ci, 0)
    vec = lambda a: a.reshape(1, width)
    cs = lambda a: _const_spec(a.shape)
    args = [vec(w0), w2.astype(BF16), vec(a0), a2.astype(BF16), g2.astype(BF16),
            vec(k_k), vec(k_a), vec(r_k), vec(gn_g), vec(gn_b)]
    return pl.pallas_call(
        _rwkv_kernel,
        grid=(b, t // c),
        in_specs=[pl.BlockSpec((1, c, width), row)] * 3 + [pl.BlockSpec((1, c, 256), row)]
                 + [cs(a) for a in args],
        out_specs=pl.BlockSpec((1, c, width), row),
        out_shape=jax.ShapeDtypeStruct((b, t, width), F32),
        scratch_shapes=[pltpu.VMEM((width // HALF, HALF, HALF), F32)],
        compiler_params=_params(("arbitrary", "arbitrary")),
        name="rwkv7",
    )(r, k, v, lo, *args)


def _mem_kv_kernel(m_ref, g_ref, wk_ref, wv_ref, kt_ref, v_ref):
    m = _rms(m_ref[0], g_ref[...]).astype(BF16)
    kt_ref[0] = jnp.dot(m, wk_ref[...], preferred_element_type=F32).T.astype(BF16)
    v_ref[0] = jnp.dot(m, wv_ref[...], preferred_element_type=F32).astype(BF16)


def _mem_kv(mem, g, wk, wv):
    b, mt, d = mem.shape
    return pl.pallas_call(
        _mem_kv_kernel,
        grid=(b,),
        in_specs=[pl.BlockSpec((1, mt, d), lambda bi: (bi, 0, 0)), _const_spec((1, d)),
                  _const_spec((d, d)), _const_spec((d, d))],
        out_specs=[pl.BlockSpec((1, d, mt), lambda bi: (bi, 0, 0)),
                   pl.BlockSpec((1, mt, d), lambda bi: (bi, 0, 0))],
        out_shape=[jax.ShapeDtypeStruct((b, d, mt), BF16), jax.ShapeDtypeStruct((b, mt, d), BF16)],
        compiler_params=_params(("arbitrary",)),
        name="mem_kv",
    )(mem, g.reshape(1, d), wk.astype(BF16), wv.astype(BF16))


def _out_mem_kernel(x_ref, on_ref, or_ref, ng_ref, wo1_ref, wo2_ref, mpost_ref,
                    mpre_ref, wq_ref, kt_ref, v_ref, wo_ref, mempost_ref, o_ref):
    x = x_ref[0]
    a = _rms(on_ref[0], ng_ref[...]).astype(BF16)
    mixed = (jnp.dot(a, wo1_ref[...], preferred_element_type=F32)
             + jnp.dot(or_ref[0].astype(BF16), wo2_ref[...], preferred_element_type=F32))
    x = x + _rms(mixed, mpost_ref[...])

    h = _rms(x, mpre_ref[...]).astype(BF16)
    d = x.shape[-1]
    hd = d // MEM_HEADS
    q = jnp.dot(h, wq_ref[...], preferred_element_type=F32) * (hd ** -0.5)
    heads = []
    for hi in range(MEM_HEADS):
        cols = slice(hi * hd, (hi + 1) * hd)
        s = jnp.dot(q[:, cols].astype(BF16), kt_ref[0, cols, :], preferred_element_type=F32)
        e = jnp.exp(s - jnp.max(s, axis=-1, keepdims=True))
        p = e / jnp.sum(e, axis=-1, keepdims=True)
        heads.append(jnp.dot(p.astype(BF16), v_ref[0, :, cols], preferred_element_type=F32))
    o = jnp.concatenate(heads, axis=-1).astype(BF16)
    att = jnp.dot(o, wo_ref[...], preferred_element_type=F32)
    o_ref[0] = x + _rms(att, mempost_ref[...])


def _out_mem(x3, o_nsa, o_rwkv, nsa_g, w_out, mix_post_g, mem_pre_g, wq, kt, vm, wo, mem_post_g,
             *, tm=256):
    b, t, d = x3.shape
    mt = vm.shape[1]
    row = lambda bi, ti: (bi, ti, 0)
    per_b = lambda bi, ti: (bi, 0, 0)
    w_out = w_out.astype(BF16)
    return pl.pallas_call(
        _out_mem_kernel,
        grid=(b, t // tm),
        in_specs=[
            pl.BlockSpec((1, tm, d), row),
            pl.BlockSpec((1, tm, NSA_WIDTH), row),
            pl.BlockSpec((1, tm, RWKV_WIDTH), row),
            _const_spec((1, NSA_WIDTH)),
            _const_spec((NSA_WIDTH, d)),
            _const_spec((RWKV_WIDTH, d)),
            _const_spec((1, d)),
            _const_spec((1, d)),
            _const_spec((d, d)),
            pl.BlockSpec((1, d, mt), per_b),
            pl.BlockSpec((1, mt, d), per_b),
            _const_spec((d, d)),
            _const_spec((1, d)),
        ],
        out_specs=pl.BlockSpec((1, tm, d), row),
        out_shape=jax.ShapeDtypeStruct((b, t, d), F32),
        compiler_params=_params(("arbitrary", "arbitrary")),
        name="out_mem",
    )(x3, o_nsa, o_rwkv, nsa_g.reshape(1, -1), w_out[:NSA_WIDTH], w_out[NSA_WIDTH:],
      mix_post_g.reshape(1, d), mem_pre_g.reshape(1, d), wq.astype(BF16), kt, vm,
      wo.astype(BF16), mem_post_g.reshape(1, d))


def _rope_tables(pos):
    half = HEAD_DIM // 2
    inv = ROPE_THETA ** (-jnp.arange(half, dtype=F32) / half)
    ang = pos.astype(F32)[:, None] * inv[None, :]
    cos, sin = jnp.cos(ang), jnp.sin(ang)
    cos_t = jnp.concatenate([cos, cos, cos, cos], axis=-1)
    sin_t = jnp.concatenate([-sin, sin, -sin, sin], axis=-1)
    return cos_t, sin_t


def _overlap_t(ns, ncp):
    c0 = np.arange(ncp)[None, :] * CMP_STRIDE
    s0 = np.arange(ns)[:, None] * SEL_BLOCK
    ov = (c0 < s0 + SEL_BLOCK) & (c0 + CMP_BLOCK > s0) & (np.arange(ncp)[None, :] < ncp - 1)
    return jnp.asarray(ov.astype(np.float32), dtype=BF16)


def _pad_cols(w, n):
    return jnp.pad(w, ((0, 0), (0, n - w.shape[1])))


def _cmp_weights(pe, w1, w2):
    per = CMP_STRIDE
    w1r = w1.reshape(CMP_BLOCK, HEAD_DIM, CMP_HIDDEN)
    blocks = []
    for part in range(CMP_BLOCK // per):
        for g in range(NSA_GROUPS):
            z = jnp.zeros((per, NSA_GROUPS, HEAD_DIM, CMP_HIDDEN), F32)
            z = z.at[:, g].set(w1r[part * per:(part + 1) * per])
            blocks.append(z.reshape(per * NSA_GROUPS * HEAD_DIM, CMP_HIDDEN))
    w1cat = jnp.concatenate(blocks, axis=1).astype(BF16)
    w2bd = jnp.zeros((NSA_GROUPS * CMP_HIDDEN, NSA_GROUPS * HEAD_DIM), F32)
    for g in range(NSA_GROUPS):
        w2bd = w2bd.at[g * CMP_HIDDEN:(g + 1) * CMP_HIDDEN, g * HEAD_DIM:(g + 1) * HEAD_DIM].set(w2)
    pe8 = jnp.broadcast_to(pe.reshape(1, CMP_BLOCK * HEAD_DIM), (8, CMP_BLOCK * HEAD_DIM))
    return pe8.astype(BF16), w1.astype(BF16), w1cat, w2bd.astype(BF16)


def kernel(x, mem, ffn1_pre_g, ffn1_w_gate, ffn1_w_up, ffn1_w_down, ffn1_post_g, mix_pre_g, w_in, cmp_pe_k, cmp_w1_k, cmp_w2_k, cmp_pe_v, cmp_w1_v, cmp_w2_v, nsa_out_g, rwkv_mu, rwkv_w0, rwkv_w2, rwkv_a0, rwkv_a2, rwkv_g2, rwkv_k_k, rwkv_k_a, rwkv_r_k, rwkv_gn_g, rwkv_gn_b, w_out, mix_post_g, mem_pre_g, mem_kv_g, mem_wq, mem_wk, mem_wv, mem_wo, mem_post_g, ffn2_pre_g, ffn2_w_gate, ffn2_w_up, ffn2_w_down, ffn2_post_g):
    b, t, d = x.shape
    ncp = t // CMP_STRIDE
    ns = t // SEL_BLOCK
    cos_t, sin_t = _rope_tables(jnp.arange(t))
    cos_c, sin_c = _rope_tables(jnp.arange(ncp) * CMP_STRIDE + (CMP_BLOCK - 1))
    ovt = _overlap_t(ns, ncp)

    for l in range(ffn1_pre_g.shape[0]):
        x2 = _ffn_block(x.reshape(b * t, d), ffn1_pre_g[l], ffn1_w_gate[l], ffn1_w_up[l],
                        ffn1_w_down[l], ffn1_post_g[l])
        x3 = x2.reshape(b, t, d)

        wi = w_in[l]
        nsa_w = NSA_WIDTH + 6 * NSA_KV_WIDTH
        gate_w = wi[:, nsa_w:nsa_w + 3 * NSA_HEADS]
        w_cols = jnp.concatenate([wi[:, :nsa_w], _pad_cols(gate_w, LANES),
                                  wi[:, nsa_w + 3 * NSA_HEADS:]], axis=1).astype(BF16)
        (q, k_cmp, v_cmp, k_slc, v_slc_t, k_win, v_win_t, gates_t, r, k, v, lo) = _in_proj(
            x3, mix_pre_g[l], w_cols, rwkv_mu[l].reshape(1, -1), cos_t, sin_t)

        pek, w1k, w1kc, w2k = _cmp_weights(cmp_pe_k[l], cmp_w1_k[l], cmp_w2_k[l])
        pev, w1v, w1vc, w2v = _cmp_weights(cmp_pe_v[l], cmp_w1_v[l], cmp_w2_v[l])
        row_w = CMP_STRIDE * NSA_KV_WIDTH
        kc, vct = _compress(k_cmp.reshape(b, ncp, row_w), v_cmp.reshape(b, ncp, row_w),
                            pek, w1k, w1kc, w2k, pev, w1v, w1vc, w2v, cos_c, sin_c)
        o_nsa = _nsa_attn(q, gates_t, kc, vct, k_slc, v_slc_t, k_win, v_win_t, ovt)

        o_rwkv = _rwkv(r, k, v, lo, rwkv_w0[l], rwkv_w2[l], rwkv_a0[l], rwkv_a2[l], rwkv_g2[l],
                       rwkv_k_k[l], rwkv_k_a[l], rwkv_r_k[l], rwkv_gn_g[l], rwkv_gn_b[l])

        kt, vm = _mem_kv(mem, mem_kv_g[l], mem_wk[l], mem_wv[l])
        x4 = _out_mem(x3, o_nsa, o_rwkv, nsa_out_g[l], w_out[l], mix_post_g[l], mem_pre_g[l],
                      mem_wq[l], kt, vm, mem_wo[l], mem_post_g[l])

        x = _ffn_block(x4.reshape(b * t, d), ffn2_pre_g[l], ffn2_w_gate[l], ffn2_w_up[l],
                       ffn2_w_down[l], ffn2_post_g[l]).reshape(b, t, d)
    return x
```

```python
import functools

import numpy as np
import jax
import jax.numpy as jnp
from jax import lax
from jax.experimental import pallas as pl
from jax.experimental.pallas import tpu as pltpu

F32 = jnp.float32
BF16 = jnp.bfloat16

HEAD_DIM = 64
NSA_HEADS = 8
NSA_GROUPS = 2
NSA_GROUP_SIZE = 4
NSA_WIDTH = 512
NSA_KV_WIDTH = 128
CMP_BLOCK = 32
CMP_STRIDE = 16
CMP_HIDDEN = 256
SEL_BLOCK = 64
SEL_TOPK = 16
SEL_FORCE = 1e4
WINDOW = 512
RWKV_HEADS = 8
RWKV_WIDTH = 512
DECAY_LORA = 64
AAA_LORA = 64
GATE_LORA = 128
RWKV_GN_EPS = 64e-5
MEM_HEADS = 4
ROPE_THETA = 10000.0
NORM_EPS = 1e-6
NEG_INF = -1e30
LOG2_E = 1.4426950408889634

LANES = 128
KEY_BLOCK = 128
Q_TILE = 128
SEL_TRIP = 4
RWKV_CHUNK = 64
HALF = 256
VMEM_LIMIT = 56 * 1024 * 1024


def _bdot(a, b):
    return jnp.dot(a.astype(BF16), b.astype(BF16), preferred_element_type=F32)


def _split3(x):
    h1 = x.astype(BF16)
    r1 = x - h1.astype(F32)
    h2 = r1.astype(BF16)
    r2 = r1 - h2.astype(F32)
    return h1, h2, r2.astype(BF16)


def _dot3_right(x, m):
    h1, h2, h3 = _split3(x)
    d = lambda h: jnp.dot(h, m, preferred_element_type=F32)
    return d(h1) + d(h2) + d(h3)


def _dot3_left(m, x):
    h1, h2, h3 = _split3(x)
    d = lambda h: jnp.dot(m, h, preferred_element_type=F32)
    return d(h1) + d(h2) + d(h3)


def _rms(x, g):
    return x * lax.rsqrt(jnp.mean(x * x, axis=-1, keepdims=True) + NORM_EPS) * g


def _silu(x):
    return x / (1.0 + jnp.exp(-x))


def _sigmoid(x):
    return 1.0 / (1.0 + jnp.exp(-x))


def _const_spec(shape):
    nd = len(shape)
    return pl.BlockSpec(shape, lambda *_: (0,) * nd)


def _params(sem):
    return pltpu.CompilerParams(dimension_semantics=sem, vmem_limit_bytes=VMEM_LIMIT)


def _ffn_kernel(x_ref, pre_ref, wg_ref, wu_ref, wd_ref, post_ref, o_ref, *, ff_chunk):
    x = x_ref[...]
    h = _rms(x, pre_ref[...]).astype(BF16)
    d_ff = wg_ref.shape[1]
    acc = jnp.zeros(x.shape, F32)
    for c0 in range(0, d_ff, ff_chunk):
        g = jnp.dot(h, wg_ref[:, c0:c0 + ff_chunk], preferred_element_type=F32)
        u = jnp.dot(h, wu_ref[:, c0:c0 + ff_chunk], preferred_element_type=F32)
        a = (_silu(g) * u).astype(BF16)
        acc = acc + jnp.dot(a, wd_ref[c0:c0 + ff_chunk, :], preferred_element_type=F32)
    o_ref[...] = x + 0.5 * _rms(acc, post_ref[...])


def _ffn_block(x2, pre_g, wg, wu, wd, post_g, *, tm=512, ff_chunk=256):
    m, d = x2.shape
    d_ff = wg.shape[1]
    return pl.pallas_call(
        functools.partial(_ffn_kernel, ff_chunk=ff_chunk),
        grid=(m // tm,),
        in_specs=[
            pl.BlockSpec((tm, d), lambda i: (i, 0)),
            _const_spec((1, d)),
            _const_spec((d, d_ff)),
            _const_spec((d, d_ff)),
            _const_spec((d_ff, d)),
            _const_spec((1, d)),
        ],
        out_specs=pl.BlockSpec((tm, d), lambda i: (i, 0)),
        out_shape=jax.ShapeDtypeStruct((m, d), F32),
        compiler_params=_params(("arbitrary",)),
        name="ffn_block",
    )(x2, pre_g.reshape(1, d), wg.astype(BF16), wu.astype(BF16), wd.astype(BF16),
      post_g.reshape(1, d))


def _swap_halves(x):
    n = x.shape[-1]
    lane = lax.broadcasted_iota(jnp.int32, x.shape, x.ndim - 1)
    fwd = pltpu.roll(x, n - HEAD_DIM // 2, x.ndim - 1)
    bwd = pltpu.roll(x, HEAD_DIM // 2, x.ndim - 1)
    return jnp.where((lane % HEAD_DIM) < HEAD_DIM // 2, fwd, bwd)


def _rope(x, cos, sin_signed):
    reps = x.shape[-1] // LANES
    c = jnp.concatenate([cos] * reps, axis=-1) if reps > 1 else cos
    s = jnp.concatenate([sin_signed] * reps, axis=-1) if reps > 1 else sin_signed
    return x * c + _swap_halves(x) * s


def _in_proj_kernel(x_ref, g_ref, w_ref, mu_ref, cos_ref, sin_ref,
                    q_ref, kc_ref, vc_ref, ks_ref, vst_ref, kw_ref, vwt_ref, gt_ref,
                    r_ref, k_ref, v_ref, lo_ref, carry_ref):
    @pl.when(pl.program_id(1) == 0)
    def _():
        carry_ref[...] = jnp.zeros_like(carry_ref)

    h = _rms(x_ref[0], g_ref[...]).astype(BF16)
    p = jnp.dot(h, w_ref[...], preferred_element_type=F32)
    cos = cos_ref[...]
    sin = sin_ref[...]
    tm = p.shape[0]

    o = 0
    q = _rope(p[:, o:o + NSA_WIDTH], cos, sin) * (HEAD_DIM ** -0.5 * LOG2_E)
    q_ref[0] = q.astype(BF16)
    o += NSA_WIDTH
    kc_ref[0] = p[:, o:o + LANES].astype(BF16); o += LANES
    vc_ref[0] = p[:, o:o + LANES].astype(BF16); o += LANES
    ks_ref[0] = _rope(p[:, o:o + LANES], cos, sin).astype(BF16); o += LANES
    vst_ref[0] = p[:, o:o + LANES].T.astype(BF16); o += LANES
    kw_ref[0] = _rope(p[:, o:o + LANES], cos, sin).astype(BF16); o += LANES
    vwt_ref[0] = p[:, o:o + LANES].T.astype(BF16); o += LANES
    gates_t = _sigmoid(p[:, o:o + LANES]).T
    gt_ref[0] = gates_t[:gt_ref.shape[1], :]
    o += LANES

    rw = p[:, o:]
    row = lax.broadcasted_iota(jnp.int32, rw.shape, 0)
    prev = jnp.where(row == 0, carry_ref[0:1, :], pltpu.roll(rw, 1, 0))
    carry_ref[...] = jnp.broadcast_to(rw[tm - 1:tm, :], carry_ref.shape)
    mixed = rw + (prev - rw) * mu_ref[...]
    r_ref[0] = mixed[:, 0:512]
    k_ref[0] = mixed[:, 512:1024]
    v_ref[0] = mixed[:, 1024:1536]
    lo_ref[0] = mixed[:, 1536:1792]


def _in_proj(x3, g, w_cols, mu_cols, cos_t, sin_t, *, tm=256):
    b, t, d = x3.shape
    n = w_cols.shape[1]
    row = lambda bi, ti: (bi, ti, 0)
    col = lambda bi, ti: (bi, 0, ti)
    out_shapes = [
        jax.ShapeDtypeStruct((b, t, NSA_WIDTH), BF16),
        jax.ShapeDtypeStruct((b, t, LANES), BF16),
        jax.ShapeDtypeStruct((b, t, LANES), BF16),
        jax.ShapeDtypeStruct((b, t, LANES), BF16),
        jax.ShapeDtypeStruct((b, LANES, t), BF16),
        jax.ShapeDtypeStruct((b, t, LANES), BF16),
        jax.ShapeDtypeStruct((b, LANES, t), BF16),
        jax.ShapeDtypeStruct((b, 32, t), F32),
        jax.ShapeDtypeStruct((b, t, RWKV_WIDTH), F32),
        jax.ShapeDtypeStruct((b, t, RWKV_WIDTH), F32),
        jax.ShapeDtypeStruct((b, t, RWKV_WIDTH), F32),
        jax.ShapeDtypeStruct((b, t, 256), F32),
    ]
    out_specs = [
        pl.BlockSpec((1, tm, NSA_WIDTH), row),
        pl.BlockSpec((1, tm, LANES), row),
        pl.BlockSpec((1, tm, LANES), row),
        pl.BlockSpec((1, tm, LANES), row),
        pl.BlockSpec((1, LANES, tm), col),
        pl.BlockSpec((1, tm, LANES), row),
        pl.BlockSpec((1, LANES, tm), col),
        pl.BlockSpec((1, 32, tm), col),
        pl.BlockSpec((1, tm, RWKV_WIDTH), row),
        pl.BlockSpec((1, tm, RWKV_WIDTH), row),
        pl.BlockSpec((1, tm, RWKV_WIDTH), row),
        pl.BlockSpec((1, tm, 256), row),
    ]
    return pl.pallas_call(
        _in_proj_kernel,
        grid=(b, t // tm),
        in_specs=[
            pl.BlockSpec((1, tm, d), row),
            _const_spec((1, d)),
            _const_spec((d, n)),
            _const_spec((1, 1792)),
            pl.BlockSpec((tm, LANES), lambda bi, ti: (ti, 0)),
            pl.BlockSpec((tm, LANES), lambda bi, ti: (ti, 0)),
        ],
        out_specs=out_specs,
        out_shape=out_shapes,
        scratch_shapes=[pltpu.VMEM((8, 1792), F32)],
        compiler_params=_params(("arbitrary", "arbitrary")),
        name="in_proj",
    )(x3, g.reshape(1, d), w_cols, mu_cols, cos_t, sin_t)


def _compress_kernel(kin_ref, vin_ref, pek_ref, w1k_ref, w1kc_ref, w2k_ref,
                     pev_ref, w1v_ref, w1vc_ref, w2v_ref, cos_ref, sin_ref,
                     kc_ref, vct_ref):
    def phi(rows, pe, w1, w1cat, w2bd):
        n = rows.shape[0]
        pr = jnp.dot(rows, w1cat, preferred_element_type=F32)
        bias = jnp.dot(pe, w1, preferred_element_type=F32)[0:1, :]
        hid = []
        for g in range(NSA_GROUPS):
            top = pr[:, g * CMP_HIDDEN:(g + 1) * CMP_HIDDEN]
            bot = pr[:, (NSA_GROUPS + g) * CMP_HIDDEN:(NSA_GROUPS + g + 1) * CMP_HIDDEN]
            hid.append(top + pltpu.roll(bot, n - 1, 0) + bias)
        act = _silu(jnp.concatenate(hid, axis=-1)).astype(BF16)
        return jnp.dot(act, w2bd, preferred_element_type=F32)

    kc = phi(kin_ref[0], pek_ref[...], w1k_ref[...], w1kc_ref[...], w2k_ref[...])
    kc_ref[0] = _rope(kc, cos_ref[...], sin_ref[...]).astype(BF16)
    vc = phi(vin_ref[0], pev_ref[...], w1v_ref[...], w1vc_ref[...], w2v_ref[...])
    vct_ref[0] = vc.T.astype(BF16)


def _compress(kin, vin, pek, w1k, w1kc, w2k, pev, w1v, w1vc, w2v, cos_c, sin_c):
    b, ncp, width = kin.shape
    blk = pl.BlockSpec((1, ncp, width), lambda bi: (bi, 0, 0))
    cs = lambda a: _const_spec(a.shape)
    return pl.pallas_call(
        _compress_kernel,
        grid=(b,),
        in_specs=[blk, blk, cs(pek), cs(w1k), cs(w1kc), cs(w2k),
                  cs(pev), cs(w1v), cs(w1vc), cs(w2v), cs(cos_c), cs(sin_c)],
        out_specs=[pl.BlockSpec((1, ncp, LANES), lambda bi: (bi, 0, 0)),
                   pl.BlockSpec((1, LANES, ncp), lambda bi: (bi, 0, 0))],
        out_shape=[jax.ShapeDtypeStruct((b, ncp, LANES), BF16),
                   jax.ShapeDtypeStruct((b, LANES, ncp), BF16)],
        compiler_params=_params(("arbitrary",)),
        name="nsa_compress",
    )(kin, vin, pek, w1k, w1kc, w2k, pev, w1v, w1vc, w2v, cos_c, sin_c)


def _nsa_kernel(q_ref, gt_ref, kc_ref, vct_ref, ks_ref, vst_ref, kw_ref, vwt_ref, ovt_ref, ind_ref,
                o_ref, bias_ref, acc_ref):
    i = pl.program_id(1)
    q0 = i * Q_TILE
    ncp = kc_ref.shape[1]
    ns = ovt_ref.shape[0]
    width = NSA_GROUP_SIZE * Q_TILE
    t_row = q0 + lax.broadcasted_iota(jnp.int32, (1, width), 1) % Q_TILE
    sel_keys = SEL_TRIP * KEY_BLOCK
    win_keys = WINDOW + Q_TILE

    qf = q_ref[0].astype(F32)
    zeros_half = jnp.zeros((HEAD_DIM, Q_TILE), F32)

    def softmax_step(s, vt, carry):
        m_run, l_run = carry
        m_new = jnp.maximum(m_run, jnp.max(s, axis=0, keepdims=True))
        alpha = jnp.exp2(m_run - m_new)
        p = jnp.exp2(s - m_new)
        l_new = alpha * l_run + jnp.sum(p, axis=0, keepdims=True)
        pv = jnp.dot(vt, p.astype(BF16), preferred_element_type=F32)
        acc_ref[...] = acc_ref[...] * alpha + pv
        return m_new, l_new

    init = (jnp.full((1, width), NEG_INF, F32), jnp.zeros((1, width), F32))
    outs = []
    for g in range(NSA_GROUPS):
        feat_rows = slice(g * HEAD_DIM, (g + 1) * HEAD_DIM)
        parts = []
        for pair in range(NSA_GROUP_SIZE // 2):
            slab_t = qf[:, (2 * g + pair) * LANES:(2 * g + pair + 1) * LANES].T
            for half in range(2):
                feat = slab_t[half * HEAD_DIM:(half + 1) * HEAD_DIM, :]
                parts.append(jnp.concatenate([feat, zeros_half] if g == 0 else [zeros_half, feat], axis=0))
        qt = jnp.concatenate(parts, axis=1).astype(BF16)

        sc = jnp.dot(kc_ref[0], qt, preferred_element_type=F32)
        c_end = lax.broadcasted_iota(jnp.int32, (ncp, width), 0) * CMP_STRIDE + (CMP_BLOCK - 1)
        c_mask = c_end <= t_row
        sc = jnp.where(c_mask, sc, NEG_INF)
        m_c = jnp.max(sc, axis=0, keepdims=True)
        e_c = jnp.where(c_mask, jnp.exp2(sc - m_c), 0.0)
        p_c = e_c / jnp.maximum(jnp.sum(e_c, axis=0, keepdims=True), 1e-30)
        o_c = jnp.dot(vct_ref[0, feat_rows, :], p_c.astype(BF16), preferred_element_type=F32)

        p_sum = p_c[:, 0:Q_TILE]
        for r in range(1, NSA_GROUP_SIZE):
            p_sum = p_sum + p_c[:, r * Q_TILE:(r + 1) * Q_TILE]
        imp = _dot3_left(ovt_ref[...], p_sum)
        s_id = lax.broadcasted_iota(jnp.int32, (ns, Q_TILE), 0)
        cur = t_row[:, 0:Q_TILE] // SEL_BLOCK
        forced = (s_id == 0) | (s_id == cur) | (s_id == cur - 1)
        score = jnp.where(forced, SEL_FORCE, jnp.where(s_id <= cur, imp, -SEL_FORCE))

        def pick_round(_, carry):
            score, bias = carry
            mx = jnp.max(score, axis=0, keepdims=True)
            first = jnp.min(jnp.where(score == mx, s_id, ns), axis=0, keepdims=True)
            hit = s_id == first
            return jnp.where(hit, -3e38, score), jnp.where(hit, 0.0, bias)

        _, bias = lax.fori_loop(0, min(SEL_TOPK, ns), pick_round,
                                (score, jnp.full((ns, Q_TILE), NEG_INF, F32)))
        bias_ref[...] = bias

        def sel_trip(j, carry, causal):
            k0 = pl.multiple_of(j * sel_keys, sel_keys)
            b0 = pl.multiple_of(j * (sel_keys // SEL_BLOCK), sel_keys // SEL_BLOCK)
            lhs = jnp.concatenate([ks_ref[0, pl.ds(k0, sel_keys), :], ind_ref[...]], axis=1)
            rows = bias_ref[pl.ds(b0, sel_keys // SEL_BLOCK), :]
            rows = jnp.concatenate([rows] * NSA_GROUP_SIZE, axis=1)
            rows = jnp.concatenate([rows, jnp.zeros_like(rows)], axis=0).astype(BF16)
            rhs = jnp.concatenate([qt, rows, jnp.zeros((LANES - rows.shape[0], width), BF16)], axis=0)
            s = jnp.dot(lhs, rhs, preferred_element_type=F32)
            if causal:
                key = k0 + lax.broadcasted_iota(jnp.int32, (sel_keys, width), 0)
                s = jnp.where(key <= t_row, s, NEG_INF)
            return softmax_step(s, vst_ref[0, feat_rows, pl.ds(k0, sel_keys)], carry)

        acc_ref[...] = jnp.zeros_like(acc_ref)
        n_full = i // SEL_TRIP
        carry = lax.fori_loop(0, n_full, lambda j, c: sel_trip(j, c, False), init)
        _, l_s = sel_trip(n_full, carry, True)
        o_s = acc_ref[...] / l_s

        acc_ref[...] = jnp.zeros_like(acc_ref)
        w0 = pl.multiple_of(jnp.maximum(q0 - WINDOW, 0), KEY_BLOCK)
        sw = jnp.dot(kw_ref[0, pl.ds(w0, win_keys), :], qt, preferred_element_type=F32)
        rel = t_row - (w0 + lax.broadcasted_iota(jnp.int32, (win_keys, width), 0))
        sw = jnp.where((rel >= 0) & (rel < WINDOW), sw, NEG_INF)
        _, l_w = softmax_step(sw, vwt_ref[0, feat_rows, pl.ds(w0, win_keys)], init)
        o_w = acc_ref[...] / l_w

        heads = []
        for r in range(NSA_GROUP_SIZE):
            base = (g * NSA_GROUP_SIZE + r) * 3
            cols = slice(r * Q_TILE, (r + 1) * Q_TILE)
            heads.append(gt_ref[0, base:base + 1, :] * o_c[:, cols]
                         + gt_ref[0, base + 1:base + 2, :] * o_s[:, cols]
                         + gt_ref[0, base + 2:base + 3, :] * o_w[:, cols])
        for pair in range(NSA_GROUP_SIZE // 2):
            outs.append(jnp.concatenate(heads[2 * pair:2 * pair + 2], axis=0).T)
    o_ref[0] = jnp.concatenate(outs, axis=1)


def _nsa_attn(q, gates_t, kc, vct, ks, vst, kw, vwt, ovt):
    b, t, _ = q.shape
    ncp = kc.shape[1]
    ns = ovt.shape[0]
    sel_keys = SEL_TRIP * KEY_BLOCK
    assert t % sel_keys == 0 and t >= WINDOW + Q_TILE
    ind = (np.arange(sel_keys)[:, None] // SEL_BLOCK == np.arange(LANES)[None, :])
    ind = jnp.asarray(ind.astype(np.float32), dtype=BF16)
    full_rows = lambda bi, qi: (bi, 0, 0)
    return pl.pallas_call(
        _nsa_kernel,
        grid=(b, t // Q_TILE),
        in_specs=[
            pl.BlockSpec((1, Q_TILE, NSA_WIDTH), lambda bi, qi: (bi, qi, 0)),
            pl.BlockSpec((1, 32, Q_TILE), lambda bi, qi: (bi, 0, qi)),
            pl.BlockSpec((1, ncp, LANES), full_rows),
            pl.BlockSpec((1, LANES, ncp), full_rows),
            pl.BlockSpec((1, t, LANES), full_rows),
            pl.BlockSpec((1, LANES, t), full_rows),
            pl.BlockSpec((1, t, LANES), full_rows),
            pl.BlockSpec((1, LANES, t), full_rows),
            _const_spec((ns, ncp)),
            _const_spec((sel_keys, LANES)),
        ],
        out_specs=pl.BlockSpec((1, Q_TILE, NSA_WIDTH), lambda bi, qi: (bi, qi, 0)),
        out_shape=jax.ShapeDtypeStruct((b, t, NSA_WIDTH), F32),
        scratch_shapes=[pltpu.VMEM((ns, Q_TILE), F32),
                        pltpu.VMEM((HEAD_DIM, NSA_GROUP_SIZE * Q_TILE), F32)],
        compiler_params=_params(("arbitrary", "arbitrary")),
        name="nsa_attn",
    )(q, gates_t, kc, vct, ks, vst, kw, vwt, ovt, ind)


def _rwkv_kernel(r_ref, k_ref, v_ref, lo_ref, w0_ref, w2_ref, a0_ref, a2_ref, g2_ref,
                 kk_ref, ka_ref, rk_ref, gng_ref, gnb_ref, o_ref, s_ref):
    @pl.when(pl.program_id(1) == 0)
    def _():
        s_ref[...] = jnp.zeros_like(s_ref)

    c = RWKV_CHUNK
    r = r_ref[0]
    k = k_ref[0]
    v = v_ref[0]
    lo = lo_ref[0]
    w_lo = lo[:, 0:DECAY_LORA]
    a_lo = lo[:, DECAY_LORA:DECAY_LORA + AAA_LORA]
    g_lo = lo[:, DECAY_LORA + AAA_LORA:]

    z = -(w0_ref[...] + _bdot(jnp.tanh(w_lo), w2_ref[...]))
    softplus = jnp.maximum(z, 0.0) + jnp.log(1.0 + jnp.exp(-jnp.abs(z)))
    log_decay = -jnp.exp(-softplus - 0.5)
    lr = _sigmoid(a0_ref[...] + _bdot(a_lo, a2_ref[...]))
    gate = _bdot(_sigmoid(g_lo), g2_ref[...])

    row_i = lax.broadcasted_iota(jnp.int32, (c, c), 0)
    col_i = lax.broadcasted_iota(jnp.int32, (c, c), 1)
    tril_incl = (row_i >= col_i).astype(BF16)
    cum = _dot3_left(tril_incl, log_decay)
    g_incl = jnp.exp(cum)
    g_excl = jnp.exp(cum - log_decay)
    g_inv = jnp.exp(-cum)

    bi = lax.broadcasted_iota(jnp.int32, (HALF, HALF), 0) // HEAD_DIM
    bj = lax.broadcasted_iota(jnp.int32, (HALF, HALF), 1) // HEAD_DIM
    same_head = bi == bj
    ones_bd = same_head.astype(BF16)
    t_id = lax.broadcasted_iota(jnp.int32, (c, HALF), 0)
    j_id = lax.broadcasted_iota(jnp.int32, (c, HALF), 1) % HEAD_DIM
    strict_lower = t_id > j_id
    incl_lower = t_id >= j_id
    eye_all = (t_id == j_id).astype(F32)

    def bd_rows(x):
        xb = x.astype(BF16)
        return jnp.where(same_head, jnp.concatenate([xb] * (HALF // c), axis=0), jnp.zeros((), BF16))

    def bd_cols(xt):
        xb = xt.astype(BF16)
        return jnp.where(same_head, jnp.concatenate([xb, xb], axis=1), jnp.zeros((), BF16))

    outs = []
    for hh in range(RWKV_WIDTH // HALF):
        ln = slice(hh * HALF, (hh + 1) * HALF)
        rh, kh, vh = r[:, ln], k[:, ln], v[:, ln]
        lrh = lr[:, ln]
        kk = kh * kk_ref[:, ln]
        ssq = _dot3_right(kk * kk, ones_bd)
        kk = kk / jnp.maximum(jnp.sqrt(ssq), 1e-12)
        k2 = kh * (1.0 + (lrh - 1.0) * ka_ref[:, ln])

        at = -kk * g_excl[:, ln]
        bt = kk * lrh * g_inv[:, ln]
        kt = k2 * g_inv[:, ln]
        rt = rh * g_incl[:, ln]

        bt_bd = bd_cols(jnp.concatenate([bt, bt], axis=0).T)
        kt_bd = bd_cols(jnp.concatenate([kt, kt], axis=0).T)
        ar = jnp.concatenate([at, rt], axis=0).astype(BF16)
        ab = jnp.dot(ar, bt_bd, preferred_element_type=F32)
        ak = jnp.dot(ar, kt_bd, preferred_element_type=F32)
        a_ab = jnp.where(strict_lower, ab[0:c], 0.0)
        a_rb = jnp.where(incl_lower, ab[c:], 0.0)
        a_ak = jnp.where(strict_lower, ak[0:c], 0.0)
        a_rk = jnp.where(incl_lower, ak[c:], 0.0)

        inv = eye_all + a_ab
        pw = a_ab
        for _ in range(5):
            pw = jnp.dot(pw.astype(BF16), bd_rows(pw), preferred_element_type=F32)
            inv = inv + jnp.dot(inv.astype(BF16), bd_rows(pw), preferred_element_type=F32)

        v_bd = bd_rows(vh)
        w_mat = jnp.dot(inv.astype(BF16), bd_rows(at), preferred_element_type=F32)
        akv = jnp.dot(a_ak.astype(BF16), v_bd, preferred_element_type=F32)
        z_mat = jnp.dot(inv.astype(BF16), bd_rows(akv), preferred_element_type=F32)
        rkv = jnp.dot(a_rk.astype(BF16), v_bd, preferred_element_type=F32)

        s0 = s_ref[hh]
        s0b = s0.astype(BF16)
        u = jnp.dot(w_mat.astype(BF16), s0b, preferred_element_type=F32) + z_mat
        y = (jnp.dot(rt.astype(BF16), s0b, preferred_element_type=F32)
             + jnp.dot(a_rb.astype(BF16), bd_rows(u), preferred_element_type=F32) + rkv)
        bk_t = jnp.concatenate([bt, kt], axis=0).T.astype(BF16)
        uv = jnp.concatenate([u, vh], axis=0).astype(BF16)
        upd = jnp.dot(bk_t, uv, preferred_element_type=F32)
        g_last = jnp.broadcast_to(g_incl[c - 1:c, ln], (LANES, HALF)).T
        g_col = jnp.concatenate([g_last, g_last], axis=1)
        s_ref[hh] = g_col * (s0 + jnp.where(same_head, upd, 0.0))

        mu = _dot3_right(y, ones_bd) * (1.0 / HEAD_DIM)
        yc = y - mu
        var = _dot3_right(yc * yc, ones_bd) * (1.0 / HEAD_DIM)
        yn = yc * lax.rsqrt(var + RWKV_GN_EPS) * gng_ref[:, ln] + gnb_ref[:, ln]
        bonus = _dot3_right(rh * k2 * rk_ref[:, ln], ones_bd) * vh
        outs.append((yn + bonus) * gate[:, ln])
    o_ref[0] = jnp.concatenate(outs, axis=1)


def _rwkv(r, k, v, lo, w0, w2, a0, a2, g2, k_k, k_a, r_k, gn_g, gn_b):
    b, t, width = r.shape
    c = RWKV_CHUNK
    row = lambda bi, ci: (bi, ci, 0)
    vec = lambda a: a.reshape(1, width)
    cs = lambda a: _const_spec(a.shape)
    args = [vec(w0), w2.astype(BF16), vec(a0), a2.astype(BF16), g2.astype(BF16),
            vec(k_k), vec(k_a), vec(r_k), vec(gn_g), vec(gn_b)]
    return pl.pallas_call(
        _rwkv_kernel,
        grid=(b, t // c),
        in_specs=[pl.BlockSpec((1, c, width), row)] * 3 + [pl.BlockSpec((1, c, 256), row)]
                 + [cs(a) for a in args],
        out_specs=pl.BlockSpec((1, c, width), row),
        out_shape=jax.ShapeDtypeStruct((b, t, width), F32),
        scratch_shapes=[pltpu.VMEM((width // HALF, HALF, HALF), F32)],
        compiler_params=_params(("arbitrary", "arbitrary")),
        name="rwkv7",
    )(r, k, v, lo, *args)


def _mem_kv_kernel(m_ref, g_ref, wk_ref, wv_ref, kt_ref, v_ref):
    m = _rms(m_ref[0], g_ref[...]).astype(BF16)
    kt_ref[0] = jnp.dot(m, wk_ref[...], preferred_element_type=F32).T.astype(BF16)
    v_ref[0] = jnp.dot(m, wv_ref[...], preferred_element_type=F32).astype(BF16)


def _mem_kv(mem, g, wk, wv):
    b, mt, d = mem.shape
    return pl.pallas_call(
        _mem_kv_kernel,
        grid=(b,),
        in_specs=[pl.BlockSpec((1, mt, d), lambda bi: (bi, 0, 0)), _const_spec((1, d)),
                  _const_spec((d, d)), _const_spec((d, d))],
        out_specs=[pl.BlockSpec((1, d, mt), lambda bi: (bi, 0, 0)),
                   pl.BlockSpec((1, mt, d), lambda bi: (bi, 0, 0))],
        out_shape=[jax.ShapeDtypeStruct((b, d, mt), BF16), jax.ShapeDtypeStruct((b, mt, d), BF16)],
        compiler_params=_params(("arbitrary",)),
        name="mem_kv",
    )(mem, g.reshape(1, d), wk.astype(BF16), wv.astype(BF16))


def _out_mem_kernel(x_ref, on_ref, or_ref, ng_ref, wo1_ref, wo2_ref, mpost_ref,
                    mpre_ref, wq_ref, kt_ref, v_ref, wo_ref, mempost_ref, o_ref):
    x = x_ref[0]
    a = _rms(on_ref[0], ng_ref[...]).astype(BF16)
    mixed = (jnp.dot(a, wo1_ref[...], preferred_element_type=F32)
             + jnp.dot(or_ref[0].astype(BF16), wo2_ref[...], preferred_element_type=F32))
    x = x + _rms(mixed, mpost_ref[...])

    h = _rms(x, mpre_ref[...]).astype(BF16)
    d = x.shape[-1]
    hd = d // MEM_HEADS
    q = jnp.dot(h, wq_ref[...], preferred_element_type=F32) * (hd ** -0.5)
    heads = []
    for hi in range(MEM_HEADS):
        cols = slice(hi * hd, (hi + 1) * hd)
        s = jnp.dot(q[:, cols].astype(BF16), kt_ref[0, cols, :], preferred_element_type=F32)
        e = jnp.exp(s - jnp.max(s, axis=-1, keepdims=True))
        p = e / jnp.sum(e, axis=-1, keepdims=True)
        heads.append(jnp.dot(p.astype(BF16), v_ref[0, :, cols], preferred_element_type=F32))
    o = jnp.concatenate(heads, axis=-1).astype(BF16)
    att = jnp.dot(o, wo_ref[...], preferred_element_type=F32)
    o_ref[0] = x + _rms(att, mempost_ref[...])


def _out_mem(x3, o_nsa, o_rwkv, nsa_g, w_out, mix_post_g, mem_pre_g, wq, kt, vm, wo, mem_post_g,
             *, tm=256):
    b, t, d = x3.shape
    mt = vm.shape[1]
    row = lambda bi, ti: (bi, ti, 0)
    per_b = lambda bi, ti: (bi, 0, 0)
    w_out = w_out.astype(BF16)
    return pl.pallas_call(
        _out_mem_kernel,
        grid=(b, t // tm),
        in_specs=[
            pl.BlockSpec((1, tm, d), row),
            pl.BlockSpec((1, tm, NSA_WIDTH), row),
            pl.BlockSpec((1, tm, RWKV_WIDTH), row),
            _const_spec((1, NSA_WIDTH)),
            _const_spec((NSA_WIDTH, d)),
            _const_spec((RWKV_WIDTH, d)),
            _const_spec((1, d)),
            _const_spec((1, d)),
            _const_spec((d, d)),
            pl.BlockSpec((1, d, mt), per_b),
            pl.BlockSpec((1, mt, d), per_b),
            _const_spec((d, d)),
            _const_spec((1, d)),
        ],
        out_specs=pl.BlockSpec((1, tm, d), row),
        out_shape=jax.ShapeDtypeStruct((b, t, d), F32),
        compiler_params=_params(("arbitrary", "arbitrary")),
        name="out_mem",
    )(x3, o_nsa, o_rwkv, nsa_g.reshape(1, -1), w_out[:NSA_WIDTH], w_out[NSA_WIDTH:],
      mix_post_g.reshape(1, d), mem_pre_g.reshape(1, d), wq.astype(BF16), kt, vm,
      wo.astype(BF16), mem_post_g.reshape(1, d))


def _rope_tables(pos):
    half = HEAD_DIM // 2
    inv = ROPE_THETA ** (-jnp.arange(half, dtype=F32) / half)
    ang = pos.astype(F32)[:, None] * inv[None, :]
    cos, sin = jnp.cos(ang), jnp.sin(ang)
    cos_t = jnp.concatenate([cos, cos, cos, cos], axis=-1)
    sin_t = jnp.concatenate([-sin, sin, -sin, sin], axis=-1)
    return cos_t, sin_t


def _overlap_t(ns, ncp):
    c0 = np.arange(ncp)[None, :] * CMP_STRIDE
    s0 = np.arange(ns)[:, None] * SEL_BLOCK
    ov = (c0 < s0 + SEL_BLOCK) & (c0 + CMP_BLOCK > s0) & (np.arange(ncp)[None, :] < ncp - 1)
    return jnp.asarray(ov.astype(np.float32), dtype=BF16)


def _pad_cols(w, n):
    return jnp.pad(w, ((0, 0), (0, n - w.shape[1])))


def _cmp_weights(pe, w1, w2):
    per = CMP_STRIDE
    w1r = w1.reshape(CMP_BLOCK, HEAD_DIM, CMP_HIDDEN)
    blocks = []
    for part in range(CMP_BLOCK // per):
        for g in range(NSA_GROUPS):
            z = jnp.zeros((per, NSA_GROUPS, HEAD_DIM, CMP_HIDDEN), F32)
            z = z.at[:, g].set(w1r[part * per:(part + 1) * per])
            blocks.append(z.reshape(per * NSA_GROUPS * HEAD_DIM, CMP_HIDDEN))
    w1cat = jnp.concatenate(blocks, axis=1).astype(BF16)
    w2bd = jnp.zeros((NSA_GROUPS * CMP_HIDDEN, NSA_GROUPS * HEAD_DIM), F32)
    for g in range(NSA_GROUPS):
        w2bd = w2bd.at[g * CMP_HIDDEN:(g + 1) * CMP_HIDDEN, g * HEAD_DIM:(g + 1) * HEAD_DIM].set(w2)
    pe8 = jnp.broadcast_to(pe.reshape(1, CMP_BLOCK * HEAD_DIM), (8, CMP_BLOCK * HEAD_DIM))
    return pe8.astype(BF16), w1.astype(BF16), w1cat, w2bd.astype(BF16)


def kernel(x, mem, ffn1_pre_g, ffn1_w_gate, ffn1_w_up, ffn1_w_down, ffn1_post_g, mix_pre_g, w_in, cmp_pe_k, cmp_w1_k, cmp_w2_k, cmp_pe_v, cmp_w1_v, cmp_w2_v, nsa_out_g, rwkv_mu, rwkv_w0, rwkv_w2, rwkv_a0, rwkv_a2, rwkv_g2, rwkv_k_k, rwkv_k_a, rwkv_r_k, rwkv_gn_g, rwkv_gn_b, w_out, mix_post_g, mem_pre_g, mem_kv_g, mem_wq, mem_wk, mem_wv, mem_wo, mem_post_g, ffn2_pre_g, ffn2_w_gate, ffn2_w_up, ffn2_w_down, ffn2_post_g):
    b, t, d = x.shape
    ncp = t // CMP_STRIDE
    ns = t // SEL_BLOCK
    cos_t, sin_t = _rope_tables(jnp.arange(t))
    cos_c, sin_c = _rope_tables(jnp.arange(ncp) * CMP_STRIDE + (CMP_BLOCK - 1))
    ovt = _overlap_t(ns, ncp)

    for l in range(ffn1_pre_g.shape[0]):
        x2 = _ffn_block(x.reshape(b * t, d), ffn1_pre_g[l], ffn1_w_gate[l], ffn1_w_up[l],
                        ffn1_w_down[l], ffn1_post_g[l])
        x3 = x2.reshape(b, t, d)

        wi = w_in[l]
        nsa_w = NSA_WIDTH + 6 * NSA_KV_WIDTH
        gate_w = wi[:, nsa_w:nsa_w + 3 * NSA_HEADS]
        w_cols = jnp.concatenate([wi[:, :nsa_w], _pad_cols(gate_w, LANES),
                                  wi[:, nsa_w + 3 * NSA_HEADS:]], axis=1).astype(BF16)
        (q, k_cmp, v_cmp, k_slc, v_slc_t, k_win, v_win_t, gates_t, r, k, v, lo) = _in_proj(
            x3, mix_pre_g[l], w_cols, rwkv_mu[l].reshape(1, -1), cos_t, sin_t)

        pek, w1k, w1kc, w2k = _cmp_weights(cmp_pe_k[l], cmp_w1_k[l], cmp_w2_k[l])
        pev, w1v, w1vc, w2v = _cmp_weights(cmp_pe_v[l], cmp_w1_v[l], cmp_w2_v[l])
        row_w = CMP_STRIDE * NSA_KV_WIDTH
        kc, vct = _compress(k_cmp.reshape(b, ncp, row_w), v_cmp.reshape(b, ncp, row_w),
                            pek, w1k, w1kc, w2k, pev, w1v, w1vc, w2v, cos_c, sin_c)
        o_nsa = _nsa_attn(q, gates_t, kc, vct, k_slc, v_slc_t, k_win, v_win_t, ovt)

        o_rwkv = _rwkv(r, k, v, lo, rwkv_w0[l], rwkv_w2[l], rwkv_a0[l], rwkv_a2[l], rwkv_g2[l],
                       rwkv_k_k[l], rwkv_k_a[l], rwkv_r_k[l], rwkv_gn_g[l], rwkv_gn_b[l])

        kt, vm = _mem_kv(mem, mem_kv_g[l], mem_wk[l], mem_wv[l])
        x4 = _out_mem(x3, o_nsa, o_rwkv, nsa_out_g[l], w_out[l], mix_post_g[l], mem_pre_g[l],
                      mem_wq[l], kt, vm, mem_wo[l], mem_post_g[l])

        x = _ffn_block(x4.reshape(b * t, d), ffn2_pre_g[l], ffn2_w_gate[l], ffn2_w_up[l],
                       ffn2_w_down[l], ffn2_post_g[l]).reshape(b, t, d)
    return x
```

```python
import functools

import numpy as np
import jax
import jax.numpy as jnp
from jax import lax
from jax.experimental import pallas as pl
from jax.experimental.pallas import tpu as pltpu

F32 = jnp.float32
BF16 = jnp.bfloat16

HEAD_DIM = 64
NSA_HEADS = 8
NSA_GROUPS = 2
NSA_GROUP_SIZE = 4
NSA_WIDTH = 512
NSA_KV_WIDTH = 128
CMP_BLOCK = 32
CMP_STRIDE = 16
CMP_HIDDEN = 256
SEL_BLOCK = 64
SEL_TOPK = 16
SEL_FORCE = 1e4
SEL_FORCED = 3
WINDOW = 512
RWKV_HEADS = 8
RWKV_WIDTH = 512
DECAY_LORA = 64
AAA_LORA = 64
GATE_LORA = 128
RWKV_GN_EPS = 64e-5
MEM_HEADS = 4
ROPE_THETA = 10000.0
NORM_EPS = 1e-6
NEG_INF = -1e30
LOG2_E = 1.4426950408889634

LANES = 128
KEY_BLOCK = 128
Q_TILE = 128
SEL_TRIP = 4
RWKV_CHUNK = 64
HALF = 256
VMEM_LIMIT = 56 * 1024 * 1024


def _bdot(a, b):
    return jnp.dot(a.astype(BF16), b.astype(BF16), preferred_element_type=F32)


def _split3(x):
    h1 = x.astype(BF16)
    r1 = x - h1.astype(F32)
    h2 = r1.astype(BF16)
    r2 = r1 - h2.astype(F32)
    return h1, h2, r2.astype(BF16)


def _split2(x):
    h1 = x.astype(BF16)
    return h1, (x - h1.astype(F32)).astype(BF16)


def _dot2_right(x, m):
    h1, h2 = _split2(x)
    return (jnp.dot(h1, m, preferred_element_type=F32) + jnp.dot(h2, m, preferred_element_type=F32))


def _dot3_right(x, m):
    h1, h2, h3 = _split3(x)
    d = lambda h: jnp.dot(h, m, preferred_element_type=F32)
    return d(h1) + d(h2) + d(h3)


def _dot3_left(m, x):
    h1, h2, h3 = _split3(x)
    d = lambda h: jnp.dot(m, h, preferred_element_type=F32)
    return d(h1) + d(h2) + d(h3)


def _rms(x, g):
    return x * lax.rsqrt(jnp.mean(x * x, axis=-1, keepdims=True) + NORM_EPS) * g


def _silu(x):
    return x / (1.0 + jnp.exp(-x))


def _sigmoid(x):
    return 1.0 / (1.0 + jnp.exp(-x))


def _const_spec(shape):
    nd = len(shape)
    return pl.BlockSpec(shape, lambda *_: (0,) * nd)


def _params(sem):
    return pltpu.CompilerParams(dimension_semantics=sem, vmem_limit_bytes=VMEM_LIMIT)


def _ffn_kernel(x_ref, pre_ref, wg_ref, wu_ref, wd_ref, post_ref, o_ref, *, ff_chunk):
    x = x_ref[...]
    h = _rms(x, pre_ref[...]).astype(BF16)
    d_ff = wg_ref.shape[1]
    acc = jnp.zeros(x.shape, F32)
    for c0 in range(0, d_ff, ff_chunk):
        g = jnp.dot(h, wg_ref[:, c0:c0 + ff_chunk], preferred_element_type=F32)
        u = jnp.dot(h, wu_ref[:, c0:c0 + ff_chunk], preferred_element_type=F32)
        a = (_silu(g) * u).astype(BF16)
        acc = acc + jnp.dot(a, wd_ref[c0:c0 + ff_chunk, :], preferred_element_type=F32)
    o_ref[...] = x + 0.5 * _rms(acc, post_ref[...])


def _ffn_block(x2, pre_g, wg, wu, wd, post_g, *, tm=512, ff_chunk=256):
    m, d = x2.shape
    d_ff = wg.shape[1]
    return pl.pallas_call(
        functools.partial(_ffn_kernel, ff_chunk=ff_chunk),
        grid=(m // tm,),
        in_specs=[
            pl.BlockSpec((tm, d), lambda i: (i, 0)),
            _const_spec((1, d)),
            _const_spec((d, d_ff)),
            _const_spec((d, d_ff)),
            _const_spec((d_ff, d)),
            _const_spec((1, d)),
        ],
        out_specs=pl.BlockSpec((tm, d), lambda i: (i, 0)),
        out_shape=jax.ShapeDtypeStruct((m, d), F32),
        compiler_params=_params(("arbitrary",)),
        name="ffn_block",
    )(x2, pre_g.reshape(1, d), wg.astype(BF16), wu.astype(BF16), wd.astype(BF16),
      post_g.reshape(1, d))


def _swap_halves(x):
    n = x.shape[-1]
    lane = lax.broadcasted_iota(jnp.int32, x.shape, x.ndim - 1)
    fwd = pltpu.roll(x, n - HEAD_DIM // 2, x.ndim - 1)
    bwd = pltpu.roll(x, HEAD_DIM // 2, x.ndim - 1)
    return jnp.where((lane % HEAD_DIM) < HEAD_DIM // 2, fwd, bwd)


def _rope(x, cos, sin_signed):
    reps = x.shape[-1] // LANES
    c = jnp.concatenate([cos] * reps, axis=-1) if reps > 1 else cos
    s = jnp.concatenate([sin_signed] * reps, axis=-1) if reps > 1 else sin_signed
    return x * c + _swap_halves(x) * s


def _in_proj_kernel(x_ref, g_ref, w_ref, mu_ref, cos_ref, sin_ref,
                    q_ref, kc_ref, vc_ref, ks_ref, vst_ref, kw_ref, vwt_ref, gt_ref,
                    r_ref, k_ref, v_ref, lo_ref, carry_ref):
    @pl.when(pl.program_id(1) == 0)
    def _():
        carry_ref[...] = jnp.zeros_like(carry_ref)

    h = _rms(x_ref[0], g_ref[...]).astype(BF16)
    p = jnp.dot(h, w_ref[...], preferred_element_type=F32)
    cos = cos_ref[...]
    sin = sin_ref[...]
    tm = p.shape[0]

    o = 0
    q = _rope(p[:, o:o + NSA_WIDTH], cos, sin) * (HEAD_DIM ** -0.5 * LOG2_E)
    q_ref[0] = q.astype(BF16)
    o += NSA_WIDTH
    kc_ref[0] = p[:, o:o + LANES].astype(BF16); o += LANES
    vc_ref[0] = p[:, o:o + LANES].astype(BF16); o += LANES
    ks_ref[0] = _rope(p[:, o:o + LANES], cos, sin).astype(BF16); o += LANES
    vst_ref[0] = p[:, o:o + LANES].T.astype(BF16); o += LANES
    kw_ref[0] = _rope(p[:, o:o + LANES], cos, sin).astype(BF16); o += LANES
    vwt_ref[0] = p[:, o:o + LANES].T.astype(BF16); o += LANES
    gates_t = _sigmoid(p[:, o:o + LANES]).T
    gt_ref[0] = gates_t[:gt_ref.shape[1], :]
    o += LANES

    rw = p[:, o:]
    row = lax.broadcasted_iota(jnp.int32, rw.shape, 0)
    prev = jnp.where(row == 0, carry_ref[0:1, :], pltpu.roll(rw, 1, 0))
    carry_ref[...] = jnp.broadcast_to(rw[tm - 1:tm, :], carry_ref.shape)
    mixed = rw + (prev - rw) * mu_ref[...]
    r_ref[0] = mixed[:, 0:512]
    k_ref[0] = mixed[:, 512:1024]
    v_ref[0] = mixed[:, 1024:1536]
    lo_ref[0] = mixed[:, 1536:1792]


def _in_proj(x3, g, w_cols, mu_cols, cos_t, sin_t, *, tm=256):
    b, t, d = x3.shape
    n = w_cols.shape[1]
    row = lambda bi, ti: (bi, ti, 0)
    col = lambda bi, ti: (bi, 0, ti)
    out_shapes = [
        jax.ShapeDtypeStruct((b, t, NSA_WIDTH), BF16),
        jax.ShapeDtypeStruct((b, t, LANES), BF16),
        jax.ShapeDtypeStruct((b, t, LANES), BF16),
        jax.ShapeDtypeStruct((b, t, LANES), BF16),
        jax.ShapeDtypeStruct((b, LANES, t), BF16),
        jax.ShapeDtypeStruct((b, t, LANES), BF16),
        jax.ShapeDtypeStruct((b, LANES, t), BF16),
        jax.ShapeDtypeStruct((b, 32, t), F32),
        jax.ShapeDtypeStruct((b, t, RWKV_WIDTH), F32),
        jax.ShapeDtypeStruct((b, t, RWKV_WIDTH), F32),
        jax.ShapeDtypeStruct((b, t, RWKV_WIDTH), F32),
        jax.ShapeDtypeStruct((b, t, 256), F32),
    ]
    out_specs = [
        pl.BlockSpec((1, tm, NSA_WIDTH), row),
        pl.BlockSpec((1, tm, LANES), row),
        pl.BlockSpec((1, tm, LANES), row),
        pl.BlockSpec((1, tm, LANES), row),
        pl.BlockSpec((1, LANES, tm), col),
        pl.BlockSpec((1, tm, LANES), row),
        pl.BlockSpec((1, LANES, tm), col),
        pl.BlockSpec((1, 32, tm), col),
        pl.BlockSpec((1, tm, RWKV_WIDTH), row),
        pl.BlockSpec((1, tm, RWKV_WIDTH), row),
        pl.BlockSpec((1, tm, RWKV_WIDTH), row),
        pl.BlockSpec((1, tm, 256), row),
    ]
    return pl.pallas_call(
        _in_proj_kernel,
        grid=(b, t // tm),
        in_specs=[
            pl.BlockSpec((1, tm, d), row),
            _const_spec((1, d)),
            _const_spec((d, n)),
            _const_spec((1, 1792)),
            pl.BlockSpec((tm, LANES), lambda bi, ti: (ti, 0)),
            pl.BlockSpec((tm, LANES), lambda bi, ti: (ti, 0)),
        ],
        out_specs=out_specs,
        out_shape=out_shapes,
        scratch_shapes=[pltpu.VMEM((8, 1792), F32)],
        compiler_params=_params(("arbitrary", "arbitrary")),
        name="in_proj",
    )(x3, g.reshape(1, d), w_cols, mu_cols, cos_t, sin_t)


def _compress_kernel(kin_ref, vin_ref, pek_ref, w1k_ref, w1kc_ref, w2k_ref,
                     pev_ref, w1v_ref, w1vc_ref, w2v_ref, cos_ref, sin_ref,
                     kc_ref, vct_ref):
    def phi(rows, pe, w1, w1cat, w2bd):
        n = rows.shape[0]
        pr = jnp.dot(rows, w1cat, preferred_element_type=F32)
        bias = jnp.dot(pe, w1, preferred_element_type=F32)[0:1, :]
        hid = []
        for g in range(NSA_GROUPS):
            top = pr[:, g * CMP_HIDDEN:(g + 1) * CMP_HIDDEN]
            bot = pr[:, (NSA_GROUPS + g) * CMP_HIDDEN:(NSA_GROUPS + g + 1) * CMP_HIDDEN]
            hid.append(top + pltpu.roll(bot, n - 1, 0) + bias)
        act = _silu(jnp.concatenate(hid, axis=-1)).astype(BF16)
        return jnp.dot(act, w2bd, preferred_element_type=F32)

    kc = phi(kin_ref[0], pek_ref[...], w1k_ref[...], w1kc_ref[...], w2k_ref[...])
    kc_ref[0] = _rope(kc, cos_ref[...], sin_ref[...]).astype(BF16)
    vc = phi(vin_ref[0], pev_ref[...], w1v_ref[...], w1vc_ref[...], w2v_ref[...])
    vct_ref[0] = vc.T.astype(BF16)


def _compress(kin, vin, pek, w1k, w1kc, w2k, pev, w1v, w1vc, w2v, cos_c, sin_c):
    b, ncp, width = kin.shape
    blk = pl.BlockSpec((1, ncp, width), lambda bi: (bi, 0, 0))
    cs = lambda a: _const_spec(a.shape)
    return pl.pallas_call(
        _compress_kernel,
        grid=(b,),
        in_specs=[blk, blk, cs(pek), cs(w1k), cs(w1kc), cs(w2k),
                  cs(pev), cs(w1v), cs(w1vc), cs(w2v), cs(cos_c), cs(sin_c)],
        out_specs=[pl.BlockSpec((1, ncp, LANES), lambda bi: (bi, 0, 0)),
                   pl.BlockSpec((1, LANES, ncp), lambda bi: (bi, 0, 0))],
        out_shape=[jax.ShapeDtypeStruct((b, ncp, LANES), BF16),
                   jax.ShapeDtypeStruct((b, LANES, ncp), BF16)],
        compiler_params=_params(("arbitrary",)),
        name="nsa_compress",
    )(kin, vin, pek, w1k, w1kc, w2k, pev, w1v, w1vc, w2v, cos_c, sin_c)


def _nsa_kernel(q_ref, gt_ref, kc_ref, vct_ref, ks_ref, vst_ref, kw_ref, vwt_ref, ovt_ref, ind_ref,
                o_ref, bias_ref, sa_ref, sb_ref, acc_ref):
    i = pl.program_id(1)
    q0 = i * Q_TILE
    ncp = kc_ref.shape[1]
    ns = ovt_ref.shape[0]
    gw = NSA_GROUP_SIZE * Q_TILE
    width = NSA_GROUPS * gw
    lanes = [slice(g * gw, (g + 1) * gw) for g in range(NSA_GROUPS)]
    feat = [slice(g * HEAD_DIM, (g + 1) * HEAD_DIM) for g in range(NSA_GROUPS)]
    t_row = q0 + lax.broadcasted_iota(jnp.int32, (1, width), 1) % Q_TILE
    sel_keys = SEL_TRIP * KEY_BLOCK
    sel_rows = sel_keys // SEL_BLOCK
    win_keys = WINDOW + Q_TILE
    dot = lambda x, y: jnp.dot(x, y, preferred_element_type=F32)

    qf = q_ref[0].astype(F32)
    zeros_half = jnp.zeros((HEAD_DIM, Q_TILE), F32)
    parts = []
    for g in range(NSA_GROUPS):
        for pair in range(NSA_GROUP_SIZE // 2):
            slab_t = qf[:, (2 * g + pair) * LANES:(2 * g + pair + 1) * LANES].T
            for half in range(2):
                f = slab_t[half * HEAD_DIM:(half + 1) * HEAD_DIM, :]
                parts.append(jnp.concatenate([f, zeros_half] if g == 0 else [zeros_half, f], axis=0))
    qt = jnp.concatenate(parts, axis=1).astype(BF16)

    def v_ext(vt_ref, g, k0, n):
        return jnp.concatenate([vt_ref[0, feat[g], pl.ds(k0, n)], jnp.ones((16, n), BF16)], axis=0)

    sc = dot(kc_ref[0], qt)
    c_end = lax.broadcasted_iota(jnp.int32, (ncp, width), 0) * CMP_STRIDE + (CMP_BLOCK - 1)
    c_mask = c_end <= t_row
    sc = jnp.where(c_mask, sc, NEG_INF)
    m_c = jnp.max(sc, axis=0, keepdims=True)
    e_c = jnp.where(c_mask, jnp.exp2(sc - m_c), 0.0)
    p_c = e_c / jnp.maximum(jnp.sum(e_c, axis=0, keepdims=True), 1e-30)
    o_c = [dot(vct_ref[0, feat[g], :], p_c[:, lanes[g]].astype(BF16)) for g in range(NSA_GROUPS)]

    p_sum = []
    for g in range(NSA_GROUPS):
        acc = p_c[:, g * gw:g * gw + Q_TILE]
        for r in range(1, NSA_GROUP_SIZE):
            acc = acc + p_c[:, g * gw + r * Q_TILE:g * gw + (r + 1) * Q_TILE]
        p_sum.append(acc)
    sw_ = NSA_GROUPS * Q_TILE
    imp = _dot3_left(ovt_ref[...], jnp.concatenate(p_sum, axis=1))
    s_id = lax.broadcasted_iota(jnp.int32, (ns, sw_), 0)
    cur = t_row[:, 0:sw_] // SEL_BLOCK
    forced = (s_id == 0) | (s_id == cur) | (s_id == cur - 1)
    score = jnp.where(forced, -3e38, jnp.where(s_id <= cur, imp, -SEL_FORCE))

    def pick_round(_, carry):
        score, bias = carry
        mx = jnp.max(score, axis=0, keepdims=True)
        first = jnp.min(jnp.where(score == mx, s_id, ns), axis=0, keepdims=True)
        hit = s_id == first
        return jnp.where(hit, -3e38, score), jnp.where(hit, 0.0, bias)

    _, bias = lax.fori_loop(0, min(SEL_TOPK, ns) - SEL_FORCED, pick_round,
                            (score, jnp.where(forced, 0.0, NEG_INF)))
    bias_ref[...] = bias

    def scores(j, s_ref):
        k0 = pl.multiple_of(j * sel_keys, sel_keys)
        b0 = pl.multiple_of(j * sel_rows, sel_rows)
        rows = bias_ref[pl.ds(b0, sel_rows), :]
        rows = jnp.concatenate([rows[:, g * Q_TILE:(g + 1) * Q_TILE]
                                for g in range(NSA_GROUPS) for _ in range(NSA_GROUP_SIZE)], axis=1)
        rows = jnp.concatenate([rows, jnp.zeros_like(rows)], axis=0).astype(BF16)
        rhs = jnp.concatenate([qt, rows, jnp.zeros((LANES - rows.shape[0], width), BF16)], axis=0)
        lhs = jnp.concatenate([ks_ref[0, pl.ds(k0, sel_keys), :], ind_ref[...]], axis=1)
        s = dot(lhs, rhs)
        s_ref[...] = s
        return jnp.max(s, axis=0, keepdims=True)

    def softmax_pv(j, s_ref, mb, m_run):
        k0 = pl.multiple_of(j * sel_keys, sel_keys)
        m_new = jnp.maximum(m_run, mb)
        alpha = jnp.exp2(m_run - m_new)
        for g in range(NSA_GROUPS):
            p = jnp.exp2(s_ref[:, lanes[g]] - m_new[:, lanes[g]]).astype(BF16)
            pv = dot(v_ext(vst_ref, g, k0, sel_keys), p)
            acc_ref[g] = acc_ref[g] * alpha[:, lanes[g]] + pv
        return m_new

    def causal_tail(j, s_ref, m_run):
        k0 = pl.multiple_of(j * sel_keys, sel_keys)
        key = k0 + lax.broadcasted_iota(jnp.int32, (sel_keys, width), 0)
        s = jnp.where(key <= t_row, s_ref[...], NEG_INF)
        s_ref[...] = s
        softmax_pv(j, s_ref, jnp.max(s, axis=0, keepdims=True), m_run)

    acc_ref[...] = jnp.zeros_like(acc_ref)
    n_full = q0 // sel_keys
    n_pairs = n_full // 2

    def pair_body(kk, carry):
        m_run, mb_a = carry
        j = 2 * kk
        mb_b = scores(j + 1, sb_ref)
        m_run = softmax_pv(j, sa_ref, mb_a, m_run)
        mb_a = scores(j + 2, sa_ref)
        m_run = softmax_pv(j + 1, sb_ref, mb_b, m_run)
        return m_run, mb_a

    m_run, mb_a = lax.fori_loop(0, n_pairs, pair_body,
                                (jnp.full((1, width), NEG_INF, F32), scores(0, sa_ref)))
    j_last = 2 * n_pairs

    @pl.when(n_full % 2 == 1)
    def _():
        scores(j_last + 1, sb_ref)
        causal_tail(j_last + 1, sb_ref, softmax_pv(j_last, sa_ref, mb_a, m_run))

    @pl.when(n_full % 2 == 0)
    def _():
        causal_tail(j_last, sa_ref, m_run)

    o_s = [acc_ref[g, 0:HEAD_DIM, :] / acc_ref[g, HEAD_DIM:HEAD_DIM + 1, :] for g in range(NSA_GROUPS)]

    w0 = pl.multiple_of(jnp.maximum(q0 - WINDOW, 0), KEY_BLOCK)
    sw = dot(kw_ref[0, pl.ds(w0, win_keys), :], qt)
    rel = t_row - (w0 + lax.broadcasted_iota(jnp.int32, (win_keys, width), 0))
    sw = jnp.where((rel >= 0) & (rel < WINDOW), sw, NEG_INF)
    m_w = jnp.max(sw, axis=0, keepdims=True)
    o_w = []
    for g in range(NSA_GROUPS):
        p = jnp.exp2(sw[:, lanes[g]] - m_w[:, lanes[g]]).astype(BF16)
        pv = dot(v_ext(vwt_ref, g, w0, win_keys), p)
        o_w.append(pv[0:HEAD_DIM, :] / pv[HEAD_DIM:HEAD_DIM + 1, :])

    outs = []
    for g in range(NSA_GROUPS):
        heads = []
        for r in range(NSA_GROUP_SIZE):
            base = (g * NSA_GROUP_SIZE + r) * 3
            cols = slice(r * Q_TILE, (r + 1) * Q_TILE)
            heads.append(gt_ref[0, base:base + 1, :] * o_c[g][:, cols]
                         + gt_ref[0, base + 1:base + 2, :] * o_s[g][:, cols]
                         + gt_ref[0, base + 2:base + 3, :] * o_w[g][:, cols])
        for pair in range(NSA_GROUP_SIZE // 2):
            outs.append(jnp.concatenate(heads[2 * pair:2 * pair + 2], axis=0).T)
    o_ref[0] = jnp.concatenate(outs, axis=1)


def _nsa_attn(q, gates_t, kc, vct, ks, vst, kw, vwt, ovt):
    b, t, _ = q.shape
    ncp = kc.shape[1]
    ns = ovt.shape[0]
    sel_keys = SEL_TRIP * KEY_BLOCK
    assert t % sel_keys == 0 and sel_keys % Q_TILE == 0 and t >= WINDOW + Q_TILE
    ind = (np.arange(sel_keys)[:, None] // SEL_BLOCK == np.arange(LANES)[None, :])
    ind = jnp.asarray(ind.astype(np.float32), dtype=BF16)
    full_rows = lambda bi, qi: (bi, 0, 0)
    return pl.pallas_call(
        _nsa_kernel,
        grid=(b, t // Q_TILE),
        in_specs=[
            pl.BlockSpec((1, Q_TILE, NSA_WIDTH), lambda bi, qi: (bi, qi, 0)),
            pl.BlockSpec((1, 32, Q_TILE), lambda bi, qi: (bi, 0, qi)),
            pl.BlockSpec((1, ncp, LANES), full_rows),
            pl.BlockSpec((1, LANES, ncp), full_rows),
            pl.BlockSpec((1, t, LANES), full_rows),
            pl.BlockSpec((1, LANES, t), full_rows),
            pl.BlockSpec((1, t, LANES), full_rows),
            pl.BlockSpec((1, LANES, t), full_rows),
            _const_spec((ns, ncp)),
            _const_spec((sel_keys, LANES)),
        ],
        out_specs=pl.BlockSpec((1, Q_TILE, NSA_WIDTH), lambda bi, qi: (bi, qi, 0)),
        out_shape=jax.ShapeDtypeStruct((b, t, NSA_WIDTH), F32),
        scratch_shapes=[pltpu.VMEM((ns, NSA_GROUPS * Q_TILE), F32),
                        pltpu.VMEM((sel_keys, NSA_GROUPS * NSA_GROUP_SIZE * Q_TILE), F32),
                        pltpu.VMEM((sel_keys, NSA_GROUPS * NSA_GROUP_SIZE * Q_TILE), F32),
                        pltpu.VMEM((NSA_GROUPS, HEAD_DIM + 16, NSA_GROUP_SIZE * Q_TILE), F32)],
        compiler_params=_params(("arbitrary", "arbitrary")),
        name="nsa_attn",
    )(q, gates_t, kc, vct, ks, vst, kw, vwt, ovt, ind)


def _same_head_mask():
    bi = lax.broadcasted_iota(jnp.int32, (HALF, HALF), 0) // HEAD_DIM
    bj = lax.broadcasted_iota(jnp.int32, (HALF, HALF), 1) // HEAD_DIM
    return bi == bj


def _bd_rows(x, same_head):
    xb = x.astype(BF16)
    tiled = jnp.concatenate([xb] * (HALF // x.shape[0]), axis=0)
    return jnp.where(same_head, tiled, jnp.zeros((), BF16))


def _rwkv_prep_kernel(r_ref, k_ref, v_ref, lo_ref, w0_ref, w2_ref, a0_ref, a2_ref, g2_ref,
                      kk_ref, ka_ref, rk_ref,
                      wm_ref, zm_ref, arb_ref, rkv_ref, rt_ref, vb_ref, bkt_ref, gl_ref,
                      bonus_ref, gate_ref, *, chunks):
    c = RWKV_CHUNK
    same_head = _same_head_mask()
    ones_bd = same_head.astype(BF16)
    row_i = lax.broadcasted_iota(jnp.int32, (chunks * c, chunks * c), 0)
    col_i = lax.broadcasted_iota(jnp.int32, (chunks * c, chunks * c), 1)
    tril_incl = ((row_i >= col_i) & (row_i // c == col_i // c)).astype(BF16)
    t_id = lax.broadcasted_iota(jnp.int32, (c, HALF), 0)
    j_id = lax.broadcasted_iota(jnp.int32, (c, HALF), 1) % HEAD_DIM
    strict_lower = t_id > j_id
    incl_lower = t_id >= j_id
    eye_all = (t_id == j_id).astype(F32)

    def bd_cols(xt):
        xb = xt.astype(BF16)
        return jnp.where(same_head, jnp.concatenate([xb, xb], axis=1), jnp.zeros((), BF16))

    halves = RWKV_WIDTH // HALF
    chains = [(ch, hh) for ch in range(chunks) for hh in range(halves)]
    rows = [slice(ch * c, (ch + 1) * c) for ch in range(chunks)]
    lanes = [slice(hh * HALF, (hh + 1) * HALF) for hh in range(halves)]
    each = lambda fn: [fn(n, ch, hh) for n, (ch, hh) in enumerate(chains)]

    r = [r_ref[0, rw, :] for rw in rows]
    k = [k_ref[0, rw, :] for rw in rows]
    v = [v_ref[0, rw, :] for rw in rows]
    lo = [lo_ref[0, rw, :] for rw in rows]
    zs = [-(w0_ref[...] + _bdot(jnp.tanh(x[:, 0:DECAY_LORA]), w2_ref[...])) for x in lo]
    lr = [_sigmoid(a0_ref[...] + _bdot(x[:, DECAY_LORA:DECAY_LORA + AAA_LORA], a2_ref[...])) for x in lo]
    gate = [_bdot(_sigmoid(x[:, DECAY_LORA + AAA_LORA:]), g2_ref[...]) for x in lo]
    log_decay = []
    for z in zs:
        softplus = jnp.maximum(z, 0.0) + jnp.log(1.0 + jnp.exp(-jnp.abs(z)))
        log_decay.append(-jnp.exp(-softplus - 0.5))
    cum_all = _dot3_left(tril_incl, jnp.concatenate(log_decay, axis=0))
    cum = [cum_all[rw, :] for rw in rows]
    g_incl = [jnp.exp(x) for x in cum]
    g_excl = [jnp.exp(x - y) for x, y in zip(cum, log_decay)]
    g_inv = [jnp.exp(-x) for x in cum]
    for ch in range(chunks):
        gate_ref[0, rows[ch], :] = gate[ch]
        gl_ref[0, ch * 8:(ch + 1) * 8, :] = jnp.broadcast_to(g_incl[ch][c - 1:c, :], (8, RWKV_WIDTH))
        vb_ref[0, rows[ch], :] = v[ch].astype(BF16)

    kk = each(lambda n, ch, hh: k[ch][:, lanes[hh]] * kk_ref[:, lanes[hh]])
    ssq = each(lambda n, ch, hh: _dot2_right(kk[n] * kk[n], ones_bd))
    kk = each(lambda n, ch, hh: kk[n] / jnp.maximum(jnp.sqrt(ssq[n]), 1e-12))
    k2 = each(lambda n, ch, hh: k[ch][:, lanes[hh]] * (1.0 + (lr[ch][:, lanes[hh]] - 1.0) * ka_ref[:, lanes[hh]]))
    at = each(lambda n, ch, hh: -kk[n] * g_excl[ch][:, lanes[hh]])
    bt = each(lambda n, ch, hh: kk[n] * lr[ch][:, lanes[hh]] * g_inv[ch][:, lanes[hh]])
    kt = each(lambda n, ch, hh: k2[n] * g_inv[ch][:, lanes[hh]])
    rt = each(lambda n, ch, hh: r[ch][:, lanes[hh]] * g_incl[ch][:, lanes[hh]])

    bt_bd = each(lambda n, ch, hh: bd_cols(jnp.concatenate([bt[n], bt[n]], axis=0).T))
    kt_bd = each(lambda n, ch, hh: bd_cols(jnp.concatenate([kt[n], kt[n]], axis=0).T))
    ar = each(lambda n, ch, hh: jnp.concatenate([at[n], rt[n]], axis=0).astype(BF16))
    ab = each(lambda n, ch, hh: jnp.dot(ar[n], bt_bd[n], preferred_element_type=F32))
    ak = each(lambda n, ch, hh: jnp.dot(ar[n], kt_bd[n], preferred_element_type=F32))
    a_ab = [jnp.where(strict_lower, x[0:c], 0.0) for x in ab]
    a_rb = [jnp.where(incl_lower, x[c:], 0.0) for x in ab]
    a_ak = [jnp.where(strict_lower, x[0:c], 0.0) for x in ak]
    a_rk = [jnp.where(incl_lower, x[c:], 0.0) for x in ak]

    inv = [eye_all + x for x in a_ab]
    pw = a_ab
    for _ in range(5):
        pw = [jnp.dot(x.astype(BF16), _bd_rows(x, same_head), preferred_element_type=F32) for x in pw]
        inv = [x + jnp.dot(x.astype(BF16), _bd_rows(y, same_head), preferred_element_type=F32)
               for x, y in zip(inv, pw)]

    v_bd = each(lambda n, ch, hh: _bd_rows(v[ch][:, lanes[hh]], same_head))
    inv_b = [x.astype(BF16) for x in inv]
    wm = each(lambda n, ch, hh: jnp.dot(inv_b[n], _bd_rows(at[n], same_head), preferred_element_type=F32))
    akv = each(lambda n, ch, hh: jnp.dot(a_ak[n].astype(BF16), v_bd[n], preferred_element_type=F32))
    zm = each(lambda n, ch, hh: jnp.dot(inv_b[n], _bd_rows(akv[n], same_head), preferred_element_type=F32))
    rkv = each(lambda n, ch, hh: jnp.dot(a_rk[n].astype(BF16), v_bd[n], preferred_element_type=F32))
    bonus = each(lambda n, ch, hh: _dot2_right(
        r[ch][:, lanes[hh]] * k2[n] * rk_ref[:, lanes[hh]], ones_bd) * v[ch][:, lanes[hh]])
    for n, (ch, hh) in enumerate(chains):
        wm_ref[0, rows[ch], lanes[hh]] = wm[n].astype(BF16)
        zm_ref[0, rows[ch], lanes[hh]] = zm[n]
        rkv_ref[0, rows[ch], lanes[hh]] = rkv[n]
        arb_ref[0, rows[ch], lanes[hh]] = a_rb[n].astype(BF16)
        rt_ref[0, rows[ch], lanes[hh]] = rt[n].astype(BF16)
        bkt_ref[0, n * HALF:(n + 1) * HALF, :] = (
            jnp.concatenate([bt[n], kt[n]], axis=0).T.astype(BF16))
        bonus_ref[0, rows[ch], lanes[hh]] = bonus[n]


def _rwkv_scan_kernel(wm_ref, zm_ref, arb_ref, rkv_ref, rt_ref, vb_ref, bkt_ref, gl_ref,
                      bonus_ref, gate_ref, gng_ref, gnb_ref, o_ref, s_ref):
    @pl.when(pl.program_id(0) == 0)
    def _():
        s_ref[...] = jnp.zeros_like(s_ref)

    same_head = _same_head_mask()
    ones_bd = same_head.astype(BF16)
    chains = [(b, hh) for b in range(o_ref.shape[0]) for hh in range(RWKV_WIDTH // HALF)]
    lanes = [slice(hh * HALF, (hh + 1) * HALF) for hh in range(RWKV_WIDTH // HALF)]
    each = lambda fn: [fn(n, b, lanes[hh]) for n, (b, hh) in enumerate(chains)]
    dot = lambda x, y: jnp.dot(x, y, preferred_element_type=F32)

    s0 = [s_ref[b, hh] for b, hh in chains]
    s0b = [x.astype(BF16) for x in s0]
    u = each(lambda n, b, ln: dot(wm_ref[b, :, ln], s0b[n]) + zm_ref[b, :, ln])
    y0 = each(lambda n, b, ln: dot(rt_ref[b, :, ln], s0b[n]) + rkv_ref[b, :, ln])
    y = each(lambda n, b, ln: y0[n] + dot(arb_ref[b, :, ln], _bd_rows(u[n], same_head)))
    uv = each(lambda n, b, ln: jnp.concatenate([u[n].astype(BF16), vb_ref[b, :, ln]], axis=0))
    upd = [dot(bkt_ref[b, hh * HALF:(hh + 1) * HALF, :], uv[n]) for n, (b, hh) in enumerate(chains)]
    for n, (b, hh) in enumerate(chains):
        g_last = jnp.broadcast_to(gl_ref[b, 0:1, lanes[hh]], (LANES, HALF)).T
        g_col = jnp.concatenate([g_last, g_last], axis=1)
        s_ref[b, hh] = g_col * (s0[n] + jnp.where(same_head, upd[n], 0.0))

    mu = [_dot2_right(x, ones_bd) * (1.0 / HEAD_DIM) for x in y]
    yc = [x - m for x, m in zip(y, mu)]
    var = [_dot2_right(x * x, ones_bd) * (1.0 / HEAD_DIM) for x in yc]
    for n, (b, hh) in enumerate(chains):
        ln = lanes[hh]
        yn = yc[n] * lax.rsqrt(var[n] + RWKV_GN_EPS) * gng_ref[:, ln] + gnb_ref[:, ln]
        o_ref[b, :, ln] = (yn + bonus_ref[b, :, ln]) * gate_ref[b, :, ln]


def _rwkv(r, k, v, lo, w0, w2, a0, a2, g2, k_k, k_a, r_k, gn_g, gn_b, *, chunks=4):
    b, t, width = r.shape
    c = RWKV_CHUNK
    nc = t // c
    halves = width // HALF
    vec = lambda a: a.reshape(1, width)
    cs = lambda a: _const_spec(a.shape)
    row = lambda bi, ci: (bi, ci, 0)
    args = [vec(w0), w2.astype(BF16), vec(a0), a2.astype(BF16), g2.astype(BF16),
            vec(k_k), vec(k_a), vec(r_k)]
    tok = lambda dt: jax.ShapeDtypeStruct((b, t, width), dt)
    tok_spec = pl.BlockSpec((1, chunks * c, width), row)
    prep = pl.pallas_call(
        functools.partial(_rwkv_prep_kernel, chunks=chunks),
        grid=(b, nc // chunks),
        in_specs=[tok_spec] * 3 + [pl.BlockSpec((1, chunks * c, 256), row)] + [cs(a) for a in args],
        out_specs=[tok_spec] * 6
                  + [pl.BlockSpec((1, chunks * halves * HALF, LANES), row),
                     pl.BlockSpec((1, chunks * 8, width), row), tok_spec, tok_spec],
        out_shape=[tok(BF16), tok(F32), tok(BF16), tok(F32), tok(BF16), tok(BF16),
                   jax.ShapeDtypeStruct((b, nc * halves * HALF, LANES), BF16),
                   jax.ShapeDtypeStruct((b, nc * 8, width), F32), tok(F32), tok(F32)],
        compiler_params=_params(("arbitrary", "arbitrary")),
        name="rwkv7_prep",
    )(r, k, v, lo, *args)

    step = lambda ci: (0, ci, 0)
    scan_spec = pl.BlockSpec((b, c, width), step)
    return pl.pallas_call(
        _rwkv_scan_kernel,
        grid=(nc,),
        in_specs=[scan_spec] * 6
                 + [pl.BlockSpec((b, halves * HALF, LANES), step), pl.BlockSpec((b, 8, width), step),
                    scan_spec, scan_spec, cs(vec(gn_g)), cs(vec(gn_b))],
        out_specs=scan_spec,
        out_shape=jax.ShapeDtypeStruct((b, t, width), F32),
        scratch_shapes=[pltpu.VMEM((b, halves, HALF, HALF), F32)],
        compiler_params=_params(("arbitrary",)),
        name="rwkv7_scan",
    )(*prep, vec(gn_g), vec(gn_b))


def _mem_kv_kernel(m_ref, g_ref, wk_ref, wv_ref, kt_ref, v_ref):
    m = _rms(m_ref[0], g_ref[...]).astype(BF16)
    kt_ref[0] = jnp.dot(m, wk_ref[...], preferred_element_type=F32).T.astype(BF16)
    v_ref[0] = jnp.dot(m, wv_ref[...], preferred_element_type=F32).astype(BF16)


def _mem_kv(mem, g, wk, wv):
    b, mt, d = mem.shape
    return pl.pallas_call(
        _mem_kv_kernel,
        grid=(b,),
        in_specs=[pl.BlockSpec((1, mt, d), lambda bi: (bi, 0, 0)), _const_spec((1, d)),
                  _const_spec((d, d)), _const_spec((d, d))],
        out_specs=[pl.BlockSpec((1, d, mt), lambda bi: (bi, 0, 0)),
                   pl.BlockSpec((1, mt, d), lambda bi: (bi, 0, 0))],
        out_shape=[jax.ShapeDtypeStruct((b, d, mt), BF16), jax.ShapeDtypeStruct((b, mt, d), BF16)],
        compiler_params=_params(("arbitrary",)),
        name="mem_kv",
    )(mem, g.reshape(1, d), wk.astype(BF16), wv.astype(BF16))


def _out_mem_kernel(x_ref, on_ref, or_ref, ng_ref, wo1_ref, wo2_ref, mpost_ref,
                    mpre_ref, wq_ref, kt_ref, v_ref, wo_ref, mempost_ref, o_ref):
    x = x_ref[0]
    a = _rms(on_ref[0], ng_ref[...]).astype(BF16)
    mixed = (jnp.dot(a, wo1_ref[...], preferred_element_type=F32)
             + jnp.dot(or_ref[0].astype(BF16), wo2_ref[...], preferred_element_type=F32))
    x = x + _rms(mixed, mpost_ref[...])

    h = _rms(x, mpre_ref[...]).astype(BF16)
    d = x.shape[-1]
    hd = d // MEM_HEADS
    q = jnp.dot(h, wq_ref[...], preferred_element_type=F32) * (hd ** -0.5)
    heads = []
    for hi in range(MEM_HEADS):
        cols = slice(hi * hd, (hi + 1) * hd)
        s = jnp.dot(q[:, cols].astype(BF16), kt_ref[0, cols, :], preferred_element_type=F32)
        e = jnp.exp(s - jnp.max(s, axis=-1, keepdims=True))
        p = e / jnp.sum(e, axis=-1, keepdims=True)
        heads.append(jnp.dot(p.astype(BF16), v_ref[0, :, cols], preferred_element_type=F32))
    o = jnp.concatenate(heads, axis=-1).astype(BF16)
    att = jnp.dot(o, wo_ref[...], preferred_element_type=F32)
    o_ref[0] = x + _rms(att, mempost_ref[...])


def _out_mem(x3, o_nsa, o_rwkv, nsa_g, w_out, mix_post_g, mem_pre_g, wq, kt, vm, wo, mem_post_g,
             *, tm=256):
    b, t, d = x3.shape
    mt = vm.shape[1]
    row = lambda bi, ti: (bi, ti, 0)
    per_b = lambda bi, ti: (bi, 0, 0)
    w_out = w_out.astype(BF16)
    return pl.pallas_call(
        _out_mem_kernel,
        grid=(b, t // tm),
        in_specs=[
            pl.BlockSpec((1, tm, d), row),
            pl.BlockSpec((1, tm, NSA_WIDTH), row),
            pl.BlockSpec((1, tm, RWKV_WIDTH), row),
            _const_spec((1, NSA_WIDTH)),
            _const_spec((NSA_WIDTH, d)),
            _const_spec((RWKV_WIDTH, d)),
            _const_spec((1, d)),
            _const_spec((1, d)),
            _const_spec((d, d)),
            pl.BlockSpec((1, d, mt), per_b),
            pl.BlockSpec((1, mt, d), per_b),
            _const_spec((d, d)),
            _const_spec((1, d)),
        ],
        out_specs=pl.BlockSpec((1, tm, d), row),
        out_shape=jax.ShapeDtypeStruct((b, t, d), F32),
        compiler_params=_params(("arbitrary", "arbitrary")),
        name="out_mem",
    )(x3, o_nsa, o_rwkv, nsa_g.reshape(1, -1), w_out[:NSA_WIDTH], w_out[NSA_WIDTH:],
      mix_post_g.reshape(1, d), mem_pre_g.reshape(1, d), wq.astype(BF16), kt, vm,
      wo.astype(BF16), mem_post_g.reshape(1, d))


def _rope_tables(pos):
    half = HEAD_DIM // 2
    inv = ROPE_THETA ** (-jnp.arange(half, dtype=F32) / half)
    ang = pos.astype(F32)[:, None] * inv[None, :]
    cos, sin = jnp.cos(ang), jnp.sin(ang)
    cos_t = jnp.concatenate([cos, cos, cos, cos], axis=-1)
    sin_t = jnp.concatenate([-sin, sin, -sin, sin], axis=-1)
    return cos_t, sin_t


def _overlap_t(ns, ncp):
    c0 = np.arange(ncp)[None, :] * CMP_STRIDE
    s0 = np.arange(ns)[:, None] * SEL_BLOCK
    ov = (c0 < s0 + SEL_BLOCK) & (c0 + CMP_BLOCK > s0) & (np.arange(ncp)[None, :] < ncp - 1)
    return jnp.asarray(ov.astype(np.float32), dtype=BF16)


def _pad_cols(w, n):
    return jnp.pad(w, ((0, 0), (0, n - w.shape[1])))


def _cmp_weights(pe, w1, w2):
    per = CMP_STRIDE
    w1r = w1.reshape(CMP_BLOCK, HEAD_DIM, CMP_HIDDEN)
    blocks = []
    for part in range(CMP_BLOCK // per):
        for g in range(NSA_GROUPS):
            z = jnp.zeros((per, NSA_GROUPS, HEAD_DIM, CMP_HIDDEN), F32)
            z = z.at[:, g].set(w1r[part * per:(part + 1) * per])
            blocks.append(z.reshape(per * NSA_GROUPS * HEAD_DIM, CMP_HIDDEN))
    w1cat = jnp.concatenate(blocks, axis=1).astype(BF16)
    w2bd = jnp.zeros((NSA_GROUPS * CMP_HIDDEN, NSA_GROUPS * HEAD_DIM), F32)
    for g in range(NSA_GROUPS):
        w2bd = w2bd.at[g * CMP_HIDDEN:(g + 1) * CMP_HIDDEN, g * HEAD_DIM:(g + 1) * HEAD_DIM].set(w2)
    pe8 = jnp.broadcast_to(pe.reshape(1, CMP_BLOCK * HEAD_DIM), (8, CMP_BLOCK * HEAD_DIM))
    return pe8.astype(BF16), w1.astype(BF16), w1cat, w2bd.astype(BF16)


def kernel(x, mem, ffn1_pre_g, ffn1_w_gate, ffn1_w_up, ffn1_w_down, ffn1_post_g, mix_pre_g, w_in, cmp_pe_k, cmp_w1_k, cmp_w2_k, cmp_pe_v, cmp_w1_v, cmp_w2_v, nsa_out_g, rwkv_mu, rwkv_w0, rwkv_w2, rwkv_a0, rwkv_a2, rwkv_g2, rwkv_k_k, rwkv_k_a, rwkv_r_k, rwkv_gn_g, rwkv_gn_b, w_out, mix_post_g, mem_pre_g, mem_kv_g, mem_wq, mem_wk, mem_wv, mem_wo, mem_post_g, ffn2_pre_g, ffn2_w_gate, ffn2_w_up, ffn2_w_down, ffn2_post_g):
    b, t, d = x.shape
    ncp = t // CMP_STRIDE
    ns = t // SEL_BLOCK
    cos_t, sin_t = _rope_tables(jnp.arange(t))
    cos_c, sin_c = _rope_tables(jnp.arange(ncp) * CMP_STRIDE + (CMP_BLOCK - 1))
    ovt = _overlap_t(ns, ncp)

    for l in range(ffn1_pre_g.shape[0]):
        x2 = _ffn_block(x.reshape(b * t, d), ffn1_pre_g[l], ffn1_w_gate[l], ffn1_w_up[l],
                        ffn1_w_down[l], ffn1_post_g[l])
        x3 = x2.reshape(b, t, d)

        wi = w_in[l]
        nsa_w = NSA_WIDTH + 6 * NSA_KV_WIDTH
        gate_w = wi[:, nsa_w:nsa_w + 3 * NSA_HEADS]
        w_cols = jnp.concatenate([wi[:, :nsa_w], _pad_cols(gate_w, LANES),
                                  wi[:, nsa_w + 3 * NSA_HEADS:]], axis=1).astype(BF16)
        (q, k_cmp, v_cmp, k_slc, v_slc_t, k_win, v_win_t, gates_t, r, k, v, lo) = _in_proj(
            x3, mix_pre_g[l], w_cols, rwkv_mu[l].reshape(1, -1), cos_t, sin_t)

        pek, w1k, w1kc, w2k = _cmp_weights(cmp_pe_k[l], cmp_w1_k[l], cmp_w2_k[l])
        pev, w1v, w1vc, w2v = _cmp_weights(cmp_pe_v[l], cmp_w1_v[l], cmp_w2_v[l])
        row_w = CMP_STRIDE * NSA_KV_WIDTH
        kc, vct = _compress(k_cmp.reshape(b, ncp, row_w), v_cmp.reshape(b, ncp, row_w),
                            pek, w1k, w1kc, w2k, pev, w1v, w1vc, w2v, cos_c, sin_c)
        o_nsa = _nsa_attn(q, gates_t, kc, vct, k_slc, v_slc_t, k_win, v_win_t, ovt)

        o_rwkv = _rwkv(r, k, v, lo, rwkv_w0[l], rwkv_w2[l], rwkv_a0[l], rwkv_a2[l], rwkv_g2[l],
                       rwkv_k_k[l], rwkv_k_a[l], rwkv_r_k[l], rwkv_gn_g[l], rwkv_gn_b[l])

        kt, vm = _mem_kv(mem, mem_kv_g[l], mem_wk[l], mem_wv[l])
        x4 = _out_mem(x3, o_nsa, o_rwkv, nsa_out_g[l], w_out[l], mix_post_g[l], mem_pre_g[l],
                      mem_wq[l], kt, vm, mem_wo[l], mem_post_g[l])

        x = _ffn_block(x4.reshape(b * t, d), ffn2_pre_g[l], ffn2_w_gate[l], ffn2_w_up[l],
                       ffn2_w_down[l], ffn2_post_g[l]).reshape(b, t, d)
    return x
```

```python
import functools

import numpy as np
import jax
import jax.numpy as jnp
from jax import lax
from jax.experimental import pallas as pl
from jax.experimental.pallas import tpu as pltpu

F32 = jnp.float32
BF16 = jnp.bfloat16

HEAD_DIM = 64
NSA_HEADS = 8
NSA_GROUPS = 2
NSA_GROUP_SIZE = 4
NSA_WIDTH = 512
NSA_KV_WIDTH = 128
CMP_BLOCK = 32
CMP_STRIDE = 16
CMP_HIDDEN = 256
SEL_BLOCK = 64
SEL_TOPK = 16
SEL_FORCE = 1e4
SEL_FORCED = 3
WINDOW = 512
RWKV_HEADS = 8
RWKV_WIDTH = 512
DECAY_LORA = 64
AAA_LORA = 64
GATE_LORA = 128
RWKV_GN_EPS = 64e-5
MEM_HEADS = 4
ROPE_THETA = 10000.0
NORM_EPS = 1e-6
NEG_INF = -1e30
LOG2_E = 1.4426950408889634

LANES = 128
KEY_BLOCK = 128
Q_TILE = 128
SEL_TRIP = 4
RWKV_CHUNK = 64
HALF = 256
VMEM_LIMIT = 56 * 1024 * 1024


def _bdot(a, b):
    return jnp.dot(a.astype(BF16), b.astype(BF16), preferred_element_type=F32)


def _split3(x):
    h1 = x.astype(BF16)
    r1 = x - h1.astype(F32)
    h2 = r1.astype(BF16)
    r2 = r1 - h2.astype(F32)
    return h1, h2, r2.astype(BF16)


def _split2(x):
    h1 = x.astype(BF16)
    return h1, (x - h1.astype(F32)).astype(BF16)


def _dot2_right(x, m):
    h1, h2 = _split2(x)
    return (jnp.dot(h1, m, preferred_element_type=F32) + jnp.dot(h2, m, preferred_element_type=F32))


def _dot3_right(x, m):
    h1, h2, h3 = _split3(x)
    d = lambda h: jnp.dot(h, m, preferred_element_type=F32)
    return d(h1) + d(h2) + d(h3)


def _dot2_left(m, x):
    h1, h2 = _split2(x)
    return (jnp.dot(m, h1, preferred_element_type=F32) + jnp.dot(m, h2, preferred_element_type=F32))


def _dot3_left(m, x):
    h1, h2, h3 = _split3(x)
    d = lambda h: jnp.dot(m, h, preferred_element_type=F32)
    return d(h1) + d(h2) + d(h3)


def _rms(x, g):
    return x * lax.rsqrt(jnp.mean(x * x, axis=-1, keepdims=True) + NORM_EPS) * g


def _silu(x):
    return x / (1.0 + jnp.exp(-x))


def _sigmoid(x):
    return 1.0 / (1.0 + jnp.exp(-x))


def _const_spec(shape):
    nd = len(shape)
    return pl.BlockSpec(shape, lambda *_: (0,) * nd)


def _params(sem):
    return pltpu.CompilerParams(dimension_semantics=sem, vmem_limit_bytes=VMEM_LIMIT)


def _ffn_kernel(x_ref, pre_ref, wg_ref, wu_ref, wd_ref, post_ref, o_ref, *, ff_chunk):
    x = x_ref[...]
    h = _rms(x, pre_ref[...]).astype(BF16)
    d_ff = wg_ref.shape[1]
    acc = jnp.zeros(x.shape, F32)
    for c0 in range(0, d_ff, ff_chunk):
        g = jnp.dot(h, wg_ref[:, c0:c0 + ff_chunk], preferred_element_type=F32)
        u = jnp.dot(h, wu_ref[:, c0:c0 + ff_chunk], preferred_element_type=F32)
        a = (_silu(g) * u).astype(BF16)
        acc = acc + jnp.dot(a, wd_ref[c0:c0 + ff_chunk, :], preferred_element_type=F32)
    o_ref[...] = x + 0.5 * _rms(acc, post_ref[...])


def _ffn_block(x2, pre_g, wg, wu, wd, post_g, *, tm=512, ff_chunk=256):
    m, d = x2.shape
    d_ff = wg.shape[1]
    return pl.pallas_call(
        functools.partial(_ffn_kernel, ff_chunk=ff_chunk),
        grid=(m // tm,),
        in_specs=[
            pl.BlockSpec((tm, d), lambda i: (i, 0)),
            _const_spec((1, d)),
            _const_spec((d, d_ff)),
            _const_spec((d, d_ff)),
            _const_spec((d_ff, d)),
            _const_spec((1, d)),
        ],
        out_specs=pl.BlockSpec((tm, d), lambda i: (i, 0)),
        out_shape=jax.ShapeDtypeStruct((m, d), F32),
        compiler_params=_params(("arbitrary",)),
        name="ffn_block",
    )(x2, pre_g.reshape(1, d), wg.astype(BF16), wu.astype(BF16), wd.astype(BF16),
      post_g.reshape(1, d))


def _swap_halves(x):
    n = x.shape[-1]
    lane = lax.broadcasted_iota(jnp.int32, x.shape, x.ndim - 1)
    fwd = pltpu.roll(x, n - HEAD_DIM // 2, x.ndim - 1)
    bwd = pltpu.roll(x, HEAD_DIM // 2, x.ndim - 1)
    return jnp.where((lane % HEAD_DIM) < HEAD_DIM // 2, fwd, bwd)


def _rope(x, cos, sin_signed):
    reps = x.shape[-1] // LANES
    c = jnp.concatenate([cos] * reps, axis=-1) if reps > 1 else cos
    s = jnp.concatenate([sin_signed] * reps, axis=-1) if reps > 1 else sin_signed
    return x * c + _swap_halves(x) * s


def _in_proj_kernel(x_ref, g_ref, w_ref, mu_ref, cos_ref, sin_ref,
                    q_ref, kc_ref, vc_ref, ks_ref, vst_ref, kw_ref, vwt_ref, gt_ref,
                    r_ref, k_ref, v_ref, lo_ref, carry_ref):
    @pl.when(pl.program_id(1) == 0)
    def _():
        carry_ref[...] = jnp.zeros_like(carry_ref)

    h = _rms(x_ref[0], g_ref[...]).astype(BF16)
    p = jnp.dot(h, w_ref[...], preferred_element_type=F32)
    cos = cos_ref[...]
    sin = sin_ref[...]
    tm = p.shape[0]

    o = 0
    q = _rope(p[:, o:o + NSA_WIDTH], cos, sin) * (HEAD_DIM ** -0.5 * LOG2_E)
    q_ref[0] = q.astype(BF16)
    o += NSA_WIDTH
    kc_ref[0] = p[:, o:o + LANES].astype(BF16); o += LANES
    vc_ref[0] = p[:, o:o + LANES].astype(BF16); o += LANES
    ks_ref[0] = _rope(p[:, o:o + LANES], cos, sin).astype(BF16); o += LANES
    vst_ref[0] = p[:, o:o + LANES].T.astype(BF16); o += LANES
    kw_ref[0] = _rope(p[:, o:o + LANES], cos, sin).astype(BF16); o += LANES
    vwt_ref[0] = p[:, o:o + LANES].T.astype(BF16); o += LANES
    gates_t = _sigmoid(p[:, o:o + LANES]).T
    gt_ref[0] = gates_t[:gt_ref.shape[1], :]
    o += LANES

    rw = p[:, o:]
    row = lax.broadcasted_iota(jnp.int32, rw.shape, 0)
    prev = jnp.where(row == 0, carry_ref[0:1, :], pltpu.roll(rw, 1, 0))
    carry_ref[...] = jnp.broadcast_to(rw[tm - 1:tm, :], carry_ref.shape)
    mixed = rw + (prev - rw) * mu_ref[...]
    r_ref[0] = mixed[:, 0:512]
    k_ref[0] = mixed[:, 512:1024]
    v_ref[0] = mixed[:, 1024:1536]
    lo_ref[0] = mixed[:, 1536:1792]


def _in_proj(x3, g, w_cols, mu_cols, cos_t, sin_t, *, tm=256):
    b, t, d = x3.shape
    n = w_cols.shape[1]
    row = lambda bi, ti: (bi, ti, 0)
    col = lambda bi, ti: (bi, 0, ti)
    out_shapes = [
        jax.ShapeDtypeStruct((b, t, NSA_WIDTH), BF16),
        jax.ShapeDtypeStruct((b, t, LANES), BF16),
        jax.ShapeDtypeStruct((b, t, LANES), BF16),
        jax.ShapeDtypeStruct((b, t, LANES), BF16),
        jax.ShapeDtypeStruct((b, LANES, t), BF16),
        jax.ShapeDtypeStruct((b, t, LANES), BF16),
        jax.ShapeDtypeStruct((b, LANES, t), BF16),
        jax.ShapeDtypeStruct((b, 32, t), F32),
        jax.ShapeDtypeStruct((b, t, RWKV_WIDTH), F32),
        jax.ShapeDtypeStruct((b, t, RWKV_WIDTH), F32),
        jax.ShapeDtypeStruct((b, t, RWKV_WIDTH), F32),
        jax.ShapeDtypeStruct((b, t, 256), F32),
    ]
    out_specs = [
        pl.BlockSpec((1, tm, NSA_WIDTH), row),
        pl.BlockSpec((1, tm, LANES), row),
        pl.BlockSpec((1, tm, LANES), row),
        pl.BlockSpec((1, tm, LANES), row),
        pl.BlockSpec((1, LANES, tm), col),
        pl.BlockSpec((1, tm, LANES), row),
        pl.BlockSpec((1, LANES, tm), col),
        pl.BlockSpec((1, 32, tm), col),
        pl.BlockSpec((1, tm, RWKV_WIDTH), row),
        pl.BlockSpec((1, tm, RWKV_WIDTH), row),
        pl.BlockSpec((1, tm, RWKV_WIDTH), row),
        pl.BlockSpec((1, tm, 256), row),
    ]
    return pl.pallas_call(
        _in_proj_kernel,
        grid=(b, t // tm),
        in_specs=[
            pl.BlockSpec((1, tm, d), row),
            _const_spec((1, d)),
            _const_spec((d, n)),
            _const_spec((1, 1792)),
            pl.BlockSpec((tm, LANES), lambda bi, ti: (ti, 0)),
            pl.BlockSpec((tm, LANES), lambda bi, ti: (ti, 0)),
        ],
        out_specs=out_specs,
        out_shape=out_shapes,
        scratch_shapes=[pltpu.VMEM((8, 1792), F32)],
        compiler_params=_params(("arbitrary", "arbitrary")),
        name="in_proj",
    )(x3, g.reshape(1, d), w_cols, mu_cols, cos_t, sin_t)


def _compress_kernel(kin_ref, vin_ref, pek_ref, w1k_ref, w1kc_ref, w2k_ref,
                     pev_ref, w1v_ref, w1vc_ref, w2v_ref, cos_ref, sin_ref,
                     kc_ref, vct_ref):
    def phi(rows, pe, w1, w1cat, w2bd):
        n = rows.shape[0]
        pr = jnp.dot(rows, w1cat, preferred_element_type=F32)
        bias = jnp.dot(pe, w1, preferred_element_type=F32)[0:1, :]
        hid = []
        for g in range(NSA_GROUPS):
            top = pr[:, g * CMP_HIDDEN:(g + 1) * CMP_HIDDEN]
            bot = pr[:, (NSA_GROUPS + g) * CMP_HIDDEN:(NSA_GROUPS + g + 1) * CMP_HIDDEN]
            hid.append(top + pltpu.roll(bot, n - 1, 0) + bias)
        act = _silu(jnp.concatenate(hid, axis=-1)).astype(BF16)
        return jnp.dot(act, w2bd, preferred_element_type=F32)

    kc = phi(kin_ref[0], pek_ref[...], w1k_ref[...], w1kc_ref[...], w2k_ref[...])
    kc_ref[0] = _rope(kc, cos_ref[...], sin_ref[...]).astype(BF16)
    vc = phi(vin_ref[0], pev_ref[...], w1v_ref[...], w1vc_ref[...], w2v_ref[...])
    vct_ref[0] = vc.T.astype(BF16)


def _compress(kin, vin, pek, w1k, w1kc, w2k, pev, w1v, w1vc, w2v, cos_c, sin_c):
    b, ncp, width = kin.shape
    blk = pl.BlockSpec((1, ncp, width), lambda bi: (bi, 0, 0))
    cs = lambda a: _const_spec(a.shape)
    return pl.pallas_call(
        _compress_kernel,
        grid=(b,),
        in_specs=[blk, blk, cs(pek), cs(w1k), cs(w1kc), cs(w2k),
                  cs(pev), cs(w1v), cs(w1vc), cs(w2v), cs(cos_c), cs(sin_c)],
        out_specs=[pl.BlockSpec((1, ncp, LANES), lambda bi: (bi, 0, 0)),
                   pl.BlockSpec((1, LANES, ncp), lambda bi: (bi, 0, 0))],
        out_shape=[jax.ShapeDtypeStruct((b, ncp, LANES), BF16),
                   jax.ShapeDtypeStruct((b, LANES, ncp), BF16)],
        compiler_params=_params(("arbitrary",)),
        name="nsa_compress",
    )(kin, vin, pek, w1k, w1kc, w2k, pev, w1v, w1vc, w2v, cos_c, sin_c)


def _nsa_kernel(q_ref, gt_ref, kc_ref, vct_ref, ks_ref, vst_ref, kw_ref, vwt_ref, ovt_ref, ind_ref,
                o_ref, bias_ref, sa_ref, sb_ref, acc_ref):
    i = pl.program_id(1)
    q0 = i * Q_TILE
    ncp = kc_ref.shape[1]
    ns = ovt_ref.shape[0]
    gw = NSA_GROUP_SIZE * Q_TILE
    width = NSA_GROUPS * gw
    lanes = [slice(g * gw, (g + 1) * gw) for g in range(NSA_GROUPS)]
    feat = [slice(g * HEAD_DIM, (g + 1) * HEAD_DIM) for g in range(NSA_GROUPS)]
    t_row = q0 + lax.broadcasted_iota(jnp.int32, (1, width), 1) % Q_TILE
    sel_keys = SEL_TRIP * KEY_BLOCK
    sel_rows = sel_keys // SEL_BLOCK
    win_keys = WINDOW + Q_TILE
    dot = lambda x, y: jnp.dot(x, y, preferred_element_type=F32)

    qf = q_ref[0].astype(F32)
    zeros_half = jnp.zeros((HEAD_DIM, Q_TILE), F32)
    parts = []
    for g in range(NSA_GROUPS):
        for pair in range(NSA_GROUP_SIZE // 2):
            slab_t = qf[:, (2 * g + pair) * LANES:(2 * g + pair + 1) * LANES].T
            for half in range(2):
                f = slab_t[half * HEAD_DIM:(half + 1) * HEAD_DIM, :]
                parts.append(jnp.concatenate([f, zeros_half] if g == 0 else [zeros_half, f], axis=0))
    qt = jnp.concatenate(parts, axis=1).astype(BF16)

    def v_ext(vt_ref, g, k0, n):
        return jnp.concatenate([vt_ref[0, feat[g], pl.ds(k0, n)], jnp.ones((16, n), BF16)], axis=0)

    sw_ = NSA_GROUPS * Q_TILE
    n_vis = (q0 + Q_TILE - CMP_BLOCK) // CMP_STRIDE + 1
    blocks_needed = jnp.maximum(n_vis + KEY_BLOCK - 1, KEY_BLOCK) // KEY_BLOCK
    def compressed(nb):
        rows = nb * KEY_BLOCK
        sc = dot(kc_ref[0, 0:rows, :], qt)
        c_end = lax.broadcasted_iota(jnp.int32, (rows, width), 0) * CMP_STRIDE + (CMP_BLOCK - 1)
        c_mask = c_end <= t_row
        sc = jnp.where(c_mask, sc, NEG_INF)
        m_c = jnp.max(sc, axis=0, keepdims=True)
        e_c = jnp.where(c_mask, jnp.exp2(sc - m_c), 0.0)
        p_c = e_c / jnp.maximum(jnp.sum(e_c, axis=0, keepdims=True), 1e-30)
        outs, p_sum = [], []
        for g in range(NSA_GROUPS):
            outs.append(dot(vct_ref[0, feat[g], 0:rows], p_c[:, lanes[g]].astype(BF16)))
            acc = p_c[:, g * gw:g * gw + Q_TILE]
            for r in range(1, NSA_GROUP_SIZE):
                acc = acc + p_c[:, g * gw + r * Q_TILE:g * gw + (r + 1) * Q_TILE]
            p_sum.append(acc)
        return outs + [_dot2_left(ovt_ref[:, 0:rows], jnp.concatenate(p_sum, axis=1))]

    *o_c, imp = lax.switch(blocks_needed - 1,
                           [functools.partial(compressed, nb) for nb in range(1, ncp // KEY_BLOCK + 1)])

    w0 = pl.multiple_of(jnp.maximum(q0 - WINDOW, 0), KEY_BLOCK)
    sw = dot(kw_ref[0, pl.ds(w0, win_keys), :], qt)
    t_loc = t_row - w0
    k_loc = lax.broadcasted_iota(jnp.int32, (win_keys, width), 0)
    ok = k_loc <= t_loc
    old = lax.broadcasted_iota(jnp.int32, (KEY_BLOCK, width), 0) <= t_loc - WINDOW
    sw = jnp.where(ok, sw, NEG_INF)
    sw = jnp.concatenate([jnp.where(old, NEG_INF, sw[0:KEY_BLOCK]), sw[KEY_BLOCK:]], axis=0)
    m_w = jnp.max(sw, axis=0, keepdims=True)
    o_w = []
    for g in range(NSA_GROUPS):
        p = jnp.exp2(sw[:, lanes[g]] - m_w[:, lanes[g]]).astype(BF16)
        pv = dot(v_ext(vwt_ref, g, w0, win_keys), p)
        o_w.append(pv[0:HEAD_DIM, :] / pv[HEAD_DIM:HEAD_DIM + 1, :])

    s_id = lax.broadcasted_iota(jnp.int32, (ns, sw_), 0)
    cur = t_row[:, 0:sw_] // SEL_BLOCK
    forced = (s_id == 0) | (s_id == cur) | (s_id == cur - 1)
    score = jnp.where(forced, -3e38, jnp.where(s_id <= cur, imp, -SEL_FORCE))
    bias = jnp.where(forced, 0.0, NEG_INF)
    for _ in range(min(SEL_TOPK, ns) - SEL_FORCED):
        mx = jnp.max(score, axis=0, keepdims=True)
        first = jnp.min(jnp.where(score == mx, s_id, ns), axis=0, keepdims=True)
        hit = s_id == first
        score = jnp.where(hit, -3e38, score)
        bias = jnp.where(hit, 0.0, bias)
    bias_ref[...] = bias

    def scores(j, s_ref):
        k0 = pl.multiple_of(j * sel_keys, sel_keys)
        b0 = pl.multiple_of(j * sel_rows, sel_rows)
        rows = bias_ref[pl.ds(b0, sel_rows), :]
        rows = jnp.concatenate([rows[:, g * Q_TILE:(g + 1) * Q_TILE]
                                for g in range(NSA_GROUPS) for _ in range(NSA_GROUP_SIZE)], axis=1)
        rows = jnp.concatenate([rows, jnp.zeros_like(rows)], axis=0).astype(BF16)
        rhs = jnp.concatenate([qt, rows, jnp.zeros((LANES - rows.shape[0], width), BF16)], axis=0)
        lhs = jnp.concatenate([ks_ref[0, pl.ds(k0, sel_keys), :], ind_ref[...]], axis=1)
        s = dot(lhs, rhs)
        s_ref[...] = s
        return jnp.max(s, axis=0, keepdims=True)

    def softmax_pv(j, s_ref, mb, m_run):
        k0 = pl.multiple_of(j * sel_keys, sel_keys)
        m_new = jnp.maximum(m_run, mb)
        alpha = jnp.exp2(m_run - m_new)
        for g in range(NSA_GROUPS):
            p = jnp.exp2(s_ref[:, lanes[g]] - m_new[:, lanes[g]]).astype(BF16)
            pv = dot(v_ext(vst_ref, g, k0, sel_keys), p)
            acc_ref[g] = acc_ref[g] * alpha[:, lanes[g]] + pv
        return m_new

    def causal_tail(j, s_ref, m_run):
        k0 = pl.multiple_of(j * sel_keys, sel_keys)
        key = k0 + lax.broadcasted_iota(jnp.int32, (sel_keys, width), 0)
        s = jnp.where(key <= t_row, s_ref[...], NEG_INF)
        s_ref[...] = s
        softmax_pv(j, s_ref, jnp.max(s, axis=0, keepdims=True), m_run)

    acc_ref[...] = jnp.zeros_like(acc_ref)
    n_full = q0 // sel_keys
    n_pairs = n_full // 2

    def pair_body(kk, carry):
        m_run, mb_a = carry
        j = 2 * kk
        mb_b = scores(j + 1, sb_ref)
        m_run = softmax_pv(j, sa_ref, mb_a, m_run)
        mb_a = scores(j + 2, sa_ref)
        m_run = softmax_pv(j + 1, sb_ref, mb_b, m_run)
        return m_run, mb_a

    m_run, mb_a = lax.fori_loop(0, n_pairs, pair_body,
                                (jnp.full((1, width), NEG_INF, F32), scores(0, sa_ref)))
    j_last = 2 * n_pairs

    @pl.when(n_full % 2 == 1)
    def _():
        scores(j_last + 1, sb_ref)
        causal_tail(j_last + 1, sb_ref, softmax_pv(j_last, sa_ref, mb_a, m_run))

    @pl.when(n_full % 2 == 0)
    def _():
        causal_tail(j_last, sa_ref, m_run)

    o_s = [acc_ref[g, 0:HEAD_DIM, :] / acc_ref[g, HEAD_DIM:HEAD_DIM + 1, :] for g in range(NSA_GROUPS)]

    outs = []
    for g in range(NSA_GROUPS):
        heads = []
        for r in range(NSA_GROUP_SIZE):
            base = (g * NSA_GROUP_SIZE + r) * 3
            cols = slice(r * Q_TILE, (r + 1) * Q_TILE)
            heads.append(gt_ref[0, base:base + 1, :] * o_c[g][:, cols]
                         + gt_ref[0, base + 1:base + 2, :] * o_s[g][:, cols]
                         + gt_ref[0, base + 2:base + 3, :] * o_w[g][:, cols])
        for pair in range(NSA_GROUP_SIZE // 2):
            outs.append(jnp.concatenate(heads[2 * pair:2 * pair + 2], axis=0).T)
    o_ref[0] = jnp.concatenate(outs, axis=1)


def _nsa_attn(q, gates_t, kc, vct, ks, vst, kw, vwt, ovt):
    b, t, _ = q.shape
    ncp = kc.shape[1]
    ns = ovt.shape[0]
    sel_keys = SEL_TRIP * KEY_BLOCK
    assert t % sel_keys == 0 and sel_keys % Q_TILE == 0 and t >= WINDOW + Q_TILE and ncp % KEY_BLOCK == 0
    ind = (np.arange(sel_keys)[:, None] // SEL_BLOCK == np.arange(LANES)[None, :])
    ind = jnp.asarray(ind.astype(np.float32), dtype=BF16)
    full_rows = lambda bi, qi: (bi, 0, 0)
    return pl.pallas_call(
        _nsa_kernel,
        grid=(b, t // Q_TILE),
        in_specs=[
            pl.BlockSpec((1, Q_TILE, NSA_WIDTH), lambda bi, qi: (bi, qi, 0)),
            pl.BlockSpec((1, 32, Q_TILE), lambda bi, qi: (bi, 0, qi)),
            pl.BlockSpec((1, ncp, LANES), full_rows),
            pl.BlockSpec((1, LANES, ncp), full_rows),
            pl.BlockSpec((1, t, LANES), full_rows),
            pl.BlockSpec((1, LANES, t), full_rows),
            pl.BlockSpec((1, t, LANES), full_rows),
            pl.BlockSpec((1, LANES, t), full_rows),
            _const_spec((ns, ncp)),
            _const_spec((sel_keys, LANES)),
        ],
        out_specs=pl.BlockSpec((1, Q_TILE, NSA_WIDTH), lambda bi, qi: (bi, qi, 0)),
        out_shape=jax.ShapeDtypeStruct((b, t, NSA_WIDTH), F32),
        scratch_shapes=[pltpu.VMEM((ns, NSA_GROUPS * Q_TILE), F32),
                        pltpu.VMEM((sel_keys, NSA_GROUPS * NSA_GROUP_SIZE * Q_TILE), F32),
                        pltpu.VMEM((sel_keys, NSA_GROUPS * NSA_GROUP_SIZE * Q_TILE), F32),
                        pltpu.VMEM((NSA_GROUPS, HEAD_DIM + 16, NSA_GROUP_SIZE * Q_TILE), F32)],
        compiler_params=_params(("arbitrary", "arbitrary")),
        name="nsa_attn",
    )(q, gates_t, kc, vct, ks, vst, kw, vwt, ovt, ind)


def _same_head_mask():
    bi = lax.broadcasted_iota(jnp.int32, (HALF, HALF), 0) // HEAD_DIM
    bj = lax.broadcasted_iota(jnp.int32, (HALF, HALF), 1) // HEAD_DIM
    return bi == bj


def _bd_rows(x, same_head):
    xb = x.astype(BF16)
    tiled = jnp.concatenate([xb] * (HALF // x.shape[0]), axis=0)
    return jnp.where(same_head, tiled, jnp.zeros((), BF16))


def _rwkv_prep_kernel(r_ref, k_ref, v_ref, lo_ref, w0_ref, w2_ref, a0_ref, a2_ref, g2_ref,
                      kk_ref, ka_ref, rk_ref,
                      wm_ref, zm_ref, arb_ref, rkv_ref, rt_ref, vb_ref, bkt_ref, gl_ref,
                      bonus_ref, gate_ref, *, chunks):
    c = RWKV_CHUNK
    same_head = _same_head_mask()
    ones_bd = same_head.astype(BF16)
    row_i = lax.broadcasted_iota(jnp.int32, (chunks * c, chunks * c), 0)
    col_i = lax.broadcasted_iota(jnp.int32, (chunks * c, chunks * c), 1)
    tril_incl = ((row_i >= col_i) & (row_i // c == col_i // c)).astype(BF16)
    t_id = lax.broadcasted_iota(jnp.int32, (c, HALF), 0)
    j_id = lax.broadcasted_iota(jnp.int32, (c, HALF), 1) % HEAD_DIM
    strict_lower = t_id > j_id
    incl_lower = t_id >= j_id
    eye_all = (t_id == j_id).astype(F32)

    def bd_cols(xt):
        xb = xt.astype(BF16)
        return jnp.where(same_head, jnp.concatenate([xb, xb], axis=1), jnp.zeros((), BF16))

    halves = RWKV_WIDTH // HALF
    chains = [(ch, hh) for ch in range(chunks) for hh in range(halves)]
    rows = [slice(ch * c, (ch + 1) * c) for ch in range(chunks)]
    lanes = [slice(hh * HALF, (hh + 1) * HALF) for hh in range(halves)]
    each = lambda fn: [fn(n, ch, hh) for n, (ch, hh) in enumerate(chains)]

    r = [r_ref[0, rw, :] for rw in rows]
    k = [k_ref[0, rw, :] for rw in rows]
    v = [v_ref[0, rw, :] for rw in rows]
    lo = [lo_ref[0, rw, :] for rw in rows]
    zs = [-(w0_ref[...] + _bdot(jnp.tanh(x[:, 0:DECAY_LORA]), w2_ref[...])) for x in lo]
    lr = [_sigmoid(a0_ref[...] + _bdot(x[:, DECAY_LORA:DECAY_LORA + AAA_LORA], a2_ref[...])) for x in lo]
    gate = [_bdot(_sigmoid(x[:, DECAY_LORA + AAA_LORA:]), g2_ref[...]) for x in lo]
    log_decay = []
    for z in zs:
        softplus = jnp.maximum(z, 0.0) + jnp.log(1.0 + jnp.exp(-jnp.abs(z)))
        log_decay.append(-jnp.exp(-softplus - 0.5))
    cum_all = _dot3_left(tril_incl, jnp.concatenate(log_decay, axis=0))
    cum = [cum_all[rw, :] for rw in rows]
    g_incl = [jnp.exp(x) for x in cum]
    g_excl = [jnp.exp(x - y) for x, y in zip(cum, log_decay)]
    g_inv = [jnp.exp(-x) for x in cum]
    for ch in range(chunks):
        gate_ref[0, rows[ch], :] = gate[ch]
        gl_ref[0, ch * 8:(ch + 1) * 8, :] = jnp.broadcast_to(g_incl[ch][c - 1:c, :], (8, RWKV_WIDTH))
        vb_ref[0, rows[ch], :] = v[ch].astype(BF16)

    kk = each(lambda n, ch, hh: k[ch][:, lanes[hh]] * kk_ref[:, lanes[hh]])
    ssq = each(lambda n, ch, hh: _dot2_right(kk[n] * kk[n], ones_bd))
    kk = each(lambda n, ch, hh: kk[n] / jnp.maximum(jnp.sqrt(ssq[n]), 1e-12))
    k2 = each(lambda n, ch, hh: k[ch][:, lanes[hh]] * (1.0 + (lr[ch][:, lanes[hh]] - 1.0) * ka_ref[:, lanes[hh]]))
    at = each(lambda n, ch, hh: -kk[n] * g_excl[ch][:, lanes[hh]])
    bt = each(lambda n, ch, hh: kk[n] * lr[ch][:, lanes[hh]] * g_inv[ch][:, lanes[hh]])
    kt = each(lambda n, ch, hh: k2[n] * g_inv[ch][:, lanes[hh]])
    rt = each(lambda n, ch, hh: r[ch][:, lanes[hh]] * g_incl[ch][:, lanes[hh]])

    bt_bd = each(lambda n, ch, hh: bd_cols(jnp.concatenate([bt[n], bt[n]], axis=0).T))
    kt_bd = each(lambda n, ch, hh: bd_cols(jnp.concatenate([kt[n], kt[n]], axis=0).T))
    ar = each(lambda n, ch, hh: jnp.concatenate([at[n], rt[n]], axis=0).astype(BF16))
    ab = each(lambda n, ch, hh: jnp.dot(ar[n], bt_bd[n], preferred_element_type=F32))
    ak = each(lambda n, ch, hh: jnp.dot(ar[n], kt_bd[n], preferred_element_type=F32))
    a_ab = [jnp.where(strict_lower, x[0:c], 0.0) for x in ab]
    a_rb = [jnp.where(incl_lower, x[c:], 0.0) for x in ab]
    a_ak = [jnp.where(strict_lower, x[0:c], 0.0) for x in ak]
    a_rk = [jnp.where(incl_lower, x[c:], 0.0) for x in ak]

    inv = [eye_all + x for x in a_ab]
    pw = a_ab
    for _ in range(5):
        pw = [jnp.dot(x.astype(BF16), _bd_rows(x, same_head), preferred_element_type=F32) for x in pw]
        inv = [x + jnp.dot(x.astype(BF16), _bd_rows(y, same_head), preferred_element_type=F32)
               for x, y in zip(inv, pw)]

    v_bd = each(lambda n, ch, hh: _bd_rows(v[ch][:, lanes[hh]], same_head))
    inv_b = [x.astype(BF16) for x in inv]
    wm = each(lambda n, ch, hh: jnp.dot(inv_b[n], _bd_rows(at[n], same_head), preferred_element_type=F32))
    akv = each(lambda n, ch, hh: jnp.dot(a_ak[n].astype(BF16), v_bd[n], preferred_element_type=F32))
    zm = each(lambda n, ch, hh: jnp.dot(inv_b[n], _bd_rows(akv[n], same_head), preferred_element_type=F32))
    rkv = each(lambda n, ch, hh: jnp.dot(a_rk[n].astype(BF16), v_bd[n], preferred_element_type=F32))
    bonus = each(lambda n, ch, hh: _dot2_right(
        r[ch][:, lanes[hh]] * k2[n] * rk_ref[:, lanes[hh]], ones_bd) * v[ch][:, lanes[hh]])
    for n, (ch, hh) in enumerate(chains):
        wm_ref[0, rows[ch], lanes[hh]] = wm[n].astype(BF16)
        zm_ref[0, rows[ch], lanes[hh]] = zm[n]
        rkv_ref[0, rows[ch], lanes[hh]] = rkv[n]
        arb_ref[0, rows[ch], lanes[hh]] = a_rb[n].astype(BF16)
        rt_ref[0, rows[ch], lanes[hh]] = rt[n].astype(BF16)
        bkt_ref[0, n * HALF:(n + 1) * HALF, :] = (
            jnp.concatenate([bt[n], kt[n]], axis=0).T.astype(BF16))
        bonus_ref[0, rows[ch], lanes[hh]] = bonus[n]


def _rwkv_scan_kernel(wm_ref, zm_ref, arb_ref, rkv_ref, rt_ref, vb_ref, bkt_ref, gl_ref,
                      bonus_ref, gate_ref, gng_ref, gnb_ref, o_ref, s_ref):
    @pl.when(pl.program_id(0) == 0)
    def _():
        s_ref[...] = jnp.zeros_like(s_ref)

    same_head = _same_head_mask()
    ones_bd = same_head.astype(BF16)
    chains = [(b, hh) for b in range(o_ref.shape[0]) for hh in range(RWKV_WIDTH // HALF)]
    lanes = [slice(hh * HALF, (hh + 1) * HALF) for hh in range(RWKV_WIDTH // HALF)]
    each = lambda fn: [fn(n, b, lanes[hh]) for n, (b, hh) in enumerate(chains)]
    dot = lambda x, y: jnp.dot(x, y, preferred_element_type=F32)

    s0 = [s_ref[b, hh] for b, hh in chains]
    s0b = [x.astype(BF16) for x in s0]
    u = each(lambda n, b, ln: dot(wm_ref[b, :, ln], s0b[n]) + zm_ref[b, :, ln])
    y0 = each(lambda n, b, ln: dot(rt_ref[b, :, ln], s0b[n]) + rkv_ref[b, :, ln])
    y = each(lambda n, b, ln: y0[n] + dot(arb_ref[b, :, ln], _bd_rows(u[n], same_head)))
    uv = each(lambda n, b, ln: jnp.concatenate([u[n].astype(BF16), vb_ref[b, :, ln]], axis=0))
    upd = [dot(bkt_ref[b, hh * HALF:(hh + 1) * HALF, :], uv[n]) for n, (b, hh) in enumerate(chains)]
    for n, (b, hh) in enumerate(chains):
        g_last = jnp.broadcast_to(gl_ref[b, 0:1, lanes[hh]], (LANES, HALF)).T
        g_col = jnp.concatenate([g_last, g_last], axis=1)
        s_ref[b, hh] = g_col * (s0[n] + jnp.where(same_head, upd[n], 0.0))

    mu = [_dot2_right(x, ones_bd) * (1.0 / HEAD_DIM) for x in y]
    yc = [x - m for x, m in zip(y, mu)]
    var = [_dot2_right(x * x, ones_bd) * (1.0 / HEAD_DIM) for x in yc]
    for n, (b, hh) in enumerate(chains):
        ln = lanes[hh]
        yn = yc[n] * lax.rsqrt(var[n] + RWKV_GN_EPS) * gng_ref[:, ln] + gnb_ref[:, ln]
        o_ref[b, :, ln] = (yn + bonus_ref[b, :, ln]) * gate_ref[b, :, ln]


def _rwkv(r, k, v, lo, w0, w2, a0, a2, g2, k_k, k_a, r_k, gn_g, gn_b, *, chunks=4):
    b, t, width = r.shape
    c = RWKV_CHUNK
    nc = t // c
    halves = width // HALF
    vec = lambda a: a.reshape(1, width)
    cs = lambda a: _const_spec(a.shape)
    row = lambda bi, ci: (bi, ci, 0)
    args = [vec(w0), w2.astype(BF16), vec(a0), a2.astype(BF16), g2.astype(BF16),
            vec(k_k), vec(k_a), vec(r_k)]
    tok = lambda dt: jax.ShapeDtypeStruct((b, t, width), dt)
    tok_spec = pl.BlockSpec((1, chunks * c, width), row)
    prep = pl.pallas_call(
        functools.partial(_rwkv_prep_kernel, chunks=chunks),
        grid=(b, nc // chunks),
        in_specs=[tok_spec] * 3 + [pl.BlockSpec((1, chunks * c, 256), row)] + [cs(a) for a in args],
        out_specs=[tok_spec] * 6
                  + [pl.BlockSpec((1, chunks * halves * HALF, LANES), row),
                     pl.BlockSpec((1, chunks * 8, width), row), tok_spec, tok_spec],
        out_shape=[tok(BF16), tok(F32), tok(BF16), tok(F32), tok(BF16), tok(BF16),
                   jax.ShapeDtypeStruct((b, nc * halves * HALF, LANES), BF16),
                   jax.ShapeDtypeStruct((b, nc * 8, width), F32), tok(F32), tok(F32)],
        compiler_params=_params(("arbitrary", "arbitrary")),
        name="rwkv7_prep",
    )(r, k, v, lo, *args)

    step = lambda ci: (0, ci, 0)
    scan_spec = pl.BlockSpec((b, c, width), step)
    return pl.pallas_call(
        _rwkv_scan_kernel,
        grid=(nc,),
        in_specs=[scan_spec] * 6
                 + [pl.BlockSpec((b, halves * HALF, LANES), step), pl.BlockSpec((b, 8, width), step),
                    scan_spec, scan_spec, cs(vec(gn_g)), cs(vec(gn_b))],
        out_specs=scan_spec,
        out_shape=jax.ShapeDtypeStruct((b, t, width), F32),
        scratch_shapes=[pltpu.VMEM((b, halves, HALF, HALF), F32)],
        compiler_params=_params(("arbitrary",)),
        name="rwkv7_scan",
    )(*prep, vec(gn_g), vec(gn_b))


def _mem_kv_kernel(m_ref, g_ref, wk_ref, wv_ref, kt_ref, v_ref):
    m = _rms(m_ref[0], g_ref[...]).astype(BF16)
    kt_ref[0] = jnp.dot(m, wk_ref[...], preferred_element_type=F32).T.astype(BF16)
    v_ref[0] = jnp.dot(m, wv_ref[...], preferred_element_type=F32).astype(BF16)


def _mem_kv(mem, g, wk, wv):
    b, mt, d = mem.shape
    return pl.pallas_call(
        _mem_kv_kernel,
        grid=(b,),
        in_specs=[pl.BlockSpec((1, mt, d), lambda bi: (bi, 0, 0)), _const_spec((1, d)),
                  _const_spec((d, d)), _const_spec((d, d))],
        out_specs=[pl.BlockSpec((1, d, mt), lambda bi: (bi, 0, 0)),
                   pl.BlockSpec((1, mt, d), lambda bi: (bi, 0, 0))],
        out_shape=[jax.ShapeDtypeStruct((b, d, mt), BF16), jax.ShapeDtypeStruct((b, mt, d), BF16)],
        compiler_params=_params(("arbitrary",)),
        name="mem_kv",
    )(mem, g.reshape(1, d), wk.astype(BF16), wv.astype(BF16))


def _out_mem_kernel(x_ref, on_ref, or_ref, ng_ref, wo1_ref, wo2_ref, mpost_ref,
                    mpre_ref, wq_ref, kt_ref, v_ref, wo_ref, mempost_ref, o_ref, *, parts):
    tm = x_ref.shape[1] // parts
    rows = [slice(n * tm, (n + 1) * tm) for n in range(parts)]
    dot = lambda a, b: jnp.dot(a, b, preferred_element_type=F32)
    d = x_ref.shape[-1]
    hd = d // MEM_HEADS

    a = [_rms(on_ref[0, rw, :], ng_ref[...]).astype(BF16) for rw in rows]
    mixed = [dot(a[n], wo1_ref[...]) + dot(or_ref[0, rw, :].astype(BF16), wo2_ref[...])
             for n, rw in enumerate(rows)]
    x = [x_ref[0, rw, :] + _rms(mixed[n], mpost_ref[...]) for n, rw in enumerate(rows)]
    h = [_rms(xn, mpre_ref[...]).astype(BF16) for xn in x]
    q = [(dot(hn, wq_ref[...]) * (hd ** -0.5)).astype(BF16) for hn in h]
    heads = [[] for _ in rows]
    for hi in range(MEM_HEADS):
        cols = slice(hi * hd, (hi + 1) * hd)
        s = [dot(qn[:, cols], kt_ref[0, cols, :]) for qn in q]
        e = [jnp.exp(sn - jnp.max(sn, axis=-1, keepdims=True)) for sn in s]
        p = [(en / jnp.sum(en, axis=-1, keepdims=True)).astype(BF16) for en in e]
        for n in range(parts):
            heads[n].append(dot(p[n], v_ref[0, :, cols]))
    att = [dot(jnp.concatenate(hn, axis=-1).astype(BF16), wo_ref[...]) for hn in heads]
    for n, rw in enumerate(rows):
        o_ref[0, rw, :] = x[n] + _rms(att[n], mempost_ref[...])


def _out_mem(x3, o_nsa, o_rwkv, nsa_g, w_out, mix_post_g, mem_pre_g, wq, kt, vm, wo, mem_post_g,
             *, tm=512, parts=2):
    b, t, d = x3.shape
    mt = vm.shape[1]
    row = lambda bi, ti: (bi, ti, 0)
    per_b = lambda bi, ti: (bi, 0, 0)
    w_out = w_out.astype(BF16)
    return pl.pallas_call(
        functools.partial(_out_mem_kernel, parts=parts),
        grid=(b, t // tm),
        in_specs=[
            pl.BlockSpec((1, tm, d), row),
            pl.BlockSpec((1, tm, NSA_WIDTH), row),
            pl.BlockSpec((1, tm, RWKV_WIDTH), row),
            _const_spec((1, NSA_WIDTH)),
            _const_spec((NSA_WIDTH, d)),
            _const_spec((RWKV_WIDTH, d)),
            _const_spec((1, d)),
            _const_spec((1, d)),
            _const_spec((d, d)),
            pl.BlockSpec((1, d, mt), per_b),
            pl.BlockSpec((1, mt, d), per_b),
            _const_spec((d, d)),
            _const_spec((1, d)),
        ],
        out_specs=pl.BlockSpec((1, tm, d), row),
        out_shape=jax.ShapeDtypeStruct((b, t, d), F32),
        compiler_params=_params(("arbitrary", "arbitrary")),
        name="out_mem",
    )(x3, o_nsa, o_rwkv, nsa_g.reshape(1, -1), w_out[:NSA_WIDTH], w_out[NSA_WIDTH:],
      mix_post_g.reshape(1, d), mem_pre_g.reshape(1, d), wq.astype(BF16), kt, vm,
      wo.astype(BF16), mem_post_g.reshape(1, d))


def _rope_tables(pos):
    half = HEAD_DIM // 2
    inv = ROPE_THETA ** (-jnp.arange(half, dtype=F32) / half)
    ang = pos.astype(F32)[:, None] * inv[None, :]
    cos, sin = jnp.cos(ang), jnp.sin(ang)
    cos_t = jnp.concatenate([cos, cos, cos, cos], axis=-1)
    sin_t = jnp.concatenate([-sin, sin, -sin, sin], axis=-1)
    return cos_t, sin_t


def _overlap_t(ns, ncp):
    c0 = np.arange(ncp)[None, :] * CMP_STRIDE
    s0 = np.arange(ns)[:, None] * SEL_BLOCK
    ov = (c0 < s0 + SEL_BLOCK) & (c0 + CMP_BLOCK > s0) & (np.arange(ncp)[None, :] < ncp - 1)
    return jnp.asarray(ov.astype(np.float32), dtype=BF16)


def _pad_cols(w, n):
    return jnp.pad(w, ((0, 0), (0, n - w.shape[1])))


def _cmp_weights(pe, w1, w2):
    per = CMP_STRIDE
    w1r = w1.reshape(CMP_BLOCK, HEAD_DIM, CMP_HIDDEN)
    blocks = []
    for part in range(CMP_BLOCK // per):
        for g in range(NSA_GROUPS):
            z = jnp.zeros((per, NSA_GROUPS, HEAD_DIM, CMP_HIDDEN), F32)
            z = z.at[:, g].set(w1r[part * per:(part + 1) * per])
            blocks.append(z.reshape(per * NSA_GROUPS * HEAD_DIM, CMP_HIDDEN))
    w1cat = jnp.concatenate(blocks, axis=1).astype(BF16)
    w2bd = jnp.zeros((NSA_GROUPS * CMP_HIDDEN, NSA_GROUPS * HEAD_DIM), F32)
    for g in range(NSA_GROUPS):
        w2bd = w2bd.at[g * CMP_HIDDEN:(g + 1) * CMP_HIDDEN, g * HEAD_DIM:(g + 1) * HEAD_DIM].set(w2)
    pe8 = jnp.broadcast_to(pe.reshape(1, CMP_BLOCK * HEAD_DIM), (8, CMP_BLOCK * HEAD_DIM))
    return pe8.astype(BF16), w1.astype(BF16), w1cat, w2bd.astype(BF16)


def kernel(x, mem, ffn1_pre_g, ffn1_w_gate, ffn1_w_up, ffn1_w_down, ffn1_post_g, mix_pre_g, w_in, cmp_pe_k, cmp_w1_k, cmp_w2_k, cmp_pe_v, cmp_w1_v, cmp_w2_v, nsa_out_g, rwkv_mu, rwkv_w0, rwkv_w2, rwkv_a0, rwkv_a2, rwkv_g2, rwkv_k_k, rwkv_k_a, rwkv_r_k, rwkv_gn_g, rwkv_gn_b, w_out, mix_post_g, mem_pre_g, mem_kv_g, mem_wq, mem_wk, mem_wv, mem_wo, mem_post_g, ffn2_pre_g, ffn2_w_gate, ffn2_w_up, ffn2_w_down, ffn2_post_g):
    b, t, d = x.shape
    ncp = t // CMP_STRIDE
    ns = t // SEL_BLOCK
    cos_t, sin_t = _rope_tables(jnp.arange(t))
    cos_c, sin_c = _rope_tables(jnp.arange(ncp) * CMP_STRIDE + (CMP_BLOCK - 1))
    ovt = _overlap_t(ns, ncp)

    for l in range(ffn1_pre_g.shape[0]):
        x2 = _ffn_block(x.reshape(b * t, d), ffn1_pre_g[l], ffn1_w_gate[l], ffn1_w_up[l],
                        ffn1_w_down[l], ffn1_post_g[l])
        x3 = x2.reshape(b, t, d)

        wi = w_in[l]
        nsa_w = NSA_WIDTH + 6 * NSA_KV_WIDTH
        gate_w = wi[:, nsa_w:nsa_w + 3 * NSA_HEADS]
        w_cols = jnp.concatenate([wi[:, :nsa_w], _pad_cols(gate_w, LANES),
                                  wi[:, nsa_w + 3 * NSA_HEADS:]], axis=1).astype(BF16)
        (q, k_cmp, v_cmp, k_slc, v_slc_t, k_win, v_win_t, gates_t, r, k, v, lo) = _in_proj(
            x3, mix_pre_g[l], w_cols, rwkv_mu[l].reshape(1, -1), cos_t, sin_t)

        pek, w1k, w1kc, w2k = _cmp_weights(cmp_pe_k[l], cmp_w1_k[l], cmp_w2_k[l])
        pev, w1v, w1vc, w2v = _cmp_weights(cmp_pe_v[l], cmp_w1_v[l], cmp_w2_v[l])
        row_w = CMP_STRIDE * NSA_KV_WIDTH
        kc, vct = _compress(k_cmp.reshape(b, ncp, row_w), v_cmp.reshape(b, ncp, row_w),
                            pek, w1k, w1kc, w2k, pev, w1v, w1vc, w2v, cos_c, sin_c)
        o_nsa = _nsa_attn(q, gates_t, kc, vct, k_slc, v_slc_t, k_win, v_win_t, ovt)

        o_rwkv = _rwkv(r, k, v, lo, rwkv_w0[l], rwkv_w2[l], rwkv_a0[l], rwkv_a2[l], rwkv_g2[l],
                       rwkv_k_k[l], rwkv_k_a[l], rwkv_r_k[l], rwkv_gn_g[l], rwkv_gn_b[l])

        kt, vm = _mem_kv(mem, mem_kv_g[l], mem_wk[l], mem_wv[l])
        x4 = _out_mem(x3, o_nsa, o_rwkv, nsa_out_g[l], w_out[l], mix_post_g[l], mem_pre_g[l],
                      mem_wq[l], kt, vm, mem_wo[l], mem_post_g[l])

        x = _ffn_block(x4.reshape(b * t, d), ffn2_pre_g[l], ffn2_w_gate[l], ffn2_w_up[l],
                       ffn2_w_down[l], ffn2_post_g[l]).reshape(b, t, d)
    return x
```

```python
import functools

import numpy as np
import jax
import jax.numpy as jnp
from jax import lax
from jax.experimental import pallas as pl
from jax.experimental.pallas import tpu as pltpu

F32 = jnp.float32
BF16 = jnp.bfloat16

HEAD_DIM = 64
NSA_HEADS = 8
NSA_GROUPS = 2
NSA_GROUP_SIZE = 4
NSA_WIDTH = 512
NSA_KV_WIDTH = 128
CMP_BLOCK = 32
CMP_STRIDE = 16
CMP_HIDDEN = 256
SEL_BLOCK = 64
SEL_TOPK = 16
SEL_FORCE = 1e4
SEL_FORCED = 3
WINDOW = 512
RWKV_HEADS = 8
RWKV_WIDTH = 512
DECAY_LORA = 64
AAA_LORA = 64
GATE_LORA = 128
RWKV_GN_EPS = 64e-5
MEM_HEADS = 4
ROPE_THETA = 10000.0
NORM_EPS = 1e-6
NEG_INF = -1e30
LOG2_E = 1.4426950408889634

LANES = 128
KEY_BLOCK = 128
Q_TILE = 128
SEL_TRIP = 4
RWKV_CHUNK = 64
HALF = 256
VMEM_LIMIT = 56 * 1024 * 1024


def _bdot(a, b):
    return jnp.dot(a.astype(BF16), b.astype(BF16), preferred_element_type=F32)


def _split3(x):
    h1 = x.astype(BF16)
    r1 = x - h1.astype(F32)
    h2 = r1.astype(BF16)
    r2 = r1 - h2.astype(F32)
    return h1, h2, r2.astype(BF16)


def _split2(x):
    h1 = x.astype(BF16)
    return h1, (x - h1.astype(F32)).astype(BF16)


def _dot2_right(x, m):
    h1, h2 = _split2(x)
    return (jnp.dot(h1, m, preferred_element_type=F32) + jnp.dot(h2, m, preferred_element_type=F32))


def _dot3_right(x, m):
    h1, h2, h3 = _split3(x)
    d = lambda h: jnp.dot(h, m, preferred_element_type=F32)
    return d(h1) + d(h2) + d(h3)


def _dot2_left(m, x):
    h1, h2 = _split2(x)
    return (jnp.dot(m, h1, preferred_element_type=F32) + jnp.dot(m, h2, preferred_element_type=F32))


def _dot3_left(m, x):
    h1, h2, h3 = _split3(x)
    d = lambda h: jnp.dot(m, h, preferred_element_type=F32)
    return d(h1) + d(h2) + d(h3)


def _rms(x, g):
    return x * lax.rsqrt(jnp.mean(x * x, axis=-1, keepdims=True) + NORM_EPS) * g


def _silu(x):
    return x / (1.0 + jnp.exp(-x))


def _sigmoid(x):
    return 1.0 / (1.0 + jnp.exp(-x))


def _const_spec(shape):
    nd = len(shape)
    return pl.BlockSpec(shape, lambda *_: (0,) * nd)


def _params(sem):
    return pltpu.CompilerParams(dimension_semantics=sem, vmem_limit_bytes=VMEM_LIMIT)


def _ffn_kernel(x_ref, pre_ref, wg_ref, wu_ref, wd_ref, post_ref, o_ref, *, ff_chunk):
    x = x_ref[...]
    h = _rms(x, pre_ref[...]).astype(BF16)
    d_ff = wg_ref.shape[1]
    acc = jnp.zeros(x.shape, F32)
    for c0 in range(0, d_ff, ff_chunk):
        g = jnp.dot(h, wg_ref[:, c0:c0 + ff_chunk], preferred_element_type=F32)
        u = jnp.dot(h, wu_ref[:, c0:c0 + ff_chunk], preferred_element_type=F32)
        a = (_silu(g) * u).astype(BF16)
        acc = acc + jnp.dot(a, wd_ref[c0:c0 + ff_chunk, :], preferred_element_type=F32)
    o_ref[...] = x + 0.5 * _rms(acc, post_ref[...])


def _ffn_block(x2, pre_g, wg, wu, wd, post_g, *, tm=512, ff_chunk=256):
    m, d = x2.shape
    d_ff = wg.shape[1]
    return pl.pallas_call(
        functools.partial(_ffn_kernel, ff_chunk=ff_chunk),
        grid=(m // tm,),
        in_specs=[
            pl.BlockSpec((tm, d), lambda i: (i, 0)),
            _const_spec((1, d)),
            _const_spec((d, d_ff)),
            _const_spec((d, d_ff)),
            _const_spec((d_ff, d)),
            _const_spec((1, d)),
        ],
        out_specs=pl.BlockSpec((tm, d), lambda i: (i, 0)),
        out_shape=jax.ShapeDtypeStruct((m, d), F32),
        compiler_params=_params(("arbitrary",)),
        name="ffn_block",
    )(x2, pre_g.reshape(1, d), wg.astype(BF16), wu.astype(BF16), wd.astype(BF16),
      post_g.reshape(1, d))


def _swap_halves(x):
    n = x.shape[-1]
    lane = lax.broadcasted_iota(jnp.int32, x.shape, x.ndim - 1)
    fwd = pltpu.roll(x, n - HEAD_DIM // 2, x.ndim - 1)
    bwd = pltpu.roll(x, HEAD_DIM // 2, x.ndim - 1)
    return jnp.where((lane % HEAD_DIM) < HEAD_DIM // 2, fwd, bwd)


def _rope(x, cos, sin_signed):
    reps = x.shape[-1] // LANES
    c = jnp.concatenate([cos] * reps, axis=-1) if reps > 1 else cos
    s = jnp.concatenate([sin_signed] * reps, axis=-1) if reps > 1 else sin_signed
    return x * c + _swap_halves(x) * s


def _in_proj_kernel(x_ref, g_ref, w_ref, mu_ref, cos_ref, sin_ref,
                    q_ref, kc_ref, vc_ref, ks_ref, vst_ref, kw_ref, vwt_ref, gt_ref,
                    r_ref, k_ref, v_ref, lo_ref, carry_ref):
    @pl.when(pl.program_id(1) == 0)
    def _():
        carry_ref[...] = jnp.zeros_like(carry_ref)

    h = _rms(x_ref[0], g_ref[...]).astype(BF16)
    p = jnp.dot(h, w_ref[...], preferred_element_type=F32)
    cos = cos_ref[...]
    sin = sin_ref[...]
    tm = p.shape[0]

    o = 0
    q = _rope(p[:, o:o + NSA_WIDTH], cos, sin) * (HEAD_DIM ** -0.5 * LOG2_E)
    q_ref[0] = q.astype(BF16)
    o += NSA_WIDTH
    kc_ref[0] = p[:, o:o + LANES].astype(BF16); o += LANES
    vc_ref[0] = p[:, o:o + LANES].astype(BF16); o += LANES
    ks_ref[0] = _rope(p[:, o:o + LANES], cos, sin).astype(BF16); o += LANES
    vst_ref[0] = p[:, o:o + LANES].T.astype(BF16); o += LANES
    kw_ref[0] = _rope(p[:, o:o + LANES], cos, sin).astype(BF16); o += LANES
    vwt_ref[0] = p[:, o:o + LANES].T.astype(BF16); o += LANES
    gates_t = _sigmoid(p[:, o:o + LANES]).T
    gt_ref[0] = gates_t[:gt_ref.shape[1], :]
    o += LANES

    rw = p[:, o:]
    row = lax.broadcasted_iota(jnp.int32, rw.shape, 0)
    prev = jnp.where(row == 0, carry_ref[0:1, :], pltpu.roll(rw, 1, 0))
    carry_ref[...] = jnp.broadcast_to(rw[tm - 1:tm, :], carry_ref.shape)
    mixed = rw + (prev - rw) * mu_ref[...]
    r_ref[0] = mixed[:, 0:512]
    k_ref[0] = mixed[:, 512:1024]
    v_ref[0] = mixed[:, 1024:1536]
    lo_ref[0] = mixed[:, 1536:1792]


def _in_proj(x3, g, w_cols, mu_cols, cos_t, sin_t, *, tm=512):
    b, t, d = x3.shape
    n = w_cols.shape[1]
    row = lambda bi, ti: (bi, ti, 0)
    col = lambda bi, ti: (bi, 0, ti)
    out_shapes = [
        jax.ShapeDtypeStruct((b, t, NSA_WIDTH), BF16),
        jax.ShapeDtypeStruct((b, t, LANES), BF16),
        jax.ShapeDtypeStruct((b, t, LANES), BF16),
        jax.ShapeDtypeStruct((b, t, LANES), BF16),
        jax.ShapeDtypeStruct((b, LANES, t), BF16),
        jax.ShapeDtypeStruct((b, t, LANES), BF16),
        jax.ShapeDtypeStruct((b, LANES, t), BF16),
        jax.ShapeDtypeStruct((b, 32, t), F32),
        jax.ShapeDtypeStruct((b, t, RWKV_WIDTH), F32),
        jax.ShapeDtypeStruct((b, t, RWKV_WIDTH), F32),
        jax.ShapeDtypeStruct((b, t, RWKV_WIDTH), F32),
        jax.ShapeDtypeStruct((b, t, 256), F32),
    ]
    out_specs = [
        pl.BlockSpec((1, tm, NSA_WIDTH), row),
        pl.BlockSpec((1, tm, LANES), row),
        pl.BlockSpec((1, tm, LANES), row),
        pl.BlockSpec((1, tm, LANES), row),
        pl.BlockSpec((1, LANES, tm), col),
        pl.BlockSpec((1, tm, LANES), row),
        pl.BlockSpec((1, LANES, tm), col),
        pl.BlockSpec((1, 32, tm), col),
        pl.BlockSpec((1, tm, RWKV_WIDTH), row),
        pl.BlockSpec((1, tm, RWKV_WIDTH), row),
        pl.BlockSpec((1, tm, RWKV_WIDTH), row),
        pl.BlockSpec((1, tm, 256), row),
    ]
    return pl.pallas_call(
        _in_proj_kernel,
        grid=(b, t // tm),
        in_specs=[
            pl.BlockSpec((1, tm, d), row),
            _const_spec((1, d)),
            _const_spec((d, n)),
            _const_spec((1, 1792)),
            pl.BlockSpec((tm, LANES), lambda bi, ti: (ti, 0)),
            pl.BlockSpec((tm, LANES), lambda bi, ti: (ti, 0)),
        ],
        out_specs=out_specs,
        out_shape=out_shapes,
        scratch_shapes=[pltpu.VMEM((8, 1792), F32)],
        compiler_params=_params(("arbitrary", "arbitrary")),
        name="in_proj",
    )(x3, g.reshape(1, d), w_cols, mu_cols, cos_t, sin_t)


def _compress_kernel(kin_ref, vin_ref, pek_ref, w1k_ref, w1kc_ref, w2k_ref,
                     pev_ref, w1v_ref, w1vc_ref, w2v_ref, cos_ref, sin_ref,
                     kc_ref, vct_ref):
    def phi(rows, pe, w1, w1cat, w2bd):
        n = rows.shape[0]
        pr = jnp.dot(rows, w1cat, preferred_element_type=F32)
        bias = jnp.dot(pe, w1, preferred_element_type=F32)[0:1, :]
        hid = []
        for g in range(NSA_GROUPS):
            top = pr[:, g * CMP_HIDDEN:(g + 1) * CMP_HIDDEN]
            bot = pr[:, (NSA_GROUPS + g) * CMP_HIDDEN:(NSA_GROUPS + g + 1) * CMP_HIDDEN]
            hid.append(top + pltpu.roll(bot, n - 1, 0) + bias)
        act = _silu(jnp.concatenate(hid, axis=-1)).astype(BF16)
        return jnp.dot(act, w2bd, preferred_element_type=F32)

    kc = phi(kin_ref[0], pek_ref[...], w1k_ref[...], w1kc_ref[...], w2k_ref[...])
    kc_ref[0] = _rope(kc, cos_ref[...], sin_ref[...]).astype(BF16)
    vc = phi(vin_ref[0], pev_ref[...], w1v_ref[...], w1vc_ref[...], w2v_ref[...])
    vct_ref[0] = vc.T.astype(BF16)


def _compress(kin, vin, pek, w1k, w1kc, w2k, pev, w1v, w1vc, w2v, cos_c, sin_c):
    b, ncp, width = kin.shape
    blk = pl.BlockSpec((1, ncp, width), lambda bi: (bi, 0, 0))
    cs = lambda a: _const_spec(a.shape)
    return pl.pallas_call(
        _compress_kernel,
        grid=(b,),
        in_specs=[blk, blk, cs(pek), cs(w1k), cs(w1kc), cs(w2k),
                  cs(pev), cs(w1v), cs(w1vc), cs(w2v), cs(cos_c), cs(sin_c)],
        out_specs=[pl.BlockSpec((1, ncp, LANES), lambda bi: (bi, 0, 0)),
                   pl.BlockSpec((1, LANES, ncp), lambda bi: (bi, 0, 0))],
        out_shape=[jax.ShapeDtypeStruct((b, ncp, LANES), BF16),
                   jax.ShapeDtypeStruct((b, LANES, ncp), BF16)],
        compiler_params=_params(("arbitrary",)),
        name="nsa_compress",
    )(kin, vin, pek, w1k, w1kc, w2k, pev, w1v, w1vc, w2v, cos_c, sin_c)


def _nsa_kernel(q_ref, gt_ref, kc_ref, vct_ref, ks_ref, vst_ref, kw_ref, vwt_ref, ovt_ref, ind_ref,
                o_ref, bias_ref, sa_ref, sb_ref, acc_ref):
    i = pl.program_id(1)
    q0 = i * Q_TILE
    ncp = kc_ref.shape[1]
    ns = ovt_ref.shape[0]
    gw = NSA_GROUP_SIZE * Q_TILE
    width = NSA_GROUPS * gw
    lanes = [slice(g * gw, (g + 1) * gw) for g in range(NSA_GROUPS)]
    feat = [slice(g * HEAD_DIM, (g + 1) * HEAD_DIM) for g in range(NSA_GROUPS)]
    t_row = q0 + lax.broadcasted_iota(jnp.int32, (1, width), 1) % Q_TILE
    sel_keys = SEL_TRIP * KEY_BLOCK
    sel_rows = sel_keys // SEL_BLOCK
    win_keys = WINDOW + Q_TILE
    dot = lambda x, y: jnp.dot(x, y, preferred_element_type=F32)

    qf = q_ref[0].astype(F32)
    zeros_half = jnp.zeros((HEAD_DIM, Q_TILE), F32)
    parts = []
    for g in range(NSA_GROUPS):
        for pair in range(NSA_GROUP_SIZE // 2):
            slab_t = qf[:, (2 * g + pair) * LANES:(2 * g + pair + 1) * LANES].T
            for half in range(2):
                f = slab_t[half * HEAD_DIM:(half + 1) * HEAD_DIM, :]
                parts.append(jnp.concatenate([f, zeros_half] if g == 0 else [zeros_half, f], axis=0))
    qt = jnp.concatenate(parts, axis=1).astype(BF16)

    def v_ext(vt_ref, g, k0, n):
        return jnp.concatenate([vt_ref[0, feat[g], pl.ds(k0, n)], jnp.ones((16, n), BF16)], axis=0)

    sw_ = NSA_GROUPS * Q_TILE
    n_vis = (q0 + Q_TILE - CMP_BLOCK) // CMP_STRIDE + 1
    blocks_needed = jnp.maximum(n_vis + KEY_BLOCK - 1, KEY_BLOCK) // KEY_BLOCK
    def compressed(nb):
        rows = nb * KEY_BLOCK
        sc = dot(kc_ref[0, 0:rows, :], qt)
        c_end = lax.broadcasted_iota(jnp.int32, (rows, width), 0) * CMP_STRIDE + (CMP_BLOCK - 1)
        c_mask = c_end <= t_row
        sc = jnp.where(c_mask, sc, NEG_INF)
        m_c = jnp.max(sc, axis=0, keepdims=True)
        e_c = jnp.where(c_mask, jnp.exp2(sc - m_c), 0.0)
        p_c = e_c / jnp.maximum(jnp.sum(e_c, axis=0, keepdims=True), 1e-30)
        outs, p_sum = [], []
        for g in range(NSA_GROUPS):
            outs.append(dot(vct_ref[0, feat[g], 0:rows], p_c[:, lanes[g]].astype(BF16)))
            acc = p_c[:, g * gw:g * gw + Q_TILE]
            for r in range(1, NSA_GROUP_SIZE):
                acc = acc + p_c[:, g * gw + r * Q_TILE:g * gw + (r + 1) * Q_TILE]
            p_sum.append(acc)
        return outs + [_dot2_left(ovt_ref[:, 0:rows], jnp.concatenate(p_sum, axis=1))]

    *o_c, imp = lax.switch(blocks_needed - 1,
                           [functools.partial(compressed, nb) for nb in range(1, ncp // KEY_BLOCK + 1)])

    w0 = pl.multiple_of(jnp.maximum(q0 - WINDOW, 0), KEY_BLOCK)
    sw = dot(kw_ref[0, pl.ds(w0, win_keys), :], qt)
    t_loc = t_row - w0
    k_loc = lax.broadcasted_iota(jnp.int32, (win_keys, width), 0)
    ok = k_loc <= t_loc
    old = lax.broadcasted_iota(jnp.int32, (KEY_BLOCK, width), 0) <= t_loc - WINDOW
    sw = jnp.where(ok, sw, NEG_INF)
    sw = jnp.concatenate([jnp.where(old, NEG_INF, sw[0:KEY_BLOCK]), sw[KEY_BLOCK:]], axis=0)
    m_w = jnp.max(sw, axis=0, keepdims=True)
    o_w = []
    for g in range(NSA_GROUPS):
        p = jnp.exp2(sw[:, lanes[g]] - m_w[:, lanes[g]]).astype(BF16)
        pv = dot(v_ext(vwt_ref, g, w0, win_keys), p)
        o_w.append(pv[0:HEAD_DIM, :] / pv[HEAD_DIM:HEAD_DIM + 1, :])

    s_id = lax.broadcasted_iota(jnp.int32, (ns, sw_), 0)
    cur = t_row[:, 0:sw_] // SEL_BLOCK
    forced = (s_id == 0) | (s_id == cur) | (s_id == cur - 1)
    score = jnp.where(forced, -3e38, jnp.where(s_id <= cur, imp, -SEL_FORCE))
    bias = jnp.where(forced, 0.0, NEG_INF)
    for _ in range(min(SEL_TOPK, ns) - SEL_FORCED):
        mx = jnp.max(score, axis=0, keepdims=True)
        first = jnp.min(jnp.where(score == mx, s_id, ns), axis=0, keepdims=True)
        hit = s_id == first
        score = jnp.where(hit, -3e38, score)
        bias = jnp.where(hit, 0.0, bias)
    bias_ref[...] = bias

    def scores(j, s_ref):
        k0 = pl.multiple_of(j * sel_keys, sel_keys)
        b0 = pl.multiple_of(j * sel_rows, sel_rows)
        rows = bias_ref[pl.ds(b0, sel_rows), :]
        rows = jnp.concatenate([rows[:, g * Q_TILE:(g + 1) * Q_TILE]
                                for g in range(NSA_GROUPS) for _ in range(NSA_GROUP_SIZE)], axis=1)
        rows = jnp.concatenate([rows, jnp.zeros_like(rows)], axis=0).astype(BF16)
        rhs = jnp.concatenate([qt, rows, jnp.zeros((LANES - rows.shape[0], width), BF16)], axis=0)
        lhs = jnp.concatenate([ks_ref[0, pl.ds(k0, sel_keys), :], ind_ref[...]], axis=1)
        s = dot(lhs, rhs)
        s_ref[...] = s
        return jnp.max(s, axis=0, keepdims=True)

    def softmax_pv(j, s_ref, mb, m_run):
        k0 = pl.multiple_of(j * sel_keys, sel_keys)
        m_new = jnp.maximum(m_run, mb)
        alpha = jnp.exp2(m_run - m_new)
        for g in range(NSA_GROUPS):
            p = jnp.exp2(s_ref[:, lanes[g]] - m_new[:, lanes[g]]).astype(BF16)
            pv = dot(v_ext(vst_ref, g, k0, sel_keys), p)
            acc_ref[g] = acc_ref[g] * alpha[:, lanes[g]] + pv
        return m_new

    def causal_tail(j, s_ref, m_run):
        k0 = pl.multiple_of(j * sel_keys, sel_keys)
        key = k0 + lax.broadcasted_iota(jnp.int32, (sel_keys, width), 0)
        s = jnp.where(key <= t_row, s_ref[...], NEG_INF)
        s_ref[...] = s
        softmax_pv(j, s_ref, jnp.max(s, axis=0, keepdims=True), m_run)

    acc_ref[...] = jnp.zeros_like(acc_ref)
    n_full = q0 // sel_keys
    n_pairs = n_full // 2

    def pair_body(kk, carry):
        m_run, mb_a = carry
        j = 2 * kk
        mb_b = scores(j + 1, sb_ref)
        m_run = softmax_pv(j, sa_ref, mb_a, m_run)
        mb_a = scores(j + 2, sa_ref)
        m_run = softmax_pv(j + 1, sb_ref, mb_b, m_run)
        return m_run, mb_a

    m_run, mb_a = lax.fori_loop(0, n_pairs, pair_body,
                                (jnp.full((1, width), NEG_INF, F32), scores(0, sa_ref)))
    j_last = 2 * n_pairs

    @pl.when(n_full % 2 == 1)
    def _():
        scores(j_last + 1, sb_ref)
        causal_tail(j_last + 1, sb_ref, softmax_pv(j_last, sa_ref, mb_a, m_run))

    @pl.when(n_full % 2 == 0)
    def _():
        causal_tail(j_last, sa_ref, m_run)

    o_s = [acc_ref[g, 0:HEAD_DIM, :] / acc_ref[g, HEAD_DIM:HEAD_DIM + 1, :] for g in range(NSA_GROUPS)]

    outs = []
    for g in range(NSA_GROUPS):
        heads = []
        for r in range(NSA_GROUP_SIZE):
            base = (g * NSA_GROUP_SIZE + r) * 3
            cols = slice(r * Q_TILE, (r + 1) * Q_TILE)
            heads.append(gt_ref[0, base:base + 1, :] * o_c[g][:, cols]
                         + gt_ref[0, base + 1:base + 2, :] * o_s[g][:, cols]
                         + gt_ref[0, base + 2:base + 3, :] * o_w[g][:, cols])
        for pair in range(NSA_GROUP_SIZE // 2):
            outs.append(jnp.concatenate(heads[2 * pair:2 * pair + 2], axis=0).T)
    o_ref[0] = jnp.concatenate(outs, axis=1)


def _nsa_attn(q, gates_t, kc, vct, ks, vst, kw, vwt, ovt):
    b, t, _ = q.shape
    ncp = kc.shape[1]
    ns = ovt.shape[0]
    sel_keys = SEL_TRIP * KEY_BLOCK
    assert t % sel_keys == 0 and sel_keys % Q_TILE == 0 and t >= WINDOW + Q_TILE and ncp % KEY_BLOCK == 0
    ind = (np.arange(sel_keys)[:, None] // SEL_BLOCK == np.arange(LANES)[None, :])
    ind = jnp.asarray(ind.astype(np.float32), dtype=BF16)
    full_rows = lambda bi, qi: (bi, 0, 0)
    return pl.pallas_call(
        _nsa_kernel,
        grid=(b, t // Q_TILE),
        in_specs=[
            pl.BlockSpec((1, Q_TILE, NSA_WIDTH), lambda bi, qi: (bi, qi, 0)),
            pl.BlockSpec((1, 32, Q_TILE), lambda bi, qi: (bi, 0, qi)),
            pl.BlockSpec((1, ncp, LANES), full_rows),
            pl.BlockSpec((1, LANES, ncp), full_rows),
            pl.BlockSpec((1, t, LANES), full_rows),
            pl.BlockSpec((1, LANES, t), full_rows),
            pl.BlockSpec((1, t, LANES), full_rows),
            pl.BlockSpec((1, LANES, t), full_rows),
            _const_spec((ns, ncp)),
            _const_spec((sel_keys, LANES)),
        ],
        out_specs=pl.BlockSpec((1, Q_TILE, NSA_WIDTH), lambda bi, qi: (bi, qi, 0)),
        out_shape=jax.ShapeDtypeStruct((b, t, NSA_WIDTH), F32),
        scratch_shapes=[pltpu.VMEM((ns, NSA_GROUPS * Q_TILE), F32),
                        pltpu.VMEM((sel_keys, NSA_GROUPS * NSA_GROUP_SIZE * Q_TILE), F32),
                        pltpu.VMEM((sel_keys, NSA_GROUPS * NSA_GROUP_SIZE * Q_TILE), F32),
                        pltpu.VMEM((NSA_GROUPS, HEAD_DIM + 16, NSA_GROUP_SIZE * Q_TILE), F32)],
        compiler_params=_params(("arbitrary", "arbitrary")),
        name="nsa_attn",
    )(q, gates_t, kc, vct, ks, vst, kw, vwt, ovt, ind)


def _same_head_mask():
    bi = lax.broadcasted_iota(jnp.int32, (HALF, HALF), 0) // HEAD_DIM
    bj = lax.broadcasted_iota(jnp.int32, (HALF, HALF), 1) // HEAD_DIM
    return bi == bj


def _bd_rows(x, same_head):
    xb = x.astype(BF16)
    tiled = jnp.concatenate([xb] * (HALF // x.shape[0]), axis=0)
    return jnp.where(same_head, tiled, jnp.zeros((), BF16))


def _rwkv_prep_stages(in_refs, par_refs, out_set, *, chunks):
    r_ref, k_ref, v_ref, lo_ref = in_refs
    w0_ref, w2_ref, a0_ref, a2_ref, g2_ref, kk_ref, ka_ref, rk_ref = par_refs
    wm_ref, zm_ref, arb_ref, rkv_ref, rt_ref, vb_ref, bkt_ref, gl_ref, bonus_ref, gate_ref = out_set
    c = RWKV_CHUNK
    nb = r_ref.shape[0]
    same_head = _same_head_mask()
    ones_bd = same_head.astype(BF16)
    n_tok = chunks * c
    row_i = lax.broadcasted_iota(jnp.int32, (n_tok, n_tok), 0)
    col_i = lax.broadcasted_iota(jnp.int32, (n_tok, n_tok), 1)
    tril_incl = ((row_i >= col_i) & (row_i // c == col_i // c)).astype(BF16)
    t_id = lax.broadcasted_iota(jnp.int32, (c, HALF), 0)
    j_id = lax.broadcasted_iota(jnp.int32, (c, HALF), 1) % HEAD_DIM
    strict_lower = t_id > j_id
    incl_lower = t_id >= j_id
    eye_all = (t_id == j_id).astype(F32)
    dot = lambda x, y: jnp.dot(x, y, preferred_element_type=F32)

    def bd_cols(xt):
        xb = xt.astype(BF16)
        return jnp.where(same_head, jnp.concatenate([xb, xb], axis=1), jnp.zeros((), BF16))

    halves = RWKV_WIDTH // HALF
    groups = [(b, ch) for b in range(nb) for ch in range(chunks)]
    chains = [(gi, hh) for gi in range(len(groups)) for hh in range(halves)]
    rows = [slice(ch * c, (ch + 1) * c) for ch in range(chunks)]
    lanes = [slice(hh * HALF, (hh + 1) * HALF) for hh in range(halves)]
    each = lambda fn: [fn(n, gi, lanes[hh]) for n, (gi, hh) in enumerate(chains)]

    lo = [lo_ref[b] for b in range(nb)]
    zs = [-(w0_ref[...] + _bdot(jnp.tanh(x[:, 0:DECAY_LORA]), w2_ref[...])) for x in lo]
    yield
    lr_b = [_sigmoid(a0_ref[...] + _bdot(x[:, DECAY_LORA:DECAY_LORA + AAA_LORA], a2_ref[...])) for x in lo]
    yield
    for b in range(nb):
        gate_ref[b] = _bdot(_sigmoid(lo[b][:, DECAY_LORA + AAA_LORA:]), g2_ref[...])
        vb_ref[b] = v_ref[b].astype(BF16)
    yield
    log_decay = []
    for z in zs:
        softplus = jnp.maximum(z, 0.0) + jnp.log(1.0 + jnp.exp(-jnp.abs(z)))
        log_decay.append(-jnp.exp(-softplus - 0.5))
    cum_b = [_dot3_left(tril_incl, x) for x in log_decay]
    yield
    r = [r_ref[b, rows[ch], :] for b, ch in groups]
    k = [k_ref[b, rows[ch], :] for b, ch in groups]
    v = [v_ref[b, rows[ch], :] for b, ch in groups]
    lr = [lr_b[b][rows[ch], :] for b, ch in groups]
    cum = [cum_b[b][rows[ch], :] for b, ch in groups]
    ld = [log_decay[b][rows[ch], :] for b, ch in groups]
    g_incl = [jnp.exp(x) for x in cum]
    g_excl = [jnp.exp(x - y) for x, y in zip(cum, ld)]
    g_inv = [jnp.exp(-x) for x in cum]
    for gi, (b, ch) in enumerate(groups):
        gl_ref[b, ch * 8:(ch + 1) * 8, :] = jnp.broadcast_to(g_incl[gi][c - 1:c, :], (8, RWKV_WIDTH))

    kk = each(lambda n, gi, ln: k[gi][:, ln] * kk_ref[:, ln])
    k2 = each(lambda n, gi, ln: k[gi][:, ln] * (1.0 + (lr[gi][:, ln] - 1.0) * ka_ref[:, ln]))
    sums = each(lambda n, gi, ln: _dot2_right(
        jnp.concatenate([kk[n] * kk[n], r[gi][:, ln] * k2[n] * rk_ref[:, ln]], axis=0), ones_bd))
    ssq = [x[0:c] for x in sums]
    for n, (gi, hh) in enumerate(chains):
        b, ch = groups[gi]
        bonus_ref[b, rows[ch], lanes[hh]] = sums[n][c:] * v[gi][:, lanes[hh]]
    yield
    kk = each(lambda n, gi, ln: kk[n] / jnp.maximum(jnp.sqrt(ssq[n]), 1e-12))
    at = each(lambda n, gi, ln: -kk[n] * g_excl[gi][:, ln])
    bt = each(lambda n, gi, ln: kk[n] * lr[gi][:, ln] * g_inv[gi][:, ln])
    kt = each(lambda n, gi, ln: k2[n] * g_inv[gi][:, ln])
    rt = each(lambda n, gi, ln: r[gi][:, ln] * g_incl[gi][:, ln])
    for n, (gi, hh) in enumerate(chains):
        b, ch = groups[gi]
        rt_ref[b, rows[ch], lanes[hh]] = rt[n].astype(BF16)
        bkt_ref[b, (ch * halves + hh) * HALF:(ch * halves + hh + 1) * HALF, :] = (
            jnp.concatenate([bt[n], kt[n]], axis=0).T.astype(BF16))

    bt_bd = each(lambda n, gi, ln: bd_cols(jnp.concatenate([bt[n], bt[n]], axis=0).T))
    ar = each(lambda n, gi, ln: jnp.concatenate([at[n], rt[n]], axis=0).astype(BF16))
    ab = each(lambda n, gi, ln: dot(ar[n], bt_bd[n]))
    yield
    kt_bd = each(lambda n, gi, ln: bd_cols(jnp.concatenate([kt[n], kt[n]], axis=0).T))
    ak = each(lambda n, gi, ln: dot(ar[n], kt_bd[n]))
    yield
    a_ab = [jnp.where(strict_lower, x[0:c], 0.0) for x in ab]
    a_rb = [jnp.where(incl_lower, x[c:], 0.0) for x in ab]
    a_ak = [jnp.where(strict_lower, x[0:c], 0.0) for x in ak]
    a_rk = [jnp.where(incl_lower, x[c:], 0.0) for x in ak]
    for n, (gi, hh) in enumerate(chains):
        b, ch = groups[gi]
        arb_ref[b, rows[ch], lanes[hh]] = a_rb[n].astype(BF16)

    inv = [eye_all + x for x in a_ab]
    pw = [dot(x.astype(BF16), _bd_rows(x, same_head)) for x in a_ab]
    yield
    for _ in range(4):
        both = [dot(jnp.concatenate([x, y], axis=0).astype(BF16), _bd_rows(x, same_head))
                for x, y in zip(pw, inv)]
        pw = [x[0:c] for x in both]
        inv = [y + x[c:] for x, y in zip(both, inv)]
        yield
    inv = [y + dot(y.astype(BF16), _bd_rows(x, same_head)) for x, y in zip(pw, inv)]
    yield

    v_bd = each(lambda n, gi, ln: _bd_rows(v[gi][:, ln], same_head))
    inv_b = [x.astype(BF16) for x in inv]
    wm = each(lambda n, gi, ln: dot(inv_b[n], _bd_rows(at[n], same_head)))
    yield
    av = each(lambda n, gi, ln: dot(jnp.concatenate([a_ak[n], a_rk[n]], axis=0).astype(BF16), v_bd[n]))
    yield
    zm = each(lambda n, gi, ln: dot(inv_b[n], _bd_rows(av[n][0:c], same_head)))
    for n, (gi, hh) in enumerate(chains):
        b, ch = groups[gi]
        wm_ref[b, rows[ch], lanes[hh]] = wm[n].astype(BF16)
        zm_ref[b, rows[ch], lanes[hh]] = zm[n]
        rkv_ref[b, rows[ch], lanes[hh]] = av[n][c:]


def _rwkv_scan_stages(in_set, gng_ref, gnb_ref, o_ref, s_ref, *, chunks):
    wm_ref, zm_ref, arb_ref, rkv_ref, rt_ref, vb_ref, bkt_ref, gl_ref, bonus_ref, gate_ref = in_set
    c = RWKV_CHUNK
    halves = RWKV_WIDTH // HALF
    same_head = _same_head_mask()
    ones_bd = same_head.astype(BF16)
    chains = [(b, hh) for b in range(o_ref.shape[0]) for hh in range(halves)]
    lanes = [slice(hh * HALF, (hh + 1) * HALF) for hh in range(halves)]
    dot = lambda x, y: jnp.dot(x, y, preferred_element_type=F32)

    for ch in range(chunks):
        rw = slice(ch * c, (ch + 1) * c)
        each = lambda fn: [fn(n, b, lanes[hh]) for n, (b, hh) in enumerate(chains)]
        s0 = [s_ref[b, hh] for b, hh in chains]
        s0b = [x.astype(BF16) for x in s0]
        u = each(lambda n, b, ln: dot(wm_ref[b, rw, ln], s0b[n]) + zm_ref[b, rw, ln])
        yield
        y0 = each(lambda n, b, ln: dot(rt_ref[b, rw, ln], s0b[n]) + rkv_ref[b, rw, ln])
        yield
        y = each(lambda n, b, ln: y0[n] + dot(arb_ref[b, rw, ln], _bd_rows(u[n], same_head)))
        yield
        uv = each(lambda n, b, ln: jnp.concatenate([u[n].astype(BF16), vb_ref[b, rw, ln]], axis=0))
        upd = [dot(bkt_ref[b, (ch * halves + hh) * HALF:(ch * halves + hh + 1) * HALF, :], uv[n])
               for n, (b, hh) in enumerate(chains)]
        for n, (b, hh) in enumerate(chains):
            g_last = jnp.broadcast_to(gl_ref[b, ch * 8:ch * 8 + 1, lanes[hh]], (LANES, HALF)).T
            g_col = jnp.concatenate([g_last, g_last], axis=1)
            s_ref[b, hh] = g_col * (s0[n] + jnp.where(same_head, upd[n], 0.0))
        yield
        mu = [_dot2_right(x, ones_bd) * (1.0 / HEAD_DIM) for x in y]
        yield
        yc = [x - m for x, m in zip(y, mu)]
        var = [_dot2_right(x * x, ones_bd) * (1.0 / HEAD_DIM) for x in yc]
        yield
        for n, (b, hh) in enumerate(chains):
            ln = lanes[hh]
            yn = yc[n] * lax.rsqrt(var[n] + RWKV_GN_EPS) * gng_ref[:, ln] + gnb_ref[:, ln]
            o_ref[b, rw, ln] = (yn + bonus_ref[b, rw, ln]) * gate_ref[b, rw, ln]


def _rwkv_kernel(*refs, chunks):
    in_refs, par_refs = refs[0:4], refs[4:12]
    gng_ref, gnb_ref, o_ref, s_ref = refs[12:16]
    sets = (refs[16:26], refs[26:36])
    step = pl.program_id(0)

    @pl.when(step == 0)
    def _():
        s_ref[...] = jnp.zeros_like(s_ref)
        for ref in sets[1]:
            ref[...] = jnp.zeros_like(ref)

    def run(write_set, read_set):
        prep = _rwkv_prep_stages(in_refs, par_refs, write_set, chunks=chunks)
        scan = _rwkv_scan_stages(read_set, gng_ref, gnb_ref, o_ref, s_ref, chunks=chunks)
        live = [prep, scan]
        while live:
            for gen in list(live):
                if next(gen, "done") == "done":
                    live.remove(gen)

    @pl.when(step % 2 == 0)
    def _():
        run(sets[0], sets[1])

    @pl.when(step % 2 == 1)
    def _():
        run(sets[1], sets[0])


def _rwkv(r, k, v, lo, w0, w2, a0, a2, g2, k_k, k_a, r_k, gn_g, gn_b, *, chunks=2):
    b, t, width = r.shape
    c = RWKV_CHUNK
    n_tok = chunks * c
    n_groups = t // n_tok
    halves = width // HALF
    vec = lambda a: a.reshape(1, width)
    cs = lambda a: _const_spec(a.shape)
    args = [vec(w0), w2.astype(BF16), vec(a0), a2.astype(BF16), g2.astype(BF16),
            vec(k_k), vec(k_a), vec(r_k), vec(gn_g), vec(gn_b)]
    ahead = lambda s: (0, jnp.minimum(s, n_groups - 1), 0)
    behind = lambda s: (0, jnp.maximum(s - 1, 0), 0)
    tok = lambda dt: pltpu.VMEM((b, n_tok, width), dt)
    scratch_set = [tok(BF16), tok(F32), tok(BF16), tok(F32), tok(BF16), tok(BF16),
                   pltpu.VMEM((b, chunks * halves * HALF, LANES), BF16),
                   pltpu.VMEM((b, chunks * 8, width), F32), tok(F32), tok(F32)]
    return pl.pallas_call(
        functools.partial(_rwkv_kernel, chunks=chunks),
        grid=(n_groups + 1,),
        in_specs=[pl.BlockSpec((b, n_tok, width), ahead)] * 3 + [pl.BlockSpec((b, n_tok, 256), ahead)]
                 + [cs(a) for a in args],
        out_specs=pl.BlockSpec((b, n_tok, width), behind),
        out_shape=jax.ShapeDtypeStruct((b, t, width), F32),
        scratch_shapes=[pltpu.VMEM((b, halves, HALF, HALF), F32)] + scratch_set + scratch_set,
        compiler_params=_params(("arbitrary",)),
        name="rwkv7",
    )(r, k, v, lo, *args)


def _mem_kv_kernel(m_ref, g_ref, wk_ref, wv_ref, kt_ref, v_ref):
    m = _rms(m_ref[0], g_ref[...]).astype(BF16)
    kt_ref[0] = jnp.dot(m, wk_ref[...], preferred_element_type=F32).T.astype(BF16)
    v_ref[0] = jnp.dot(m, wv_ref[...], preferred_element_type=F32).astype(BF16)


def _mem_kv(mem, g, wk, wv):
    b, mt, d = mem.shape
    return pl.pallas_call(
        _mem_kv_kernel,
        grid=(b,),
        in_specs=[pl.BlockSpec((1, mt, d), lambda bi: (bi, 0, 0)), _const_spec((1, d)),
                  _const_spec((d, d)), _const_spec((d, d))],
        out_specs=[pl.BlockSpec((1, d, mt), lambda bi: (bi, 0, 0)),
                   pl.BlockSpec((1, mt, d), lambda bi: (bi, 0, 0))],
        out_shape=[jax.ShapeDtypeStruct((b, d, mt), BF16), jax.ShapeDtypeStruct((b, mt, d), BF16)],
        compiler_params=_params(("arbitrary",)),
        name="mem_kv",
    )(mem, g.reshape(1, d), wk.astype(BF16), wv.astype(BF16))


def _out_mem_kernel(x_ref, on_ref, or_ref, ng_ref, wo1_ref, wo2_ref, mpost_ref,
                    mpre_ref, wq_ref, kt_ref, v_ref, wo_ref, mempost_ref, o_ref, *, parts):
    tm = x_ref.shape[1] // parts
    rows = [slice(n * tm, (n + 1) * tm) for n in range(parts)]
    dot = lambda a, b: jnp.dot(a, b, preferred_element_type=F32)
    d = x_ref.shape[-1]
    hd = d // MEM_HEADS

    a = [_rms(on_ref[0, rw, :], ng_ref[...]).astype(BF16) for rw in rows]
    mixed = [dot(a[n], wo1_ref[...]) + dot(or_ref[0, rw, :].astype(BF16), wo2_ref[...])
             for n, rw in enumerate(rows)]
    x = [x_ref[0, rw, :] + _rms(mixed[n], mpost_ref[...]) for n, rw in enumerate(rows)]
    h = [_rms(xn, mpre_ref[...]).astype(BF16) for xn in x]
    q = [(dot(hn, wq_ref[...]) * (hd ** -0.5)).astype(BF16) for hn in h]
    heads = [[] for _ in rows]
    for hi in range(MEM_HEADS):
        cols = slice(hi * hd, (hi + 1) * hd)
        s = [dot(qn[:, cols], kt_ref[0, cols, :]) for qn in q]
        e = [jnp.exp(sn - jnp.max(sn, axis=-1, keepdims=True)) for sn in s]
        p = [(en / jnp.sum(en, axis=-1, keepdims=True)).astype(BF16) for en in e]
        for n in range(parts):
            heads[n].append(dot(p[n], v_ref[0, :, cols]))
    att = [dot(jnp.concatenate(hn, axis=-1).astype(BF16), wo_ref[...]) for hn in heads]
    for n, rw in enumerate(rows):
        o_ref[0, rw, :] = x[n] + _rms(att[n], mempost_ref[...])


def _out_mem(x3, o_nsa, o_rwkv, nsa_g, w_out, mix_post_g, mem_pre_g, wq, kt, vm, wo, mem_post_g,
             *, tm=1024, parts=4):
    b, t, d = x3.shape
    mt = vm.shape[1]
    row = lambda bi, ti: (bi, ti, 0)
    per_b = lambda bi, ti: (bi, 0, 0)
    w_out = w_out.astype(BF16)
    return pl.pallas_call(
        functools.partial(_out_mem_kernel, parts=parts),
        grid=(b, t // tm),
        in_specs=[
            pl.BlockSpec((1, tm, d), row),
            pl.BlockSpec((1, tm, NSA_WIDTH), row),
            pl.BlockSpec((1, tm, RWKV_WIDTH), row),
            _const_spec((1, NSA_WIDTH)),
            _const_spec((NSA_WIDTH, d)),
            _const_spec((RWKV_WIDTH, d)),
            _const_spec((1, d)),
            _const_spec((1, d)),
            _const_spec((d, d)),
            pl.BlockSpec((1, d, mt), per_b),
            pl.BlockSpec((1, mt, d), per_b),
            _const_spec((d, d)),
            _const_spec((1, d)),
        ],
        out_specs=pl.BlockSpec((1, tm, d), row),
        out_shape=jax.ShapeDtypeStruct((b, t, d), F32),
        compiler_params=_params(("arbitrary", "arbitrary")),
        name="out_mem",
    )(x3, o_nsa, o_rwkv, nsa_g.reshape(1, -1), w_out[:NSA_WIDTH], w_out[NSA_WIDTH:],
      mix_post_g.reshape(1, d), mem_pre_g.reshape(1, d), wq.astype(BF16), kt, vm,
      wo.astype(BF16), mem_post_g.reshape(1, d))


def _rope_tables(pos):
    half = HEAD_DIM // 2
    inv = ROPE_THETA ** (-jnp.arange(half, dtype=F32) / half)
    ang = pos.astype(F32)[:, None] * inv[None, :]
    cos, sin = jnp.cos(ang), jnp.sin(ang)
    cos_t = jnp.concatenate([cos, cos, cos, cos], axis=-1)
    sin_t = jnp.concatenate([-sin, sin, -sin, sin], axis=-1)
    return cos_t, sin_t


def _overlap_t(ns, ncp):
    c0 = np.arange(ncp)[None, :] * CMP_STRIDE
    s0 = np.arange(ns)[:, None] * SEL_BLOCK
    ov = (c0 < s0 + SEL_BLOCK) & (c0 + CMP_BLOCK > s0) & (np.arange(ncp)[None, :] < ncp - 1)
    return jnp.asarray(ov.astype(np.float32), dtype=BF16)


def _pad_cols(w, n):
    return jnp.pad(w, ((0, 0), (0, n - w.shape[1])))


def _cmp_weights(pe, w1, w2):
    per = CMP_STRIDE
    w1r = w1.reshape(CMP_BLOCK, HEAD_DIM, CMP_HIDDEN)
    blocks = []
    for part in range(CMP_BLOCK // per):
        for g in range(NSA_GROUPS):
            z = jnp.zeros((per, NSA_GROUPS, HEAD_DIM, CMP_HIDDEN), F32)
            z = z.at[:, g].set(w1r[part * per:(part + 1) * per])
            blocks.append(z.reshape(per * NSA_GROUPS * HEAD_DIM, CMP_HIDDEN))
    w1cat = jnp.concatenate(blocks, axis=1).astype(BF16)
    w2bd = jnp.zeros((NSA_GROUPS * CMP_HIDDEN, NSA_GROUPS * HEAD_DIM), F32)
    for g in range(NSA_GROUPS):
        w2bd = w2bd.at[g * CMP_HIDDEN:(g + 1) * CMP_HIDDEN, g * HEAD_DIM:(g + 1) * HEAD_DIM].set(w2)
    pe8 = jnp.broadcast_to(pe.reshape(1, CMP_BLOCK * HEAD_DIM), (8, CMP_BLOCK * HEAD_DIM))
    return pe8.astype(BF16), w1.astype(BF16), w1cat, w2bd.astype(BF16)


def kernel(x, mem, ffn1_pre_g, ffn1_w_gate, ffn1_w_up, ffn1_w_down, ffn1_post_g, mix_pre_g, w_in, cmp_pe_k, cmp_w1_k, cmp_w2_k, cmp_pe_v, cmp_w1_v, cmp_w2_v, nsa_out_g, rwkv_mu, rwkv_w0, rwkv_w2, rwkv_a0, rwkv_a2, rwkv_g2, rwkv_k_k, rwkv_k_a, rwkv_r_k, rwkv_gn_g, rwkv_gn_b, w_out, mix_post_g, mem_pre_g, mem_kv_g, mem_wq, mem_wk, mem_wv, mem_wo, mem_post_g, ffn2_pre_g, ffn2_w_gate, ffn2_w_up, ffn2_w_down, ffn2_post_g):
    b, t, d = x.shape
    ncp = t // CMP_STRIDE
    ns = t // SEL_BLOCK
    cos_t, sin_t = _rope_tables(jnp.arange(t))
    cos_c, sin_c = _rope_tables(jnp.arange(ncp) * CMP_STRIDE + (CMP_BLOCK - 1))
    ovt = _overlap_t(ns, ncp)

    for l in range(ffn1_pre_g.shape[0]):
        x2 = _ffn_block(x.reshape(b * t, d), ffn1_pre_g[l], ffn1_w_gate[l], ffn1_w_up[l],
                        ffn1_w_down[l], ffn1_post_g[l])
        x3 = x2.reshape(b, t, d)

        wi = w_in[l]
        nsa_w = NSA_WIDTH + 6 * NSA_KV_WIDTH
        gate_w = wi[:, nsa_w:nsa_w + 3 * NSA_HEADS]
        w_cols = jnp.concatenate([wi[:, :nsa_w], _pad_cols(gate_w, LANES),
                                  wi[:, nsa_w + 3 * NSA_HEADS:]], axis=1).astype(BF16)
        (q, k_cmp, v_cmp, k_slc, v_slc_t, k_win, v_win_t, gates_t, r, k, v, lo) = _in_proj(
            x3, mix_pre_g[l], w_cols, rwkv_mu[l].reshape(1, -1), cos_t, sin_t)

        pek, w1k, w1kc, w2k = _cmp_weights(cmp_pe_k[l], cmp_w1_k[l], cmp_w2_k[l])
        pev, w1v, w1vc, w2v = _cmp_weights(cmp_pe_v[l], cmp_w1_v[l], cmp_w2_v[l])
        row_w = CMP_STRIDE * NSA_KV_WIDTH
        kc, vct = _compress(k_cmp.reshape(b, ncp, row_w), v_cmp.reshape(b, ncp, row_w),
                            pek, w1k, w1kc, w2k, pev, w1v, w1vc, w2v, cos_c, sin_c)
        o_nsa = _nsa_attn(q, gates_t, kc, vct, k_slc, v_slc_t, k_win, v_win_t, ovt)

        o_rwkv = _rwkv(r, k, v, lo, rwkv_w0[l], rwkv_w2[l], rwkv_a0[l], rwkv_a2[l], rwkv_g2[l],
                       rwkv_k_k[l], rwkv_k_a[l], rwkv_r_k[l], rwkv_gn_g[l], rwkv_gn_b[l])

        kt, vm = _mem_kv(mem, mem_kv_g[l], mem_wk[l], mem_wv[l])
        x4 = _out_mem(x3, o_nsa, o_rwkv, nsa_out_g[l], w_out[l], mix_post_g[l], mem_pre_g[l],
                      mem_wq[l], kt, vm, mem_wo[l], mem_post_g[l])

        x = _ffn_block(x4.reshape(b * t, d), ffn2_pre_g[l], ffn2_w_gate[l], ffn2_w_up[l],
                       ffn2_w_down[l], ffn2_post_g[l]).reshape(b, t, d)
    return x
```

```python
import functools

import numpy as np
import jax
import jax.numpy as jnp
from jax import lax
from jax.experimental import pallas as pl
from jax.experimental.pallas import tpu as pltpu

F32 = jnp.float32
BF16 = jnp.bfloat16

HEAD_DIM = 64
NSA_HEADS = 8
NSA_GROUPS = 2
NSA_GROUP_SIZE = 4
NSA_WIDTH = 512
NSA_KV_WIDTH = 128
CMP_BLOCK = 32
CMP_STRIDE = 16
CMP_HIDDEN = 256
SEL_BLOCK = 64
SEL_TOPK = 16
SEL_FORCE = 1e4
SEL_FORCED = 3
WINDOW = 512
RWKV_HEADS = 8
RWKV_WIDTH = 512
DECAY_LORA = 64
AAA_LORA = 64
GATE_LORA = 128
RWKV_GN_EPS = 64e-5
MEM_HEADS = 4
ROPE_THETA = 10000.0
NORM_EPS = 1e-6
NEG_INF = -1e30
LOG2_E = 1.4426950408889634

LANES = 128
KEY_BLOCK = 128
Q_TILE = 128
SEL_TRIP = 4
LOOP_TRIPS = 4
RWKV_CHUNK = 64
HALF = 256
VMEM_LIMIT = 56 * 1024 * 1024


def _bdot(a, b):
    return jnp.dot(a.astype(BF16), b.astype(BF16), preferred_element_type=F32)


def _split3(x):
    h1 = x.astype(BF16)
    r1 = x - h1.astype(F32)
    h2 = r1.astype(BF16)
    r2 = r1 - h2.astype(F32)
    return h1, h2, r2.astype(BF16)


def _split2(x):
    h1 = x.astype(BF16)
    return h1, (x - h1.astype(F32)).astype(BF16)


def _dot2_right(x, m):
    h1, h2 = _split2(x)
    return (jnp.dot(h1, m, preferred_element_type=F32) + jnp.dot(h2, m, preferred_element_type=F32))


def _dot3_right(x, m):
    h1, h2, h3 = _split3(x)
    d = lambda h: jnp.dot(h, m, preferred_element_type=F32)
    return d(h1) + d(h2) + d(h3)


def _dot2_left(m, x):
    h1, h2 = _split2(x)
    return (jnp.dot(m, h1, preferred_element_type=F32) + jnp.dot(m, h2, preferred_element_type=F32))


def _dot3_left(m, x):
    h1, h2, h3 = _split3(x)
    d = lambda h: jnp.dot(m, h, preferred_element_type=F32)
    return d(h1) + d(h2) + d(h3)


def _rms(x, g):
    return x * lax.rsqrt(jnp.mean(x * x, axis=-1, keepdims=True) + NORM_EPS) * g


def _silu(x):
    return x / (1.0 + jnp.exp(-x))


def _sigmoid(x):
    return 1.0 / (1.0 + jnp.exp(-x))


def _const_spec(shape):
    nd = len(shape)
    return pl.BlockSpec(shape, lambda *_: (0,) * nd)


def _params(sem):
    return pltpu.CompilerParams(dimension_semantics=sem, vmem_limit_bytes=VMEM_LIMIT)


def _ffn_kernel(x_ref, pre_ref, wg_ref, wu_ref, wd_ref, post_ref, o_ref, *, ff_chunk):
    x = x_ref[...]
    h = _rms(x, pre_ref[...]).astype(BF16)
    d_ff = wg_ref.shape[1]
    acc = jnp.zeros(x.shape, F32)
    for c0 in range(0, d_ff, ff_chunk):
        g = jnp.dot(h, wg_ref[:, c0:c0 + ff_chunk], preferred_element_type=F32)
        u = jnp.dot(h, wu_ref[:, c0:c0 + ff_chunk], preferred_element_type=F32)
        a = (_silu(g) * u).astype(BF16)
        acc = acc + jnp.dot(a, wd_ref[c0:c0 + ff_chunk, :], preferred_element_type=F32)
    o_ref[...] = x + 0.5 * _rms(acc, post_ref[...])


def _ffn_block(x2, pre_g, wg, wu, wd, post_g, *, tm=512, ff_chunk=256):
    m, d = x2.shape
    d_ff = wg.shape[1]
    return pl.pallas_call(
        functools.partial(_ffn_kernel, ff_chunk=ff_chunk),
        grid=(m // tm,),
        in_specs=[
            pl.BlockSpec((tm, d), lambda i: (i, 0)),
            _const_spec((1, d)),
            _const_spec((d, d_ff)),
            _const_spec((d, d_ff)),
            _const_spec((d_ff, d)),
            _const_spec((1, d)),
        ],
        out_specs=pl.BlockSpec((tm, d), lambda i: (i, 0)),
        out_shape=jax.ShapeDtypeStruct((m, d), F32),
        compiler_params=_params(("arbitrary",)),
        name="ffn_block",
    )(x2, pre_g.reshape(1, d), wg.astype(BF16), wu.astype(BF16), wd.astype(BF16),
      post_g.reshape(1, d))


def _swap_halves(x):
    n = x.shape[-1]
    lane = lax.broadcasted_iota(jnp.int32, x.shape, x.ndim - 1)
    fwd = pltpu.roll(x, n - HEAD_DIM // 2, x.ndim - 1)
    bwd = pltpu.roll(x, HEAD_DIM // 2, x.ndim - 1)
    return jnp.where((lane % HEAD_DIM) < HEAD_DIM // 2, fwd, bwd)


def _rope(x, cos, sin_signed):
    reps = x.shape[-1] // LANES
    c = jnp.concatenate([cos] * reps, axis=-1) if reps > 1 else cos
    s = jnp.concatenate([sin_signed] * reps, axis=-1) if reps > 1 else sin_signed
    return x * c + _swap_halves(x) * s


def _in_proj_kernel(x_ref, g_ref, w_ref, mu_ref, cos_ref, sin_ref,
                    q_ref, kc_ref, vc_ref, ks_ref, vst_ref, kw_ref, vwt_ref, gt_ref,
                    r_ref, k_ref, v_ref, lo_ref, carry_ref):
    @pl.when(pl.program_id(1) == 0)
    def _():
        carry_ref[...] = jnp.zeros_like(carry_ref)

    h = _rms(x_ref[0], g_ref[...]).astype(BF16)
    p = jnp.dot(h, w_ref[...], preferred_element_type=F32)
    cos = cos_ref[...]
    sin = sin_ref[...]
    tm = p.shape[0]

    o = 0
    q = _rope(p[:, o:o + NSA_WIDTH], cos, sin) * (HEAD_DIM ** -0.5 * LOG2_E)
    q_ref[0] = q.astype(BF16)
    o += NSA_WIDTH
    kc_ref[0] = p[:, o:o + LANES].astype(BF16); o += LANES
    vc_ref[0] = p[:, o:o + LANES].astype(BF16); o += LANES
    ks_ref[0] = _rope(p[:, o:o + LANES], cos, sin).astype(BF16); o += LANES
    vst_ref[0] = p[:, o:o + LANES].T.astype(BF16); o += LANES
    kw_ref[0] = _rope(p[:, o:o + LANES], cos, sin).astype(BF16); o += LANES
    vwt_ref[0] = p[:, o:o + LANES].T.astype(BF16); o += LANES
    gates_t = _sigmoid(p[:, o:o + LANES]).T
    gt_ref[0] = gates_t[:gt_ref.shape[1], :]
    o += LANES

    rw = p[:, o:]
    row = lax.broadcasted_iota(jnp.int32, rw.shape, 0)
    prev = jnp.where(row == 0, carry_ref[0:1, :], pltpu.roll(rw, 1, 0))
    carry_ref[...] = jnp.broadcast_to(rw[tm - 1:tm, :], carry_ref.shape)
    mixed = rw + (prev - rw) * mu_ref[...]
    r_ref[0] = mixed[:, 0:512]
    k_ref[0] = mixed[:, 512:1024]
    v_ref[0] = mixed[:, 1024:1536]
    lo_ref[0] = mixed[:, 1536:1792]


def _in_proj(x3, g, w_cols, mu_cols, cos_t, sin_t, *, tm=512):
    b, t, d = x3.shape
    n = w_cols.shape[1]
    row = lambda bi, ti: (bi, ti, 0)
    col = lambda bi, ti: (bi, 0, ti)
    out_shapes = [
        jax.ShapeDtypeStruct((b, t, NSA_WIDTH), BF16),
        jax.ShapeDtypeStruct((b, t, LANES), BF16),
        jax.ShapeDtypeStruct((b, t, LANES), BF16),
        jax.ShapeDtypeStruct((b, t, LANES), BF16),
        jax.ShapeDtypeStruct((b, LANES, t), BF16),
        jax.ShapeDtypeStruct((b, t, LANES), BF16),
        jax.ShapeDtypeStruct((b, LANES, t), BF16),
        jax.ShapeDtypeStruct((b, 32, t), F32),
        jax.ShapeDtypeStruct((b, t, RWKV_WIDTH), F32),
        jax.ShapeDtypeStruct((b, t, RWKV_WIDTH), F32),
        jax.ShapeDtypeStruct((b, t, RWKV_WIDTH), F32),
        jax.ShapeDtypeStruct((b, t, 256), F32),
    ]
    out_specs = [
        pl.BlockSpec((1, tm, NSA_WIDTH), row),
        pl.BlockSpec((1, tm, LANES), row),
        pl.BlockSpec((1, tm, LANES), row),
        pl.BlockSpec((1, tm, LANES), row),
        pl.BlockSpec((1, LANES, tm), col),
        pl.BlockSpec((1, tm, LANES), row),
        pl.BlockSpec((1, LANES, tm), col),
        pl.BlockSpec((1, 32, tm), col),
        pl.BlockSpec((1, tm, RWKV_WIDTH), row),
        pl.BlockSpec((1, tm, RWKV_WIDTH), row),
        pl.BlockSpec((1, tm, RWKV_WIDTH), row),
        pl.BlockSpec((1, tm, 256), row),
    ]
    return pl.pallas_call(
        _in_proj_kernel,
        grid=(b, t // tm),
        in_specs=[
            pl.BlockSpec((1, tm, d), row),
            _const_spec((1, d)),
            _const_spec((d, n)),
            _const_spec((1, 1792)),
            pl.BlockSpec((tm, LANES), lambda bi, ti: (ti, 0)),
            pl.BlockSpec((tm, LANES), lambda bi, ti: (ti, 0)),
        ],
        out_specs=out_specs,
        out_shape=out_shapes,
        scratch_shapes=[pltpu.VMEM((8, 1792), F32)],
        compiler_params=_params(("arbitrary", "arbitrary")),
        name="in_proj",
    )(x3, g.reshape(1, d), w_cols, mu_cols, cos_t, sin_t)


def _compress_kernel(kin_ref, vin_ref, pek_ref, w1k_ref, w1kc_ref, w2k_ref,
                     pev_ref, w1v_ref, w1vc_ref, w2v_ref, cos_ref, sin_ref,
                     kc_ref, vct_ref):
    def phi(rows, pe, w1, w1cat, w2bd):
        n = rows.shape[0]
        pr = jnp.dot(rows, w1cat, preferred_element_type=F32)
        bias = jnp.dot(pe, w1, preferred_element_type=F32)[0:1, :]
        hid = []
        for g in range(NSA_GROUPS):
            top = pr[:, g * CMP_HIDDEN:(g + 1) * CMP_HIDDEN]
            bot = pr[:, (NSA_GROUPS + g) * CMP_HIDDEN:(NSA_GROUPS + g + 1) * CMP_HIDDEN]
            hid.append(top + pltpu.roll(bot, n - 1, 0) + bias)
        act = _silu(jnp.concatenate(hid, axis=-1)).astype(BF16)
        return jnp.dot(act, w2bd, preferred_element_type=F32)

    kc = phi(kin_ref[0], pek_ref[...], w1k_ref[...], w1kc_ref[...], w2k_ref[...])
    kc_ref[0] = _rope(kc, cos_ref[...], sin_ref[...]).astype(BF16)
    vc = phi(vin_ref[0], pev_ref[...], w1v_ref[...], w1vc_ref[...], w2v_ref[...])
    vct_ref[0] = vc.T.astype(BF16)


def _compress(kin, vin, pek, w1k, w1kc, w2k, pev, w1v, w1vc, w2v, cos_c, sin_c):
    b, ncp, width = kin.shape
    blk = pl.BlockSpec((1, ncp, width), lambda bi: (bi, 0, 0))
    cs = lambda a: _const_spec(a.shape)
    return pl.pallas_call(
        _compress_kernel,
        grid=(b,),
        in_specs=[blk, blk, cs(pek), cs(w1k), cs(w1kc), cs(w2k),
                  cs(pev), cs(w1v), cs(w1vc), cs(w2v), cs(cos_c), cs(sin_c)],
        out_specs=[pl.BlockSpec((1, ncp, LANES), lambda bi: (bi, 0, 0)),
                   pl.BlockSpec((1, LANES, ncp), lambda bi: (bi, 0, 0))],
        out_shape=[jax.ShapeDtypeStruct((b, ncp, LANES), BF16),
                   jax.ShapeDtypeStruct((b, LANES, ncp), BF16)],
        compiler_params=_params(("arbitrary",)),
        name="nsa_compress",
    )(kin, vin, pek, w1k, w1kc, w2k, pev, w1v, w1vc, w2v, cos_c, sin_c)


def _nsa_kernel(q_ref, gt_ref, kc_ref, vct_ref, ks_ref, vst_ref, kw_ref, vwt_ref, ovt_ref, ind_ref,
                o_ref, bias_ref, sa_ref, sb_ref, acc_ref):
    i = pl.program_id(1)
    q0 = i * Q_TILE
    ncp = kc_ref.shape[1]
    ns = ovt_ref.shape[0]
    gw = NSA_GROUP_SIZE * Q_TILE
    width = NSA_GROUPS * gw
    lanes = [slice(g * gw, (g + 1) * gw) for g in range(NSA_GROUPS)]
    feat = [slice(g * HEAD_DIM, (g + 1) * HEAD_DIM) for g in range(NSA_GROUPS)]
    t_row = q0 + lax.broadcasted_iota(jnp.int32, (1, width), 1) % Q_TILE
    sel_keys = SEL_TRIP * KEY_BLOCK
    sel_rows = sel_keys // SEL_BLOCK
    win_keys = WINDOW + Q_TILE
    dot = lambda x, y: jnp.dot(x, y, preferred_element_type=F32)

    qf = q_ref[0].astype(F32)
    zeros_half = jnp.zeros((HEAD_DIM, Q_TILE), F32)
    parts = []
    for g in range(NSA_GROUPS):
        for pair in range(NSA_GROUP_SIZE // 2):
            slab_t = qf[:, (2 * g + pair) * LANES:(2 * g + pair + 1) * LANES].T
            for half in range(2):
                f = slab_t[half * HEAD_DIM:(half + 1) * HEAD_DIM, :]
                parts.append(jnp.concatenate([f, zeros_half] if g == 0 else [zeros_half, f], axis=0))
    qt = jnp.concatenate(parts, axis=1).astype(BF16)

    def v_ext(vt_ref, g, k0, n):
        return jnp.concatenate([vt_ref[0, feat[g], pl.ds(k0, n)], jnp.ones((16, n), BF16)], axis=0)

    sw_ = NSA_GROUPS * Q_TILE
    n_vis = (q0 + Q_TILE - CMP_BLOCK) // CMP_STRIDE + 1
    blocks_needed = jnp.maximum(n_vis + KEY_BLOCK - 1, KEY_BLOCK) // KEY_BLOCK
    def compressed(nb):
        rows = nb * KEY_BLOCK
        sc = dot(kc_ref[0, 0:rows, :], qt)
        c_end = lax.broadcasted_iota(jnp.int32, (rows, width), 0) * CMP_STRIDE + (CMP_BLOCK - 1)
        c_mask = c_end <= t_row
        sc = jnp.where(c_mask, sc, NEG_INF)
        m_c = jnp.max(sc, axis=0, keepdims=True)
        e_c = jnp.where(c_mask, jnp.exp2(sc - m_c), 0.0)
        p_c = e_c / jnp.maximum(jnp.sum(e_c, axis=0, keepdims=True), 1e-30)
        outs, p_sum = [], []
        for g in range(NSA_GROUPS):
            outs.append(dot(vct_ref[0, feat[g], 0:rows], p_c[:, lanes[g]].astype(BF16)))
            acc = p_c[:, g * gw:g * gw + Q_TILE]
            for r in range(1, NSA_GROUP_SIZE):
                acc = acc + p_c[:, g * gw + r * Q_TILE:g * gw + (r + 1) * Q_TILE]
            p_sum.append(acc)
        return outs + [_dot2_left(ovt_ref[:, 0:rows], jnp.concatenate(p_sum, axis=1))]

    *o_c, imp = lax.switch(blocks_needed - 1,
                           [functools.partial(compressed, nb) for nb in range(1, ncp // KEY_BLOCK + 1)])

    w0 = pl.multiple_of(jnp.maximum(q0 - WINDOW, 0), KEY_BLOCK)
    sw = dot(kw_ref[0, pl.ds(w0, win_keys), :], qt)
    t_loc = t_row - w0
    k_loc = lax.broadcasted_iota(jnp.int32, (win_keys, width), 0)
    ok = k_loc <= t_loc
    old = lax.broadcasted_iota(jnp.int32, (KEY_BLOCK, width), 0) <= t_loc - WINDOW
    sw = jnp.where(ok, sw, NEG_INF)
    sw = jnp.concatenate([jnp.where(old, NEG_INF, sw[0:KEY_BLOCK]), sw[KEY_BLOCK:]], axis=0)
    m_w = jnp.max(sw, axis=0, keepdims=True)
    o_w = []
    for g in range(NSA_GROUPS):
        p = jnp.exp2(sw[:, lanes[g]] - m_w[:, lanes[g]]).astype(BF16)
        pv = dot(v_ext(vwt_ref, g, w0, win_keys), p)
        o_w.append(pv[0:HEAD_DIM, :] / pv[HEAD_DIM:HEAD_DIM + 1, :])

    s_id = lax.broadcasted_iota(jnp.int32, (ns, sw_), 0)
    cur = t_row[:, 0:sw_] // SEL_BLOCK
    forced = (s_id == 0) | (s_id == cur) | (s_id == cur - 1)
    score = jnp.where(forced, -3e38, jnp.where(s_id <= cur, imp, -SEL_FORCE))
    bias = jnp.where(forced, 0.0, NEG_INF)
    for _ in range(min(SEL_TOPK, ns) - SEL_FORCED):
        mx = jnp.max(score, axis=0, keepdims=True)
        first = jnp.min(jnp.where(score == mx, s_id, ns), axis=0, keepdims=True)
        hit = s_id == first
        score = jnp.where(hit, -3e38, score)
        bias = jnp.where(hit, 0.0, bias)
    bias_ref[...] = bias

    def scores(j, s_ref):
        k0 = pl.multiple_of(j * sel_keys, sel_keys)
        b0 = pl.multiple_of(j * sel_rows, sel_rows)
        rows = bias_ref[pl.ds(b0, sel_rows), :]
        rows = jnp.concatenate([rows[:, g * Q_TILE:(g + 1) * Q_TILE]
                                for g in range(NSA_GROUPS) for _ in range(NSA_GROUP_SIZE)], axis=1)
        rows = jnp.concatenate([rows, jnp.zeros_like(rows)], axis=0).astype(BF16)
        rhs = jnp.concatenate([qt, rows, jnp.zeros((LANES - rows.shape[0], width), BF16)], axis=0)
        lhs = jnp.concatenate([ks_ref[0, pl.ds(k0, sel_keys), :], ind_ref[...]], axis=1)
        s = dot(lhs, rhs)
        s_ref[...] = s
        return jnp.max(s, axis=0, keepdims=True)

    def softmax_pv(j, s_ref, mb, m_run):
        k0 = pl.multiple_of(j * sel_keys, sel_keys)
        m_new = jnp.maximum(m_run, mb)
        alpha = jnp.exp2(m_run - m_new)
        for g in range(NSA_GROUPS):
            p = jnp.exp2(s_ref[:, lanes[g]] - m_new[:, lanes[g]]).astype(BF16)
            pv = dot(v_ext(vst_ref, g, k0, sel_keys), p)
            acc_ref[g] = acc_ref[g] * alpha[:, lanes[g]] + pv
        return m_new

    def causal_tail(j, s_ref, m_run):
        k0 = pl.multiple_of(j * sel_keys, sel_keys)
        key = k0 + lax.broadcasted_iota(jnp.int32, (sel_keys, width), 0)
        s = jnp.where(key <= t_row, s_ref[...], NEG_INF)
        s_ref[...] = s
        softmax_pv(j, s_ref, jnp.max(s, axis=0, keepdims=True), m_run)

    acc_ref[...] = jnp.zeros_like(acc_ref)
    n_full = q0 // sel_keys
    n_loops = n_full // LOOP_TRIPS
    bufs = (sa_ref, sb_ref)

    def full_trips(j0, count, m_run, mb):
        for t in range(count):
            mb_next = scores(j0 + t + 1, bufs[(t + 1) % 2])
            m_run = softmax_pv(j0 + t, bufs[t % 2], mb, m_run)
            mb = mb_next
        return m_run, mb

    m_run, mb_a = lax.fori_loop(
        0, n_loops, lambda kk, c: full_trips(LOOP_TRIPS * kk, LOOP_TRIPS, *c),
        (jnp.full((1, width), NEG_INF, F32), scores(0, sa_ref)))
    j_last = LOOP_TRIPS * n_loops

    for rem in range(LOOP_TRIPS):
        @pl.when(n_full - j_last == rem)
        def _(rem=rem):
            m_fin, _ = full_trips(j_last, rem, m_run, mb_a)
            causal_tail(j_last + rem, bufs[rem % 2], m_fin)

    o_s = [acc_ref[g, 0:HEAD_DIM, :] / acc_ref[g, HEAD_DIM:HEAD_DIM + 1, :] for g in range(NSA_GROUPS)]

    outs = []
    for g in range(NSA_GROUPS):
        heads = []
        for r in range(NSA_GROUP_SIZE):
            base = (g * NSA_GROUP_SIZE + r) * 3
            cols = slice(r * Q_TILE, (r + 1) * Q_TILE)
            heads.append(gt_ref[0, base:base + 1, :] * o_c[g][:, cols]
                         + gt_ref[0, base + 1:base + 2, :] * o_s[g][:, cols]
                         + gt_ref[0, base + 2:base + 3, :] * o_w[g][:, cols])
        for pair in range(NSA_GROUP_SIZE // 2):
            outs.append(jnp.concatenate(heads[2 * pair:2 * pair + 2], axis=0).T)
    o_ref[0] = jnp.concatenate(outs, axis=1)


def _nsa_attn(q, gates_t, kc, vct, ks, vst, kw, vwt, ovt):
    b, t, _ = q.shape
    ncp = kc.shape[1]
    ns = ovt.shape[0]
    sel_keys = SEL_TRIP * KEY_BLOCK
    assert t % sel_keys == 0 and sel_keys % Q_TILE == 0 and t >= WINDOW + Q_TILE and ncp % KEY_BLOCK == 0
    ind = (np.arange(sel_keys)[:, None] // SEL_BLOCK == np.arange(LANES)[None, :])
    ind = jnp.asarray(ind.astype(np.float32), dtype=BF16)
    full_rows = lambda bi, qi: (bi, 0, 0)
    return pl.pallas_call(
        _nsa_kernel,
        grid=(b, t // Q_TILE),
        in_specs=[
            pl.BlockSpec((1, Q_TILE, NSA_WIDTH), lambda bi, qi: (bi, qi, 0)),
            pl.BlockSpec((1, 32, Q_TILE), lambda bi, qi: (bi, 0, qi)),
            pl.BlockSpec((1, ncp, LANES), full_rows),
            pl.BlockSpec((1, LANES, ncp), full_rows),
            pl.BlockSpec((1, t, LANES), full_rows),
            pl.BlockSpec((1, LANES, t), full_rows),
            pl.BlockSpec((1, t, LANES), full_rows),
            pl.BlockSpec((1, LANES, t), full_rows),
            _const_spec((ns, ncp)),
            _const_spec((sel_keys, LANES)),
        ],
        out_specs=pl.BlockSpec((1, Q_TILE, NSA_WIDTH), lambda bi, qi: (bi, qi, 0)),
        out_shape=jax.ShapeDtypeStruct((b, t, NSA_WIDTH), F32),
        scratch_shapes=[pltpu.VMEM((ns, NSA_GROUPS * Q_TILE), F32),
                        pltpu.VMEM((sel_keys, NSA_GROUPS * NSA_GROUP_SIZE * Q_TILE), F32),
                        pltpu.VMEM((sel_keys, NSA_GROUPS * NSA_GROUP_SIZE * Q_TILE), F32),
                        pltpu.VMEM((NSA_GROUPS, HEAD_DIM + 16, NSA_GROUP_SIZE * Q_TILE), F32)],
        compiler_params=_params(("arbitrary", "arbitrary")),
        name="nsa_attn",
    )(q, gates_t, kc, vct, ks, vst, kw, vwt, ovt, ind)


def _same_head_mask():
    bi = lax.broadcasted_iota(jnp.int32, (HALF, HALF), 0) // HEAD_DIM
    bj = lax.broadcasted_iota(jnp.int32, (HALF, HALF), 1) // HEAD_DIM
    return bi == bj


def _bd_rows(x, same_head):
    xb = x.astype(BF16)
    tiled = jnp.concatenate([xb] * (HALF // x.shape[0]), axis=0)
    return jnp.where(same_head, tiled, jnp.zeros((), BF16))


def _rwkv_prep_stages(in_refs, par_refs, out_set, *, chunks):
    r_ref, k_ref, v_ref, lo_ref = in_refs
    w0_ref, w2_ref, a0_ref, a2_ref, g2_ref, kk_ref, ka_ref, rk_ref = par_refs
    wm_ref, zm_ref, arb_ref, rkv_ref, rt_ref, vb_ref, bkt_ref, gl_ref, bonus_ref, gate_ref = out_set
    c = RWKV_CHUNK
    nb = r_ref.shape[0]
    same_head = _same_head_mask()
    ones_bd = same_head.astype(BF16)
    n_tok = chunks * c
    row_i = lax.broadcasted_iota(jnp.int32, (n_tok, n_tok), 0)
    col_i = lax.broadcasted_iota(jnp.int32, (n_tok, n_tok), 1)
    tril_incl = ((row_i >= col_i) & (row_i // c == col_i // c)).astype(BF16)
    t_id = lax.broadcasted_iota(jnp.int32, (c, HALF), 0)
    j_id = lax.broadcasted_iota(jnp.int32, (c, HALF), 1) % HEAD_DIM
    strict_lower = t_id > j_id
    incl_lower = t_id >= j_id
    eye_all = (t_id == j_id).astype(F32)
    dot = lambda x, y: jnp.dot(x, y, preferred_element_type=F32)

    def bd_cols(xt):
        xb = xt.astype(BF16)
        return jnp.where(same_head, jnp.concatenate([xb, xb], axis=1), jnp.zeros((), BF16))

    halves = RWKV_WIDTH // HALF
    groups = [(b, ch) for b in range(nb) for ch in range(chunks)]
    chains = [(gi, hh) for gi in range(len(groups)) for hh in range(halves)]
    rows = [slice(ch * c, (ch + 1) * c) for ch in range(chunks)]
    lanes = [slice(hh * HALF, (hh + 1) * HALF) for hh in range(halves)]
    each = lambda fn: [fn(n, gi, lanes[hh]) for n, (gi, hh) in enumerate(chains)]

    lo = [lo_ref[b] for b in range(nb)]
    zs = [-(w0_ref[...] + _bdot(jnp.tanh(x[:, 0:DECAY_LORA]), w2_ref[...])) for x in lo]
    yield
    lr_b = [_sigmoid(a0_ref[...] + _bdot(x[:, DECAY_LORA:DECAY_LORA + AAA_LORA], a2_ref[...])) for x in lo]
    yield
    for b in range(nb):
        gate_ref[b] = _bdot(_sigmoid(lo[b][:, DECAY_LORA + AAA_LORA:]), g2_ref[...])
        vb_ref[b] = v_ref[b].astype(BF16)
    yield
    log_decay = []
    for z in zs:
        softplus = jnp.maximum(z, 0.0) + jnp.log(1.0 + jnp.exp(-jnp.abs(z)))
        log_decay.append(-jnp.exp(-softplus - 0.5))
    cum_b = [_dot3_left(tril_incl, x) for x in log_decay]
    yield
    r = [r_ref[b, rows[ch], :] for b, ch in groups]
    k = [k_ref[b, rows[ch], :] for b, ch in groups]
    v = [v_ref[b, rows[ch], :] for b, ch in groups]
    lr = [lr_b[b][rows[ch], :] for b, ch in groups]
    cum = [cum_b[b][rows[ch], :] for b, ch in groups]
    ld = [log_decay[b][rows[ch], :] for b, ch in groups]
    g_incl = [jnp.exp(x) for x in cum]
    g_excl = [jnp.exp(x - y) for x, y in zip(cum, ld)]
    g_inv = [jnp.exp(-x) for x in cum]
    for gi, (b, ch) in enumerate(groups):
        gl_ref[b, ch * 8:(ch + 1) * 8, :] = jnp.broadcast_to(g_incl[gi][c - 1:c, :], (8, RWKV_WIDTH))

    kk = each(lambda n, gi, ln: k[gi][:, ln] * kk_ref[:, ln])
    k2 = each(lambda n, gi, ln: k[gi][:, ln] * (1.0 + (lr[gi][:, ln] - 1.0) * ka_ref[:, ln]))
    sums = each(lambda n, gi, ln: _dot2_right(
        jnp.concatenate([kk[n] * kk[n], r[gi][:, ln] * k2[n] * rk_ref[:, ln]], axis=0), ones_bd))
    ssq = [x[0:c] for x in sums]
    for n, (gi, hh) in enumerate(chains):
        b, ch = groups[gi]
        bonus_ref[b, rows[ch], lanes[hh]] = sums[n][c:] * v[gi][:, lanes[hh]]
    yield
    kk = each(lambda n, gi, ln: kk[n] / jnp.maximum(jnp.sqrt(ssq[n]), 1e-12))
    at = each(lambda n, gi, ln: -kk[n] * g_excl[gi][:, ln])
    bt = each(lambda n, gi, ln: kk[n] * lr[gi][:, ln] * g_inv[gi][:, ln])
    kt = each(lambda n, gi, ln: k2[n] * g_inv[gi][:, ln])
    rt = each(lambda n, gi, ln: r[gi][:, ln] * g_incl[gi][:, ln])
    for n, (gi, hh) in enumerate(chains):
        b, ch = groups[gi]
        rt_ref[b, rows[ch], lanes[hh]] = rt[n].astype(BF16)
        bkt_ref[b, (ch * halves + hh) * HALF:(ch * halves + hh + 1) * HALF, :] = (
            jnp.concatenate([bt[n], kt[n]], axis=0).T.astype(BF16))

    bt_bd = each(lambda n, gi, ln: bd_cols(jnp.concatenate([bt[n], bt[n]], axis=0).T))
    ar = each(lambda n, gi, ln: jnp.concatenate([at[n], rt[n]], axis=0).astype(BF16))
    ab = each(lambda n, gi, ln: dot(ar[n], bt_bd[n]))
    yield
    kt_bd = each(lambda n, gi, ln: bd_cols(jnp.concatenate([kt[n], kt[n]], axis=0).T))
    ak = each(lambda n, gi, ln: dot(ar[n], kt_bd[n]))
    yield
    a_ab = [jnp.where(strict_lower, x[0:c], 0.0) for x in ab]
    a_rb = [jnp.where(incl_lower, x[c:], 0.0) for x in ab]
    a_ak = [jnp.where(strict_lower, x[0:c], 0.0) for x in ak]
    a_rk = [jnp.where(incl_lower, x[c:], 0.0) for x in ak]
    for n, (gi, hh) in enumerate(chains):
        b, ch = groups[gi]
        arb_ref[b, rows[ch], lanes[hh]] = a_rb[n].astype(BF16)

    inv = [eye_all + x for x in a_ab]
    pw = [dot(x.astype(BF16), _bd_rows(x, same_head)) for x in a_ab]
    yield
    for _ in range(4):
        both = [dot(jnp.concatenate([x, y], axis=0).astype(BF16), _bd_rows(x, same_head))
                for x, y in zip(pw, inv)]
        pw = [x[0:c] for x in both]
        inv = [y + x[c:] for x, y in zip(both, inv)]
        yield
    inv = [y + dot(y.astype(BF16), _bd_rows(x, same_head)) for x, y in zip(pw, inv)]
    yield

    v_bd = each(lambda n, gi, ln: _bd_rows(v[gi][:, ln], same_head))
    inv_b = [x.astype(BF16) for x in inv]
    wm = each(lambda n, gi, ln: dot(inv_b[n], _bd_rows(at[n], same_head)))
    yield
    av = each(lambda n, gi, ln: dot(jnp.concatenate([a_ak[n], a_rk[n]], axis=0).astype(BF16), v_bd[n]))
    yield
    zm = each(lambda n, gi, ln: dot(inv_b[n], _bd_rows(av[n][0:c], same_head)))
    for n, (gi, hh) in enumerate(chains):
        b, ch = groups[gi]
        wm_ref[b, rows[ch], lanes[hh]] = wm[n].astype(BF16)
        zm_ref[b, rows[ch], lanes[hh]] = zm[n]
        rkv_ref[b, rows[ch], lanes[hh]] = av[n][c:]


def _rwkv_scan_stages(in_set, gng_ref, gnb_ref, o_ref, s_ref, *, chunks):
    wm_ref, zm_ref, arb_ref, rkv_ref, rt_ref, vb_ref, bkt_ref, gl_ref, bonus_ref, gate_ref = in_set
    c = RWKV_CHUNK
    halves = RWKV_WIDTH // HALF
    same_head = _same_head_mask()
    ones_bd = same_head.astype(BF16)
    chains = [(b, hh) for b in range(o_ref.shape[0]) for hh in range(halves)]
    lanes = [slice(hh * HALF, (hh + 1) * HALF) for hh in range(halves)]
    dot = lambda x, y: jnp.dot(x, y, preferred_element_type=F32)

    for ch in range(chunks):
        rw = slice(ch * c, (ch + 1) * c)
        each = lambda fn: [fn(n, b, lanes[hh]) for n, (b, hh) in enumerate(chains)]
        s0 = [s_ref[b, hh] for b, hh in chains]
        s0b = [x.astype(BF16) for x in s0]
        u = each(lambda n, b, ln: dot(wm_ref[b, rw, ln], s0b[n]) + zm_ref[b, rw, ln])
        yield
        y0 = each(lambda n, b, ln: dot(rt_ref[b, rw, ln], s0b[n]) + rkv_ref[b, rw, ln])
        yield
        y = each(lambda n, b, ln: y0[n] + dot(arb_ref[b, rw, ln], _bd_rows(u[n], same_head)))
        yield
        uv = each(lambda n, b, ln: jnp.concatenate([u[n].astype(BF16), vb_ref[b, rw, ln]], axis=0))
        upd = [dot(bkt_ref[b, (ch * halves + hh) * HALF:(ch * halves + hh + 1) * HALF, :], uv[n])
               for n, (b, hh) in enumerate(chains)]
        for n, (b, hh) in enumerate(chains):
            g_last = jnp.broadcast_to(gl_ref[b, ch * 8:ch * 8 + 1, lanes[hh]], (LANES, HALF)).T
            g_col = jnp.concatenate([g_last, g_last], axis=1)
            s_ref[b, hh] = g_col * (s0[n] + jnp.where(same_head, upd[n], 0.0))
        yield
        mu = [_dot2_right(x, ones_bd) * (1.0 / HEAD_DIM) for x in y]
        yield
        yc = [x - m for x, m in zip(y, mu)]
        var = [_dot2_right(x * x, ones_bd) * (1.0 / HEAD_DIM) for x in yc]
        yield
        for n, (b, hh) in enumerate(chains):
            ln = lanes[hh]
            yn = yc[n] * lax.rsqrt(var[n] + RWKV_GN_EPS) * gng_ref[:, ln] + gnb_ref[:, ln]
            o_ref[b, rw, ln] = (yn + bonus_ref[b, rw, ln]) * gate_ref[b, rw, ln]


def _rwkv_kernel(*refs, chunks):
    in_refs, par_refs = refs[0:4], refs[4:12]
    gng_ref, gnb_ref, o_ref, s_ref = refs[12:16]
    sets = (refs[16:26], refs[26:36])
    step = pl.program_id(0)

    @pl.when(step == 0)
    def _():
        s_ref[...] = jnp.zeros_like(s_ref)
        for ref in sets[1]:
            ref[...] = jnp.zeros_like(ref)

    def run(write_set, read_set):
        prep = _rwkv_prep_stages(in_refs, par_refs, write_set, chunks=chunks)
        scan = _rwkv_scan_stages(read_set, gng_ref, gnb_ref, o_ref, s_ref, chunks=chunks)
        live = [prep, scan]
        while live:
            for gen in list(live):
                if next(gen, "done") == "done":
                    live.remove(gen)

    @pl.when(step % 2 == 0)
    def _():
        run(sets[0], sets[1])

    @pl.when(step % 2 == 1)
    def _():
        run(sets[1], sets[0])


def _rwkv(r, k, v, lo, w0, w2, a0, a2, g2, k_k, k_a, r_k, gn_g, gn_b, *, chunks=2):
    b, t, width = r.shape
    c = RWKV_CHUNK
    n_tok = chunks * c
    n_groups = t // n_tok
    halves = width // HALF
    vec = lambda a: a.reshape(1, width)
    cs = lambda a: _const_spec(a.shape)
    args = [vec(w0), w2.astype(BF16), vec(a0), a2.astype(BF16), g2.astype(BF16),
            vec(k_k), vec(k_a), vec(r_k), vec(gn_g), vec(gn_b)]
    ahead = lambda s: (0, jnp.minimum(s, n_groups - 1), 0)
    behind = lambda s: (0, jnp.maximum(s - 1, 0), 0)
    tok = lambda dt: pltpu.VMEM((b, n_tok, width), dt)
    scratch_set = [tok(BF16), tok(F32), tok(BF16), tok(F32), tok(BF16), tok(BF16),
                   pltpu.VMEM((b, chunks * halves * HALF, LANES), BF16),
                   pltpu.VMEM((b, chunks * 8, width), F32), tok(F32), tok(F32)]
    return pl.pallas_call(
        functools.partial(_rwkv_kernel, chunks=chunks),
        grid=(n_groups + 1,),
        in_specs=[pl.BlockSpec((b, n_tok, width), ahead)] * 3 + [pl.BlockSpec((b, n_tok, 256), ahead)]
                 + [cs(a) for a in args],
        out_specs=pl.BlockSpec((b, n_tok, width), behind),
        out_shape=jax.ShapeDtypeStruct((b, t, width), F32),
        scratch_shapes=[pltpu.VMEM((b, halves, HALF, HALF), F32)] + scratch_set + scratch_set,
        compiler_params=_params(("arbitrary",)),
        name="rwkv7",
    )(r, k, v, lo, *args)


def _mem_kv_kernel(m_ref, g_ref, wk_ref, wv_ref, kt_ref, v_ref):
    m = _rms(m_ref[0], g_ref[...]).astype(BF16)
    kt_ref[0] = jnp.dot(m, wk_ref[...], preferred_element_type=F32).T.astype(BF16)
    v_ref[0] = jnp.dot(m, wv_ref[...], preferred_element_type=F32).astype(BF16)


def _mem_kv(mem, g, wk, wv):
    b, mt, d = mem.shape
    return pl.pallas_call(
        _mem_kv_kernel,
        grid=(b,),
        in_specs=[pl.BlockSpec((1, mt, d), lambda bi: (bi, 0, 0)), _const_spec((1, d)),
                  _const_spec((d, d)), _const_spec((d, d))],
        out_specs=[pl.BlockSpec((1, d, mt), lambda bi: (bi, 0, 0)),
                   pl.BlockSpec((1, mt, d), lambda bi: (bi, 0, 0))],
        out_shape=[jax.ShapeDtypeStruct((b, d, mt), BF16), jax.ShapeDtypeStruct((b, mt, d), BF16)],
        compiler_params=_params(("arbitrary",)),
        name="mem_kv",
    )(mem, g.reshape(1, d), wk.astype(BF16), wv.astype(BF16))


def _out_mem_kernel(x_ref, on_ref, or_ref, ng_ref, wo1_ref, wo2_ref, mpost_ref,
                    mpre_ref, wq_ref, kt_ref, v_ref, wo_ref, mempost_ref, o_ref, *, parts):
    tm = x_ref.shape[1] // parts
    rows = [slice(n * tm, (n + 1) * tm) for n in range(parts)]
    dot = lambda a, b: jnp.dot(a, b, preferred_element_type=F32)
    d = x_ref.shape[-1]
    hd = d // MEM_HEADS

    a = [_rms(on_ref[0, rw, :], ng_ref[...]).astype(BF16) for rw in rows]
    mixed = [dot(a[n], wo1_ref[...]) + dot(or_ref[0, rw, :].astype(BF16), wo2_ref[...])
             for n, rw in enumerate(rows)]
    x = [x_ref[0, rw, :] + _rms(mixed[n], mpost_ref[...]) for n, rw in enumerate(rows)]
    h = [_rms(xn, mpre_ref[...]).astype(BF16) for xn in x]
    q = [(dot(hn, wq_ref[...]) * (hd ** -0.5)).astype(BF16) for hn in h]
    heads = [[] for _ in rows]
    for hi in range(MEM_HEADS):
        cols = slice(hi * hd, (hi + 1) * hd)
        s = [dot(qn[:, cols], kt_ref[0, cols, :]) for qn in q]
        e = [jnp.exp(sn - jnp.max(sn, axis=-1, keepdims=True)) for sn in s]
        p = [(en / jnp.sum(en, axis=-1, keepdims=True)).astype(BF16) for en in e]
        for n in range(parts):
            heads[n].append(dot(p[n], v_ref[0, :, cols]))
    att = [dot(jnp.concatenate(hn, axis=-1).astype(BF16), wo_ref[...]) for hn in heads]
    for n, rw in enumerate(rows):
        o_ref[0, rw, :] = x[n] + _rms(att[n], mempost_ref[...])


def _out_mem(x3, o_nsa, o_rwkv, nsa_g, w_out, mix_post_g, mem_pre_g, wq, kt, vm, wo, mem_post_g,
             *, tm=1024, parts=4):
    b, t, d = x3.shape
    mt = vm.shape[1]
    row = lambda bi, ti: (bi, ti, 0)
    per_b = lambda bi, ti: (bi, 0, 0)
    w_out = w_out.astype(BF16)
    return pl.pallas_call(
        functools.partial(_out_mem_kernel, parts=parts),
        grid=(b, t // tm),
        in_specs=[
            pl.BlockSpec((1, tm, d), row),
            pl.BlockSpec((1, tm, NSA_WIDTH), row),
            pl.BlockSpec((1, tm, RWKV_WIDTH), row),
            _const_spec((1, NSA_WIDTH)),
            _const_spec((NSA_WIDTH, d)),
            _const_spec((RWKV_WIDTH, d)),
            _const_spec((1, d)),
            _const_spec((1, d)),
            _const_spec((d, d)),
            pl.BlockSpec((1, d, mt), per_b),
            pl.BlockSpec((1, mt, d), per_b),
            _const_spec((d, d)),
            _const_spec((1, d)),
        ],
        out_specs=pl.BlockSpec((1, tm, d), row),
        out_shape=jax.ShapeDtypeStruct((b, t, d), F32),
        compiler_params=_params(("arbitrary", "arbitrary")),
        name="out_mem",
    )(x3, o_nsa, o_rwkv, nsa_g.reshape(1, -1), w_out[:NSA_WIDTH], w_out[NSA_WIDTH:],
      mix_post_g.reshape(1, d), mem_pre_g.reshape(1, d), wq.astype(BF16), kt, vm,
      wo.astype(BF16), mem_post_g.reshape(1, d))


def _rope_tables(pos):
    half = HEAD_DIM // 2
    inv = ROPE_THETA ** (-jnp.arange(half, dtype=F32) / half)
    ang = pos.astype(F32)[:, None] * inv[None, :]
    cos, sin = jnp.cos(ang), jnp.sin(ang)
    cos_t = jnp.concatenate([cos, cos, cos, cos], axis=-1)
    sin_t = jnp.concatenate([-sin, sin, -sin, sin], axis=-1)
    return cos_t, sin_t


def _overlap_t(ns, ncp):
    c0 = np.arange(ncp)[None, :] * CMP_STRIDE
    s0 = np.arange(ns)[:, None] * SEL_BLOCK
    ov = (c0 < s0 + SEL_BLOCK) & (c0 + CMP_BLOCK > s0) & (np.arange(ncp)[None, :] < ncp - 1)
    return jnp.asarray(ov.astype(np.float32), dtype=BF16)


def _pad_cols(w, n):
    return jnp.pad(w, ((0, 0), (0, n - w.shape[1])))


def _cmp_weights(pe, w1, w2):
    per = CMP_STRIDE
    eye = jnp.eye(NSA_GROUPS, dtype=F32)
    w1p = w1.reshape(CMP_BLOCK // per, per, HEAD_DIM, CMP_HIDDEN)
    w1cat = jnp.einsum('pldh,kg->lkdpgh', w1p, eye).reshape(
        per * NSA_GROUPS * HEAD_DIM, (CMP_BLOCK // per) * NSA_GROUPS * CMP_HIDDEN).astype(BF16)
    w2bd = jnp.kron(eye, w2)
    pe8 = jnp.broadcast_to(pe.reshape(1, CMP_BLOCK * HEAD_DIM), (8, CMP_BLOCK * HEAD_DIM))
    return pe8.astype(BF16), w1.astype(BF16), w1cat, w2bd.astype(BF16)


def kernel(x, mem, ffn1_pre_g, ffn1_w_gate, ffn1_w_up, ffn1_w_down, ffn1_post_g, mix_pre_g, w_in, cmp_pe_k, cmp_w1_k, cmp_w2_k, cmp_pe_v, cmp_w1_v, cmp_w2_v, nsa_out_g, rwkv_mu, rwkv_w0, rwkv_w2, rwkv_a0, rwkv_a2, rwkv_g2, rwkv_k_k, rwkv_k_a, rwkv_r_k, rwkv_gn_g, rwkv_gn_b, w_out, mix_post_g, mem_pre_g, mem_kv_g, mem_wq, mem_wk, mem_wv, mem_wo, mem_post_g, ffn2_pre_g, ffn2_w_gate, ffn2_w_up, ffn2_w_down, ffn2_post_g):
    b, t, d = x.shape
    ncp = t // CMP_STRIDE
    ns = t // SEL_BLOCK
    cos_t, sin_t = _rope_tables(jnp.arange(t))
    cos_c, sin_c = _rope_tables(jnp.arange(ncp) * CMP_STRIDE + (CMP_BLOCK - 1))
    ovt = _overlap_t(ns, ncp)

    for l in range(ffn1_pre_g.shape[0]):
        x2 = _ffn_block(x.reshape(b * t, d), ffn1_pre_g[l], ffn1_w_gate[l], ffn1_w_up[l],
                        ffn1_w_down[l], ffn1_post_g[l])
        x3 = x2.reshape(b, t, d)

        wi = w_in[l]
        nsa_w = NSA_WIDTH + 6 * NSA_KV_WIDTH
        gate_w = wi[:, nsa_w:nsa_w + 3 * NSA_HEADS]
        w_cols = jnp.concatenate([wi[:, :nsa_w], _pad_cols(gate_w, LANES),
                                  wi[:, nsa_w + 3 * NSA_HEADS:]], axis=1).astype(BF16)
        (q, k_cmp, v_cmp, k_slc, v_slc_t, k_win, v_win_t, gates_t, r, k, v, lo) = _in_proj(
            x3, mix_pre_g[l], w_cols, rwkv_mu[l].reshape(1, -1), cos_t, sin_t)

        pek, w1k, w1kc, w2k = _cmp_weights(cmp_pe_k[l], cmp_w1_k[l], cmp_w2_k[l])
        pev, w1v, w1vc, w2v = _cmp_weights(cmp_pe_v[l], cmp_w1_v[l], cmp_w2_v[l])
        row_w = CMP_STRIDE * NSA_KV_WIDTH
        kc, vct = _compress(k_cmp.reshape(b, ncp, row_w), v_cmp.reshape(b, ncp, row_w),
                            pek, w1k, w1kc, w2k, pev, w1v, w1vc, w2v, cos_c, sin_c)
        o_nsa = _nsa_attn(q, gates_t, kc, vct, k_slc, v_slc_t, k_win, v_win_t, ovt)

        o_rwkv = _rwkv(r, k, v, lo, rwkv_w0[l], rwkv_w2[l], rwkv_a0[l], rwkv_a2[l], rwkv_g2[l],
                       rwkv_k_k[l], rwkv_k_a[l], rwkv_r_k[l], rwkv_gn_g[l], rwkv_gn_b[l])

        kt, vm = _mem_kv(mem, mem_kv_g[l], mem_wk[l], mem_wv[l])
        x4 = _out_mem(x3, o_nsa, o_rwkv, nsa_out_g[l], w_out[l], mix_post_g[l], mem_pre_g[l],
                      mem_wq[l], kt, vm, mem_wo[l], mem_post_g[l])

        x = _ffn_block(x4.reshape(b * t, d), ffn2_pre_g[l], ffn2_w_gate[l], ffn2_w_up[l],
                       ffn2_w_down[l], ffn2_post_g[l]).reshape(b, t, d)
    return x
```

```python
import functools

import numpy as np
import jax
import jax.numpy as jnp
from jax import lax
from jax.experimental import pallas as pl
from jax.experimental.pallas import tpu as pltpu

F32 = jnp.float32
BF16 = jnp.bfloat16

HEAD_DIM = 64
NSA_HEADS = 8
NSA_GROUPS = 2
NSA_GROUP_SIZE = 4
NSA_WIDTH = 512
NSA_KV_WIDTH = 128
CMP_BLOCK = 32
CMP_STRIDE = 16
CMP_HIDDEN = 256
SEL_BLOCK = 64
SEL_TOPK = 16
SEL_FORCE = 1e4
SEL_FORCED = 3
WINDOW = 512
RWKV_HEADS = 8
RWKV_WIDTH = 512
DECAY_LORA = 64
AAA_LORA = 64
GATE_LORA = 128
RWKV_GN_EPS = 64e-5
MEM_HEADS = 4
ROPE_THETA = 10000.0
NORM_EPS = 1e-6
NEG_INF = -1e30
LOG2_E = 1.4426950408889634

LANES = 128
KEY_BLOCK = 128
Q_TILE = 128
SEL_TRIP = 4
LOOP_TRIPS = 4
RWKV_CHUNK = 64
HALF = 256
VMEM_LIMIT = 56 * 1024 * 1024


def _bdot(a, b):
    return jnp.dot(a.astype(BF16), b.astype(BF16), preferred_element_type=F32)


def _split3(x):
    h1 = x.astype(BF16)
    r1 = x - h1.astype(F32)
    h2 = r1.astype(BF16)
    r2 = r1 - h2.astype(F32)
    return h1, h2, r2.astype(BF16)


def _split2(x):
    h1 = x.astype(BF16)
    return h1, (x - h1.astype(F32)).astype(BF16)


def _dot2_right(x, m):
    h1, h2 = _split2(x)
    return (jnp.dot(h1, m, preferred_element_type=F32) + jnp.dot(h2, m, preferred_element_type=F32))


def _dot3_right(x, m):
    h1, h2, h3 = _split3(x)
    d = lambda h: jnp.dot(h, m, preferred_element_type=F32)
    return d(h1) + d(h2) + d(h3)


def _dot2_left(m, x):
    h1, h2 = _split2(x)
    return (jnp.dot(m, h1, preferred_element_type=F32) + jnp.dot(m, h2, preferred_element_type=F32))


def _dot3_left(m, x):
    h1, h2, h3 = _split3(x)
    d = lambda h: jnp.dot(m, h, preferred_element_type=F32)
    return d(h1) + d(h2) + d(h3)


def _rms(x, g):
    return x * lax.rsqrt(jnp.mean(x * x, axis=-1, keepdims=True) + NORM_EPS) * g


def _silu(x):
    return x / (1.0 + jnp.exp(-x))


def _sigmoid(x):
    return 1.0 / (1.0 + jnp.exp(-x))


def _const_spec(shape):
    nd = len(shape)
    return pl.BlockSpec(shape, lambda *_: (0,) * nd)


def _params(sem):
    return pltpu.CompilerParams(dimension_semantics=sem, vmem_limit_bytes=VMEM_LIMIT)


def _ffn_kernel(x_ref, pre_ref, wg_ref, wu_ref, wd_ref, post_ref, o_ref, *, ff_chunk):
    x = x_ref[...]
    h = _rms(x, pre_ref[...]).astype(BF16)
    d_ff = wg_ref.shape[1]
    acc = jnp.zeros(x.shape, F32)
    for c0 in range(0, d_ff, ff_chunk):
        g = jnp.dot(h, wg_ref[:, c0:c0 + ff_chunk], preferred_element_type=F32)
        u = jnp.dot(h, wu_ref[:, c0:c0 + ff_chunk], preferred_element_type=F32)
        a = (_silu(g) * u).astype(BF16)
        acc = acc + jnp.dot(a, wd_ref[c0:c0 + ff_chunk, :], preferred_element_type=F32)
    o_ref[...] = x + 0.5 * _rms(acc, post_ref[...])


def _ffn_block(x2, pre_g, wg, wu, wd, post_g, *, tm=512, ff_chunk=256):
    m, d = x2.shape
    d_ff = wg.shape[1]
    return pl.pallas_call(
        functools.partial(_ffn_kernel, ff_chunk=ff_chunk),
        grid=(m // tm,),
        in_specs=[
            pl.BlockSpec((tm, d), lambda i: (i, 0)),
            _const_spec((1, d)),
            _const_spec((d, d_ff)),
            _const_spec((d, d_ff)),
            _const_spec((d_ff, d)),
            _const_spec((1, d)),
        ],
        out_specs=pl.BlockSpec((tm, d), lambda i: (i, 0)),
        out_shape=jax.ShapeDtypeStruct((m, d), F32),
        compiler_params=_params(("arbitrary",)),
        name="ffn_block",
    )(x2, pre_g.reshape(1, d), wg.astype(BF16), wu.astype(BF16), wd.astype(BF16),
      post_g.reshape(1, d))


def _swap_halves(x):
    n = x.shape[-1]
    lane = lax.broadcasted_iota(jnp.int32, x.shape, x.ndim - 1)
    fwd = pltpu.roll(x, n - HEAD_DIM // 2, x.ndim - 1)
    bwd = pltpu.roll(x, HEAD_DIM // 2, x.ndim - 1)
    return jnp.where((lane % HEAD_DIM) < HEAD_DIM // 2, fwd, bwd)


def _rope(x, cos, sin_signed):
    reps = x.shape[-1] // LANES
    c = jnp.concatenate([cos] * reps, axis=-1) if reps > 1 else cos
    s = jnp.concatenate([sin_signed] * reps, axis=-1) if reps > 1 else sin_signed
    return x * c + _swap_halves(x) * s


def _in_proj_kernel(x_ref, g_ref, w_ref, mu_ref, cos_ref, sin_ref,
                    q_ref, kc_ref, vc_ref, ks_ref, vst_ref, kw_ref, vwt_ref, gt_ref,
                    r_ref, k_ref, v_ref, lo_ref, carry_ref):
    @pl.when(pl.program_id(1) == 0)
    def _():
        carry_ref[...] = jnp.zeros_like(carry_ref)

    h = _rms(x_ref[0], g_ref[...]).astype(BF16)
    p = jnp.dot(h, w_ref[...], preferred_element_type=F32)
    cos = cos_ref[...]
    sin = sin_ref[...]
    tm = p.shape[0]

    o = 0
    q = _rope(p[:, o:o + NSA_WIDTH], cos, sin) * (HEAD_DIM ** -0.5 * LOG2_E)
    q_ref[0] = q.astype(BF16)
    o += NSA_WIDTH
    kc_ref[0] = p[:, o:o + LANES]; o += LANES
    vc_ref[0] = p[:, o:o + LANES]; o += LANES
    ks_ref[0] = _rope(p[:, o:o + LANES], cos, sin).astype(BF16); o += LANES
    vst_ref[0] = p[:, o:o + LANES].T.astype(BF16); o += LANES
    kw_ref[0] = _rope(p[:, o:o + LANES], cos, sin).astype(BF16); o += LANES
    vwt_ref[0] = p[:, o:o + LANES].T.astype(BF16); o += LANES
    gates_t = _sigmoid(p[:, o:o + LANES]).T
    gt_ref[0] = gates_t[:gt_ref.shape[1], :]
    o += LANES

    rw = p[:, o:]
    row = lax.broadcasted_iota(jnp.int32, rw.shape, 0)
    prev = jnp.where(row == 0, carry_ref[0:1, :], pltpu.roll(rw, 1, 0))
    carry_ref[...] = jnp.broadcast_to(rw[tm - 1:tm, :], carry_ref.shape)
    mixed = rw + (prev - rw) * mu_ref[...]
    r_ref[0] = mixed[:, 0:512]
    k_ref[0] = mixed[:, 512:1024]
    v_ref[0] = mixed[:, 1024:1536]
    lo_ref[0] = mixed[:, 1536:1792]


def _in_proj(x3, g, w_cols, mu_cols, cos_t, sin_t, *, tm=512):
    b, t, d = x3.shape
    n = w_cols.shape[1]
    row = lambda bi, ti: (bi, ti, 0)
    col = lambda bi, ti: (bi, 0, ti)
    out_shapes = [
        jax.ShapeDtypeStruct((b, t, NSA_WIDTH), BF16),
        jax.ShapeDtypeStruct((b, t, LANES), F32),
        jax.ShapeDtypeStruct((b, t, LANES), F32),
        jax.ShapeDtypeStruct((b, t, LANES), BF16),
        jax.ShapeDtypeStruct((b, LANES, t), BF16),
        jax.ShapeDtypeStruct((b, t, LANES), BF16),
        jax.ShapeDtypeStruct((b, LANES, t), BF16),
        jax.ShapeDtypeStruct((b, 32, t), F32),
        jax.ShapeDtypeStruct((b, t, RWKV_WIDTH), F32),
        jax.ShapeDtypeStruct((b, t, RWKV_WIDTH), F32),
        jax.ShapeDtypeStruct((b, t, RWKV_WIDTH), F32),
        jax.ShapeDtypeStruct((b, t, 256), F32),
    ]
    out_specs = [
        pl.BlockSpec((1, tm, NSA_WIDTH), row),
        pl.BlockSpec((1, tm, LANES), row),
        pl.BlockSpec((1, tm, LANES), row),
        pl.BlockSpec((1, tm, LANES), row),
        pl.BlockSpec((1, LANES, tm), col),
        pl.BlockSpec((1, tm, LANES), row),
        pl.BlockSpec((1, LANES, tm), col),
        pl.BlockSpec((1, 32, tm), col),
        pl.BlockSpec((1, tm, RWKV_WIDTH), row),
        pl.BlockSpec((1, tm, RWKV_WIDTH), row),
        pl.BlockSpec((1, tm, RWKV_WIDTH), row),
        pl.BlockSpec((1, tm, 256), row),
    ]
    return pl.pallas_call(
        _in_proj_kernel,
        grid=(b, t // tm),
        in_specs=[
            pl.BlockSpec((1, tm, d), row),
            _const_spec((1, d)),
            _const_spec((d, n)),
            _const_spec((1, 1792)),
            pl.BlockSpec((tm, LANES), lambda bi, ti: (ti, 0)),
            pl.BlockSpec((tm, LANES), lambda bi, ti: (ti, 0)),
        ],
        out_specs=out_specs,
        out_shape=out_shapes,
        scratch_shapes=[pltpu.VMEM((8, 1792), F32)],
        compiler_params=_params(("arbitrary", "arbitrary")),
        name="in_proj",
    )(x3, g.reshape(1, d), w_cols, mu_cols, cos_t, sin_t)


def _compress_kernel(kin_ref, vin_ref, pek_ref, w1k_ref, w1kc_ref, w2k_ref,
                     pev_ref, w1v_ref, w1vc_ref, w2v_ref, cos_ref, sin_ref,
                     kc_ref, vct_ref):
    def phi(tok_ref, pe, w1, w1cat_ref, w2bd):
        n = tok_ref.shape[1] // CMP_STRIDE
        pr = None
        for l in range(CMP_STRIDE):
            rows = tok_ref[0, pl.ds(l, n, stride=CMP_STRIDE), :].astype(BF16)
            term = jnp.dot(rows, w1cat_ref[l * LANES:(l + 1) * LANES, :], preferred_element_type=F32)
            pr = term if pr is None else pr + term
        bias = jnp.dot(pe, w1, preferred_element_type=F32)[0:1, :]
        hid = []
        for g in range(NSA_GROUPS):
            top = pr[:, g * CMP_HIDDEN:(g + 1) * CMP_HIDDEN]
            bot = pr[:, (NSA_GROUPS + g) * CMP_HIDDEN:(NSA_GROUPS + g + 1) * CMP_HIDDEN]
            hid.append(top + pltpu.roll(bot, n - 1, 0) + bias)
        act = _silu(jnp.concatenate(hid, axis=-1)).astype(BF16)
        return jnp.dot(act, w2bd, preferred_element_type=F32)

    kc = phi(kin_ref, pek_ref[...], w1k_ref[...], w1kc_ref, w2k_ref[...])
    kc_ref[0] = _rope(kc, cos_ref[...], sin_ref[...]).astype(BF16)
    vc = phi(vin_ref, pev_ref[...], w1v_ref[...], w1vc_ref, w2v_ref[...])
    vct_ref[0] = vc.T.astype(BF16)


def _compress(kin, vin, pek, w1k, w1kc, w2k, pev, w1v, w1vc, w2v, cos_c, sin_c):
    b, t, width = kin.shape
    ncp = t // CMP_STRIDE
    blk = pl.BlockSpec((1, t, width), lambda bi: (bi, 0, 0))
    cs = lambda a: _const_spec(a.shape)
    return pl.pallas_call(
        _compress_kernel,
        grid=(b,),
        in_specs=[blk, blk, cs(pek), cs(w1k), cs(w1kc), cs(w2k),
                  cs(pev), cs(w1v), cs(w1vc), cs(w2v), cs(cos_c), cs(sin_c)],
        out_specs=[pl.BlockSpec((1, ncp, LANES), lambda bi: (bi, 0, 0)),
                   pl.BlockSpec((1, LANES, ncp), lambda bi: (bi, 0, 0))],
        out_shape=[jax.ShapeDtypeStruct((b, ncp, LANES), BF16),
                   jax.ShapeDtypeStruct((b, LANES, ncp), BF16)],
        compiler_params=_params(("arbitrary",)),
        name="nsa_compress",
    )(kin, vin, pek, w1k, w1kc, w2k, pev, w1v, w1vc, w2v, cos_c, sin_c)


def _nsa_kernel(q_ref, gt_ref, kc_ref, vct_ref, ks_ref, vst_ref, kw_ref, vwt_ref, ovt_ref, ind_ref,
                o_ref, bias_ref, sa_ref, sb_ref, acc_ref):
    i = pl.program_id(1)
    q0 = i * Q_TILE
    ncp = kc_ref.shape[1]
    ns = ovt_ref.shape[0]
    gw = NSA_GROUP_SIZE * Q_TILE
    width = NSA_GROUPS * gw
    lanes = [slice(g * gw, (g + 1) * gw) for g in range(NSA_GROUPS)]
    feat = [slice(g * HEAD_DIM, (g + 1) * HEAD_DIM) for g in range(NSA_GROUPS)]
    t_row = q0 + lax.broadcasted_iota(jnp.int32, (1, width), 1) % Q_TILE
    sel_keys = SEL_TRIP * KEY_BLOCK
    sel_rows = sel_keys // SEL_BLOCK
    win_keys = WINDOW + Q_TILE
    dot = lambda x, y: jnp.dot(x, y, preferred_element_type=F32)

    qf = q_ref[0].astype(F32)
    zeros_half = jnp.zeros((HEAD_DIM, Q_TILE), F32)
    parts = []
    for g in range(NSA_GROUPS):
        for pair in range(NSA_GROUP_SIZE // 2):
            slab_t = qf[:, (2 * g + pair) * LANES:(2 * g + pair + 1) * LANES].T
            for half in range(2):
                f = slab_t[half * HEAD_DIM:(half + 1) * HEAD_DIM, :]
                parts.append(jnp.concatenate([f, zeros_half] if g == 0 else [zeros_half, f], axis=0))
    qt = jnp.concatenate(parts, axis=1).astype(BF16)

    def v_ext(vt_ref, g, k0, n):
        return jnp.concatenate([vt_ref[0, feat[g], pl.ds(k0, n)], jnp.ones((16, n), BF16)], axis=0)

    sw_ = NSA_GROUPS * Q_TILE
    n_vis = (q0 + Q_TILE - CMP_BLOCK) // CMP_STRIDE + 1
    blocks_needed = jnp.maximum(n_vis + KEY_BLOCK - 1, KEY_BLOCK) // KEY_BLOCK
    def compressed(nb):
        rows = nb * KEY_BLOCK
        sc = dot(kc_ref[0, 0:rows, :], qt)
        c_end = lax.broadcasted_iota(jnp.int32, (rows, width), 0) * CMP_STRIDE + (CMP_BLOCK - 1)
        c_mask = c_end <= t_row
        sc = jnp.where(c_mask, sc, NEG_INF)
        m_c = jnp.max(sc, axis=0, keepdims=True)
        e_c = jnp.where(c_mask, jnp.exp2(sc - m_c), 0.0)
        p_c = e_c / jnp.maximum(jnp.sum(e_c, axis=0, keepdims=True), 1e-30)
        outs, p_sum = [], []
        for g in range(NSA_GROUPS):
            outs.append(dot(vct_ref[0, feat[g], 0:rows], p_c[:, lanes[g]].astype(BF16)))
            acc = p_c[:, g * gw:g * gw + Q_TILE]
            for r in range(1, NSA_GROUP_SIZE):
                acc = acc + p_c[:, g * gw + r * Q_TILE:g * gw + (r + 1) * Q_TILE]
            p_sum.append(acc)
        return outs + [_dot2_left(ovt_ref[:, 0:rows], jnp.concatenate(p_sum, axis=1))]

    *o_c, imp = lax.switch(blocks_needed - 1,
                           [functools.partial(compressed, nb) for nb in range(1, ncp // KEY_BLOCK + 1)])

    w0 = pl.multiple_of(jnp.maximum(q0 - WINDOW, 0), KEY_BLOCK)
    sw = dot(kw_ref[0, pl.ds(w0, win_keys), :], qt)
    t_loc = t_row - w0
    k_loc = lax.broadcasted_iota(jnp.int32, (win_keys, width), 0)
    ok = k_loc <= t_loc
    old = lax.broadcasted_iota(jnp.int32, (KEY_BLOCK, width), 0) <= t_loc - WINDOW
    sw = jnp.where(ok, sw, NEG_INF)
    sw = jnp.concatenate([jnp.where(old, NEG_INF, sw[0:KEY_BLOCK]), sw[KEY_BLOCK:]], axis=0)
    m_w = jnp.max(sw, axis=0, keepdims=True)
    o_w = []
    for g in range(NSA_GROUPS):
        p = jnp.exp2(sw[:, lanes[g]] - m_w[:, lanes[g]]).astype(BF16)
        pv = dot(v_ext(vwt_ref, g, w0, win_keys), p)
        o_w.append(pv[0:HEAD_DIM, :] / pv[HEAD_DIM:HEAD_DIM + 1, :])

    s_id = lax.broadcasted_iota(jnp.int32, (ns, sw_), 0)
    cur = t_row[:, 0:sw_] // SEL_BLOCK
    forced = (s_id == 0) | (s_id == cur) | (s_id == cur - 1)
    score = jnp.where(forced, -3e38, jnp.where(s_id <= cur, imp, -SEL_FORCE))
    bias = jnp.where(forced, 0.0, NEG_INF)
    for _ in range(min(SEL_TOPK, ns) - SEL_FORCED):
        mx = jnp.max(score, axis=0, keepdims=True)
        first = jnp.min(jnp.where(score == mx, s_id, ns), axis=0, keepdims=True)
        hit = s_id == first
        score = jnp.where(hit, -3e38, score)
        bias = jnp.where(hit, 0.0, bias)
    bias_ref[...] = bias

    def scores(j, s_ref):
        k0 = pl.multiple_of(j * sel_keys, sel_keys)
        b0 = pl.multiple_of(j * sel_rows, sel_rows)
        rows = bias_ref[pl.ds(b0, sel_rows), :]
        rows = jnp.concatenate([rows[:, g * Q_TILE:(g + 1) * Q_TILE]
                                for g in range(NSA_GROUPS) for _ in range(NSA_GROUP_SIZE)], axis=1)
        rows = jnp.concatenate([rows, jnp.zeros_like(rows)], axis=0).astype(BF16)
        rhs = jnp.concatenate([qt, rows, jnp.zeros((LANES - rows.shape[0], width), BF16)], axis=0)
        lhs = jnp.concatenate([ks_ref[0, pl.ds(k0, sel_keys), :], ind_ref[...]], axis=1)
        s = dot(lhs, rhs)
        s_ref[...] = s
        return jnp.max(s, axis=0, keepdims=True)

    def softmax_pv(j, s_ref, mb, m_run):
        k0 = pl.multiple_of(j * sel_keys, sel_keys)
        m_new = jnp.maximum(m_run, mb)
        alpha = jnp.exp2(m_run - m_new)
        for g in range(NSA_GROUPS):
            p = jnp.exp2(s_ref[:, lanes[g]] - m_new[:, lanes[g]]).astype(BF16)
            pv = dot(v_ext(vst_ref, g, k0, sel_keys), p)
            acc_ref[g] = acc_ref[g] * alpha[:, lanes[g]] + pv
        return m_new

    def causal_tail(j, s_ref, m_run):
        k0 = pl.multiple_of(j * sel_keys, sel_keys)
        key = k0 + lax.broadcasted_iota(jnp.int32, (sel_keys, width), 0)
        s = jnp.where(key <= t_row, s_ref[...], NEG_INF)
        s_ref[...] = s
        softmax_pv(j, s_ref, jnp.max(s, axis=0, keepdims=True), m_run)

    acc_ref[...] = jnp.zeros_like(acc_ref)
    n_full = q0 // sel_keys
    n_loops = n_full // LOOP_TRIPS
    bufs = (sa_ref, sb_ref)

    def full_trips(j0, count, m_run, mb):
        for t in range(count):
            mb_next = scores(j0 + t + 1, bufs[(t + 1) % 2])
            m_run = softmax_pv(j0 + t, bufs[t % 2], mb, m_run)
            mb = mb_next
        return m_run, mb

    m_run, mb_a = lax.fori_loop(
        0, n_loops, lambda kk, c: full_trips(LOOP_TRIPS * kk, LOOP_TRIPS, *c),
        (jnp.full((1, width), NEG_INF, F32), scores(0, sa_ref)))
    j_last = LOOP_TRIPS * n_loops

    for rem in range(LOOP_TRIPS):
        @pl.when(n_full - j_last == rem)
        def _(rem=rem):
            m_fin, _ = full_trips(j_last, rem, m_run, mb_a)
            causal_tail(j_last + rem, bufs[rem % 2], m_fin)

    o_s = [acc_ref[g, 0:HEAD_DIM, :] / acc_ref[g, HEAD_DIM:HEAD_DIM + 1, :] for g in range(NSA_GROUPS)]

    outs = []
    for g in range(NSA_GROUPS):
        heads = []
        for r in range(NSA_GROUP_SIZE):
            base = (g * NSA_GROUP_SIZE + r) * 3
            cols = slice(r * Q_TILE, (r + 1) * Q_TILE)
            heads.append(gt_ref[0, base:base + 1, :] * o_c[g][:, cols]
                         + gt_ref[0, base + 1:base + 2, :] * o_s[g][:, cols]
                         + gt_ref[0, base + 2:base + 3, :] * o_w[g][:, cols])
        for pair in range(NSA_GROUP_SIZE // 2):
            outs.append(jnp.concatenate(heads[2 * pair:2 * pair + 2], axis=0).T)
    o_ref[0] = jnp.concatenate(outs, axis=1)


def _nsa_attn(q, gates_t, kc, vct, ks, vst, kw, vwt, ovt):
    b, t, _ = q.shape
    ncp = kc.shape[1]
    ns = ovt.shape[0]
    sel_keys = SEL_TRIP * KEY_BLOCK
    assert t % sel_keys == 0 and sel_keys % Q_TILE == 0 and t >= WINDOW + Q_TILE and ncp % KEY_BLOCK == 0
    ind = (np.arange(sel_keys)[:, None] // SEL_BLOCK == np.arange(LANES)[None, :])
    ind = jnp.asarray(ind.astype(np.float32), dtype=BF16)
    full_rows = lambda bi, qi: (bi, 0, 0)
    return pl.pallas_call(
        _nsa_kernel,
        grid=(b, t // Q_TILE),
        in_specs=[
            pl.BlockSpec((1, Q_TILE, NSA_WIDTH), lambda bi, qi: (bi, qi, 0)),
            pl.BlockSpec((1, 32, Q_TILE), lambda bi, qi: (bi, 0, qi)),
            pl.BlockSpec((1, ncp, LANES), full_rows),
            pl.BlockSpec((1, LANES, ncp), full_rows),
            pl.BlockSpec((1, t, LANES), full_rows),
            pl.BlockSpec((1, LANES, t), full_rows),
            pl.BlockSpec((1, t, LANES), full_rows),
            pl.BlockSpec((1, LANES, t), full_rows),
            _const_spec((ns, ncp)),
            _const_spec((sel_keys, LANES)),
        ],
        out_specs=pl.BlockSpec((1, Q_TILE, NSA_WIDTH), lambda bi, qi: (bi, qi, 0)),
        out_shape=jax.ShapeDtypeStruct((b, t, NSA_WIDTH), F32),
        scratch_shapes=[pltpu.VMEM((ns, NSA_GROUPS * Q_TILE), F32),
                        pltpu.VMEM((sel_keys, NSA_GROUPS * NSA_GROUP_SIZE * Q_TILE), F32),
                        pltpu.VMEM((sel_keys, NSA_GROUPS * NSA_GROUP_SIZE * Q_TILE), F32),
                        pltpu.VMEM((NSA_GROUPS, HEAD_DIM + 16, NSA_GROUP_SIZE * Q_TILE), F32)],
        compiler_params=_params(("arbitrary", "arbitrary")),
        name="nsa_attn",
    )(q, gates_t, kc, vct, ks, vst, kw, vwt, ovt, ind)


def _same_head_mask():
    bi = lax.broadcasted_iota(jnp.int32, (HALF, HALF), 0) // HEAD_DIM
    bj = lax.broadcasted_iota(jnp.int32, (HALF, HALF), 1) // HEAD_DIM
    return bi == bj


def _bd_rows(x, same_head):
    xb = x.astype(BF16)
    tiled = jnp.concatenate([xb] * (HALF // x.shape[0]), axis=0)
    return jnp.where(same_head, tiled, jnp.zeros((), BF16))


def _rwkv_prep_stages(in_refs, par_refs, out_set, *, chunks):
    r_ref, k_ref, v_ref, lo_ref = in_refs
    w0_ref, w2_ref, a0_ref, a2_ref, g2_ref, kk_ref, ka_ref, rk_ref = par_refs
    wm_ref, zm_ref, arb_ref, rkv_ref, rt_ref, vb_ref, bkt_ref, gl_ref, bonus_ref, gate_ref = out_set
    c = RWKV_CHUNK
    nb = r_ref.shape[0]
    same_head = _same_head_mask()
    ones_bd = same_head.astype(BF16)
    n_tok = chunks * c
    row_i = lax.broadcasted_iota(jnp.int32, (n_tok, n_tok), 0)
    col_i = lax.broadcasted_iota(jnp.int32, (n_tok, n_tok), 1)
    tril_incl = ((row_i >= col_i) & (row_i // c == col_i // c)).astype(BF16)
    t_id = lax.broadcasted_iota(jnp.int32, (c, HALF), 0)
    j_id = lax.broadcasted_iota(jnp.int32, (c, HALF), 1) % HEAD_DIM
    strict_lower = t_id > j_id
    incl_lower = t_id >= j_id
    eye_all = (t_id == j_id).astype(F32)
    dot = lambda x, y: jnp.dot(x, y, preferred_element_type=F32)

    def bd_cols(xt):
        xb = xt.astype(BF16)
        return jnp.where(same_head, jnp.concatenate([xb, xb], axis=1), jnp.zeros((), BF16))

    halves = RWKV_WIDTH // HALF
    groups = [(b, ch) for b in range(nb) for ch in range(chunks)]
    chains = [(gi, hh) for gi in range(len(groups)) for hh in range(halves)]
    rows = [slice(ch * c, (ch + 1) * c) for ch in range(chunks)]
    lanes = [slice(hh * HALF, (hh + 1) * HALF) for hh in range(halves)]
    each = lambda fn: [fn(n, gi, lanes[hh]) for n, (gi, hh) in enumerate(chains)]

    lo = [lo_ref[b] for b in range(nb)]
    zs = [-(w0_ref[...] + _bdot(jnp.tanh(x[:, 0:DECAY_LORA]), w2_ref[...])) for x in lo]
    yield
    lr_b = [_sigmoid(a0_ref[...] + _bdot(x[:, DECAY_LORA:DECAY_LORA + AAA_LORA], a2_ref[...])) for x in lo]
    yield
    for b in range(nb):
        gate_ref[b] = _bdot(_sigmoid(lo[b][:, DECAY_LORA + AAA_LORA:]), g2_ref[...])
        vb_ref[b] = v_ref[b].astype(BF16)
    yield
    log_decay = []
    for z in zs:
        softplus = jnp.maximum(z, 0.0) + jnp.log(1.0 + jnp.exp(-jnp.abs(z)))
        log_decay.append(-jnp.exp(-softplus - 0.5))
    cum_b = [_dot3_left(tril_incl, x) for x in log_decay]
    yield
    r = [r_ref[b, rows[ch], :] for b, ch in groups]
    k = [k_ref[b, rows[ch], :] for b, ch in groups]
    v = [v_ref[b, rows[ch], :] for b, ch in groups]
    lr = [lr_b[b][rows[ch], :] for b, ch in groups]
    cum = [cum_b[b][rows[ch], :] for b, ch in groups]
    ld = [log_decay[b][rows[ch], :] for b, ch in groups]
    g_incl = [jnp.exp(x) for x in cum]
    g_excl = [jnp.exp(x - y) for x, y in zip(cum, ld)]
    g_inv = [jnp.exp(-x) for x in cum]
    for gi, (b, ch) in enumerate(groups):
        gl_ref[b, ch * 8:(ch + 1) * 8, :] = jnp.broadcast_to(g_incl[gi][c - 1:c, :], (8, RWKV_WIDTH))

    kk = each(lambda n, gi, ln: k[gi][:, ln] * kk_ref[:, ln])
    k2 = each(lambda n, gi, ln: k[gi][:, ln] * (1.0 + (lr[gi][:, ln] - 1.0) * ka_ref[:, ln]))
    sums = each(lambda n, gi, ln: _dot2_right(
        jnp.concatenate([kk[n] * kk[n], r[gi][:, ln] * k2[n] * rk_ref[:, ln]], axis=0), ones_bd))
    ssq = [x[0:c] for x in sums]
    for n, (gi, hh) in enumerate(chains):
        b, ch = groups[gi]
        bonus_ref[b, rows[ch], lanes[hh]] = sums[n][c:] * v[gi][:, lanes[hh]]
    yield
    kk = each(lambda n, gi, ln: kk[n] / jnp.maximum(jnp.sqrt(ssq[n]), 1e-12))
    at = each(lambda n, gi, ln: -kk[n] * g_excl[gi][:, ln])
    bt = each(lambda n, gi, ln: kk[n] * lr[gi][:, ln] * g_inv[gi][:, ln])
    kt = each(lambda n, gi, ln: k2[n] * g_inv[gi][:, ln])
    rt = each(lambda n, gi, ln: r[gi][:, ln] * g_incl[gi][:, ln])
    for n, (gi, hh) in enumerate(chains):
        b, ch = groups[gi]
        rt_ref[b, rows[ch], lanes[hh]] = rt[n].astype(BF16)
        bkt_ref[b, (ch * halves + hh) * HALF:(ch * halves + hh + 1) * HALF, :] = (
            jnp.concatenate([bt[n], kt[n]], axis=0).T.astype(BF16))

    bt_bd = each(lambda n, gi, ln: bd_cols(jnp.concatenate([bt[n], bt[n]], axis=0).T))
    ar = each(lambda n, gi, ln: jnp.concatenate([at[n], rt[n]], axis=0).astype(BF16))
    ab = each(lambda n, gi, ln: dot(ar[n], bt_bd[n]))
    yield
    kt_bd = each(lambda n, gi, ln: bd_cols(jnp.concatenate([kt[n], kt[n]], axis=0).T))
    ak = each(lambda n, gi, ln: dot(ar[n], kt_bd[n]))
    yield
    a_ab = [jnp.where(strict_lower, x[0:c], 0.0) for x in ab]
    a_rb = [jnp.where(incl_lower, x[c:], 0.0) for x in ab]
    a_ak = [jnp.where(strict_lower, x[0:c], 0.0) for x in ak]
    a_rk = [jnp.where(incl_lower, x[c:], 0.0) for x in ak]
    for n, (gi, hh) in enumerate(chains):
        b, ch = groups[gi]
        arb_ref[b, rows[ch], lanes[hh]] = a_rb[n].astype(BF16)

    inv = [eye_all + x for x in a_ab]
    pw = [dot(x.astype(BF16), _bd_rows(x, same_head)) for x in a_ab]
    yield
    for _ in range(4):
        both = [dot(jnp.concatenate([x, y], axis=0).astype(BF16), _bd_rows(x, same_head))
                for x, y in zip(pw, inv)]
        pw = [x[0:c] for x in both]
        inv = [y + x[c:] for x, y in zip(both, inv)]
        yield
    inv = [y + dot(y.astype(BF16), _bd_rows(x, same_head)) for x, y in zip(pw, inv)]
    yield

    v_bd = each(lambda n, gi, ln: _bd_rows(v[gi][:, ln], same_head))
    inv_b = [x.astype(BF16) for x in inv]
    wm = each(lambda n, gi, ln: dot(inv_b[n], _bd_rows(at[n], same_head)))
    yield
    av = each(lambda n, gi, ln: dot(jnp.concatenate([a_ak[n], a_rk[n]], axis=0).astype(BF16), v_bd[n]))
    yield
    zm = each(lambda n, gi, ln: dot(inv_b[n], _bd_rows(av[n][0:c], same_head)))
    for n, (gi, hh) in enumerate(chains):
        b, ch = groups[gi]
        wm_ref[b, rows[ch], lanes[hh]] = wm[n].astype(BF16)
        zm_ref[b, rows[ch], lanes[hh]] = zm[n]
        rkv_ref[b, rows[ch], lanes[hh]] = av[n][c:]


def _rwkv_scan_stages(in_set, gng_ref, gnb_ref, o_ref, s_ref, *, chunks):
    wm_ref, zm_ref, arb_ref, rkv_ref, rt_ref, vb_ref, bkt_ref, gl_ref, bonus_ref, gate_ref = in_set
    c = RWKV_CHUNK
    halves = RWKV_WIDTH // HALF
    same_head = _same_head_mask()
    ones_bd = same_head.astype(BF16)
    chains = [(b, hh) for b in range(o_ref.shape[0]) for hh in range(halves)]
    lanes = [slice(hh * HALF, (hh + 1) * HALF) for hh in range(halves)]
    dot = lambda x, y: jnp.dot(x, y, preferred_element_type=F32)

    for ch in range(chunks):
        rw = slice(ch * c, (ch + 1) * c)
        each = lambda fn: [fn(n, b, lanes[hh]) for n, (b, hh) in enumerate(chains)]
        s0 = [s_ref[b, hh] for b, hh in chains]
        s0b = [x.astype(BF16) for x in s0]
        u = each(lambda n, b, ln: dot(wm_ref[b, rw, ln], s0b[n]) + zm_ref[b, rw, ln])
        yield
        y0 = each(lambda n, b, ln: dot(rt_ref[b, rw, ln], s0b[n]) + rkv_ref[b, rw, ln])
        yield
        y = each(lambda n, b, ln: y0[n] + dot(arb_ref[b, rw, ln], _bd_rows(u[n], same_head)))
        yield
        uv = each(lambda n, b, ln: jnp.concatenate([u[n].astype(BF16), vb_ref[b, rw, ln]], axis=0))
        upd = [dot(bkt_ref[b, (ch * halves + hh) * HALF:(ch * halves + hh + 1) * HALF, :], uv[n])
               for n, (b, hh) in enumerate(chains)]
        for n, (b, hh) in enumerate(chains):
            g_last = jnp.broadcast_to(gl_ref[b, ch * 8:ch * 8 + 1, lanes[hh]], (LANES, HALF)).T
            g_col = jnp.concatenate([g_last, g_last], axis=1)
            s_ref[b, hh] = g_col * (s0[n] + jnp.where(same_head, upd[n], 0.0))
        yield
        mu = [_dot2_right(x, ones_bd) * (1.0 / HEAD_DIM) for x in y]
        yield
        yc = [x - m for x, m in zip(y, mu)]
        var = [_dot2_right(x * x, ones_bd) * (1.0 / HEAD_DIM) for x in yc]
        yield
        for n, (b, hh) in enumerate(chains):
            ln = lanes[hh]
            yn = yc[n] * lax.rsqrt(var[n] + RWKV_GN_EPS) * gng_ref[:, ln] + gnb_ref[:, ln]
            o_ref[b, rw, ln] = (yn + bonus_ref[b, rw, ln]) * gate_ref[b, rw, ln]


def _rwkv_kernel(*refs, chunks):
    in_refs, par_refs = refs[0:4], refs[4:12]
    gng_ref, gnb_ref, o_ref, s_ref = refs[12:16]
    sets = (refs[16:26], refs[26:36])
    step = pl.program_id(0)

    @pl.when(step == 0)
    def _():
        s_ref[...] = jnp.zeros_like(s_ref)
        for ref in sets[1]:
            ref[...] = jnp.zeros_like(ref)

    def run(write_set, read_set):
        prep = _rwkv_prep_stages(in_refs, par_refs, write_set, chunks=chunks)
        scan = _rwkv_scan_stages(read_set, gng_ref, gnb_ref, o_ref, s_ref, chunks=chunks)
        live = [prep, scan]
        while live:
            for gen in list(live):
                if next(gen, "done") == "done":
                    live.remove(gen)

    @pl.when(step % 2 == 0)
    def _():
        run(sets[0], sets[1])

    @pl.when(step % 2 == 1)
    def _():
        run(sets[1], sets[0])


def _rwkv(r, k, v, lo, w0, w2, a0, a2, g2, k_k, k_a, r_k, gn_g, gn_b, *, chunks=2):
    b, t, width = r.shape
    c = RWKV_CHUNK
    n_tok = chunks * c
    n_groups = t // n_tok
    halves = width // HALF
    vec = lambda a: a.reshape(1, width)
    cs = lambda a: _const_spec(a.shape)
    args = [vec(w0), w2.astype(BF16), vec(a0), a2.astype(BF16), g2.astype(BF16),
            vec(k_k), vec(k_a), vec(r_k), vec(gn_g), vec(gn_b)]
    ahead = lambda s: (0, jnp.minimum(s, n_groups - 1), 0)
    behind = lambda s: (0, jnp.maximum(s - 1, 0), 0)
    tok = lambda dt: pltpu.VMEM((b, n_tok, width), dt)
    scratch_set = [tok(BF16), tok(F32), tok(BF16), tok(F32), tok(BF16), tok(BF16),
                   pltpu.VMEM((b, chunks * halves * HALF, LANES), BF16),
                   pltpu.VMEM((b, chunks * 8, width), F32), tok(F32), tok(F32)]
    return pl.pallas_call(
        functools.partial(_rwkv_kernel, chunks=chunks),
        grid=(n_groups + 1,),
        in_specs=[pl.BlockSpec((b, n_tok, width), ahead)] * 3 + [pl.BlockSpec((b, n_tok, 256), ahead)]
                 + [cs(a) for a in args],
        out_specs=pl.BlockSpec((b, n_tok, width), behind),
        out_shape=jax.ShapeDtypeStruct((b, t, width), F32),
        scratch_shapes=[pltpu.VMEM((b, halves, HALF, HALF), F32)] + scratch_set + scratch_set,
        compiler_params=_params(("arbitrary",)),
        name="rwkv7",
    )(r, k, v, lo, *args)


def _mem_kv_kernel(m_ref, g_ref, wk_ref, wv_ref, kt_ref, v_ref):
    m = _rms(m_ref[0], g_ref[...]).astype(BF16)
    kt_ref[0] = jnp.dot(m, wk_ref[...], preferred_element_type=F32).T.astype(BF16)
    v_ref[0] = jnp.dot(m, wv_ref[...], preferred_element_type=F32).astype(BF16)


def _mem_kv(mem, g, wk, wv):
    b, mt, d = mem.shape
    return pl.pallas_call(
        _mem_kv_kernel,
        grid=(b,),
        in_specs=[pl.BlockSpec((1, mt, d), lambda bi: (bi, 0, 0)), _const_spec((1, d)),
                  _const_spec((d, d)), _const_spec((d, d))],
        out_specs=[pl.BlockSpec((1, d, mt), lambda bi: (bi, 0, 0)),
                   pl.BlockSpec((1, mt, d), lambda bi: (bi, 0, 0))],
        out_shape=[jax.ShapeDtypeStruct((b, d, mt), BF16), jax.ShapeDtypeStruct((b, mt, d), BF16)],
        compiler_params=_params(("arbitrary",)),
        name="mem_kv",
    )(mem, g.reshape(1, d), wk.astype(BF16), wv.astype(BF16))


def _out_mem_kernel(x_ref, on_ref, or_ref, ng_ref, wo1_ref, wo2_ref, mpost_ref,
                    mpre_ref, wq_ref, kt_ref, v_ref, wo_ref, mempost_ref, o_ref, *, parts):
    tm = x_ref.shape[1] // parts
    rows = [slice(n * tm, (n + 1) * tm) for n in range(parts)]
    dot = lambda a, b: jnp.dot(a, b, preferred_element_type=F32)
    d = x_ref.shape[-1]
    hd = d // MEM_HEADS

    a = [_rms(on_ref[0, rw, :], ng_ref[...]).astype(BF16) for rw in rows]
    mixed = [dot(a[n], wo1_ref[...]) + dot(or_ref[0, rw, :].astype(BF16), wo2_ref[...])
             for n, rw in enumerate(rows)]
    x = [x_ref[0, rw, :] + _rms(mixed[n], mpost_ref[...]) for n, rw in enumerate(rows)]
    h = [_rms(xn, mpre_ref[...]).astype(BF16) for xn in x]
    q = [(dot(hn, wq_ref[...]) * (hd ** -0.5)).astype(BF16) for hn in h]
    heads = [[] for _ in rows]
    for hi in range(MEM_HEADS):
        cols = slice(hi * hd, (hi + 1) * hd)
        s = [dot(qn[:, cols], kt_ref[0, cols, :]) for qn in q]
        e = [jnp.exp(sn - jnp.max(sn, axis=-1, keepdims=True)) for sn in s]
        p = [(en / jnp.sum(en, axis=-1, keepdims=True)).astype(BF16) for en in e]
        for n in range(parts):
            heads[n].append(dot(p[n], v_ref[0, :, cols]))
    att = [dot(jnp.concatenate(hn, axis=-1).astype(BF16), wo_ref[...]) for hn in heads]
    for n, rw in enumerate(rows):
        o_ref[0, rw, :] = x[n] + _rms(att[n], mempost_ref[...])


def _out_mem(x3, o_nsa, o_rwkv, nsa_g, w_out, mix_post_g, mem_pre_g, wq, kt, vm, wo, mem_post_g,
             *, tm=1024, parts=4):
    b, t, d = x3.shape
    mt = vm.shape[1]
    row = lambda bi, ti: (bi, ti, 0)
    per_b = lambda bi, ti: (bi, 0, 0)
    w_out = w_out.astype(BF16)
    return pl.pallas_call(
        functools.partial(_out_mem_kernel, parts=parts),
        grid=(b, t // tm),
        in_specs=[
            pl.BlockSpec((1, tm, d), row),
            pl.BlockSpec((1, tm, NSA_WIDTH), row),
            pl.BlockSpec((1, tm, RWKV_WIDTH), row),
            _const_spec((1, NSA_WIDTH)),
            _const_spec((NSA_WIDTH, d)),
            _const_spec((RWKV_WIDTH, d)),
            _const_spec((1, d)),
            _const_spec((1, d)),
            _const_spec((d, d)),
            pl.BlockSpec((1, d, mt), per_b),
            pl.BlockSpec((1, mt, d), per_b),
            _const_spec((d, d)),
            _const_spec((1, d)),
        ],
        out_specs=pl.BlockSpec((1, tm, d), row),
        out_shape=jax.ShapeDtypeStruct((b, t, d), F32),
        compiler_params=_params(("arbitrary", "arbitrary")),
        name="out_mem",
    )(x3, o_nsa, o_rwkv, nsa_g.reshape(1, -1), w_out[:NSA_WIDTH], w_out[NSA_WIDTH:],
      mix_post_g.reshape(1, d), mem_pre_g.reshape(1, d), wq.astype(BF16), kt, vm,
      wo.astype(BF16), mem_post_g.reshape(1, d))


def _rope_tables(pos):
    half = HEAD_DIM // 2
    inv = ROPE_THETA ** (-jnp.arange(half, dtype=F32) / half)
    ang = pos.astype(F32)[:, None] * inv[None, :]
    cos, sin = jnp.cos(ang), jnp.sin(ang)
    cos_t = jnp.concatenate([cos, cos, cos, cos], axis=-1)
    sin_t = jnp.concatenate([-sin, sin, -sin, sin], axis=-1)
    return cos_t, sin_t


def _overlap_t(ns, ncp):
    c0 = np.arange(ncp)[None, :] * CMP_STRIDE
    s0 = np.arange(ns)[:, None] * SEL_BLOCK
    ov = (c0 < s0 + SEL_BLOCK) & (c0 + CMP_BLOCK > s0) & (np.arange(ncp)[None, :] < ncp - 1)
    return jnp.asarray(ov.astype(np.float32), dtype=BF16)


def _pad_cols(w, n):
    return jnp.pad(w, ((0, 0), (0, n - w.shape[1])))


def _cmp_weights(pe, w1, w2):
    per = CMP_STRIDE
    w1r = w1.reshape(CMP_BLOCK, HEAD_DIM, CMP_HIDDEN)
    blocks = []
    for part in range(CMP_BLOCK // per):
        for g in range(NSA_GROUPS):
            z = jnp.zeros((per, NSA_GROUPS, HEAD_DIM, CMP_HIDDEN), F32)
            z = z.at[:, g].set(w1r[part * per:(part + 1) * per])
            blocks.append(z.reshape(per * NSA_GROUPS * HEAD_DIM, CMP_HIDDEN))
    w1cat = jnp.concatenate(blocks, axis=1).astype(BF16)
    w2bd = jnp.zeros((NSA_GROUPS * CMP_HIDDEN, NSA_GROUPS * HEAD_DIM), F32)
    for g in range(NSA_GROUPS):
        w2bd = w2bd.at[g * CMP_HIDDEN:(g + 1) * CMP_HIDDEN, g * HEAD_DIM:(g + 1) * HEAD_DIM].set(w2)
    pe8 = jnp.broadcast_to(pe.reshape(1, CMP_BLOCK * HEAD_DIM), (8, CMP_BLOCK * HEAD_DIM))
    return pe8.astype(BF16), w1.astype(BF16), w1cat, w2bd.astype(BF16)


def kernel(x, mem, ffn1_pre_g, ffn1_w_gate, ffn1_w_up, ffn1_w_down, ffn1_post_g, mix_pre_g, w_in, cmp_pe_k, cmp_w1_k, cmp_w2_k, cmp_pe_v, cmp_w1_v, cmp_w2_v, nsa_out_g, rwkv_mu, rwkv_w0, rwkv_w2, rwkv_a0, rwkv_a2, rwkv_g2, rwkv_k_k, rwkv_k_a, rwkv_r_k, rwkv_gn_g, rwkv_gn_b, w_out, mix_post_g, mem_pre_g, mem_kv_g, mem_wq, mem_wk, mem_wv, mem_wo, mem_post_g, ffn2_pre_g, ffn2_w_gate, ffn2_w_up, ffn2_w_down, ffn2_post_g):
    b, t, d = x.shape
    ncp = t // CMP_STRIDE
    ns = t // SEL_BLOCK
    cos_t, sin_t = _rope_tables(jnp.arange(t))
    cos_c, sin_c = _rope_tables(jnp.arange(ncp) * CMP_STRIDE + (CMP_BLOCK - 1))
    ovt = _overlap_t(ns, ncp)

    for l in range(ffn1_pre_g.shape[0]):
        x2 = _ffn_block(x.reshape(b * t, d), ffn1_pre_g[l], ffn1_w_gate[l], ffn1_w_up[l],
                        ffn1_w_down[l], ffn1_post_g[l])
        x3 = x2.reshape(b, t, d)

        wi = w_in[l]
        nsa_w = NSA_WIDTH + 6 * NSA_KV_WIDTH
        gate_w = wi[:, nsa_w:nsa_w + 3 * NSA_HEADS]
        w_cols = jnp.concatenate([wi[:, :nsa_w], _pad_cols(gate_w, LANES),
                                  wi[:, nsa_w + 3 * NSA_HEADS:]], axis=1).astype(BF16)
        (q, k_cmp, v_cmp, k_slc, v_slc_t, k_win, v_win_t, gates_t, r, k, v, lo) = _in_proj(
            x3, mix_pre_g[l], w_cols, rwkv_mu[l].reshape(1, -1), cos_t, sin_t)

        pek, w1k, w1kc, w2k = _cmp_weights(cmp_pe_k[l], cmp_w1_k[l], cmp_w2_k[l])
        pev, w1v, w1vc, w2v = _cmp_weights(cmp_pe_v[l], cmp_w1_v[l], cmp_w2_v[l])
        kc, vct = _compress(k_cmp, v_cmp,
                            pek, w1k, w1kc, w2k, pev, w1v, w1vc, w2v, cos_c, sin_c)
        o_nsa = _nsa_attn(q, gates_t, kc, vct, k_slc, v_slc_t, k_win, v_win_t, ovt)

        o_rwkv = _rwkv(r, k, v, lo, rwkv_w0[l], rwkv_w2[l], rwkv_a0[l], rwkv_a2[l], rwkv_g2[l],
                       rwkv_k_k[l], rwkv_k_a[l], rwkv_r_k[l], rwkv_gn_g[l], rwkv_gn_b[l])

        kt, vm = _mem_kv(mem, mem_kv_g[l], mem_wk[l], mem_wv[l])
        x4 = _out_mem(x3, o_nsa, o_rwkv, nsa_out_g[l], w_out[l], mix_post_g[l], mem_pre_g[l],
                      mem_wq[l], kt, vm, mem_wo[l], mem_post_g[l])

        x = _ffn_block(x4.reshape(b * t, d), ffn2_pre_g[l], ffn2_w_gate[l], ffn2_w_up[l],
                       ffn2_w_down[l], ffn2_post_g[l]).reshape(b, t, d)
    return x
```

```python
import functools

import numpy as np
import jax
import jax.numpy as jnp
from jax import lax
from jax.experimental import pallas as pl
from jax.experimental.pallas import tpu as pltpu

F32 = jnp.float32
BF16 = jnp.bfloat16

HEAD_DIM = 64
NSA_HEADS = 8
NSA_GROUPS = 2
NSA_GROUP_SIZE = 4
NSA_WIDTH = 512
NSA_KV_WIDTH = 128
CMP_BLOCK = 32
CMP_STRIDE = 16
CMP_HIDDEN = 256
SEL_BLOCK = 64
SEL_TOPK = 16
SEL_FORCE = 1e4
SEL_FORCED = 3
WINDOW = 512
RWKV_WIDTH = 512
DECAY_LORA = 64
AAA_LORA = 64
GATE_LORA = 128
RWKV_GN_EPS = 64e-5
MEM_HEADS = 4
ROPE_THETA = 10000.0
NORM_EPS = 1e-6
NEG_INF = -1e30
LOG2_E = 1.4426950408889634

LORA_COLS = DECAY_LORA + AAA_LORA + GATE_LORA
RWKV_COLS = 3 * RWKV_WIDTH + LORA_COLS

LANES = 128
SUBLANES = 8
BF16_ROWS = 16
GATE_ROWS = 32
KEY_BLOCK = 128
Q_TILE = 128
SEL_TRIP = 4
LOOP_TRIPS = 4
RWKV_CHUNK = 64
HALF = 256
VMEM_LIMIT = 56 * 1024 * 1024


def _bdot(a, b):
    return jnp.dot(a.astype(BF16), b.astype(BF16), preferred_element_type=F32)


def _split3(x):
    h1 = x.astype(BF16)
    r1 = x - h1.astype(F32)
    h2 = r1.astype(BF16)
    r2 = r1 - h2.astype(F32)
    return h1, h2, r2.astype(BF16)


def _split2(x):
    h1 = x.astype(BF16)
    return h1, (x - h1.astype(F32)).astype(BF16)


def _dot2_right(x, m):
    h1, h2 = _split2(x)
    return (jnp.dot(h1, m, preferred_element_type=F32) + jnp.dot(h2, m, preferred_element_type=F32))


def _dot3_right(x, m):
    h1, h2, h3 = _split3(x)
    d = lambda h: jnp.dot(h, m, preferred_element_type=F32)
    return d(h1) + d(h2) + d(h3)


def _dot2_left(m, x):
    h1, h2 = _split2(x)
    return (jnp.dot(m, h1, preferred_element_type=F32) + jnp.dot(m, h2, preferred_element_type=F32))


def _dot3_left(m, x):
    h1, h2, h3 = _split3(x)
    d = lambda h: jnp.dot(m, h, preferred_element_type=F32)
    return d(h1) + d(h2) + d(h3)


def _rms(x, g):
    return x * lax.rsqrt(jnp.mean(x * x, axis=-1, keepdims=True) + NORM_EPS) * g


def _silu(x):
    return x / (1.0 + jnp.exp(-x))


def _sigmoid(x):
    return 1.0 / (1.0 + jnp.exp(-x))


def _const_spec(shape):
    nd = len(shape)
    return pl.BlockSpec(shape, lambda *_: (0,) * nd)


def _params(sem):
    return pltpu.CompilerParams(dimension_semantics=sem, vmem_limit_bytes=VMEM_LIMIT)


def _ffn_kernel(x_ref, pre_ref, wg_ref, wu_ref, wd_ref, post_ref, o_ref, *, ff_chunk):
    x = x_ref[...]
    h = _rms(x, pre_ref[...]).astype(BF16)
    d_ff = wg_ref.shape[1]
    acc = jnp.zeros(x.shape, F32)
    for c0 in range(0, d_ff, ff_chunk):
        g = jnp.dot(h, wg_ref[:, c0:c0 + ff_chunk], preferred_element_type=F32)
        u = jnp.dot(h, wu_ref[:, c0:c0 + ff_chunk], preferred_element_type=F32)
        a = (_silu(g) * u).astype(BF16)
        acc = acc + jnp.dot(a, wd_ref[c0:c0 + ff_chunk, :], preferred_element_type=F32)
    o_ref[...] = x + 0.5 * _rms(acc, post_ref[...])


def _ffn_block(x2, pre_g, wg, wu, wd, post_g, *, tm=512, ff_chunk=256):
    m, d = x2.shape
    d_ff = wg.shape[1]
    return pl.pallas_call(
        functools.partial(_ffn_kernel, ff_chunk=ff_chunk),
        grid=(m // tm,),
        in_specs=[
            pl.BlockSpec((tm, d), lambda i: (i, 0)),
            _const_spec((1, d)),
            _const_spec((d, d_ff)),
            _const_spec((d, d_ff)),
            _const_spec((d_ff, d)),
            _const_spec((1, d)),
        ],
        out_specs=pl.BlockSpec((tm, d), lambda i: (i, 0)),
        out_shape=jax.ShapeDtypeStruct((m, d), F32),
        compiler_params=_params(("arbitrary",)),
        name="ffn_block",
    )(x2, pre_g.reshape(1, d), wg.astype(BF16), wu.astype(BF16), wd.astype(BF16),
      post_g.reshape(1, d))


def _swap_halves(x):
    n = x.shape[-1]
    lane = lax.broadcasted_iota(jnp.int32, x.shape, x.ndim - 1)
    fwd = pltpu.roll(x, n - HEAD_DIM // 2, x.ndim - 1)
    bwd = pltpu.roll(x, HEAD_DIM // 2, x.ndim - 1)
    return jnp.where((lane % HEAD_DIM) < HEAD_DIM // 2, fwd, bwd)


def _rope(x, cos, sin_signed):
    reps = x.shape[-1] // LANES
    c = jnp.concatenate([cos] * reps, axis=-1) if reps > 1 else cos
    s = jnp.concatenate([sin_signed] * reps, axis=-1) if reps > 1 else sin_signed
    return x * c + _swap_halves(x) * s


def _in_proj_kernel(x_ref, g_ref, w_ref, mu_ref, cos_ref, sin_ref,
                    q_ref, kc_ref, vc_ref, ks_ref, vst_ref, kw_ref, vwt_ref, gt_ref,
                    r_ref, k_ref, v_ref, lo_ref, carry_ref):
    @pl.when(pl.program_id(1) == 0)
    def _():
        carry_ref[...] = jnp.zeros_like(carry_ref)

    h = _rms(x_ref[0], g_ref[...]).astype(BF16)
    p = jnp.dot(h, w_ref[...], preferred_element_type=F32)
    cos = cos_ref[...]
    sin = sin_ref[...]
    tm = p.shape[0]

    o = 0
    q = _rope(p[:, o:o + NSA_WIDTH], cos, sin) * (HEAD_DIM ** -0.5 * LOG2_E)
    q_ref[0] = q.astype(BF16)
    o += NSA_WIDTH
    kc_ref[0] = p[:, o:o + LANES]; o += LANES
    vc_ref[0] = p[:, o:o + LANES]; o += LANES
    ks_ref[0] = _rope(p[:, o:o + LANES], cos, sin).astype(BF16); o += LANES
    vst_ref[0] = p[:, o:o + LANES].T.astype(BF16); o += LANES
    kw_ref[0] = _rope(p[:, o:o + LANES], cos, sin).astype(BF16); o += LANES
    vwt_ref[0] = p[:, o:o + LANES].T.astype(BF16); o += LANES
    gates_t = _sigmoid(p[:, o:o + LANES]).T
    gt_ref[0] = gates_t[:gt_ref.shape[1], :]
    o += LANES

    rw = p[:, o:]
    row = lax.broadcasted_iota(jnp.int32, rw.shape, 0)
    prev = jnp.where(row == 0, carry_ref[0:1, :], pltpu.roll(rw, 1, 0))
    carry_ref[...] = jnp.broadcast_to(rw[tm - 1:tm, :], carry_ref.shape)
    mixed = rw + (prev - rw) * mu_ref[...]
    r_ref[0] = mixed[:, 0:RWKV_WIDTH]
    k_ref[0] = mixed[:, RWKV_WIDTH:2 * RWKV_WIDTH]
    v_ref[0] = mixed[:, 2 * RWKV_WIDTH:3 * RWKV_WIDTH]
    lo_ref[0] = mixed[:, 3 * RWKV_WIDTH:RWKV_COLS]


def _in_proj(x3, g, w_cols, mu_cols, cos_t, sin_t, *, tm=512):
    b, t, d = x3.shape
    n = w_cols.shape[1]
    row = lambda bi, ti: (bi, ti, 0)
    col = lambda bi, ti: (bi, 0, ti)
    out_shapes = [
        jax.ShapeDtypeStruct((b, t, NSA_WIDTH), BF16),
        jax.ShapeDtypeStruct((b, t, LANES), F32),
        jax.ShapeDtypeStruct((b, t, LANES), F32),
        jax.ShapeDtypeStruct((b, t, LANES), BF16),
        jax.ShapeDtypeStruct((b, LANES, t), BF16),
        jax.ShapeDtypeStruct((b, t, LANES), BF16),
        jax.ShapeDtypeStruct((b, LANES, t), BF16),
        jax.ShapeDtypeStruct((b, GATE_ROWS, t), F32),
        jax.ShapeDtypeStruct((b, t, RWKV_WIDTH), F32),
        jax.ShapeDtypeStruct((b, t, RWKV_WIDTH), F32),
        jax.ShapeDtypeStruct((b, t, RWKV_WIDTH), F32),
        jax.ShapeDtypeStruct((b, t, LORA_COLS), F32),
    ]
    out_specs = [
        pl.BlockSpec((1, tm, NSA_WIDTH), row),
        pl.BlockSpec((1, tm, LANES), row),
        pl.BlockSpec((1, tm, LANES), row),
        pl.BlockSpec((1, tm, LANES), row),
        pl.BlockSpec((1, LANES, tm), col),
        pl.BlockSpec((1, tm, LANES), row),
        pl.BlockSpec((1, LANES, tm), col),
        pl.BlockSpec((1, GATE_ROWS, tm), col),
        pl.BlockSpec((1, tm, RWKV_WIDTH), row),
        pl.BlockSpec((1, tm, RWKV_WIDTH), row),
        pl.BlockSpec((1, tm, RWKV_WIDTH), row),
        pl.BlockSpec((1, tm, LORA_COLS), row),
    ]
    return pl.pallas_call(
        _in_proj_kernel,
        grid=(b, t // tm),
        in_specs=[
            pl.BlockSpec((1, tm, d), row),
            _const_spec((1, d)),
            _const_spec((d, n)),
            _const_spec((1, RWKV_COLS)),
            pl.BlockSpec((tm, LANES), lambda bi, ti: (ti, 0)),
            pl.BlockSpec((tm, LANES), lambda bi, ti: (ti, 0)),
        ],
        out_specs=out_specs,
        out_shape=out_shapes,
        scratch_shapes=[pltpu.VMEM((SUBLANES, RWKV_COLS), F32)],
        compiler_params=_params(("arbitrary", "arbitrary")),
        name="in_proj",
    )(x3, g.reshape(1, d), w_cols, mu_cols, cos_t, sin_t)


def _compress_kernel(kin_ref, vin_ref, pek_ref, w1k_ref, w1kc_ref, w2k_ref,
                     pev_ref, w1v_ref, w1vc_ref, w2v_ref, cos_ref, sin_ref,
                     kc_ref, vct_ref):
    def phi(tok_ref, pe, w1, w1cat_ref, w2bd):
        n = tok_ref.shape[1] // CMP_STRIDE
        pr = None
        for l in range(CMP_STRIDE):
            rows = tok_ref[0, pl.ds(l, n, stride=CMP_STRIDE), :].astype(BF16)
            term = jnp.dot(rows, w1cat_ref[l * LANES:(l + 1) * LANES, :], preferred_element_type=F32)
            pr = term if pr is None else pr + term
        bias = jnp.dot(pe, w1, preferred_element_type=F32)[0:1, :]
        hid = []
        for g in range(NSA_GROUPS):
            top = pr[:, g * CMP_HIDDEN:(g + 1) * CMP_HIDDEN]
            bot = pr[:, (NSA_GROUPS + g) * CMP_HIDDEN:(NSA_GROUPS + g + 1) * CMP_HIDDEN]
            hid.append(top + pltpu.roll(bot, n - 1, 0) + bias)
        act = _silu(jnp.concatenate(hid, axis=-1)).astype(BF16)
        return jnp.dot(act, w2bd, preferred_element_type=F32)

    kc = phi(kin_ref, pek_ref[...], w1k_ref[...], w1kc_ref, w2k_ref[...])
    kc_ref[0] = _rope(kc, cos_ref[...], sin_ref[...]).astype(BF16)
    vc = phi(vin_ref, pev_ref[...], w1v_ref[...], w1vc_ref, w2v_ref[...])
    vct_ref[0] = vc.T.astype(BF16)


def _compress(kin, vin, pek, w1k, w1kc, w2k, pev, w1v, w1vc, w2v, cos_c, sin_c):
    b, t, width = kin.shape
    ncp = t // CMP_STRIDE
    blk = pl.BlockSpec((1, t, width), lambda bi: (bi, 0, 0))
    cs = lambda a: _const_spec(a.shape)
    return pl.pallas_call(
        _compress_kernel,
        grid=(b,),
        in_specs=[blk, blk, cs(pek), cs(w1k), cs(w1kc), cs(w2k),
                  cs(pev), cs(w1v), cs(w1vc), cs(w2v), cs(cos_c), cs(sin_c)],
        out_specs=[pl.BlockSpec((1, ncp, LANES), lambda bi: (bi, 0, 0)),
                   pl.BlockSpec((1, LANES, ncp), lambda bi: (bi, 0, 0))],
        out_shape=[jax.ShapeDtypeStruct((b, ncp, LANES), BF16),
                   jax.ShapeDtypeStruct((b, LANES, ncp), BF16)],
        compiler_params=_params(("arbitrary",)),
        name="nsa_compress",
    )(kin, vin, pek, w1k, w1kc, w2k, pev, w1v, w1vc, w2v, cos_c, sin_c)


def _nsa_kernel(q_ref, gt_ref, kc_ref, vct_ref, ks_ref, vst_ref, kw_ref, vwt_ref, ovt_ref, ind_ref,
                o_ref, bias_ref, sa_ref, sb_ref, acc_ref):
    i = pl.program_id(1)
    q0 = i * Q_TILE
    ncp = kc_ref.shape[1]
    ns = ovt_ref.shape[0]
    gw = NSA_GROUP_SIZE * Q_TILE
    width = NSA_GROUPS * gw
    lanes = [slice(g * gw, (g + 1) * gw) for g in range(NSA_GROUPS)]
    feat = [slice(g * HEAD_DIM, (g + 1) * HEAD_DIM) for g in range(NSA_GROUPS)]
    t_row = q0 + lax.broadcasted_iota(jnp.int32, (1, width), 1) % Q_TILE
    sel_keys = SEL_TRIP * KEY_BLOCK
    sel_rows = sel_keys // SEL_BLOCK
    win_keys = WINDOW + Q_TILE
    dot = lambda x, y: jnp.dot(x, y, preferred_element_type=F32)

    qf = q_ref[0].astype(F32)
    zeros_half = jnp.zeros((HEAD_DIM, Q_TILE), F32)
    parts = []
    for g in range(NSA_GROUPS):
        for pair in range(NSA_GROUP_SIZE // 2):
            slab_t = qf[:, (2 * g + pair) * LANES:(2 * g + pair + 1) * LANES].T
            for half in range(2):
                f = slab_t[half * HEAD_DIM:(half + 1) * HEAD_DIM, :]
                parts.append(jnp.concatenate([f, zeros_half] if g == 0 else [zeros_half, f], axis=0))
    qt = jnp.concatenate(parts, axis=1).astype(BF16)

    def v_ext(vt_ref, g, k0, n):
        return jnp.concatenate([vt_ref[0, feat[g], pl.ds(k0, n)], jnp.ones((BF16_ROWS, n), BF16)], axis=0)

    sw_ = NSA_GROUPS * Q_TILE
    n_vis = (q0 + Q_TILE - CMP_BLOCK) // CMP_STRIDE + 1
    blocks_needed = jnp.maximum(n_vis + KEY_BLOCK - 1, KEY_BLOCK) // KEY_BLOCK
    def compressed(nb):
        rows = nb * KEY_BLOCK
        sc = dot(kc_ref[0, 0:rows, :], qt)
        c_end = lax.broadcasted_iota(jnp.int32, (rows, width), 0) * CMP_STRIDE + (CMP_BLOCK - 1)
        c_mask = c_end <= t_row
        sc = jnp.where(c_mask, sc, NEG_INF)
        m_c = jnp.max(sc, axis=0, keepdims=True)
        e_c = jnp.where(c_mask, jnp.exp2(sc - m_c), 0.0)
        p_c = e_c / jnp.maximum(jnp.sum(e_c, axis=0, keepdims=True), 1e-30)
        outs, p_sum = [], []
        for g in range(NSA_GROUPS):
            outs.append(dot(vct_ref[0, feat[g], 0:rows], p_c[:, lanes[g]].astype(BF16)))
            acc = p_c[:, g * gw:g * gw + Q_TILE]
            for r in range(1, NSA_GROUP_SIZE):
                acc = acc + p_c[:, g * gw + r * Q_TILE:g * gw + (r + 1) * Q_TILE]
            p_sum.append(acc)
        return outs + [_dot2_left(ovt_ref[:, 0:rows], jnp.concatenate(p_sum, axis=1))]

    *o_c, imp = lax.switch(blocks_needed - 1,
                           [functools.partial(compressed, nb) for nb in range(1, ncp // KEY_BLOCK + 1)])

    w0 = pl.multiple_of(jnp.maximum(q0 - WINDOW, 0), KEY_BLOCK)
    sw = dot(kw_ref[0, pl.ds(w0, win_keys), :], qt)
    t_loc = t_row - w0
    k_loc = lax.broadcasted_iota(jnp.int32, (win_keys, width), 0)
    ok = k_loc <= t_loc
    old = lax.broadcasted_iota(jnp.int32, (KEY_BLOCK, width), 0) <= t_loc - WINDOW
    sw = jnp.where(ok, sw, NEG_INF)
    sw = jnp.concatenate([jnp.where(old, NEG_INF, sw[0:KEY_BLOCK]), sw[KEY_BLOCK:]], axis=0)
    m_w = jnp.max(sw, axis=0, keepdims=True)
    o_w = []
    for g in range(NSA_GROUPS):
        p = jnp.exp2(sw[:, lanes[g]] - m_w[:, lanes[g]]).astype(BF16)
        pv = dot(v_ext(vwt_ref, g, w0, win_keys), p)
        o_w.append(pv[0:HEAD_DIM, :] / pv[HEAD_DIM:HEAD_DIM + 1, :])

    s_id = lax.broadcasted_iota(jnp.int32, (ns, sw_), 0)
    cur = t_row[:, 0:sw_] // SEL_BLOCK
    forced = (s_id == 0) | (s_id == cur) | (s_id == cur - 1)
    score = jnp.where(forced, -3e38, jnp.where(s_id <= cur, imp, -SEL_FORCE))
    bias = jnp.where(forced, 0.0, NEG_INF)
    for _ in range(min(SEL_TOPK, ns) - SEL_FORCED):
        mx = jnp.max(score, axis=0, keepdims=True)
        first = jnp.min(jnp.where(score == mx, s_id, ns), axis=0, keepdims=True)
        hit = s_id == first
        score = jnp.where(hit, -3e38, score)
        bias = jnp.where(hit, 0.0, bias)
    bias_ref[...] = bias

    def scores(j, s_ref):
        k0 = pl.multiple_of(j * sel_keys, sel_keys)
        b0 = pl.multiple_of(j * sel_rows, sel_rows)
        rows = bias_ref[pl.ds(b0, sel_rows), :]
        rows = jnp.concatenate([rows[:, g * Q_TILE:(g + 1) * Q_TILE]
                                for g in range(NSA_GROUPS) for _ in range(NSA_GROUP_SIZE)], axis=1)
        rows = jnp.concatenate([rows, jnp.zeros_like(rows)], axis=0).astype(BF16)
        rhs = jnp.concatenate([qt, rows, jnp.zeros((LANES - rows.shape[0], width), BF16)], axis=0)
        lhs = jnp.concatenate([ks_ref[0, pl.ds(k0, sel_keys), :], ind_ref[...]], axis=1)
        s = dot(lhs, rhs)
        s_ref[...] = s
        return jnp.max(s, axis=0, keepdims=True)

    def softmax_pv(j, s_ref, mb, m_run):
        k0 = pl.multiple_of(j * sel_keys, sel_keys)
        m_new = jnp.maximum(m_run, mb)
        alpha = jnp.exp2(m_run - m_new)
        for g in range(NSA_GROUPS):
            p = jnp.exp2(s_ref[:, lanes[g]] - m_new[:, lanes[g]]).astype(BF16)
            pv = dot(v_ext(vst_ref, g, k0, sel_keys), p)
            acc_ref[g] = acc_ref[g] * alpha[:, lanes[g]] + pv
        return m_new

    def causal_tail(j, s_ref, m_run):
        k0 = pl.multiple_of(j * sel_keys, sel_keys)
        key = k0 + lax.broadcasted_iota(jnp.int32, (sel_keys, width), 0)
        s = jnp.where(key <= t_row, s_ref[...], NEG_INF)
        s_ref[...] = s
        softmax_pv(j, s_ref, jnp.max(s, axis=0, keepdims=True), m_run)

    acc_ref[...] = jnp.zeros_like(acc_ref)
    n_full = q0 // sel_keys
    n_loops = n_full // LOOP_TRIPS
    bufs = (sa_ref, sb_ref)

    def full_trips(j0, count, m_run, mb):
        for t in range(count):
            mb_next = scores(j0 + t + 1, bufs[(t + 1) % 2])
            m_run = softmax_pv(j0 + t, bufs[t % 2], mb, m_run)
            mb = mb_next
        return m_run, mb

    m_run, mb_a = lax.fori_loop(
        0, n_loops, lambda kk, c: full_trips(LOOP_TRIPS * kk, LOOP_TRIPS, *c),
        (jnp.full((1, width), NEG_INF, F32), scores(0, sa_ref)))
    j_last = LOOP_TRIPS * n_loops

    for rem in range(LOOP_TRIPS):
        @pl.when(n_full - j_last == rem)
        def _(rem=rem):
            m_fin, _ = full_trips(j_last, rem, m_run, mb_a)
            causal_tail(j_last + rem, bufs[rem % 2], m_fin)

    o_s = [acc_ref[g, 0:HEAD_DIM, :] / acc_ref[g, HEAD_DIM:HEAD_DIM + 1, :] for g in range(NSA_GROUPS)]

    outs = []
    for g in range(NSA_GROUPS):
        heads = []
        for r in range(NSA_GROUP_SIZE):
            base = (g * NSA_GROUP_SIZE + r) * 3
            cols = slice(r * Q_TILE, (r + 1) * Q_TILE)
            heads.append(gt_ref[0, base:base + 1, :] * o_c[g][:, cols]
                         + gt_ref[0, base + 1:base + 2, :] * o_s[g][:, cols]
                         + gt_ref[0, base + 2:base + 3, :] * o_w[g][:, cols])
        for pair in range(NSA_GROUP_SIZE // 2):
            outs.append(jnp.concatenate(heads[2 * pair:2 * pair + 2], axis=0).T)
    o_ref[0] = jnp.concatenate(outs, axis=1)


def _nsa_attn(q, gates_t, kc, vct, ks, vst, kw, vwt, ovt):
    b, t, _ = q.shape
    ncp = kc.shape[1]
    ns = ovt.shape[0]
    sel_keys = SEL_TRIP * KEY_BLOCK
    assert t % sel_keys == 0 and sel_keys % Q_TILE == 0 and t >= WINDOW + Q_TILE and ncp % KEY_BLOCK == 0
    ind = (np.arange(sel_keys)[:, None] // SEL_BLOCK == np.arange(LANES)[None, :])
    ind = jnp.asarray(ind.astype(np.float32), dtype=BF16)
    full_rows = lambda bi, qi: (bi, 0, 0)
    return pl.pallas_call(
        _nsa_kernel,
        grid=(b, t // Q_TILE),
        in_specs=[
            pl.BlockSpec((1, Q_TILE, NSA_WIDTH), lambda bi, qi: (bi, qi, 0)),
            pl.BlockSpec((1, GATE_ROWS, Q_TILE), lambda bi, qi: (bi, 0, qi)),
            pl.BlockSpec((1, ncp, LANES), full_rows),
            pl.BlockSpec((1, LANES, ncp), full_rows),
            pl.BlockSpec((1, t, LANES), full_rows),
            pl.BlockSpec((1, LANES, t), full_rows),
            pl.BlockSpec((1, t, LANES), full_rows),
            pl.BlockSpec((1, LANES, t), full_rows),
            _const_spec((ns, ncp)),
            _const_spec((sel_keys, LANES)),
        ],
        out_specs=pl.BlockSpec((1, Q_TILE, NSA_WIDTH), lambda bi, qi: (bi, qi, 0)),
        out_shape=jax.ShapeDtypeStruct((b, t, NSA_WIDTH), F32),
        scratch_shapes=[pltpu.VMEM((ns, NSA_GROUPS * Q_TILE), F32),
                        pltpu.VMEM((sel_keys, NSA_GROUPS * NSA_GROUP_SIZE * Q_TILE), F32),
                        pltpu.VMEM((sel_keys, NSA_GROUPS * NSA_GROUP_SIZE * Q_TILE), F32),
                        pltpu.VMEM((NSA_GROUPS, HEAD_DIM + BF16_ROWS, NSA_GROUP_SIZE * Q_TILE), F32)],
        compiler_params=_params(("arbitrary", "arbitrary")),
        name="nsa_attn",
    )(q, gates_t, kc, vct, ks, vst, kw, vwt, ovt, ind)


def _same_head_mask():
    bi = lax.broadcasted_iota(jnp.int32, (HALF, HALF), 0) // HEAD_DIM
    bj = lax.broadcasted_iota(jnp.int32, (HALF, HALF), 1) // HEAD_DIM
    return bi == bj


def _bd_rows(x, same_head):
    xb = x.astype(BF16)
    tiled = jnp.concatenate([xb] * (HALF // x.shape[0]), axis=0)
    return jnp.where(same_head, tiled, jnp.zeros((), BF16))


def _rwkv_prep_stages(in_refs, par_refs, out_set, *, chunks):
    r_ref, k_ref, v_ref, lo_ref = in_refs
    w0_ref, w2_ref, a0_ref, a2_ref, g2_ref, kk_ref, ka_ref, rk_ref = par_refs
    wm_ref, zm_ref, arb_ref, rkv_ref, rt_ref, vb_ref, bkt_ref, gl_ref, bonus_ref, gate_ref = out_set
    c = RWKV_CHUNK
    nb = r_ref.shape[0]
    same_head = _same_head_mask()
    ones_bd = same_head.astype(BF16)
    n_tok = chunks * c
    row_i = lax.broadcasted_iota(jnp.int32, (n_tok, n_tok), 0)
    col_i = lax.broadcasted_iota(jnp.int32, (n_tok, n_tok), 1)
    tril_incl = ((row_i >= col_i) & (row_i // c == col_i // c)).astype(BF16)
    t_id = lax.broadcasted_iota(jnp.int32, (c, HALF), 0)
    j_id = lax.broadcasted_iota(jnp.int32, (c, HALF), 1) % HEAD_DIM
    strict_lower = t_id > j_id
    incl_lower = t_id >= j_id
    eye_all = (t_id == j_id).astype(F32)
    dot = lambda x, y: jnp.dot(x, y, preferred_element_type=F32)

    def bd_cols(xt):
        xb = xt.astype(BF16)
        return jnp.where(same_head, jnp.concatenate([xb, xb], axis=1), jnp.zeros((), BF16))

    halves = RWKV_WIDTH // HALF
    groups = [(b, ch) for b in range(nb) for ch in range(chunks)]
    chains = [(gi, hh) for gi in range(len(groups)) for hh in range(halves)]
    rows = [slice(ch * c, (ch + 1) * c) for ch in range(chunks)]
    lanes = [slice(hh * HALF, (hh + 1) * HALF) for hh in range(halves)]
    each = lambda fn: [fn(n, gi, lanes[hh]) for n, (gi, hh) in enumerate(chains)]

    lo = [lo_ref[b] for b in range(nb)]
    zs = [-(w0_ref[...] + _bdot(jnp.tanh(x[:, 0:DECAY_LORA]), w2_ref[...])) for x in lo]
    yield
    lr_b = [_sigmoid(a0_ref[...] + _bdot(x[:, DECAY_LORA:DECAY_LORA + AAA_LORA], a2_ref[...])) for x in lo]
    yield
    for b in range(nb):
        gate_ref[b] = _bdot(_sigmoid(lo[b][:, DECAY_LORA + AAA_LORA:]), g2_ref[...])
        vb_ref[b] = v_ref[b].astype(BF16)
    yield
    log_decay = []
    for z in zs:
        softplus = jnp.maximum(z, 0.0) + jnp.log(1.0 + jnp.exp(-jnp.abs(z)))
        log_decay.append(-jnp.exp(-softplus - 0.5))
    cum_b = [_dot3_left(tril_incl, x) for x in log_decay]
    yield
    r = [r_ref[b, rows[ch], :] for b, ch in groups]
    k = [k_ref[b, rows[ch], :] for b, ch in groups]
    v = [v_ref[b, rows[ch], :] for b, ch in groups]
    lr = [lr_b[b][rows[ch], :] for b, ch in groups]
    cum = [cum_b[b][rows[ch], :] for b, ch in groups]
    ld = [log_decay[b][rows[ch], :] for b, ch in groups]
    g_incl = [jnp.exp(x) for x in cum]
    g_excl = [jnp.exp(x - y) for x, y in zip(cum, ld)]
    g_inv = [jnp.exp(-x) for x in cum]
    for gi, (b, ch) in enumerate(groups):
        gl_ref[b, ch * SUBLANES:(ch + 1) * SUBLANES, :] = jnp.broadcast_to(
            g_incl[gi][c - 1:c, :], (SUBLANES, RWKV_WIDTH))

    kk = each(lambda n, gi, ln: k[gi][:, ln] * kk_ref[:, ln])
    k2 = each(lambda n, gi, ln: k[gi][:, ln] * (1.0 + (lr[gi][:, ln] - 1.0) * ka_ref[:, ln]))
    sums = each(lambda n, gi, ln: _dot2_right(
        jnp.concatenate([kk[n] * kk[n], r[gi][:, ln] * k2[n] * rk_ref[:, ln]], axis=0), ones_bd))
    ssq = [x[0:c] for x in sums]
    for n, (gi, hh) in enumerate(chains):
        b, ch = groups[gi]
        bonus_ref[b, rows[ch], lanes[hh]] = sums[n][c:] * v[gi][:, lanes[hh]]
    yield
    kk = each(lambda n, gi, ln: kk[n] / jnp.maximum(jnp.sqrt(ssq[n]), 1e-12))
    at = each(lambda n, gi, ln: -kk[n] * g_excl[gi][:, ln])
    bt = each(lambda n, gi, ln: kk[n] * lr[gi][:, ln] * g_inv[gi][:, ln])
    kt = each(lambda n, gi, ln: k2[n] * g_inv[gi][:, ln])
    rt = each(lambda n, gi, ln: r[gi][:, ln] * g_incl[gi][:, ln])
    for n, (gi, hh) in enumerate(chains):
        b, ch = groups[gi]
        rt_ref[b, rows[ch], lanes[hh]] = rt[n].astype(BF16)
        bkt_ref[b, (ch * halves + hh) * HALF:(ch * halves + hh + 1) * HALF, :] = (
            jnp.concatenate([bt[n], kt[n]], axis=0).T.astype(BF16))

    bt_bd = each(lambda n, gi, ln: bd_cols(jnp.concatenate([bt[n], bt[n]], axis=0).T))
    ar = each(lambda n, gi, ln: jnp.concatenate([at[n], rt[n]], axis=0).astype(BF16))
    ab = each(lambda n, gi, ln: dot(ar[n], bt_bd[n]))
    yield
    kt_bd = each(lambda n, gi, ln: bd_cols(jnp.concatenate([kt[n], kt[n]], axis=0).T))
    ak = each(lambda n, gi, ln: dot(ar[n], kt_bd[n]))
    yield
    a_ab = [jnp.where(strict_lower, x[0:c], 0.0) for x in ab]
    a_rb = [jnp.where(incl_lower, x[c:], 0.0) for x in ab]
    a_ak = [jnp.where(strict_lower, x[0:c], 0.0) for x in ak]
    a_rk = [jnp.where(incl_lower, x[c:], 0.0) for x in ak]
    for n, (gi, hh) in enumerate(chains):
        b, ch = groups[gi]
        arb_ref[b, rows[ch], lanes[hh]] = a_rb[n].astype(BF16)

    inv = [eye_all + x for x in a_ab]
    pw = [dot(x.astype(BF16), _bd_rows(x, same_head)) for x in a_ab]
    yield
    for _ in range(4):
        both = [dot(jnp.concatenate([x, y], axis=0).astype(BF16), _bd_rows(x, same_head))
                for x, y in zip(pw, inv)]
        pw = [x[0:c] for x in both]
        inv = [y + x[c:] for x, y in zip(both, inv)]
        yield
    inv = [y + dot(y.astype(BF16), _bd_rows(x, same_head)) for x, y in zip(pw, inv)]
    yield

    v_bd = each(lambda n, gi, ln: _bd_rows(v[gi][:, ln], same_head))
    inv_b = [x.astype(BF16) for x in inv]
    wm = each(lambda n, gi, ln: dot(inv_b[n], _bd_rows(at[n], same_head)))
    yield
    av = each(lambda n, gi, ln: dot(jnp.concatenate([a_ak[n], a_rk[n]], axis=0).astype(BF16), v_bd[n]))
    yield
    zm = each(lambda n, gi, ln: dot(inv_b[n], _bd_rows(av[n][0:c], same_head)))
    for n, (gi, hh) in enumerate(chains):
        b, ch = groups[gi]
        wm_ref[b, rows[ch], lanes[hh]] = wm[n].astype(BF16)
        zm_ref[b, rows[ch], lanes[hh]] = zm[n]
        rkv_ref[b, rows[ch], lanes[hh]] = av[n][c:]


def _rwkv_scan_stages(in_set, gng_ref, gnb_ref, o_ref, s_ref, *, chunks):
    wm_ref, zm_ref, arb_ref, rkv_ref, rt_ref, vb_ref, bkt_ref, gl_ref, bonus_ref, gate_ref = in_set
    c = RWKV_CHUNK
    halves = RWKV_WIDTH // HALF
    same_head = _same_head_mask()
    ones_bd = same_head.astype(BF16)
    chains = [(b, hh) for b in range(o_ref.shape[0]) for hh in range(halves)]
    lanes = [slice(hh * HALF, (hh + 1) * HALF) for hh in range(halves)]
    dot = lambda x, y: jnp.dot(x, y, preferred_element_type=F32)

    for ch in range(chunks):
        rw = slice(ch * c, (ch + 1) * c)
        each = lambda fn: [fn(n, b, lanes[hh]) for n, (b, hh) in enumerate(chains)]
        s0 = [s_ref[b, hh] for b, hh in chains]
        s0b = [x.astype(BF16) for x in s0]
        u = each(lambda n, b, ln: dot(wm_ref[b, rw, ln], s0b[n]) + zm_ref[b, rw, ln])
        yield
        y0 = each(lambda n, b, ln: dot(rt_ref[b, rw, ln], s0b[n]) + rkv_ref[b, rw, ln])
        yield
        y = each(lambda n, b, ln: y0[n] + dot(arb_ref[b, rw, ln], _bd_rows(u[n], same_head)))
        yield
        uv = each(lambda n, b, ln: jnp.concatenate([u[n].astype(BF16), vb_ref[b, rw, ln]], axis=0))
        upd = [dot(bkt_ref[b, (ch * halves + hh) * HALF:(ch * halves + hh + 1) * HALF, :], uv[n])
               for n, (b, hh) in enumerate(chains)]
        for n, (b, hh) in enumerate(chains):
            g_last = jnp.broadcast_to(gl_ref[b, ch * SUBLANES:ch * SUBLANES + 1, lanes[hh]], (LANES, HALF)).T
            g_col = jnp.concatenate([g_last, g_last], axis=1)
            s_ref[b, hh] = g_col * (s0[n] + jnp.where(same_head, upd[n], 0.0))
        yield
        mu = [_dot2_right(x, ones_bd) * (1.0 / HEAD_DIM) for x in y]
        yield
        yc = [x - m for x, m in zip(y, mu)]
        var = [_dot2_right(x * x, ones_bd) * (1.0 / HEAD_DIM) for x in yc]
        yield
        for n, (b, hh) in enumerate(chains):
            ln = lanes[hh]
            yn = yc[n] * lax.rsqrt(var[n] + RWKV_GN_EPS) * gng_ref[:, ln] + gnb_ref[:, ln]
            o_ref[b, rw, ln] = (yn + bonus_ref[b, rw, ln]) * gate_ref[b, rw, ln]


def _rwkv_kernel(*refs, chunks):
    in_refs, par_refs = refs[0:4], refs[4:12]
    gng_ref, gnb_ref, o_ref, s_ref = refs[12:16]
    sets = (refs[16:26], refs[26:36])
    step = pl.program_id(0)

    @pl.when(step == 0)
    def _():
        s_ref[...] = jnp.zeros_like(s_ref)
        for ref in sets[1]:
            ref[...] = jnp.zeros_like(ref)

    def run(write_set, read_set):
        prep = _rwkv_prep_stages(in_refs, par_refs, write_set, chunks=chunks)
        scan = _rwkv_scan_stages(read_set, gng_ref, gnb_ref, o_ref, s_ref, chunks=chunks)
        live = [prep, scan]
        while live:
            for gen in list(live):
                if next(gen, "done") == "done":
                    live.remove(gen)

    @pl.when(step % 2 == 0)
    def _():
        run(sets[0], sets[1])

    @pl.when(step % 2 == 1)
    def _():
        run(sets[1], sets[0])


def _rwkv(r, k, v, lo, w0, w2, a0, a2, g2, k_k, k_a, r_k, gn_g, gn_b, *, chunks=2):
    b, t, width = r.shape
    c = RWKV_CHUNK
    n_tok = chunks * c
    n_groups = t // n_tok
    halves = width // HALF
    vec = lambda a: a.reshape(1, width)
    cs = lambda a: _const_spec(a.shape)
    args = [vec(w0), w2.astype(BF16), vec(a0), a2.astype(BF16), g2.astype(BF16),
            vec(k_k), vec(k_a), vec(r_k), vec(gn_g), vec(gn_b)]
    ahead = lambda s: (0, jnp.minimum(s, n_groups - 1), 0)
    behind = lambda s: (0, jnp.maximum(s - 1, 0), 0)
    tok = lambda dt: pltpu.VMEM((b, n_tok, width), dt)
    scratch_set = [tok(BF16), tok(F32), tok(BF16), tok(F32), tok(BF16), tok(BF16),
                   pltpu.VMEM((b, chunks * halves * HALF, LANES), BF16),
                   pltpu.VMEM((b, chunks * SUBLANES, width), F32), tok(F32), tok(F32)]
    return pl.pallas_call(
        functools.partial(_rwkv_kernel, chunks=chunks),
        grid=(n_groups + 1,),
        in_specs=[pl.BlockSpec((b, n_tok, width), ahead)] * 3 + [pl.BlockSpec((b, n_tok, LORA_COLS), ahead)]
                 + [cs(a) for a in args],
        out_specs=pl.BlockSpec((b, n_tok, width), behind),
        out_shape=jax.ShapeDtypeStruct((b, t, width), F32),
        scratch_shapes=[pltpu.VMEM((b, halves, HALF, HALF), F32)] + scratch_set + scratch_set,
        compiler_params=_params(("arbitrary",)),
        name="rwkv7",
    )(r, k, v, lo, *args)


def _mem_kv_kernel(m_ref, g_ref, wk_ref, wv_ref, kt_ref, v_ref):
    m = _rms(m_ref[0], g_ref[...]).astype(BF16)
    kt_ref[0] = jnp.dot(m, wk_ref[...], preferred_element_type=F32).T.astype(BF16)
    v_ref[0] = jnp.dot(m, wv_ref[...], preferred_element_type=F32).astype(BF16)


def _mem_kv(mem, g, wk, wv):
    b, mt, d = mem.shape
    return pl.pallas_call(
        _mem_kv_kernel,
        grid=(b,),
        in_specs=[pl.BlockSpec((1, mt, d), lambda bi: (bi, 0, 0)), _const_spec((1, d)),
                  _const_spec((d, d)), _const_spec((d, d))],
        out_specs=[pl.BlockSpec((1, d, mt), lambda bi: (bi, 0, 0)),
                   pl.BlockSpec((1, mt, d), lambda bi: (bi, 0, 0))],
        out_shape=[jax.ShapeDtypeStruct((b, d, mt), BF16), jax.ShapeDtypeStruct((b, mt, d), BF16)],
        compiler_params=_params(("arbitrary",)),
        name="mem_kv",
    )(mem, g.reshape(1, d), wk.astype(BF16), wv.astype(BF16))


def _out_mem_kernel(x_ref, on_ref, or_ref, ng_ref, wo1_ref, wo2_ref, mpost_ref,
                    mpre_ref, wq_ref, kt_ref, v_ref, wo_ref, mempost_ref, o_ref, *, parts):
    tm = x_ref.shape[1] // parts
    rows = [slice(n * tm, (n + 1) * tm) for n in range(parts)]
    dot = lambda a, b: jnp.dot(a, b, preferred_element_type=F32)
    d = x_ref.shape[-1]
    hd = d // MEM_HEADS

    a = [_rms(on_ref[0, rw, :], ng_ref[...]).astype(BF16) for rw in rows]
    mixed = [dot(a[n], wo1_ref[...]) + dot(or_ref[0, rw, :].astype(BF16), wo2_ref[...])
             for n, rw in enumerate(rows)]
    x = [x_ref[0, rw, :] + _rms(mixed[n], mpost_ref[...]) for n, rw in enumerate(rows)]
    h = [_rms(xn, mpre_ref[...]).astype(BF16) for xn in x]
    q = [(dot(hn, wq_ref[...]) * (hd ** -0.5)).astype(BF16) for hn in h]
    heads = [[] for _ in rows]
    for hi in range(MEM_HEADS):
        cols = slice(hi * hd, (hi + 1) * hd)
        s = [dot(qn[:, cols], kt_ref[0, cols, :]) for qn in q]
        e = [jnp.exp(sn - jnp.max(sn, axis=-1, keepdims=True)) for sn in s]
        p = [(en / jnp.sum(en, axis=-1, keepdims=True)).astype(BF16) for en in e]
        for n in range(parts):
            heads[n].append(dot(p[n], v_ref[0, :, cols]))
    att = [dot(jnp.concatenate(hn, axis=-1).astype(BF16), wo_ref[...]) for hn in heads]
    for n, rw in enumerate(rows):
        o_ref[0, rw, :] = x[n] + _rms(att[n], mempost_ref[...])


def _out_mem(x3, o_nsa, o_rwkv, nsa_g, w_out, mix_post_g, mem_pre_g, wq, kt, vm, wo, mem_post_g,
             *, tm=1024, parts=4):
    b, t, d = x3.shape
    mt = vm.shape[1]
    row = lambda bi, ti: (bi, ti, 0)
    per_b = lambda bi, ti: (bi, 0, 0)
    w_out = w_out.astype(BF16)
    return pl.pallas_call(
        functools.partial(_out_mem_kernel, parts=parts),
        grid=(b, t // tm),
        in_specs=[
            pl.BlockSpec((1, tm, d), row),
            pl.BlockSpec((1, tm, NSA_WIDTH), row),
            pl.BlockSpec((1, tm, RWKV_WIDTH), row),
            _const_spec((1, NSA_WIDTH)),
            _const_spec((NSA_WIDTH, d)),
            _const_spec((RWKV_WIDTH, d)),
            _const_spec((1, d)),
            _const_spec((1, d)),
            _const_spec((d, d)),
            pl.BlockSpec((1, d, mt), per_b),
            pl.BlockSpec((1, mt, d), per_b),
            _const_spec((d, d)),
            _const_spec((1, d)),
        ],
        out_specs=pl.BlockSpec((1, tm, d), row),
        out_shape=jax.ShapeDtypeStruct((b, t, d), F32),
        compiler_params=_params(("arbitrary", "arbitrary")),
        name="out_mem",
    )(x3, o_nsa, o_rwkv, nsa_g.reshape(1, -1), w_out[:NSA_WIDTH], w_out[NSA_WIDTH:],
      mix_post_g.reshape(1, d), mem_pre_g.reshape(1, d), wq.astype(BF16), kt, vm,
      wo.astype(BF16), mem_post_g.reshape(1, d))


def _rope_tables(pos):
    half = HEAD_DIM // 2
    inv = ROPE_THETA ** (-jnp.arange(half, dtype=F32) / half)
    ang = pos.astype(F32)[:, None] * inv[None, :]
    cos, sin = jnp.cos(ang), jnp.sin(ang)
    cos_t = jnp.concatenate([cos, cos, cos, cos], axis=-1)
    sin_t = jnp.concatenate([-sin, sin, -sin, sin], axis=-1)
    return cos_t, sin_t


def _overlap_t(ns, ncp):
    c0 = np.arange(ncp)[None, :] * CMP_STRIDE
    s0 = np.arange(ns)[:, None] * SEL_BLOCK
    ov = (c0 < s0 + SEL_BLOCK) & (c0 + CMP_BLOCK > s0) & (np.arange(ncp)[None, :] < ncp - 1)
    return jnp.asarray(ov.astype(np.float32), dtype=BF16)


def _pad_cols(w, n):
    return jnp.pad(w, ((0, 0), (0, n - w.shape[1])))


def _cmp_weights(pe, w1, w2):
    per = CMP_STRIDE
    w1r = w1.reshape(CMP_BLOCK, HEAD_DIM, CMP_HIDDEN)
    blocks = []
    for part in range(CMP_BLOCK // per):
        for g in range(NSA_GROUPS):
            z = jnp.zeros((per, NSA_GROUPS, HEAD_DIM, CMP_HIDDEN), F32)
            z = z.at[:, g].set(w1r[part * per:(part + 1) * per])
            blocks.append(z.reshape(per * NSA_GROUPS * HEAD_DIM, CMP_HIDDEN))
    w1cat = jnp.concatenate(blocks, axis=1).astype(BF16)
    w2bd = jnp.zeros((NSA_GROUPS * CMP_HIDDEN, NSA_GROUPS * HEAD_DIM), F32)
    for g in range(NSA_GROUPS):
        w2bd = w2bd.at[g * CMP_HIDDEN:(g + 1) * CMP_HIDDEN, g * HEAD_DIM:(g + 1) * HEAD_DIM].set(w2)
    pe8 = jnp.broadcast_to(pe.reshape(1, CMP_BLOCK * HEAD_DIM), (SUBLANES, CMP_BLOCK * HEAD_DIM))
    return pe8.astype(BF16), w1.astype(BF16), w1cat, w2bd.astype(BF16)


def kernel(x, mem, ffn1_pre_g, ffn1_w_gate, ffn1_w_up, ffn1_w_down, ffn1_post_g, mix_pre_g, w_in, cmp_pe_k, cmp_w1_k, cmp_w2_k, cmp_pe_v, cmp_w1_v, cmp_w2_v, nsa_out_g, rwkv_mu, rwkv_w0, rwkv_w2, rwkv_a0, rwkv_a2, rwkv_g2, rwkv_k_k, rwkv_k_a, rwkv_r_k, rwkv_gn_g, rwkv_gn_b, w_out, mix_post_g, mem_pre_g, mem_kv_g, mem_wq, mem_wk, mem_wv, mem_wo, mem_post_g, ffn2_pre_g, ffn2_w_gate, ffn2_w_up, ffn2_w_down, ffn2_post_g):
    b, t, d = x.shape
    ncp = t // CMP_STRIDE
    ns = t // SEL_BLOCK
    cos_t, sin_t = _rope_tables(jnp.arange(t))
    cos_c, sin_c = _rope_tables(jnp.arange(ncp) * CMP_STRIDE + (CMP_BLOCK - 1))
    ovt = _overlap_t(ns, ncp)

    for l in range(ffn1_pre_g.shape[0]):
        x2 = _ffn_block(x.reshape(b * t, d), ffn1_pre_g[l], ffn1_w_gate[l], ffn1_w_up[l],
                        ffn1_w_down[l], ffn1_post_g[l])
        x3 = x2.reshape(b, t, d)

        wi = w_in[l]
        nsa_w = NSA_WIDTH + 6 * NSA_KV_WIDTH
        gate_w = wi[:, nsa_w:nsa_w + 3 * NSA_HEADS]
        w_cols = jnp.concatenate([wi[:, :nsa_w], _pad_cols(gate_w, LANES),
                                  wi[:, nsa_w + 3 * NSA_HEADS:]], axis=1).astype(BF16)
        (q, k_cmp, v_cmp, k_slc, v_slc_t, k_win, v_win_t, gates_t, r, k, v, lo) = _in_proj(
            x3, mix_pre_g[l], w_cols, rwkv_mu[l].reshape(1, -1), cos_t, sin_t)

        pek, w1k, w1kc, w2k = _cmp_weights(cmp_pe_k[l], cmp_w1_k[l], cmp_w2_k[l])
        pev, w1v, w1vc, w2v = _cmp_weights(cmp_pe_v[l], cmp_w1_v[l], cmp_w2_v[l])
        kc, vct = _compress(k_cmp, v_cmp,
                            pek, w1k, w1kc, w2k, pev, w1v, w1vc, w2v, cos_c, sin_c)
        o_nsa = _nsa_attn(q, gates_t, kc, vct, k_slc, v_slc_t, k_win, v_win_t, ovt)

        o_rwkv = _rwkv(r, k, v, lo, rwkv_w0[l], rwkv_w2[l], rwkv_a0[l], rwkv_a2[l], rwkv_g2[l],
                       rwkv_k_k[l], rwkv_k_a[l], rwkv_r_k[l], rwkv_gn_g[l], rwkv_gn_b[l])

        kt, vm = _mem_kv(mem, mem_kv_g[l], mem_wk[l], mem_wv[l])
        x4 = _out_mem(x3, o_nsa, o_rwkv, nsa_out_g[l], w_out[l], mix_post_g[l], mem_pre_g[l],
                      mem_wq[l], kt, vm, mem_wo[l], mem_post_g[l])

        x = _ffn_block(x4.reshape(b * t, d), ffn2_pre_g[l], ffn2_w_gate[l], ffn2_w_up[l],
                       ffn2_w_down[l], ffn2_post_g[l]).reshape(b, t, d)
    return x
```

```python
import functools

import numpy as np
import jax
import jax.numpy as jnp
from jax import lax
from jax.experimental import pallas as pl
from jax.experimental.pallas import tpu as pltpu

F32 = jnp.float32
BF16 = jnp.bfloat16

HEAD_DIM = 64
NSA_HEADS = 8
NSA_GROUPS = 2
NSA_GROUP_SIZE = 4
NSA_WIDTH = 512
NSA_KV_WIDTH = 128
CMP_BLOCK = 32
CMP_STRIDE = 16
CMP_HIDDEN = 256
SEL_BLOCK = 64
SEL_TOPK = 16
SEL_FORCE = 1e4
SEL_FORCED = 3
WINDOW = 512
RWKV_WIDTH = 512
DECAY_LORA = 64
AAA_LORA = 64
GATE_LORA = 128
RWKV_GN_EPS = 64e-5
MEM_HEADS = 4
ROPE_THETA = 10000.0
NORM_EPS = 1e-6
NEG_INF = -1e30
LOG2_E = 1.4426950408889634

LORA_COLS = DECAY_LORA + AAA_LORA + GATE_LORA
RWKV_COLS = 3 * RWKV_WIDTH + LORA_COLS

LANES = 128
SUBLANES = 8
BF16_ROWS = 16
GATE_ROWS = 32
KEY_BLOCK = 128
Q_TILE = 128
SEL_TRIP = 4
LOOP_TRIPS = 4
RWKV_CHUNK = 64
HALF = 256
VMEM_LIMIT = 56 * 1024 * 1024


def _bdot(a, b):
    return jnp.dot(a.astype(BF16), b.astype(BF16), preferred_element_type=F32)


def _split3(x):
    h1 = x.astype(BF16)
    r1 = x - h1.astype(F32)
    h2 = r1.astype(BF16)
    r2 = r1 - h2.astype(F32)
    return h1, h2, r2.astype(BF16)


def _split2(x):
    h1 = x.astype(BF16)
    return h1, (x - h1.astype(F32)).astype(BF16)


def _dot2_right(x, m):
    h1, h2 = _split2(x)
    return (jnp.dot(h1, m, preferred_element_type=F32) + jnp.dot(h2, m, preferred_element_type=F32))


def _dot3_right(x, m):
    h1, h2, h3 = _split3(x)
    d = lambda h: jnp.dot(h, m, preferred_element_type=F32)
    return d(h1) + d(h2) + d(h3)


def _dot2_left(m, x):
    h1, h2 = _split2(x)
    return (jnp.dot(m, h1, preferred_element_type=F32) + jnp.dot(m, h2, preferred_element_type=F32))


def _dot3_left(m, x):
    h1, h2, h3 = _split3(x)
    d = lambda h: jnp.dot(m, h, preferred_element_type=F32)
    return d(h1) + d(h2) + d(h3)


def _rms(x, g):
    return x * lax.rsqrt(jnp.mean(x * x, axis=-1, keepdims=True) + NORM_EPS) * g


def _silu(x):
    return x / (1.0 + jnp.exp(-x))


def _sigmoid(x):
    return 1.0 / (1.0 + jnp.exp(-x))


def _const_spec(shape):
    nd = len(shape)
    return pl.BlockSpec(shape, lambda *_: (0,) * nd)


def _params(sem):
    return pltpu.CompilerParams(dimension_semantics=sem, vmem_limit_bytes=VMEM_LIMIT)


def _ffn_kernel(x_ref, pre_ref, wg_ref, wu_ref, wd_ref, post_ref, o_ref, *, ff_chunk):
    x = x_ref[...]
    h = _rms(x, pre_ref[...]).astype(BF16)
    d_ff = wg_ref.shape[1]
    acc = jnp.zeros(x.shape, F32)
    for c0 in range(0, d_ff, ff_chunk):
        g = jnp.dot(h, wg_ref[:, c0:c0 + ff_chunk], preferred_element_type=F32)
        u = jnp.dot(h, wu_ref[:, c0:c0 + ff_chunk], preferred_element_type=F32)
        a = (_silu(g) * u).astype(BF16)
        acc = acc + jnp.dot(a, wd_ref[c0:c0 + ff_chunk, :], preferred_element_type=F32)
    o_ref[...] = x + 0.5 * _rms(acc, post_ref[...])


def _ffn_block(x2, pre_g, wg, wu, wd, post_g, *, tm=512, ff_chunk=256):
    m, d = x2.shape
    d_ff = wg.shape[1]
    return pl.pallas_call(
        functools.partial(_ffn_kernel, ff_chunk=ff_chunk),
        grid=(m // tm,),
        in_specs=[
            pl.BlockSpec((tm, d), lambda i: (i, 0)),
            _const_spec((1, d)),
            _const_spec((d, d_ff)),
            _const_spec((d, d_ff)),
            _const_spec((d_ff, d)),
            _const_spec((1, d)),
        ],
        out_specs=pl.BlockSpec((tm, d), lambda i: (i, 0)),
        out_shape=jax.ShapeDtypeStruct((m, d), F32),
        compiler_params=_params(("arbitrary",)),
        name="ffn_block",
    )(x2, pre_g.reshape(1, d), wg.astype(BF16), wu.astype(BF16), wd.astype(BF16),
      post_g.reshape(1, d))


def _swap_halves(x):
    n = x.shape[-1]
    lane = lax.broadcasted_iota(jnp.int32, x.shape, x.ndim - 1)
    fwd = pltpu.roll(x, n - HEAD_DIM // 2, x.ndim - 1)
    bwd = pltpu.roll(x, HEAD_DIM // 2, x.ndim - 1)
    return jnp.where((lane % HEAD_DIM) < HEAD_DIM // 2, fwd, bwd)


def _rope(x, cos, sin_signed):
    reps = x.shape[-1] // LANES
    c = jnp.concatenate([cos] * reps, axis=-1) if reps > 1 else cos
    s = jnp.concatenate([sin_signed] * reps, axis=-1) if reps > 1 else sin_signed
    return x * c + _swap_halves(x) * s


def _in_proj_kernel(x_ref, g_ref, w_ref, mu_ref, cos_ref, sin_ref,
                    q_ref, kc_ref, vc_ref, ks_ref, vst_ref, kw_ref, vwt_ref, gt_ref,
                    r_ref, k_ref, v_ref, lo_ref, carry_ref):
    @pl.when(pl.program_id(1) == 0)
    def _():
        carry_ref[...] = jnp.zeros_like(carry_ref)

    h = _rms(x_ref[0], g_ref[...]).astype(BF16)
    p = jnp.dot(h, w_ref[...], preferred_element_type=F32)
    cos = cos_ref[...]
    sin = sin_ref[...]
    tm = p.shape[0]

    o = 0
    q = _rope(p[:, o:o + NSA_WIDTH], cos, sin) * (HEAD_DIM ** -0.5 * LOG2_E)
    q_ref[0] = q.astype(BF16)
    o += NSA_WIDTH
    kc_ref[0] = p[:, o:o + LANES]; o += LANES
    vc_ref[0] = p[:, o:o + LANES]; o += LANES
    ks_ref[0] = _rope(p[:, o:o + LANES], cos, sin).astype(BF16); o += LANES
    vst_ref[0] = p[:, o:o + LANES].T.astype(BF16); o += LANES
    kw_ref[0] = _rope(p[:, o:o + LANES], cos, sin).astype(BF16); o += LANES
    vwt_ref[0] = p[:, o:o + LANES].T.astype(BF16); o += LANES
    gates_t = _sigmoid(p[:, o:o + LANES]).T
    gt_ref[0] = gates_t[:gt_ref.shape[1], :]
    o += LANES

    rw = p[:, o:]
    row = lax.broadcasted_iota(jnp.int32, rw.shape, 0)
    prev = jnp.where(row == 0, carry_ref[0:1, :], pltpu.roll(rw, 1, 0))
    carry_ref[...] = jnp.broadcast_to(rw[tm - 1:tm, :], carry_ref.shape)
    mixed = rw + (prev - rw) * mu_ref[...]
    r_ref[0] = mixed[:, 0:RWKV_WIDTH]
    k_ref[0] = mixed[:, RWKV_WIDTH:2 * RWKV_WIDTH]
    v_ref[0] = mixed[:, 2 * RWKV_WIDTH:3 * RWKV_WIDTH]
    lo_ref[0] = mixed[:, 3 * RWKV_WIDTH:RWKV_COLS]


def _in_proj(x3, g, w_cols, mu_cols, cos_t, sin_t, *, tm=512):
    b, t, d = x3.shape
    n = w_cols.shape[1]
    row = lambda bi, ti: (bi, ti, 0)
    col = lambda bi, ti: (bi, 0, ti)
    out_shapes = [
        jax.ShapeDtypeStruct((b, t, NSA_WIDTH), BF16),
        jax.ShapeDtypeStruct((b, t, LANES), F32),
        jax.ShapeDtypeStruct((b, t, LANES), F32),
        jax.ShapeDtypeStruct((b, t, LANES), BF16),
        jax.ShapeDtypeStruct((b, LANES, t), BF16),
        jax.ShapeDtypeStruct((b, t, LANES), BF16),
        jax.ShapeDtypeStruct((b, LANES, t), BF16),
        jax.ShapeDtypeStruct((b, GATE_ROWS, t), F32),
        jax.ShapeDtypeStruct((b, t, RWKV_WIDTH), F32),
        jax.ShapeDtypeStruct((b, t, RWKV_WIDTH), F32),
        jax.ShapeDtypeStruct((b, t, RWKV_WIDTH), F32),
        jax.ShapeDtypeStruct((b, t, LORA_COLS), F32),
    ]
    out_specs = [
        pl.BlockSpec((1, tm, NSA_WIDTH), row),
        pl.BlockSpec((1, tm, LANES), row),
        pl.BlockSpec((1, tm, LANES), row),
        pl.BlockSpec((1, tm, LANES), row),
        pl.BlockSpec((1, LANES, tm), col),
        pl.BlockSpec((1, tm, LANES), row),
        pl.BlockSpec((1, LANES, tm), col),
        pl.BlockSpec((1, GATE_ROWS, tm), col),
        pl.BlockSpec((1, tm, RWKV_WIDTH), row),
        pl.BlockSpec((1, tm, RWKV_WIDTH), row),
        pl.BlockSpec((1, tm, RWKV_WIDTH), row),
        pl.BlockSpec((1, tm, LORA_COLS), row),
    ]
    return pl.pallas_call(
        _in_proj_kernel,
        grid=(b, t // tm),
        in_specs=[
            pl.BlockSpec((1, tm, d), row),
            _const_spec((1, d)),
            _const_spec((d, n)),
            _const_spec((1, RWKV_COLS)),
            pl.BlockSpec((tm, LANES), lambda bi, ti: (ti, 0)),
            pl.BlockSpec((tm, LANES), lambda bi, ti: (ti, 0)),
        ],
        out_specs=out_specs,
        out_shape=out_shapes,
        scratch_shapes=[pltpu.VMEM((SUBLANES, RWKV_COLS), F32)],
        compiler_params=_params(("arbitrary", "arbitrary")),
        name="in_proj",
    )(x3, g.reshape(1, d), w_cols, mu_cols, cos_t, sin_t)


def _compress_kernel(kin_ref, vin_ref, pek_ref, w1k_ref, w1kc_ref, w2k_ref,
                     pev_ref, w1v_ref, w1vc_ref, w2v_ref, cos_ref, sin_ref,
                     kc_ref, vct_ref):
    def phi(tok_ref, pe, w1, w1cat_ref, w2bd):
        n = tok_ref.shape[1] // CMP_STRIDE
        pr = None
        for l in range(CMP_STRIDE):
            rows = tok_ref[0, pl.ds(l, n, stride=CMP_STRIDE), :].astype(BF16)
            term = jnp.dot(rows, w1cat_ref[l * LANES:(l + 1) * LANES, :], preferred_element_type=F32)
            pr = term if pr is None else pr + term
        bias = jnp.dot(pe, w1, preferred_element_type=F32)[0:1, :]
        hid = []
        for g in range(NSA_GROUPS):
            top = pr[:, g * CMP_HIDDEN:(g + 1) * CMP_HIDDEN]
            bot = pr[:, (NSA_GROUPS + g) * CMP_HIDDEN:(NSA_GROUPS + g + 1) * CMP_HIDDEN]
            hid.append(top + pltpu.roll(bot, n - 1, 0) + bias)
        act = _silu(jnp.concatenate(hid, axis=-1)).astype(BF16)
        return jnp.dot(act, w2bd, preferred_element_type=F32)

    kc = phi(kin_ref, pek_ref[...], w1k_ref[...], w1kc_ref, w2k_ref[...])
    kc_ref[0] = _rope(kc, cos_ref[...], sin_ref[...]).astype(BF16)
    vc = phi(vin_ref, pev_ref[...], w1v_ref[...], w1vc_ref, w2v_ref[...])
    vct_ref[0] = vc.T.astype(BF16)


def _compress(kin, vin, pek, w1k, w1kc, w2k, pev, w1v, w1vc, w2v, cos_c, sin_c):
    b, t, width = kin.shape
    ncp = t // CMP_STRIDE
    blk = pl.BlockSpec((1, t, width), lambda bi: (bi, 0, 0))
    cs = lambda a: _const_spec(a.shape)
    return pl.pallas_call(
        _compress_kernel,
        grid=(b,),
        in_specs=[blk, blk, cs(pek), cs(w1k), cs(w1kc), cs(w2k),
                  cs(pev), cs(w1v), cs(w1vc), cs(w2v), cs(cos_c), cs(sin_c)],
        out_specs=[pl.BlockSpec((1, ncp, LANES), lambda bi: (bi, 0, 0)),
                   pl.BlockSpec((1, LANES, ncp), lambda bi: (bi, 0, 0))],
        out_shape=[jax.ShapeDtypeStruct((b, ncp, LANES), BF16),
                   jax.ShapeDtypeStruct((b, LANES, ncp), BF16)],
        compiler_params=_params(("arbitrary",)),
        name="nsa_compress",
    )(kin, vin, pek, w1k, w1kc, w2k, pev, w1v, w1vc, w2v, cos_c, sin_c)


def _nsa_kernel(q_ref, gt_ref, kc_ref, vct_ref, ks_ref, vst_ref, kw_ref, vwt_ref, ovt_ref, ind_ref,
                o_ref, bias_ref, sa_ref, sb_ref, acc_ref):
    i = pl.program_id(1)
    q0 = i * Q_TILE
    ncp = kc_ref.shape[1]
    ns = ovt_ref.shape[0]
    gw = NSA_GROUP_SIZE * Q_TILE
    width = NSA_GROUPS * gw
    lanes = [slice(g * gw, (g + 1) * gw) for g in range(NSA_GROUPS)]
    feat = [slice(g * HEAD_DIM, (g + 1) * HEAD_DIM) for g in range(NSA_GROUPS)]
    t_row = q0 + lax.broadcasted_iota(jnp.int32, (1, width), 1) % Q_TILE
    sel_keys = SEL_TRIP * KEY_BLOCK
    sel_rows = sel_keys // SEL_BLOCK
    win_keys = WINDOW + Q_TILE
    dot = lambda x, y: jnp.dot(x, y, preferred_element_type=F32)

    qf = q_ref[0].astype(F32)
    zeros_half = jnp.zeros((HEAD_DIM, Q_TILE), F32)
    parts = []
    for g in range(NSA_GROUPS):
        for pair in range(NSA_GROUP_SIZE // 2):
            slab_t = qf[:, (2 * g + pair) * LANES:(2 * g + pair + 1) * LANES].T
            for half in range(2):
                f = slab_t[half * HEAD_DIM:(half + 1) * HEAD_DIM, :]
                parts.append(jnp.concatenate([f, zeros_half] if g == 0 else [zeros_half, f], axis=0))
    qt = jnp.concatenate(parts, axis=1).astype(BF16)

    def v_ext(vt_ref, g, k0, n):
        return jnp.concatenate([vt_ref[0, feat[g], pl.ds(k0, n)], jnp.ones((BF16_ROWS, n), BF16)], axis=0)

    sw_ = NSA_GROUPS * Q_TILE
    n_vis = (q0 + Q_TILE - CMP_BLOCK) // CMP_STRIDE + 1
    blocks_needed = jnp.maximum(n_vis + KEY_BLOCK - 1, KEY_BLOCK) // KEY_BLOCK
    def compressed(nb):
        rows = nb * KEY_BLOCK
        sc = dot(kc_ref[0, 0:rows, :], qt)
        c_end = lax.broadcasted_iota(jnp.int32, (rows, width), 0) * CMP_STRIDE + (CMP_BLOCK - 1)
        c_mask = c_end <= t_row
        sc = jnp.where(c_mask, sc, NEG_INF)
        m_c = jnp.max(sc, axis=0, keepdims=True)
        e_c = jnp.where(c_mask, jnp.exp2(sc - m_c), 0.0)
        p_c = e_c / jnp.maximum(jnp.sum(e_c, axis=0, keepdims=True), 1e-30)
        outs, p_sum = [], []
        for g in range(NSA_GROUPS):
            outs.append(dot(vct_ref[0, feat[g], 0:rows], p_c[:, lanes[g]].astype(BF16)))
            acc = p_c[:, g * gw:g * gw + Q_TILE]
            for r in range(1, NSA_GROUP_SIZE):
                acc = acc + p_c[:, g * gw + r * Q_TILE:g * gw + (r + 1) * Q_TILE]
            p_sum.append(acc)
        return outs + [_dot2_left(ovt_ref[:, 0:rows], jnp.concatenate(p_sum, axis=1))]

    *o_c, imp = lax.switch(blocks_needed - 1,
                           [functools.partial(compressed, nb) for nb in range(1, ncp // KEY_BLOCK + 1)])

    w0 = pl.multiple_of(jnp.maximum(q0 - WINDOW, 0), KEY_BLOCK)
    sw = dot(kw_ref[0, pl.ds(w0, win_keys), :], qt)
    t_loc = t_row - w0
    k_loc = lax.broadcasted_iota(jnp.int32, (win_keys, width), 0)
    ok = k_loc <= t_loc
    old = lax.broadcasted_iota(jnp.int32, (KEY_BLOCK, width), 0) <= t_loc - WINDOW
    sw = jnp.where(ok, sw, NEG_INF)
    sw = jnp.concatenate([jnp.where(old, NEG_INF, sw[0:KEY_BLOCK]), sw[KEY_BLOCK:]], axis=0)
    m_w = jnp.max(sw, axis=0, keepdims=True)
    o_w = []
    for g in range(NSA_GROUPS):
        p = jnp.exp2(sw[:, lanes[g]] - m_w[:, lanes[g]]).astype(BF16)
        pv = dot(v_ext(vwt_ref, g, w0, win_keys), p)
        o_w.append(pv[0:HEAD_DIM, :] / pv[HEAD_DIM:HEAD_DIM + 1, :])

    s_id = lax.broadcasted_iota(jnp.int32, (ns, sw_), 0)
    cur = t_row[:, 0:sw_] // SEL_BLOCK
    forced = (s_id == 0) | (s_id == cur) | (s_id == cur - 1)
    score = jnp.where(forced, -3e38, jnp.where(s_id <= cur, imp, -SEL_FORCE))
    bias = jnp.where(forced, 0.0, NEG_INF)
    for _ in range(min(SEL_TOPK, ns) - SEL_FORCED):
        mx = jnp.max(score, axis=0, keepdims=True)
        first = jnp.min(jnp.where(score == mx, s_id, ns), axis=0, keepdims=True)
        hit = s_id == first
        score = jnp.where(hit, -3e38, score)
        bias = jnp.where(hit, 0.0, bias)
    bias_ref[...] = bias

    def scores(j, s_ref):
        k0 = pl.multiple_of(j * sel_keys, sel_keys)
        b0 = pl.multiple_of(j * sel_rows, sel_rows)
        rows = bias_ref[pl.ds(b0, sel_rows), :]
        rows = jnp.concatenate([rows[:, g * Q_TILE:(g + 1) * Q_TILE]
                                for g in range(NSA_GROUPS) for _ in range(NSA_GROUP_SIZE)], axis=1)
        rows = jnp.concatenate([rows, jnp.zeros_like(rows)], axis=0).astype(BF16)
        rhs = jnp.concatenate([qt, rows, jnp.zeros((LANES - rows.shape[0], width), BF16)], axis=0)
        lhs = jnp.concatenate([ks_ref[0, pl.ds(k0, sel_keys), :], ind_ref[...]], axis=1)
        s = dot(lhs, rhs)
        s_ref[...] = s
        return jnp.max(s, axis=0, keepdims=True)

    def softmax_pv(j, s_ref, mb, m_run):
        k0 = pl.multiple_of(j * sel_keys, sel_keys)
        m_new = jnp.maximum(m_run, mb)
        alpha = jnp.exp2(m_run - m_new)
        for g in range(NSA_GROUPS):
            p = jnp.exp2(s_ref[:, lanes[g]] - m_new[:, lanes[g]]).astype(BF16)
            pv = dot(v_ext(vst_ref, g, k0, sel_keys), p)
            acc_ref[g] = acc_ref[g] * alpha[:, lanes[g]] + pv
        return m_new

    def causal_tail(j, s_ref, m_run):
        k0 = pl.multiple_of(j * sel_keys, sel_keys)
        key = k0 + lax.broadcasted_iota(jnp.int32, (sel_keys, width), 0)
        s = jnp.where(key <= t_row, s_ref[...], NEG_INF)
        s_ref[...] = s
        softmax_pv(j, s_ref, jnp.max(s, axis=0, keepdims=True), m_run)

    acc_ref[...] = jnp.zeros_like(acc_ref)
    n_full = q0 // sel_keys
    n_loops = n_full // LOOP_TRIPS
    bufs = (sa_ref, sb_ref)

    def full_trips(j0, count, m_run, mb):
        for t in range(count):
            mb_next = scores(j0 + t + 1, bufs[(t + 1) % 2])
            m_run = softmax_pv(j0 + t, bufs[t % 2], mb, m_run)
            mb = mb_next
        return m_run, mb

    m_run, mb_a = lax.fori_loop(
        0, n_loops, lambda kk, c: full_trips(LOOP_TRIPS * kk, LOOP_TRIPS, *c),
        (jnp.full((1, width), NEG_INF, F32), scores(0, sa_ref)))
    j_last = LOOP_TRIPS * n_loops

    for rem in range(LOOP_TRIPS):
        @pl.when(n_full - j_last == rem)
        def _(rem=rem):
            m_fin, _ = full_trips(j_last, rem, m_run, mb_a)
            causal_tail(j_last + rem, bufs[rem % 2], m_fin)

    o_s = [acc_ref[g, 0:HEAD_DIM, :] / acc_ref[g, HEAD_DIM:HEAD_DIM + 1, :] for g in range(NSA_GROUPS)]

    outs = []
    for g in range(NSA_GROUPS):
        heads = []
        for r in range(NSA_GROUP_SIZE):
            base = (g * NSA_GROUP_SIZE + r) * 3
            cols = slice(r * Q_TILE, (r + 1) * Q_TILE)
            heads.append(gt_ref[0, base:base + 1, :] * o_c[g][:, cols]
                         + gt_ref[0, base + 1:base + 2, :] * o_s[g][:, cols]
                         + gt_ref[0, base + 2:base + 3, :] * o_w[g][:, cols])
        for pair in range(NSA_GROUP_SIZE // 2):
            outs.append(jnp.concatenate(heads[2 * pair:2 * pair + 2], axis=0).T)
    o_ref[0] = jnp.concatenate(outs, axis=1)


def _nsa_attn(q, gates_t, kc, vct, ks, vst, kw, vwt, ovt):
    b, t, _ = q.shape
    ncp = kc.shape[1]
    ns = ovt.shape[0]
    sel_keys = SEL_TRIP * KEY_BLOCK
    assert t % sel_keys == 0 and sel_keys % Q_TILE == 0 and t >= WINDOW + Q_TILE and ncp % KEY_BLOCK == 0
    ind = (np.arange(sel_keys)[:, None] // SEL_BLOCK == np.arange(LANES)[None, :])
    ind = jnp.asarray(ind.astype(np.float32), dtype=BF16)
    full_rows = lambda bi, qi: (bi, 0, 0)
    return pl.pallas_call(
        _nsa_kernel,
        grid=(b, t // Q_TILE),
        in_specs=[
            pl.BlockSpec((1, Q_TILE, NSA_WIDTH), lambda bi, qi: (bi, qi, 0)),
            pl.BlockSpec((1, GATE_ROWS, Q_TILE), lambda bi, qi: (bi, 0, qi)),
            pl.BlockSpec((1, ncp, LANES), full_rows),
            pl.BlockSpec((1, LANES, ncp), full_rows),
            pl.BlockSpec((1, t, LANES), full_rows),
            pl.BlockSpec((1, LANES, t), full_rows),
            pl.BlockSpec((1, t, LANES), full_rows),
            pl.BlockSpec((1, LANES, t), full_rows),
            _const_spec((ns, ncp)),
            _const_spec((sel_keys, LANES)),
        ],
        out_specs=pl.BlockSpec((1, Q_TILE, NSA_WIDTH), lambda bi, qi: (bi, qi, 0)),
        out_shape=jax.ShapeDtypeStruct((b, t, NSA_WIDTH), F32),
        scratch_shapes=[pltpu.VMEM((ns, NSA_GROUPS * Q_TILE), F32),
                        pltpu.VMEM((sel_keys, NSA_GROUPS * NSA_GROUP_SIZE * Q_TILE), F32),
                        pltpu.VMEM((sel_keys, NSA_GROUPS * NSA_GROUP_SIZE * Q_TILE), F32),
                        pltpu.VMEM((NSA_GROUPS, HEAD_DIM + BF16_ROWS, NSA_GROUP_SIZE * Q_TILE), F32)],
        compiler_params=_params(("arbitrary", "arbitrary")),
        name="nsa_attn",
    )(q, gates_t, kc, vct, ks, vst, kw, vwt, ovt, ind)


def _same_head_mask():
    bi = lax.broadcasted_iota(jnp.int32, (HALF, HALF), 0) // HEAD_DIM
    bj = lax.broadcasted_iota(jnp.int32, (HALF, HALF), 1) // HEAD_DIM
    return bi == bj


def _bd_rows(x, same_head):
    xb = x.astype(BF16)
    tiled = jnp.concatenate([xb] * (HALF // x.shape[0]), axis=0)
    return jnp.where(same_head, tiled, jnp.zeros((), BF16))


def _rwkv_prep_stages(in_refs, par_refs, out_set, *, chunks):
    r_ref, k_ref, v_ref, lo_ref = in_refs
    w0_ref, w2_ref, a0_ref, a2_ref, g2_ref, kk_ref, ka_ref, rk_ref = par_refs
    wm_ref, zm_ref, arb_ref, rkv_ref, rt_ref, vb_ref, bkt_ref, gl_ref, bonus_ref, gate_ref = out_set
    c = RWKV_CHUNK
    nb = r_ref.shape[0]
    same_head = _same_head_mask()
    ones_bd = same_head.astype(BF16)
    n_tok = chunks * c
    row_i = lax.broadcasted_iota(jnp.int32, (n_tok, n_tok), 0)
    col_i = lax.broadcasted_iota(jnp.int32, (n_tok, n_tok), 1)
    tril_incl = ((row_i >= col_i) & (row_i // c == col_i // c)).astype(BF16)
    t_id = lax.broadcasted_iota(jnp.int32, (c, HALF), 0)
    j_id = lax.broadcasted_iota(jnp.int32, (c, HALF), 1) % HEAD_DIM
    strict_lower = t_id > j_id
    incl_lower = t_id >= j_id
    eye_all = (t_id == j_id).astype(F32)
    dot = lambda x, y: jnp.dot(x, y, preferred_element_type=F32)

    def bd_cols(xt):
        xb = xt.astype(BF16)
        return jnp.where(same_head, jnp.concatenate([xb, xb], axis=1), jnp.zeros((), BF16))

    halves = RWKV_WIDTH // HALF
    groups = [(b, ch) for b in range(nb) for ch in range(chunks)]
    chains = [(gi, hh) for gi in range(len(groups)) for hh in range(halves)]
    rows = [slice(ch * c, (ch + 1) * c) for ch in range(chunks)]
    lanes = [slice(hh * HALF, (hh + 1) * HALF) for hh in range(halves)]
    each = lambda fn: [fn(n, gi, lanes[hh]) for n, (gi, hh) in enumerate(chains)]

    lo = [lo_ref[b] for b in range(nb)]
    zs = [-(w0_ref[...] + _bdot(jnp.tanh(x[:, 0:DECAY_LORA]), w2_ref[...])) for x in lo]
    yield
    lr_b = [_sigmoid(a0_ref[...] + _bdot(x[:, DECAY_LORA:DECAY_LORA + AAA_LORA], a2_ref[...])) for x in lo]
    yield
    for b in range(nb):
        gate_ref[b] = _bdot(_sigmoid(lo[b][:, DECAY_LORA + AAA_LORA:]), g2_ref[...])
        vb_ref[b] = v_ref[b].astype(BF16)
    yield
    log_decay = []
    for z in zs:
        softplus = jnp.maximum(z, 0.0) + jnp.log(1.0 + jnp.exp(-jnp.abs(z)))
        log_decay.append(-jnp.exp(-softplus - 0.5))
    cum_b = [_dot3_left(tril_incl, x) for x in log_decay]
    yield
    r = [r_ref[b, rows[ch], :] for b, ch in groups]
    k = [k_ref[b, rows[ch], :] for b, ch in groups]
    v = [v_ref[b, rows[ch], :] for b, ch in groups]
    lr = [lr_b[b][rows[ch], :] for b, ch in groups]
    cum = [cum_b[b][rows[ch], :] for b, ch in groups]
    ld = [log_decay[b][rows[ch], :] for b, ch in groups]
    g_incl = [jnp.exp(x) for x in cum]
    g_excl = [jnp.exp(x - y) for x, y in zip(cum, ld)]
    g_inv = [jnp.exp(-x) for x in cum]
    for gi, (b, ch) in enumerate(groups):
        gl_ref[b, ch * SUBLANES:(ch + 1) * SUBLANES, :] = jnp.broadcast_to(
            g_incl[gi][c - 1:c, :], (SUBLANES, RWKV_WIDTH))

    kk = each(lambda n, gi, ln: k[gi][:, ln] * kk_ref[:, ln])
    k2 = each(lambda n, gi, ln: k[gi][:, ln] * (1.0 + (lr[gi][:, ln] - 1.0) * ka_ref[:, ln]))
    sums = each(lambda n, gi, ln: _dot2_right(
        jnp.concatenate([kk[n] * kk[n], r[gi][:, ln] * k2[n] * rk_ref[:, ln]], axis=0), ones_bd))
    ssq = [x[0:c] for x in sums]
    for n, (gi, hh) in enumerate(chains):
        b, ch = groups[gi]
        bonus_ref[b, rows[ch], lanes[hh]] = sums[n][c:] * v[gi][:, lanes[hh]]
    yield
    kk = each(lambda n, gi, ln: kk[n] / jnp.maximum(jnp.sqrt(ssq[n]), 1e-12))
    at = each(lambda n, gi, ln: -kk[n] * g_excl[gi][:, ln])
    bt = each(lambda n, gi, ln: kk[n] * lr[gi][:, ln] * g_inv[gi][:, ln])
    kt = each(lambda n, gi, ln: k2[n] * g_inv[gi][:, ln])
    rt = each(lambda n, gi, ln: r[gi][:, ln] * g_incl[gi][:, ln])
    for n, (gi, hh) in enumerate(chains):
        b, ch = groups[gi]
        rt_ref[b, rows[ch], lanes[hh]] = rt[n].astype(BF16)
        bkt_ref[b, (ch * halves + hh) * HALF:(ch * halves + hh + 1) * HALF, :] = (
            jnp.concatenate([bt[n], kt[n]], axis=0).T.astype(BF16))

    bt_bd = each(lambda n, gi, ln: bd_cols(jnp.concatenate([bt[n], bt[n]], axis=0).T))
    ar = each(lambda n, gi, ln: jnp.concatenate([at[n], rt[n]], axis=0).astype(BF16))
    ab = each(lambda n, gi, ln: dot(ar[n], bt_bd[n]))
    yield
    kt_bd = each(lambda n, gi, ln: bd_cols(jnp.concatenate([kt[n], kt[n]], axis=0).T))
    ak = each(lambda n, gi, ln: dot(ar[n], kt_bd[n]))
    yield
    a_ab = [jnp.where(strict_lower, x[0:c], 0.0) for x in ab]
    a_rb = [jnp.where(incl_lower, x[c:], 0.0) for x in ab]
    a_ak = [jnp.where(strict_lower, x[0:c], 0.0) for x in ak]
    a_rk = [jnp.where(incl_lower, x[c:], 0.0) for x in ak]
    for n, (gi, hh) in enumerate(chains):
        b, ch = groups[gi]
        arb_ref[b, rows[ch], lanes[hh]] = a_rb[n].astype(BF16)

    base = 4
    same_block = lambda s: (t_id // s) == (j_id // s)
    nd = [jnp.where(same_block(base), x, 0.0) for x in a_ab]
    nd2 = [dot(x.astype(BF16), _bd_rows(x, same_head)) for x in nd]
    yield
    inv = [eye_all + x for x in nd]
    inv = [x + dot(x.astype(BF16), _bd_rows(y, same_head)) for x, y in zip(inv, nd2)]
    yield
    s_blk = base
    while s_blk < c:
        lower_left = same_block(2 * s_blk) & ((t_id % (2 * s_blk)) >= s_blk) & ((j_id % (2 * s_blk)) < s_blk)
        te = [dot(x.astype(BF16), _bd_rows(jnp.where(lower_left, y, 0.0), same_head))
              for x, y in zip(inv, a_ab)]
        yield
        inv = [x + dot(y.astype(BF16), _bd_rows(x, same_head)) for x, y in zip(inv, te)]
        yield
        s_blk *= 2

    v_bd = each(lambda n, gi, ln: _bd_rows(v[gi][:, ln], same_head))
    inv_b = [x.astype(BF16) for x in inv]
    wm = each(lambda n, gi, ln: dot(inv_b[n], _bd_rows(at[n], same_head)))
    yield
    av = each(lambda n, gi, ln: dot(jnp.concatenate([a_ak[n], a_rk[n]], axis=0).astype(BF16), v_bd[n]))
    yield
    zm = each(lambda n, gi, ln: dot(inv_b[n], _bd_rows(av[n][0:c], same_head)))
    for n, (gi, hh) in enumerate(chains):
        b, ch = groups[gi]
        wm_ref[b, rows[ch], lanes[hh]] = wm[n].astype(BF16)
        zm_ref[b, rows[ch], lanes[hh]] = zm[n]
        rkv_ref[b, rows[ch], lanes[hh]] = av[n][c:]


def _rwkv_scan_stages(in_set, gng_ref, gnb_ref, o_ref, s_ref, *, chunks):
    wm_ref, zm_ref, arb_ref, rkv_ref, rt_ref, vb_ref, bkt_ref, gl_ref, bonus_ref, gate_ref = in_set
    c = RWKV_CHUNK
    halves = RWKV_WIDTH // HALF
    same_head = _same_head_mask()
    ones_bd = same_head.astype(BF16)
    chains = [(b, hh) for b in range(o_ref.shape[0]) for hh in range(halves)]
    lanes = [slice(hh * HALF, (hh + 1) * HALF) for hh in range(halves)]
    dot = lambda x, y: jnp.dot(x, y, preferred_element_type=F32)

    for ch in range(chunks):
        rw = slice(ch * c, (ch + 1) * c)
        each = lambda fn: [fn(n, b, lanes[hh]) for n, (b, hh) in enumerate(chains)]
        s0 = [s_ref[b, hh] for b, hh in chains]
        s0b = [x.astype(BF16) for x in s0]
        u = each(lambda n, b, ln: dot(wm_ref[b, rw, ln], s0b[n]) + zm_ref[b, rw, ln])
        yield
        y0 = each(lambda n, b, ln: dot(rt_ref[b, rw, ln], s0b[n]) + rkv_ref[b, rw, ln])
        yield
        y = each(lambda n, b, ln: y0[n] + dot(arb_ref[b, rw, ln], _bd_rows(u[n], same_head)))
        yield
        uv = each(lambda n, b, ln: jnp.concatenate([u[n].astype(BF16), vb_ref[b, rw, ln]], axis=0))
        upd = [dot(bkt_ref[b, (ch * halves + hh) * HALF:(ch * halves + hh + 1) * HALF, :], uv[n])
               for n, (b, hh) in enumerate(chains)]
        for n, (b, hh) in enumerate(chains):
            g_last = jnp.broadcast_to(gl_ref[b, ch * SUBLANES:ch * SUBLANES + 1, lanes[hh]], (LANES, HALF)).T
            g_col = jnp.concatenate([g_last, g_last], axis=1)
            s_ref[b, hh] = g_col * (s0[n] + jnp.where(same_head, upd[n], 0.0))
        yield
        mu = [_dot2_right(x, ones_bd) * (1.0 / HEAD_DIM) for x in y]
        yield
        yc = [x - m for x, m in zip(y, mu)]
        var = [_dot2_right(x * x, ones_bd) * (1.0 / HEAD_DIM) for x in yc]
        yield
        for n, (b, hh) in enumerate(chains):
            ln = lanes[hh]
            yn = yc[n] * lax.rsqrt(var[n] + RWKV_GN_EPS) * gng_ref[:, ln] + gnb_ref[:, ln]
            o_ref[b, rw, ln] = (yn + bonus_ref[b, rw, ln]) * gate_ref[b, rw, ln]


def _rwkv_kernel(*refs, chunks):
    in_refs, par_refs = refs[0:4], refs[4:12]
    gng_ref, gnb_ref, o_ref, s_ref = refs[12:16]
    sets = (refs[16:26], refs[26:36])
    step = pl.program_id(0)

    @pl.when(step == 0)
    def _():
        s_ref[...] = jnp.zeros_like(s_ref)
        for ref in sets[1]:
            ref[...] = jnp.zeros_like(ref)

    def run(write_set, read_set):
        prep = _rwkv_prep_stages(in_refs, par_refs, write_set, chunks=chunks)
        scan = _rwkv_scan_stages(read_set, gng_ref, gnb_ref, o_ref, s_ref, chunks=chunks)
        live = [prep, scan]
        while live:
            for gen in list(live):
                if next(gen, "done") == "done":
                    live.remove(gen)

    @pl.when(step % 2 == 0)
    def _():
        run(sets[0], sets[1])

    @pl.when(step % 2 == 1)
    def _():
        run(sets[1], sets[0])


def _rwkv(r, k, v, lo, w0, w2, a0, a2, g2, k_k, k_a, r_k, gn_g, gn_b, *, chunks=2):
    b, t, width = r.shape
    c = RWKV_CHUNK
    n_tok = chunks * c
    n_groups = t // n_tok
    halves = width // HALF
    vec = lambda a: a.reshape(1, width)
    cs = lambda a: _const_spec(a.shape)
    args = [vec(w0), w2.astype(BF16), vec(a0), a2.astype(BF16), g2.astype(BF16),
            vec(k_k), vec(k_a), vec(r_k), vec(gn_g), vec(gn_b)]
    ahead = lambda s: (0, jnp.minimum(s, n_groups - 1), 0)
    behind = lambda s: (0, jnp.maximum(s - 1, 0), 0)
    tok = lambda dt: pltpu.VMEM((b, n_tok, width), dt)
    scratch_set = [tok(BF16), tok(F32), tok(BF16), tok(F32), tok(BF16), tok(BF16),
                   pltpu.VMEM((b, chunks * halves * HALF, LANES), BF16),
                   pltpu.VMEM((b, chunks * SUBLANES, width), F32), tok(F32), tok(F32)]
    return pl.pallas_call(
        functools.partial(_rwkv_kernel, chunks=chunks),
        grid=(n_groups + 1,),
        in_specs=[pl.BlockSpec((b, n_tok, width), ahead)] * 3 + [pl.BlockSpec((b, n_tok, LORA_COLS), ahead)]
                 + [cs(a) for a in args],
        out_specs=pl.BlockSpec((b, n_tok, width), behind),
        out_shape=jax.ShapeDtypeStruct((b, t, width), F32),
        scratch_shapes=[pltpu.VMEM((b, halves, HALF, HALF), F32)] + scratch_set + scratch_set,
        compiler_params=_params(("arbitrary",)),
        name="rwkv7",
    )(r, k, v, lo, *args)


def _mem_kv_kernel(m_ref, g_ref, wk_ref, wv_ref, kt_ref, v_ref):
    m = _rms(m_ref[0], g_ref[...]).astype(BF16)
    kt_ref[0] = jnp.dot(m, wk_ref[...], preferred_element_type=F32).T.astype(BF16)
    v_ref[0] = jnp.dot(m, wv_ref[...], preferred_element_type=F32).astype(BF16)


def _mem_kv(mem, g, wk, wv):
    b, mt, d = mem.shape
    return pl.pallas_call(
        _mem_kv_kernel,
        grid=(b,),
        in_specs=[pl.BlockSpec((1, mt, d), lambda bi: (bi, 0, 0)), _const_spec((1, d)),
                  _const_spec((d, d)), _const_spec((d, d))],
        out_specs=[pl.BlockSpec((1, d, mt), lambda bi: (bi, 0, 0)),
                   pl.BlockSpec((1, mt, d), lambda bi: (bi, 0, 0))],
        out_shape=[jax.ShapeDtypeStruct((b, d, mt), BF16), jax.ShapeDtypeStruct((b, mt, d), BF16)],
        compiler_params=_params(("arbitrary",)),
        name="mem_kv",
    )(mem, g.reshape(1, d), wk.astype(BF16), wv.astype(BF16))


def _out_mem_kernel(x_ref, on_ref, or_ref, ng_ref, wo1_ref, wo2_ref, mpost_ref,
                    mpre_ref, wq_ref, kt_ref, v_ref, wo_ref, mempost_ref, o_ref, *, parts):
    tm = x_ref.shape[1] // parts
    rows = [slice(n * tm, (n + 1) * tm) for n in range(parts)]
    dot = lambda a, b: jnp.dot(a, b, preferred_element_type=F32)
    d = x_ref.shape[-1]
    hd = d // MEM_HEADS

    a = [_rms(on_ref[0, rw, :], ng_ref[...]).astype(BF16) for rw in rows]
    mixed = [dot(a[n], wo1_ref[...]) + dot(or_ref[0, rw, :].astype(BF16), wo2_ref[...])
             for n, rw in enumerate(rows)]
    x = [x_ref[0, rw, :] + _rms(mixed[n], mpost_ref[...]) for n, rw in enumerate(rows)]
    h = [_rms(xn, mpre_ref[...]).astype(BF16) for xn in x]
    q = [(dot(hn, wq_ref[...]) * (hd ** -0.5)).astype(BF16) for hn in h]
    heads = [[] for _ in rows]
    for hi in range(MEM_HEADS):
        cols = slice(hi * hd, (hi + 1) * hd)
        s = [dot(qn[:, cols], kt_ref[0, cols, :]) for qn in q]
        e = [jnp.exp(sn - jnp.max(sn, axis=-1, keepdims=True)) for sn in s]
        p = [(en / jnp.sum(en, axis=-1, keepdims=True)).astype(BF16) for en in e]
        for n in range(parts):
            heads[n].append(dot(p[n], v_ref[0, :, cols]))
    att = [dot(jnp.concatenate(hn, axis=-1).astype(BF16), wo_ref[...]) for hn in heads]
    for n, rw in enumerate(rows):
        o_ref[0, rw, :] = x[n] + _rms(att[n], mempost_ref[...])


def _out_mem(x3, o_nsa, o_rwkv, nsa_g, w_out, mix_post_g, mem_pre_g, wq, kt, vm, wo, mem_post_g,
             *, tm=1024, parts=4):
    b, t, d = x3.shape
    mt = vm.shape[1]
    row = lambda bi, ti: (bi, ti, 0)
    per_b = lambda bi, ti: (bi, 0, 0)
    w_out = w_out.astype(BF16)
    return pl.pallas_call(
        functools.partial(_out_mem_kernel, parts=parts),
        grid=(b, t // tm),
        in_specs=[
            pl.BlockSpec((1, tm, d), row),
            pl.BlockSpec((1, tm, NSA_WIDTH), row),
            pl.BlockSpec((1, tm, RWKV_WIDTH), row),
            _const_spec((1, NSA_WIDTH)),
            _const_spec((NSA_WIDTH, d)),
            _const_spec((RWKV_WIDTH, d)),
            _const_spec((1, d)),
            _const_spec((1, d)),
            _const_spec((d, d)),
            pl.BlockSpec((1, d, mt), per_b),
            pl.BlockSpec((1, mt, d), per_b),
            _const_spec((d, d)),
            _const_spec((1, d)),
        ],
        out_specs=pl.BlockSpec((1, tm, d), row),
        out_shape=jax.ShapeDtypeStruct((b, t, d), F32),
        compiler_params=_params(("arbitrary", "arbitrary")),
        name="out_mem",
    )(x3, o_nsa, o_rwkv, nsa_g.reshape(1, -1), w_out[:NSA_WIDTH], w_out[NSA_WIDTH:],
      mix_post_g.reshape(1, d), mem_pre_g.reshape(1, d), wq.astype(BF16), kt, vm,
      wo.astype(BF16), mem_post_g.reshape(1, d))


def _rope_tables(pos):
    half = HEAD_DIM // 2
    inv = ROPE_THETA ** (-jnp.arange(half, dtype=F32) / half)
    ang = pos.astype(F32)[:, None] * inv[None, :]
    cos, sin = jnp.cos(ang), jnp.sin(ang)
    cos_t = jnp.concatenate([cos, cos, cos, cos], axis=-1)
    sin_t = jnp.concatenate([-sin, sin, -sin, sin], axis=-1)
    return cos_t, sin_t


def _overlap_t(ns, ncp):
    c0 = np.arange(ncp)[None, :] * CMP_STRIDE
    s0 = np.arange(ns)[:, None] * SEL_BLOCK
    ov = (c0 < s0 + SEL_BLOCK) & (c0 + CMP_BLOCK > s0) & (np.arange(ncp)[None, :] < ncp - 1)
    return jnp.asarray(ov.astype(np.float32), dtype=BF16)


def _pad_cols(w, n):
    return jnp.pad(w, ((0, 0), (0, n - w.shape[1])))


def _cmp_weights(pe, w1, w2):
    per = CMP_STRIDE
    w1r = w1.reshape(CMP_BLOCK, HEAD_DIM, CMP_HIDDEN)
    blocks = []
    for part in range(CMP_BLOCK // per):
        for g in range(NSA_GROUPS):
            z = jnp.zeros((per, NSA_GROUPS, HEAD_DIM, CMP_HIDDEN), F32)
            z = z.at[:, g].set(w1r[part * per:(part + 1) * per])
            blocks.append(z.reshape(per * NSA_GROUPS * HEAD_DIM, CMP_HIDDEN))
    w1cat = jnp.concatenate(blocks, axis=1).astype(BF16)
    w2bd = jnp.zeros((NSA_GROUPS * CMP_HIDDEN, NSA_GROUPS * HEAD_DIM), F32)
    for g in range(NSA_GROUPS):
        w2bd = w2bd.at[g * CMP_HIDDEN:(g + 1) * CMP_HIDDEN, g * HEAD_DIM:(g + 1) * HEAD_DIM].set(w2)
    pe8 = jnp.broadcast_to(pe.reshape(1, CMP_BLOCK * HEAD_DIM), (SUBLANES, CMP_BLOCK * HEAD_DIM))
    return pe8.astype(BF16), w1.astype(BF16), w1cat, w2bd.astype(BF16)


def kernel(x, mem, ffn1_pre_g, ffn1_w_gate, ffn1_w_up, ffn1_w_down, ffn1_post_g, mix_pre_g, w_in, cmp_pe_k, cmp_w1_k, cmp_w2_k, cmp_pe_v, cmp_w1_v, cmp_w2_v, nsa_out_g, rwkv_mu, rwkv_w0, rwkv_w2, rwkv_a0, rwkv_a2, rwkv_g2, rwkv_k_k, rwkv_k_a, rwkv_r_k, rwkv_gn_g, rwkv_gn_b, w_out, mix_post_g, mem_pre_g, mem_kv_g, mem_wq, mem_wk, mem_wv, mem_wo, mem_post_g, ffn2_pre_g, ffn2_w_gate, ffn2_w_up, ffn2_w_down, ffn2_post_g):
    b, t, d = x.shape
    ncp = t // CMP_STRIDE
    ns = t // SEL_BLOCK
    cos_t, sin_t = _rope_tables(jnp.arange(t))
    cos_c, sin_c = _rope_tables(jnp.arange(ncp) * CMP_STRIDE + (CMP_BLOCK - 1))
    ovt = _overlap_t(ns, ncp)

    for l in range(ffn1_pre_g.shape[0]):
        x2 = _ffn_block(x.reshape(b * t, d), ffn1_pre_g[l], ffn1_w_gate[l], ffn1_w_up[l],
                        ffn1_w_down[l], ffn1_post_g[l])
        x3 = x2.reshape(b, t, d)

        wi = w_in[l]
        nsa_w = NSA_WIDTH + 6 * NSA_KV_WIDTH
        gate_w = wi[:, nsa_w:nsa_w + 3 * NSA_HEADS]
        w_cols = jnp.concatenate([wi[:, :nsa_w], _pad_cols(gate_w, LANES),
                                  wi[:, nsa_w + 3 * NSA_HEADS:]], axis=1).astype(BF16)
        (q, k_cmp, v_cmp, k_slc, v_slc_t, k_win, v_win_t, gates_t, r, k, v, lo) = _in_proj(
            x3, mix_pre_g[l], w_cols, rwkv_mu[l].reshape(1, -1), cos_t, sin_t)

        pek, w1k, w1kc, w2k = _cmp_weights(cmp_pe_k[l], cmp_w1_k[l], cmp_w2_k[l])
        pev, w1v, w1vc, w2v = _cmp_weights(cmp_pe_v[l], cmp_w1_v[l], cmp_w2_v[l])
        kc, vct = _compress(k_cmp, v_cmp,
                            pek, w1k, w1kc, w2k, pev, w1v, w1vc, w2v, cos_c, sin_c)
        o_nsa = _nsa_attn(q, gates_t, kc, vct, k_slc, v_slc_t, k_win, v_win_t, ovt)

        o_rwkv = _rwkv(r, k, v, lo, rwkv_w0[l], rwkv_w2[l], rwkv_a0[l], rwkv_a2[l], rwkv_g2[l],
                       rwkv_k_k[l], rwkv_k_a[l], rwkv_r_k[l], rwkv_gn_g[l], rwkv_gn_b[l])

        kt, vm = _mem_kv(mem, mem_kv_g[l], mem_wk[l], mem_wv[l])
        x4 = _out_mem(x3, o_nsa, o_rwkv, nsa_out_g[l], w_out[l], mix_post_g[l], mem_pre_g[l],
                      mem_wq[l], kt, vm, mem_wo[l], mem_post_g[l])

        x = _ffn_block(x4.reshape(b * t, d), ffn2_pre_g[l], ffn2_w_gate[l], ffn2_w_up[l],
                       ffn2_w_down[l], ffn2_post_g[l]).reshape(b, t, d)
    return x
```

```python
import functools

import numpy as np
import jax
import jax.numpy as jnp
from jax import lax
from jax.experimental import pallas as pl
from jax.experimental.pallas import tpu as pltpu

F32 = jnp.float32
BF16 = jnp.bfloat16

HEAD_DIM = 64
NSA_HEADS = 8
NSA_GROUPS = 2
NSA_GROUP_SIZE = 4
NSA_WIDTH = 512
NSA_KV_WIDTH = 128
CMP_BLOCK = 32
CMP_STRIDE = 16
CMP_HIDDEN = 256
SEL_BLOCK = 64
SEL_TOPK = 16
SEL_FORCE = 1e4
SEL_FORCED = 3
WINDOW = 512
RWKV_WIDTH = 512
DECAY_LORA = 64
AAA_LORA = 64
GATE_LORA = 128
RWKV_GN_EPS = 64e-5
MEM_HEADS = 4
ROPE_THETA = 10000.0
NORM_EPS = 1e-6
NEG_INF = -1e30
LOG2_E = 1.4426950408889634

LORA_COLS = DECAY_LORA + AAA_LORA + GATE_LORA
RWKV_COLS = 3 * RWKV_WIDTH + LORA_COLS

LANES = 128
SUBLANES = 8
BF16_ROWS = 16
GATE_ROWS = 32
KEY_BLOCK = 128
Q_TILE = 128
SEL_TRIP = 4
LOOP_TRIPS = 4
RWKV_CHUNK = 64
HALF = 256
VMEM_LIMIT = 56 * 1024 * 1024


def _bdot(a, b):
    return jnp.dot(a.astype(BF16), b.astype(BF16), preferred_element_type=F32)


def _split3(x):
    h1 = x.astype(BF16)
    r1 = x - h1.astype(F32)
    h2 = r1.astype(BF16)
    r2 = r1 - h2.astype(F32)
    return h1, h2, r2.astype(BF16)


def _split2(x):
    h1 = x.astype(BF16)
    return h1, (x - h1.astype(F32)).astype(BF16)


def _dot2_right(x, m):
    h1, h2 = _split2(x)
    return (jnp.dot(h1, m, preferred_element_type=F32) + jnp.dot(h2, m, preferred_element_type=F32))


def _dot3_right(x, m):
    h1, h2, h3 = _split3(x)
    d = lambda h: jnp.dot(h, m, preferred_element_type=F32)
    return d(h1) + d(h2) + d(h3)


def _dot2_left(m, x):
    h1, h2 = _split2(x)
    return (jnp.dot(m, h1, preferred_element_type=F32) + jnp.dot(m, h2, preferred_element_type=F32))


def _dot3_left(m, x):
    h1, h2, h3 = _split3(x)
    d = lambda h: jnp.dot(m, h, preferred_element_type=F32)
    return d(h1) + d(h2) + d(h3)


def _rms(x, g):
    return x * lax.rsqrt(jnp.mean(x * x, axis=-1, keepdims=True) + NORM_EPS) * g


def _silu(x):
    return x / (1.0 + jnp.exp(-x))


def _sigmoid(x):
    return 1.0 / (1.0 + jnp.exp(-x))


def _const_spec(shape):
    nd = len(shape)
    return pl.BlockSpec(shape, lambda *_: (0,) * nd)


def _resident_spec(shape):
    nd = len(shape)
    return pl.BlockSpec(shape, lambda *_: (0,) * nd, pipeline_mode=pl.Buffered(1))


def _params(sem):
    return pltpu.CompilerParams(dimension_semantics=sem, vmem_limit_bytes=VMEM_LIMIT)


def _ffn_kernel(x_ref, pre_ref, wg_ref, wu_ref, wd_ref, post_ref, o_ref, *, ff_chunk):
    x = x_ref[...]
    h = _rms(x, pre_ref[...]).astype(BF16)
    d_ff = wg_ref.shape[1]
    acc = jnp.zeros(x.shape, F32)
    for c0 in range(0, d_ff, ff_chunk):
        g = jnp.dot(h, wg_ref[:, c0:c0 + ff_chunk].astype(BF16), preferred_element_type=F32)
        u = jnp.dot(h, wu_ref[:, c0:c0 + ff_chunk].astype(BF16), preferred_element_type=F32)
        a = (_silu(g) * u).astype(BF16)
        acc = acc + jnp.dot(a, wd_ref[c0:c0 + ff_chunk, :].astype(BF16), preferred_element_type=F32)
    o_ref[...] = x + 0.5 * _rms(acc, post_ref[...])


def _ffn_block(x2, pre_g, wg, wu, wd, post_g, *, tm=512, ff_chunk=256):
    m, d = x2.shape
    d_ff = wg.shape[1]
    return pl.pallas_call(
        functools.partial(_ffn_kernel, ff_chunk=ff_chunk),
        grid=(m // tm,),
        in_specs=[
            pl.BlockSpec((tm, d), lambda i: (i, 0)),
            _const_spec((1, d)),
            _resident_spec((d, d_ff)),
            _resident_spec((d, d_ff)),
            _resident_spec((d_ff, d)),
            _const_spec((1, d)),
        ],
        out_specs=pl.BlockSpec((tm, d), lambda i: (i, 0)),
        out_shape=jax.ShapeDtypeStruct((m, d), F32),
        compiler_params=_params(("arbitrary",)),
        name="ffn_block",
    )(x2, pre_g.reshape(1, d), wg, wu, wd, post_g.reshape(1, d))


def _swap_halves(x):
    n = x.shape[-1]
    lane = lax.broadcasted_iota(jnp.int32, x.shape, x.ndim - 1)
    fwd = pltpu.roll(x, n - HEAD_DIM // 2, x.ndim - 1)
    bwd = pltpu.roll(x, HEAD_DIM // 2, x.ndim - 1)
    return jnp.where((lane % HEAD_DIM) < HEAD_DIM // 2, fwd, bwd)


def _rope(x, cos, sin_signed):
    reps = x.shape[-1] // LANES
    c = jnp.concatenate([cos] * reps, axis=-1) if reps > 1 else cos
    s = jnp.concatenate([sin_signed] * reps, axis=-1) if reps > 1 else sin_signed
    return x * c + _swap_halves(x) * s


def _in_proj_kernel(x_ref, g_ref, w_ref, mu_ref, cos_ref, sin_ref,
                    q_ref, kc_ref, vc_ref, ks_ref, vst_ref, kw_ref, vwt_ref, gt_ref,
                    r_ref, k_ref, v_ref, lo_ref, carry_ref):
    @pl.when(pl.program_id(1) == 0)
    def _():
        carry_ref[...] = jnp.zeros_like(carry_ref)

    h = _rms(x_ref[0], g_ref[...]).astype(BF16)
    p = jnp.dot(h, w_ref[...], preferred_element_type=F32)
    cos = cos_ref[...]
    sin = sin_ref[...]
    tm = p.shape[0]

    o = 0
    q = _rope(p[:, o:o + NSA_WIDTH], cos, sin) * (HEAD_DIM ** -0.5 * LOG2_E)
    q_ref[0] = q.astype(BF16)
    o += NSA_WIDTH
    kc_ref[0] = p[:, o:o + LANES]; o += LANES
    vc_ref[0] = p[:, o:o + LANES]; o += LANES
    ks_ref[0] = _rope(p[:, o:o + LANES], cos, sin).astype(BF16); o += LANES
    vst_ref[0] = p[:, o:o + LANES].T.astype(BF16); o += LANES
    kw_ref[0] = _rope(p[:, o:o + LANES], cos, sin).astype(BF16); o += LANES
    vwt_ref[0] = p[:, o:o + LANES].T.astype(BF16); o += LANES
    gates_t = _sigmoid(p[:, o:o + LANES]).T
    gt_ref[0] = gates_t[:gt_ref.shape[1], :]
    o += LANES

    rw = p[:, o:]
    row = lax.broadcasted_iota(jnp.int32, rw.shape, 0)
    prev = jnp.where(row == 0, carry_ref[0:1, :], pltpu.roll(rw, 1, 0))
    carry_ref[...] = jnp.broadcast_to(rw[tm - 1:tm, :], carry_ref.shape)
    mixed = rw + (prev - rw) * mu_ref[...]
    r_ref[0] = mixed[:, 0:RWKV_WIDTH]
    k_ref[0] = mixed[:, RWKV_WIDTH:2 * RWKV_WIDTH]
    v_ref[0] = mixed[:, 2 * RWKV_WIDTH:3 * RWKV_WIDTH]
    lo_ref[0] = mixed[:, 3 * RWKV_WIDTH:RWKV_COLS]


def _in_proj(x3, g, w_cols, mu_cols, cos_t, sin_t, *, tm=512):
    b, t, d = x3.shape
    n = w_cols.shape[1]
    row = lambda bi, ti: (bi, ti, 0)
    col = lambda bi, ti: (bi, 0, ti)
    out_shapes = [
        jax.ShapeDtypeStruct((b, t, NSA_WIDTH), BF16),
        jax.ShapeDtypeStruct((b, t, LANES), F32),
        jax.ShapeDtypeStruct((b, t, LANES), F32),
        jax.ShapeDtypeStruct((b, t, LANES), BF16),
        jax.ShapeDtypeStruct((b, LANES, t), BF16),
        jax.ShapeDtypeStruct((b, t, LANES), BF16),
        jax.ShapeDtypeStruct((b, LANES, t), BF16),
        jax.ShapeDtypeStruct((b, GATE_ROWS, t), F32),
        jax.ShapeDtypeStruct((b, t, RWKV_WIDTH), F32),
        jax.ShapeDtypeStruct((b, t, RWKV_WIDTH), F32),
        jax.ShapeDtypeStruct((b, t, RWKV_WIDTH), F32),
        jax.ShapeDtypeStruct((b, t, LORA_COLS), F32),
    ]
    out_specs = [
        pl.BlockSpec((1, tm, NSA_WIDTH), row),
        pl.BlockSpec((1, tm, LANES), row),
        pl.BlockSpec((1, tm, LANES), row),
        pl.BlockSpec((1, tm, LANES), row),
        pl.BlockSpec((1, LANES, tm), col),
        pl.BlockSpec((1, tm, LANES), row),
        pl.BlockSpec((1, LANES, tm), col),
        pl.BlockSpec((1, GATE_ROWS, tm), col),
        pl.BlockSpec((1, tm, RWKV_WIDTH), row),
        pl.BlockSpec((1, tm, RWKV_WIDTH), row),
        pl.BlockSpec((1, tm, RWKV_WIDTH), row),
        pl.BlockSpec((1, tm, LORA_COLS), row),
    ]
    return pl.pallas_call(
        _in_proj_kernel,
        grid=(b, t // tm),
        in_specs=[
            pl.BlockSpec((1, tm, d), row),
            _const_spec((1, d)),
            _const_spec((d, n)),
            _const_spec((1, RWKV_COLS)),
            pl.BlockSpec((tm, LANES), lambda bi, ti: (ti, 0)),
            pl.BlockSpec((tm, LANES), lambda bi, ti: (ti, 0)),
        ],
        out_specs=out_specs,
        out_shape=out_shapes,
        scratch_shapes=[pltpu.VMEM((SUBLANES, RWKV_COLS), F32)],
        compiler_params=_params(("arbitrary", "arbitrary")),
        name="in_proj",
    )(x3, g.reshape(1, d), w_cols, mu_cols, cos_t, sin_t)


def _compress_kernel(kin_ref, vin_ref, pek_ref, w1k_ref, w1kc_ref, w2k_ref,
                     pev_ref, w1v_ref, w1vc_ref, w2v_ref, cos_ref, sin_ref,
                     kc_ref, vct_ref):
    def phi(tok_ref, pe, w1, w1cat_ref, w2bd):
        n = tok_ref.shape[1] // CMP_STRIDE
        pr = None
        for l in range(CMP_STRIDE):
            rows = tok_ref[0, pl.ds(l, n, stride=CMP_STRIDE), :].astype(BF16)
            term = jnp.dot(rows, w1cat_ref[l * LANES:(l + 1) * LANES, :], preferred_element_type=F32)
            pr = term if pr is None else pr + term
        bias = jnp.dot(pe, w1, preferred_element_type=F32)[0:1, :]
        hid = []
        for g in range(NSA_GROUPS):
            top = pr[:, g * CMP_HIDDEN:(g + 1) * CMP_HIDDEN]
            bot = pr[:, (NSA_GROUPS + g) * CMP_HIDDEN:(NSA_GROUPS + g + 1) * CMP_HIDDEN]
            hid.append(top + pltpu.roll(bot, n - 1, 0) + bias)
        act = _silu(jnp.concatenate(hid, axis=-1)).astype(BF16)
        return jnp.dot(act, w2bd, preferred_element_type=F32)

    kc = phi(kin_ref, pek_ref[...], w1k_ref[...], w1kc_ref, w2k_ref[...])
    kc_ref[0] = _rope(kc, cos_ref[...], sin_ref[...]).astype(BF16)
    vc = phi(vin_ref, pev_ref[...], w1v_ref[...], w1vc_ref, w2v_ref[...])
    vct_ref[0] = vc.T.astype(BF16)


def _compress(kin, vin, pek, w1k, w1kc, w2k, pev, w1v, w1vc, w2v, cos_c, sin_c):
    b, t, width = kin.shape
    ncp = t // CMP_STRIDE
    blk = pl.BlockSpec((1, t, width), lambda bi: (bi, 0, 0))
    cs = lambda a: _const_spec(a.shape)
    return pl.pallas_call(
        _compress_kernel,
        grid=(b,),
        in_specs=[blk, blk, cs(pek), cs(w1k), cs(w1kc), cs(w2k),
                  cs(pev), cs(w1v), cs(w1vc), cs(w2v), cs(cos_c), cs(sin_c)],
        out_specs=[pl.BlockSpec((1, ncp, LANES), lambda bi: (bi, 0, 0)),
                   pl.BlockSpec((1, LANES, ncp), lambda bi: (bi, 0, 0))],
        out_shape=[jax.ShapeDtypeStruct((b, ncp, LANES), BF16),
                   jax.ShapeDtypeStruct((b, LANES, ncp), BF16)],
        compiler_params=_params(("arbitrary",)),
        name="nsa_compress",
    )(kin, vin, pek, w1k, w1kc, w2k, pev, w1v, w1vc, w2v, cos_c, sin_c)


def _nsa_kernel(q_ref, gt_ref, kc_ref, vct_ref, ks_ref, vst_ref, kw_ref, vwt_ref, ovt_ref, ind_ref,
                o_ref, bias_ref, sa_ref, sb_ref, acc_ref):
    i = pl.program_id(1)
    q0 = i * Q_TILE
    ncp = kc_ref.shape[1]
    ns = ovt_ref.shape[0]
    gw = NSA_GROUP_SIZE * Q_TILE
    width = NSA_GROUPS * gw
    lanes = [slice(g * gw, (g + 1) * gw) for g in range(NSA_GROUPS)]
    feat = [slice(g * HEAD_DIM, (g + 1) * HEAD_DIM) for g in range(NSA_GROUPS)]
    t_row = q0 + lax.broadcasted_iota(jnp.int32, (1, width), 1) % Q_TILE
    sel_keys = SEL_TRIP * KEY_BLOCK
    sel_rows = sel_keys // SEL_BLOCK
    win_keys = WINDOW + Q_TILE
    dot = lambda x, y: jnp.dot(x, y, preferred_element_type=F32)

    qf = q_ref[0].astype(F32)
    zeros_half = jnp.zeros((HEAD_DIM, Q_TILE), F32)
    parts = []
    for g in range(NSA_GROUPS):
        for pair in range(NSA_GROUP_SIZE // 2):
            slab_t = qf[:, (2 * g + pair) * LANES:(2 * g + pair + 1) * LANES].T
            for half in range(2):
                f = slab_t[half * HEAD_DIM:(half + 1) * HEAD_DIM, :]
                parts.append(jnp.concatenate([f, zeros_half] if g == 0 else [zeros_half, f], axis=0))
    qt = jnp.concatenate(parts, axis=1).astype(BF16)

    def v_ext(vt_ref, g, k0, n):
        return jnp.concatenate([vt_ref[0, feat[g], pl.ds(k0, n)], jnp.ones((BF16_ROWS, n), BF16)], axis=0)

    sw_ = NSA_GROUPS * Q_TILE
    n_vis = (q0 + Q_TILE - CMP_BLOCK) // CMP_STRIDE + 1
    blocks_needed = jnp.maximum(n_vis + KEY_BLOCK - 1, KEY_BLOCK) // KEY_BLOCK
    def compressed(nb):
        rows = nb * KEY_BLOCK
        sc = dot(kc_ref[0, 0:rows, :], qt)
        c_end = lax.broadcasted_iota(jnp.int32, (rows, width), 0) * CMP_STRIDE + (CMP_BLOCK - 1)
        c_mask = c_end <= t_row
        sc = jnp.where(c_mask, sc, NEG_INF)
        m_c = jnp.max(sc, axis=0, keepdims=True)
        e_c = jnp.where(c_mask, jnp.exp2(sc - m_c), 0.0)
        p_c = e_c / jnp.maximum(jnp.sum(e_c, axis=0, keepdims=True), 1e-30)
        outs, p_sum = [], []
        for g in range(NSA_GROUPS):
            outs.append(dot(vct_ref[0, feat[g], 0:rows], p_c[:, lanes[g]].astype(BF16)))
            acc = p_c[:, g * gw:g * gw + Q_TILE]
            for r in range(1, NSA_GROUP_SIZE):
                acc = acc + p_c[:, g * gw + r * Q_TILE:g * gw + (r + 1) * Q_TILE]
            p_sum.append(acc)
        return outs + [_dot2_left(ovt_ref[:, 0:rows], jnp.concatenate(p_sum, axis=1))]

    *o_c, imp = lax.switch(blocks_needed - 1,
                           [functools.partial(compressed, nb) for nb in range(1, ncp // KEY_BLOCK + 1)])

    w0 = pl.multiple_of(jnp.maximum(q0 - WINDOW, 0), KEY_BLOCK)
    sw = dot(kw_ref[0, pl.ds(w0, win_keys), :], qt)
    t_loc = t_row - w0
    k_loc = lax.broadcasted_iota(jnp.int32, (win_keys, width), 0)
    ok = k_loc <= t_loc
    old = lax.broadcasted_iota(jnp.int32, (KEY_BLOCK, width), 0) <= t_loc - WINDOW
    sw = jnp.where(ok, sw, NEG_INF)
    sw = jnp.concatenate([jnp.where(old, NEG_INF, sw[0:KEY_BLOCK]), sw[KEY_BLOCK:]], axis=0)
    m_w = jnp.max(sw, axis=0, keepdims=True)
    o_w = []
    for g in range(NSA_GROUPS):
        p = jnp.exp2(sw[:, lanes[g]] - m_w[:, lanes[g]]).astype(BF16)
        pv = dot(v_ext(vwt_ref, g, w0, win_keys), p)
        o_w.append(pv[0:HEAD_DIM, :] / pv[HEAD_DIM:HEAD_DIM + 1, :])

    s_id = lax.broadcasted_iota(jnp.int32, (ns, sw_), 0)
    cur = t_row[:, 0:sw_] // SEL_BLOCK
    forced = (s_id == 0) | (s_id == cur) | (s_id == cur - 1)
    score = jnp.where(forced, -3e38, jnp.where(s_id <= cur, imp, -SEL_FORCE))
    bias = jnp.where(forced, 0.0, NEG_INF)
    for _ in range(min(SEL_TOPK, ns) - SEL_FORCED):
        mx = jnp.max(score, axis=0, keepdims=True)
        first = jnp.min(jnp.where(score == mx, s_id, ns), axis=0, keepdims=True)
        hit = s_id == first
        score = jnp.where(hit, -3e38, score)
        bias = jnp.where(hit, 0.0, bias)
    bias_ref[...] = bias

    def scores(j, s_ref):
        k0 = pl.multiple_of(j * sel_keys, sel_keys)
        b0 = pl.multiple_of(j * sel_rows, sel_rows)
        rows = bias_ref[pl.ds(b0, sel_rows), :]
        rows = jnp.concatenate([rows[:, g * Q_TILE:(g + 1) * Q_TILE]
                                for g in range(NSA_GROUPS) for _ in range(NSA_GROUP_SIZE)], axis=1)
        rows = jnp.concatenate([rows, jnp.zeros_like(rows)], axis=0).astype(BF16)
        rhs = jnp.concatenate([qt, rows, jnp.zeros((LANES - rows.shape[0], width), BF16)], axis=0)
        lhs = jnp.concatenate([ks_ref[0, pl.ds(k0, sel_keys), :], ind_ref[...]], axis=1)
        s = dot(lhs, rhs)
        s_ref[...] = s
        return jnp.max(s, axis=0, keepdims=True)

    def softmax_pv(j, s_ref, mb, m_run):
        k0 = pl.multiple_of(j * sel_keys, sel_keys)
        m_new = jnp.maximum(m_run, mb)
        alpha = jnp.exp2(m_run - m_new)
        for g in range(NSA_GROUPS):
            p = jnp.exp2(s_ref[:, lanes[g]] - m_new[:, lanes[g]]).astype(BF16)
            pv = dot(v_ext(vst_ref, g, k0, sel_keys), p)
            acc_ref[g] = acc_ref[g] * alpha[:, lanes[g]] + pv
        return m_new

    def causal_tail(j, s_ref, m_run):
        k0 = pl.multiple_of(j * sel_keys, sel_keys)
        key = k0 + lax.broadcasted_iota(jnp.int32, (sel_keys, width), 0)
        s = jnp.where(key <= t_row, s_ref[...], NEG_INF)
        s_ref[...] = s
        softmax_pv(j, s_ref, jnp.max(s, axis=0, keepdims=True), m_run)

    acc_ref[...] = jnp.zeros_like(acc_ref)
    n_full = q0 // sel_keys
    n_loops = n_full // LOOP_TRIPS
    bufs = (sa_ref, sb_ref)

    def full_trips(j0, count, m_run, mb):
        for t in range(count):
            mb_next = scores(j0 + t + 1, bufs[(t + 1) % 2])
            m_run = softmax_pv(j0 + t, bufs[t % 2], mb, m_run)
            mb = mb_next
        return m_run, mb

    m_run, mb_a = lax.fori_loop(
        0, n_loops, lambda kk, c: full_trips(LOOP_TRIPS * kk, LOOP_TRIPS, *c),
        (jnp.full((1, width), NEG_INF, F32), scores(0, sa_ref)))
    j_last = LOOP_TRIPS * n_loops

    for rem in range(LOOP_TRIPS):
        @pl.when(n_full - j_last == rem)
        def _(rem=rem):
            m_fin, _ = full_trips(j_last, rem, m_run, mb_a)
            causal_tail(j_last + rem, bufs[rem % 2], m_fin)

    o_s = [acc_ref[g, 0:HEAD_DIM, :] / acc_ref[g, HEAD_DIM:HEAD_DIM + 1, :] for g in range(NSA_GROUPS)]

    outs = []
    for g in range(NSA_GROUPS):
        heads = []
        for r in range(NSA_GROUP_SIZE):
            base = (g * NSA_GROUP_SIZE + r) * 3
            cols = slice(r * Q_TILE, (r + 1) * Q_TILE)
            heads.append(gt_ref[0, base:base + 1, :] * o_c[g][:, cols]
                         + gt_ref[0, base + 1:base + 2, :] * o_s[g][:, cols]
                         + gt_ref[0, base + 2:base + 3, :] * o_w[g][:, cols])
        for pair in range(NSA_GROUP_SIZE // 2):
            outs.append(jnp.concatenate(heads[2 * pair:2 * pair + 2], axis=0).T)
    o_ref[0] = jnp.concatenate(outs, axis=1)


def _nsa_attn(q, gates_t, kc, vct, ks, vst, kw, vwt, ovt):
    b, t, _ = q.shape
    ncp = kc.shape[1]
    ns = ovt.shape[0]
    sel_keys = SEL_TRIP * KEY_BLOCK
    assert t % sel_keys == 0 and sel_keys % Q_TILE == 0 and t >= WINDOW + Q_TILE and ncp % KEY_BLOCK == 0
    ind = (np.arange(sel_keys)[:, None] // SEL_BLOCK == np.arange(LANES)[None, :])
    ind = jnp.asarray(ind.astype(np.float32), dtype=BF16)
    full_rows = lambda bi, qi: (bi, 0, 0)
    return pl.pallas_call(
        _nsa_kernel,
        grid=(b, t // Q_TILE),
        in_specs=[
            pl.BlockSpec((1, Q_TILE, NSA_WIDTH), lambda bi, qi: (bi, qi, 0)),
            pl.BlockSpec((1, GATE_ROWS, Q_TILE), lambda bi, qi: (bi, 0, qi)),
            pl.BlockSpec((1, ncp, LANES), full_rows),
            pl.BlockSpec((1, LANES, ncp), full_rows),
            pl.BlockSpec((1, t, LANES), full_rows),
            pl.BlockSpec((1, LANES, t), full_rows),
            pl.BlockSpec((1, t, LANES), full_rows),
            pl.BlockSpec((1, LANES, t), full_rows),
            _const_spec((ns, ncp)),
            _const_spec((sel_keys, LANES)),
        ],
        out_specs=pl.BlockSpec((1, Q_TILE, NSA_WIDTH), lambda bi, qi: (bi, qi, 0)),
        out_shape=jax.ShapeDtypeStruct((b, t, NSA_WIDTH), F32),
        scratch_shapes=[pltpu.VMEM((ns, NSA_GROUPS * Q_TILE), F32),
                        pltpu.VMEM((sel_keys, NSA_GROUPS * NSA_GROUP_SIZE * Q_TILE), F32),
                        pltpu.VMEM((sel_keys, NSA_GROUPS * NSA_GROUP_SIZE * Q_TILE), F32),
                        pltpu.VMEM((NSA_GROUPS, HEAD_DIM + BF16_ROWS, NSA_GROUP_SIZE * Q_TILE), F32)],
        compiler_params=_params(("arbitrary", "arbitrary")),
        name="nsa_attn",
    )(q, gates_t, kc, vct, ks, vst, kw, vwt, ovt, ind)


def _same_head_mask():
    bi = lax.broadcasted_iota(jnp.int32, (HALF, HALF), 0) // HEAD_DIM
    bj = lax.broadcasted_iota(jnp.int32, (HALF, HALF), 1) // HEAD_DIM
    return bi == bj


def _bd_rows(x, same_head):
    xb = x.astype(BF16)
    tiled = jnp.concatenate([xb] * (HALF // x.shape[0]), axis=0)
    return jnp.where(same_head, tiled, jnp.zeros((), BF16))


def _rwkv_prep_stages(in_refs, par_refs, out_set, *, chunks):
    r_ref, k_ref, v_ref, lo_ref = in_refs
    w0_ref, w2_ref, a0_ref, a2_ref, g2_ref, kk_ref, ka_ref, rk_ref = par_refs
    wm_ref, zm_ref, arb_ref, rkv_ref, rt_ref, vb_ref, bkt_ref, gl_ref, bonus_ref, gate_ref = out_set
    c = RWKV_CHUNK
    nb = r_ref.shape[0]
    same_head = _same_head_mask()
    ones_bd = same_head.astype(BF16)
    n_tok = chunks * c
    row_i = lax.broadcasted_iota(jnp.int32, (n_tok, n_tok), 0)
    col_i = lax.broadcasted_iota(jnp.int32, (n_tok, n_tok), 1)
    tril_incl = ((row_i >= col_i) & (row_i // c == col_i // c)).astype(BF16)
    t_id = lax.broadcasted_iota(jnp.int32, (c, HALF), 0)
    j_id = lax.broadcasted_iota(jnp.int32, (c, HALF), 1) % HEAD_DIM
    strict_lower = t_id > j_id
    incl_lower = t_id >= j_id
    eye_all = (t_id == j_id).astype(F32)
    dot = lambda x, y: jnp.dot(x, y, preferred_element_type=F32)

    def bd_cols(xt):
        xb = xt.astype(BF16)
        return jnp.where(same_head, jnp.concatenate([xb, xb], axis=1), jnp.zeros((), BF16))

    halves = RWKV_WIDTH // HALF
    groups = [(b, ch) for b in range(nb) for ch in range(chunks)]
    chains = [(gi, hh) for gi in range(len(groups)) for hh in range(halves)]
    rows = [slice(ch * c, (ch + 1) * c) for ch in range(chunks)]
    lanes = [slice(hh * HALF, (hh + 1) * HALF) for hh in range(halves)]
    each = lambda fn: [fn(n, gi, lanes[hh]) for n, (gi, hh) in enumerate(chains)]

    lo = [lo_ref[b] for b in range(nb)]
    zs = [-(w0_ref[...] + _bdot(jnp.tanh(x[:, 0:DECAY_LORA]), w2_ref[...])) for x in lo]
    yield
    lr_b = [_sigmoid(a0_ref[...] + _bdot(x[:, DECAY_LORA:DECAY_LORA + AAA_LORA], a2_ref[...])) for x in lo]
    yield
    for b in range(nb):
        gate_ref[b] = _bdot(_sigmoid(lo[b][:, DECAY_LORA + AAA_LORA:]), g2_ref[...])
        vb_ref[b] = v_ref[b].astype(BF16)
    yield
    log_decay = []
    for z in zs:
        softplus = jnp.maximum(z, 0.0) + jnp.log(1.0 + jnp.exp(-jnp.abs(z)))
        log_decay.append(-jnp.exp(-softplus - 0.5))
    cum_b = [_dot3_left(tril_incl, x) for x in log_decay]
    yield
    r = [r_ref[b, rows[ch], :] for b, ch in groups]
    k = [k_ref[b, rows[ch], :] for b, ch in groups]
    v = [v_ref[b, rows[ch], :] for b, ch in groups]
    lr = [lr_b[b][rows[ch], :] for b, ch in groups]
    cum = [cum_b[b][rows[ch], :] for b, ch in groups]
    ld = [log_decay[b][rows[ch], :] for b, ch in groups]
    g_incl = [jnp.exp(x) for x in cum]
    g_excl = [jnp.exp(x - y) for x, y in zip(cum, ld)]
    g_inv = [jnp.exp(-x) for x in cum]
    for gi, (b, ch) in enumerate(groups):
        gl_ref[b, ch * SUBLANES:(ch + 1) * SUBLANES, :] = jnp.broadcast_to(
            g_incl[gi][c - 1:c, :], (SUBLANES, RWKV_WIDTH))

    kk = each(lambda n, gi, ln: k[gi][:, ln] * kk_ref[:, ln])
    k2 = each(lambda n, gi, ln: k[gi][:, ln] * (1.0 + (lr[gi][:, ln] - 1.0) * ka_ref[:, ln]))
    sums = each(lambda n, gi, ln: _dot2_right(
        jnp.concatenate([kk[n] * kk[n], r[gi][:, ln] * k2[n] * rk_ref[:, ln]], axis=0), ones_bd))
    ssq = [x[0:c] for x in sums]
    for n, (gi, hh) in enumerate(chains):
        b, ch = groups[gi]
        bonus_ref[b, rows[ch], lanes[hh]] = sums[n][c:] * v[gi][:, lanes[hh]]
    yield
    kk = each(lambda n, gi, ln: kk[n] / jnp.maximum(jnp.sqrt(ssq[n]), 1e-12))
    at = each(lambda n, gi, ln: -kk[n] * g_excl[gi][:, ln])
    bt = each(lambda n, gi, ln: kk[n] * lr[gi][:, ln] * g_inv[gi][:, ln])
    kt = each(lambda n, gi, ln: k2[n] * g_inv[gi][:, ln])
    rt = each(lambda n, gi, ln: r[gi][:, ln] * g_incl[gi][:, ln])
    for n, (gi, hh) in enumerate(chains):
        b, ch = groups[gi]
        rt_ref[b, rows[ch], lanes[hh]] = rt[n].astype(BF16)
        bkt_ref[b, (ch * halves + hh) * HALF:(ch * halves + hh + 1) * HALF, :] = (
            jnp.concatenate([bt[n], kt[n]], axis=0).T.astype(BF16))

    bt_bd = each(lambda n, gi, ln: bd_cols(jnp.concatenate([bt[n], bt[n]], axis=0).T))
    ar = each(lambda n, gi, ln: jnp.concatenate([at[n], rt[n]], axis=0).astype(BF16))
    ab = each(lambda n, gi, ln: dot(ar[n], bt_bd[n]))
    yield
    kt_bd = each(lambda n, gi, ln: bd_cols(jnp.concatenate([kt[n], kt[n]], axis=0).T))
    ak = each(lambda n, gi, ln: dot(ar[n], kt_bd[n]))
    yield
    a_ab = [jnp.where(strict_lower, x[0:c], 0.0) for x in ab]
    a_rb = [jnp.where(incl_lower, x[c:], 0.0) for x in ab]
    a_ak = [jnp.where(strict_lower, x[0:c], 0.0) for x in ak]
    a_rk = [jnp.where(incl_lower, x[c:], 0.0) for x in ak]
    for n, (gi, hh) in enumerate(chains):
        b, ch = groups[gi]
        arb_ref[b, rows[ch], lanes[hh]] = a_rb[n].astype(BF16)

    base = 4
    same_block = lambda s: (t_id // s) == (j_id // s)
    nd = [jnp.where(same_block(base), x, 0.0) for x in a_ab]
    nd2 = [dot(x.astype(BF16), _bd_rows(x, same_head)) for x in nd]
    yield
    inv = [eye_all + x for x in nd]
    inv = [x + dot(x.astype(BF16), _bd_rows(y, same_head)) for x, y in zip(inv, nd2)]
    yield
    s_blk = base
    while s_blk < c:
        lower_left = same_block(2 * s_blk) & ((t_id % (2 * s_blk)) >= s_blk) & ((j_id % (2 * s_blk)) < s_blk)
        te = [dot(x.astype(BF16), _bd_rows(jnp.where(lower_left, y, 0.0), same_head))
              for x, y in zip(inv, a_ab)]
        yield
        inv = [x + dot(y.astype(BF16), _bd_rows(x, same_head)) for x, y in zip(inv, te)]
        yield
        s_blk *= 2

    v_bd = each(lambda n, gi, ln: _bd_rows(v[gi][:, ln], same_head))
    inv_b = [x.astype(BF16) for x in inv]
    wm = each(lambda n, gi, ln: dot(inv_b[n], _bd_rows(at[n], same_head)))
    yield
    av = each(lambda n, gi, ln: dot(jnp.concatenate([a_ak[n], a_rk[n]], axis=0).astype(BF16), v_bd[n]))
    yield
    zm = each(lambda n, gi, ln: dot(inv_b[n], _bd_rows(av[n][0:c], same_head)))
    for n, (gi, hh) in enumerate(chains):
        b, ch = groups[gi]
        wm_ref[b, rows[ch], lanes[hh]] = wm[n].astype(BF16)
        zm_ref[b, rows[ch], lanes[hh]] = zm[n]
        rkv_ref[b, rows[ch], lanes[hh]] = av[n][c:]


def _rwkv_scan_stages(in_set, gng_ref, gnb_ref, o_ref, s_ref, *, chunks):
    wm_ref, zm_ref, arb_ref, rkv_ref, rt_ref, vb_ref, bkt_ref, gl_ref, bonus_ref, gate_ref = in_set
    c = RWKV_CHUNK
    halves = RWKV_WIDTH // HALF
    same_head = _same_head_mask()
    ones_bd = same_head.astype(BF16)
    chains = [(b, hh) for b in range(o_ref.shape[0]) for hh in range(halves)]
    lanes = [slice(hh * HALF, (hh + 1) * HALF) for hh in range(halves)]
    dot = lambda x, y: jnp.dot(x, y, preferred_element_type=F32)

    for ch in range(chunks):
        rw = slice(ch * c, (ch + 1) * c)
        each = lambda fn: [fn(n, b, lanes[hh]) for n, (b, hh) in enumerate(chains)]
        s0 = [s_ref[b, hh] for b, hh in chains]
        s0b = [x.astype(BF16) for x in s0]
        u = each(lambda n, b, ln: dot(wm_ref[b, rw, ln], s0b[n]) + zm_ref[b, rw, ln])
        yield
        y0 = each(lambda n, b, ln: dot(rt_ref[b, rw, ln], s0b[n]) + rkv_ref[b, rw, ln])
        yield
        y = each(lambda n, b, ln: y0[n] + dot(arb_ref[b, rw, ln], _bd_rows(u[n], same_head)))
        yield
        uv = each(lambda n, b, ln: jnp.concatenate([u[n].astype(BF16), vb_ref[b, rw, ln]], axis=0))
        upd = [dot(bkt_ref[b, (ch * halves + hh) * HALF:(ch * halves + hh + 1) * HALF, :], uv[n])
               for n, (b, hh) in enumerate(chains)]
        for n, (b, hh) in enumerate(chains):
            g_last = jnp.broadcast_to(gl_ref[b, ch * SUBLANES:ch * SUBLANES + 1, lanes[hh]], (LANES, HALF)).T
            g_col = jnp.concatenate([g_last, g_last], axis=1)
            s_ref[b, hh] = g_col * (s0[n] + jnp.where(same_head, upd[n], 0.0))
        yield
        mu = [_dot2_right(x, ones_bd) * (1.0 / HEAD_DIM) for x in y]
        yield
        yc = [x - m for x, m in zip(y, mu)]
        var = [_dot2_right(x * x, ones_bd) * (1.0 / HEAD_DIM) for x in yc]
        yield
        for n, (b, hh) in enumerate(chains):
            ln = lanes[hh]
            yn = yc[n] * lax.rsqrt(var[n] + RWKV_GN_EPS) * gng_ref[:, ln] + gnb_ref[:, ln]
            o_ref[b, rw, ln] = (yn + bonus_ref[b, rw, ln]) * gate_ref[b, rw, ln]


def _rwkv_kernel(*refs, chunks):
    in_refs, par_refs = refs[0:4], refs[4:12]
    gng_ref, gnb_ref, o_ref, s_ref = refs[12:16]
    sets = (refs[16:26], refs[26:36])
    step = pl.program_id(0)

    @pl.when(step == 0)
    def _():
        s_ref[...] = jnp.zeros_like(s_ref)
        for ref in sets[1]:
            ref[...] = jnp.zeros_like(ref)

    def run(write_set, read_set):
        prep = _rwkv_prep_stages(in_refs, par_refs, write_set, chunks=chunks)
        scan = _rwkv_scan_stages(read_set, gng_ref, gnb_ref, o_ref, s_ref, chunks=chunks)
        live = [prep, scan]
        while live:
            for gen in list(live):
                if next(gen, "done") == "done":
                    live.remove(gen)

    @pl.when(step % 2 == 0)
    def _():
        run(sets[0], sets[1])

    @pl.when(step % 2 == 1)
    def _():
        run(sets[1], sets[0])


def _rwkv(r, k, v, lo, w0, w2, a0, a2, g2, k_k, k_a, r_k, gn_g, gn_b, *, chunks=2):
    b, t, width = r.shape
    c = RWKV_CHUNK
    n_tok = chunks * c
    n_groups = t // n_tok
    halves = width // HALF
    vec = lambda a: a.reshape(1, width)
    cs = lambda a: _const_spec(a.shape)
    args = [vec(w0), w2.astype(BF16), vec(a0), a2.astype(BF16), g2.astype(BF16),
            vec(k_k), vec(k_a), vec(r_k), vec(gn_g), vec(gn_b)]
    ahead = lambda s: (0, jnp.minimum(s, n_groups - 1), 0)
    behind = lambda s: (0, jnp.maximum(s - 1, 0), 0)
    tok = lambda dt: pltpu.VMEM((b, n_tok, width), dt)
    scratch_set = [tok(BF16), tok(F32), tok(BF16), tok(F32), tok(BF16), tok(BF16),
                   pltpu.VMEM((b, chunks * halves * HALF, LANES), BF16),
                   pltpu.VMEM((b, chunks * SUBLANES, width), F32), tok(F32), tok(F32)]
    return pl.pallas_call(
        functools.partial(_rwkv_kernel, chunks=chunks),
        grid=(n_groups + 1,),
        in_specs=[pl.BlockSpec((b, n_tok, width), ahead)] * 3 + [pl.BlockSpec((b, n_tok, LORA_COLS), ahead)]
                 + [cs(a) for a in args],
        out_specs=pl.BlockSpec((b, n_tok, width), behind),
        out_shape=jax.ShapeDtypeStruct((b, t, width), F32),
        scratch_shapes=[pltpu.VMEM((b, halves, HALF, HALF), F32)] + scratch_set + scratch_set,
        compiler_params=_params(("arbitrary",)),
        name="rwkv7",
    )(r, k, v, lo, *args)


def _mem_kv_kernel(m_ref, g_ref, wk_ref, wv_ref, kt_ref, v_ref):
    m = _rms(m_ref[0], g_ref[...]).astype(BF16)
    kt_ref[0] = jnp.dot(m, wk_ref[...], preferred_element_type=F32).T.astype(BF16)
    v_ref[0] = jnp.dot(m, wv_ref[...], preferred_element_type=F32).astype(BF16)


def _mem_kv(mem, g, wk, wv):
    b, mt, d = mem.shape
    return pl.pallas_call(
        _mem_kv_kernel,
        grid=(b,),
        in_specs=[pl.BlockSpec((1, mt, d), lambda bi: (bi, 0, 0)), _const_spec((1, d)),
                  _const_spec((d, d)), _const_spec((d, d))],
        out_specs=[pl.BlockSpec((1, d, mt), lambda bi: (bi, 0, 0)),
                   pl.BlockSpec((1, mt, d), lambda bi: (bi, 0, 0))],
        out_shape=[jax.ShapeDtypeStruct((b, d, mt), BF16), jax.ShapeDtypeStruct((b, mt, d), BF16)],
        compiler_params=_params(("arbitrary",)),
        name="mem_kv",
    )(mem, g.reshape(1, d), wk.astype(BF16), wv.astype(BF16))


def _out_mem_kernel(x_ref, on_ref, or_ref, ng_ref, wo1_ref, wo2_ref, mpost_ref,
                    mpre_ref, wq_ref, kt_ref, v_ref, wo_ref, mempost_ref, o_ref, *, parts):
    tm = x_ref.shape[1] // parts
    rows = [slice(n * tm, (n + 1) * tm) for n in range(parts)]
    dot = lambda a, b: jnp.dot(a, b, preferred_element_type=F32)
    d = x_ref.shape[-1]
    hd = d // MEM_HEADS

    a = [_rms(on_ref[0, rw, :], ng_ref[...]).astype(BF16) for rw in rows]
    mixed = [dot(a[n], wo1_ref[...]) + dot(or_ref[0, rw, :].astype(BF16), wo2_ref[...])
             for n, rw in enumerate(rows)]
    x = [x_ref[0, rw, :] + _rms(mixed[n], mpost_ref[...]) for n, rw in enumerate(rows)]
    h = [_rms(xn, mpre_ref[...]).astype(BF16) for xn in x]
    q = [(dot(hn, wq_ref[...]) * (hd ** -0.5)).astype(BF16) for hn in h]
    heads = [[] for _ in rows]
    for hi in range(MEM_HEADS):
        cols = slice(hi * hd, (hi + 1) * hd)
        s = [dot(qn[:, cols], kt_ref[0, cols, :]) for qn in q]
        e = [jnp.exp(sn - jnp.max(sn, axis=-1, keepdims=True)) for sn in s]
        p = [(en / jnp.sum(en, axis=-1, keepdims=True)).astype(BF16) for en in e]
        for n in range(parts):
            heads[n].append(dot(p[n], v_ref[0, :, cols]))
    att = [dot(jnp.concatenate(hn, axis=-1).astype(BF16), wo_ref[...]) for hn in heads]
    for n, rw in enumerate(rows):
        o_ref[0, rw, :] = x[n] + _rms(att[n], mempost_ref[...])


def _out_mem(x3, o_nsa, o_rwkv, nsa_g, w_out, mix_post_g, mem_pre_g, wq, kt, vm, wo, mem_post_g,
             *, tm=1024, parts=4):
    b, t, d = x3.shape
    mt = vm.shape[1]
    row = lambda bi, ti: (bi, ti, 0)
    per_b = lambda bi, ti: (bi, 0, 0)
    w_out = w_out.astype(BF16)
    return pl.pallas_call(
        functools.partial(_out_mem_kernel, parts=parts),
        grid=(b, t // tm),
        in_specs=[
            pl.BlockSpec((1, tm, d), row),
            pl.BlockSpec((1, tm, NSA_WIDTH), row),
            pl.BlockSpec((1, tm, RWKV_WIDTH), row),
            _const_spec((1, NSA_WIDTH)),
            _const_spec((NSA_WIDTH, d)),
            _const_spec((RWKV_WIDTH, d)),
            _const_spec((1, d)),
            _const_spec((1, d)),
            _const_spec((d, d)),
            pl.BlockSpec((1, d, mt), per_b),
            pl.BlockSpec((1, mt, d), per_b),
            _const_spec((d, d)),
            _const_spec((1, d)),
        ],
        out_specs=pl.BlockSpec((1, tm, d), row),
        out_shape=jax.ShapeDtypeStruct((b, t, d), F32),
        compiler_params=_params(("arbitrary", "arbitrary")),
        name="out_mem",
    )(x3, o_nsa, o_rwkv, nsa_g.reshape(1, -1), w_out[:NSA_WIDTH], w_out[NSA_WIDTH:],
      mix_post_g.reshape(1, d), mem_pre_g.reshape(1, d), wq.astype(BF16), kt, vm,
      wo.astype(BF16), mem_post_g.reshape(1, d))


def _rope_tables(pos):
    half = HEAD_DIM // 2
    inv = ROPE_THETA ** (-jnp.arange(half, dtype=F32) / half)
    ang = pos.astype(F32)[:, None] * inv[None, :]
    cos, sin = jnp.cos(ang), jnp.sin(ang)
    cos_t = jnp.concatenate([cos, cos, cos, cos], axis=-1)
    sin_t = jnp.concatenate([-sin, sin, -sin, sin], axis=-1)
    return cos_t, sin_t


def _overlap_t(ns, ncp):
    c0 = np.arange(ncp)[None, :] * CMP_STRIDE
    s0 = np.arange(ns)[:, None] * SEL_BLOCK
    ov = (c0 < s0 + SEL_BLOCK) & (c0 + CMP_BLOCK > s0) & (np.arange(ncp)[None, :] < ncp - 1)
    return jnp.asarray(ov.astype(np.float32), dtype=BF16)


def _pad_cols(w, n):
    return jnp.pad(w, ((0, 0), (0, n - w.shape[1])))


def _cmp_weights(pe, w1, w2):
    per = CMP_STRIDE
    w1r = w1.reshape(CMP_BLOCK, HEAD_DIM, CMP_HIDDEN)
    blocks = []
    for part in range(CMP_BLOCK // per):
        for g in range(NSA_GROUPS):
            z = jnp.zeros((per, NSA_GROUPS, HEAD_DIM, CMP_HIDDEN), F32)
            z = z.at[:, g].set(w1r[part * per:(part + 1) * per])
            blocks.append(z.reshape(per * NSA_GROUPS * HEAD_DIM, CMP_HIDDEN))
    w1cat = jnp.concatenate(blocks, axis=1).astype(BF16)
    w2bd = jnp.zeros((NSA_GROUPS * CMP_HIDDEN, NSA_GROUPS * HEAD_DIM), F32)
    for g in range(NSA_GROUPS):
        w2bd = w2bd.at[g * CMP_HIDDEN:(g + 1) * CMP_HIDDEN, g * HEAD_DIM:(g + 1) * HEAD_DIM].set(w2)
    pe8 = jnp.broadcast_to(pe.reshape(1, CMP_BLOCK * HEAD_DIM), (SUBLANES, CMP_BLOCK * HEAD_DIM))
    return pe8.astype(BF16), w1.astype(BF16), w1cat, w2bd.astype(BF16)


def kernel(x, mem, ffn1_pre_g, ffn1_w_gate, ffn1_w_up, ffn1_w_down, ffn1_post_g, mix_pre_g, w_in, cmp_pe_k, cmp_w1_k, cmp_w2_k, cmp_pe_v, cmp_w1_v, cmp_w2_v, nsa_out_g, rwkv_mu, rwkv_w0, rwkv_w2, rwkv_a0, rwkv_a2, rwkv_g2, rwkv_k_k, rwkv_k_a, rwkv_r_k, rwkv_gn_g, rwkv_gn_b, w_out, mix_post_g, mem_pre_g, mem_kv_g, mem_wq, mem_wk, mem_wv, mem_wo, mem_post_g, ffn2_pre_g, ffn2_w_gate, ffn2_w_up, ffn2_w_down, ffn2_post_g):
    b, t, d = x.shape
    ncp = t // CMP_STRIDE
    ns = t // SEL_BLOCK
    cos_t, sin_t = _rope_tables(jnp.arange(t))
    cos_c, sin_c = _rope_tables(jnp.arange(ncp) * CMP_STRIDE + (CMP_BLOCK - 1))
    ovt = _overlap_t(ns, ncp)

    for l in range(ffn1_pre_g.shape[0]):
        x2 = _ffn_block(x.reshape(b * t, d), ffn1_pre_g[l], ffn1_w_gate[l], ffn1_w_up[l],
                        ffn1_w_down[l], ffn1_post_g[l])
        x3 = x2.reshape(b, t, d)

        wi = w_in[l]
        nsa_w = NSA_WIDTH + 6 * NSA_KV_WIDTH
        gate_w = wi[:, nsa_w:nsa_w + 3 * NSA_HEADS]
        w_cols = jnp.concatenate([wi[:, :nsa_w], _pad_cols(gate_w, LANES),
                                  wi[:, nsa_w + 3 * NSA_HEADS:]], axis=1).astype(BF16)
        (q, k_cmp, v_cmp, k_slc, v_slc_t, k_win, v_win_t, gates_t, r, k, v, lo) = _in_proj(
            x3, mix_pre_g[l], w_cols, rwkv_mu[l].reshape(1, -1), cos_t, sin_t)

        pek, w1k, w1kc, w2k = _cmp_weights(cmp_pe_k[l], cmp_w1_k[l], cmp_w2_k[l])
        pev, w1v, w1vc, w2v = _cmp_weights(cmp_pe_v[l], cmp_w1_v[l], cmp_w2_v[l])
        kc, vct = _compress(k_cmp, v_cmp,
                            pek, w1k, w1kc, w2k, pev, w1v, w1vc, w2v, cos_c, sin_c)
        o_nsa = _nsa_attn(q, gates_t, kc, vct, k_slc, v_slc_t, k_win, v_win_t, ovt)

        o_rwkv = _rwkv(r, k, v, lo, rwkv_w0[l], rwkv_w2[l], rwkv_a0[l], rwkv_a2[l], rwkv_g2[l],
                       rwkv_k_k[l], rwkv_k_a[l], rwkv_r_k[l], rwkv_gn_g[l], rwkv_gn_b[l])

        kt, vm = _mem_kv(mem, mem_kv_g[l], mem_wk[l], mem_wv[l])
        x4 = _out_mem(x3, o_nsa, o_rwkv, nsa_out_g[l], w_out[l], mix_post_g[l], mem_pre_g[l],
                      mem_wq[l], kt, vm, mem_wo[l], mem_post_g[l])

        x = _ffn_block(x4.reshape(b * t, d), ffn2_pre_g[l], ffn2_w_gate[l], ffn2_w_up[l],
                       ffn2_w_down[l], ffn2_post_g[l]).reshape(b, t, d)
    return x
```

```python
import functools

import numpy as np
import jax
import jax.numpy as jnp
from jax import lax
from jax.experimental import pallas as pl
from jax.experimental.pallas import tpu as pltpu

F32 = jnp.float32
BF16 = jnp.bfloat16

HEAD_DIM = 64
NSA_HEADS = 8
NSA_GROUPS = 2
NSA_GROUP_SIZE = 4
NSA_WIDTH = 512
NSA_KV_WIDTH = 128
CMP_BLOCK = 32
CMP_STRIDE = 16
CMP_HIDDEN = 256
SEL_BLOCK = 64
SEL_TOPK = 16
SEL_FORCE = 1e4
SEL_FORCED = 3
WINDOW = 512
RWKV_WIDTH = 512
DECAY_LORA = 64
AAA_LORA = 64
GATE_LORA = 128
RWKV_GN_EPS = 64e-5
MEM_HEADS = 4
ROPE_THETA = 10000.0
NORM_EPS = 1e-6
NEG_INF = -1e30
LOG2_E = 1.4426950408889634

LORA_COLS = DECAY_LORA + AAA_LORA + GATE_LORA
RWKV_COLS = 3 * RWKV_WIDTH + LORA_COLS

LANES = 128
SUBLANES = 8
BF16_ROWS = 16
GATE_ROWS = 32
KEY_BLOCK = 128
Q_TILE = 128
SEL_TRIP = 4
LOOP_TRIPS = 4
HEAD_TRIPS = 8
RWKV_CHUNK = 64
HALF = 256
VMEM_LIMIT = 56 * 1024 * 1024


def _bdot(a, b):
    return jnp.dot(a.astype(BF16), b.astype(BF16), preferred_element_type=F32)


def _split3(x):
    h1 = x.astype(BF16)
    r1 = x - h1.astype(F32)
    h2 = r1.astype(BF16)
    r2 = r1 - h2.astype(F32)
    return h1, h2, r2.astype(BF16)


def _split2(x):
    h1 = x.astype(BF16)
    return h1, (x - h1.astype(F32)).astype(BF16)


def _dot2_right(x, m):
    h1, h2 = _split2(x)
    return (jnp.dot(h1, m, preferred_element_type=F32) + jnp.dot(h2, m, preferred_element_type=F32))


def _dot3_right(x, m):
    h1, h2, h3 = _split3(x)
    d = lambda h: jnp.dot(h, m, preferred_element_type=F32)
    return d(h1) + d(h2) + d(h3)


def _dot2_left(m, x):
    h1, h2 = _split2(x)
    return (jnp.dot(m, h1, preferred_element_type=F32) + jnp.dot(m, h2, preferred_element_type=F32))


def _dot3_left(m, x):
    h1, h2, h3 = _split3(x)
    d = lambda h: jnp.dot(m, h, preferred_element_type=F32)
    return d(h1) + d(h2) + d(h3)


def _rms(x, g):
    return x * lax.rsqrt(jnp.mean(x * x, axis=-1, keepdims=True) + NORM_EPS) * g


def _silu(x):
    return x / (1.0 + jnp.exp(-x))


def _sigmoid(x):
    return 1.0 / (1.0 + jnp.exp(-x))


def _const_spec(shape):
    nd = len(shape)
    return pl.BlockSpec(shape, lambda *_: (0,) * nd)


def _params(sem):
    return pltpu.CompilerParams(dimension_semantics=sem, vmem_limit_bytes=VMEM_LIMIT)


def _ffn_kernel(x_ref, pre_ref, wg_ref, wu_ref, wd_ref, post_ref, o_ref, *, ff_chunk):
    x = x_ref[...]
    h = _rms(x, pre_ref[...]).astype(BF16)
    d_ff = wg_ref.shape[1]
    acc = jnp.zeros(x.shape, F32)
    for c0 in range(0, d_ff, ff_chunk):
        g = jnp.dot(h, wg_ref[:, c0:c0 + ff_chunk], preferred_element_type=F32)
        u = jnp.dot(h, wu_ref[:, c0:c0 + ff_chunk], preferred_element_type=F32)
        a = (_silu(g) * u).astype(BF16)
        acc = acc + jnp.dot(a, wd_ref[c0:c0 + ff_chunk, :], preferred_element_type=F32)
    o_ref[...] = x + 0.5 * _rms(acc, post_ref[...])


def _ffn_block(x2, pre_g, wg, wu, wd, post_g, *, tm=512, ff_chunk=256):
    m, d = x2.shape
    d_ff = wg.shape[1]
    return pl.pallas_call(
        functools.partial(_ffn_kernel, ff_chunk=ff_chunk),
        grid=(m // tm,),
        in_specs=[
            pl.BlockSpec((tm, d), lambda i: (i, 0)),
            _const_spec((1, d)),
            _const_spec((d, d_ff)),
            _const_spec((d, d_ff)),
            _const_spec((d_ff, d)),
            _const_spec((1, d)),
        ],
        out_specs=pl.BlockSpec((tm, d), lambda i: (i, 0)),
        out_shape=jax.ShapeDtypeStruct((m, d), F32),
        compiler_params=_params(("arbitrary",)),
        name="ffn_block",
    )(x2, pre_g.reshape(1, d), wg.astype(BF16), wu.astype(BF16), wd.astype(BF16),
      post_g.reshape(1, d))


def _swap_halves(x):
    n = x.shape[-1]
    lane = lax.broadcasted_iota(jnp.int32, x.shape, x.ndim - 1)
    fwd = pltpu.roll(x, n - HEAD_DIM // 2, x.ndim - 1)
    bwd = pltpu.roll(x, HEAD_DIM // 2, x.ndim - 1)
    return jnp.where((lane % HEAD_DIM) < HEAD_DIM // 2, fwd, bwd)


def _rope(x, cos, sin_signed):
    reps = x.shape[-1] // LANES
    c = jnp.concatenate([cos] * reps, axis=-1) if reps > 1 else cos
    s = jnp.concatenate([sin_signed] * reps, axis=-1) if reps > 1 else sin_signed
    return x * c + _swap_halves(x) * s


def _in_proj_kernel(x_ref, g_ref, w_ref, mu_ref, cos_ref, sin_ref,
                    q_ref, kc_ref, vc_ref, ks_ref, vst_ref, kw_ref, vwt_ref, gt_ref,
                    r_ref, k_ref, v_ref, lo_ref, carry_ref):
    @pl.when(pl.program_id(1) == 0)
    def _():
        carry_ref[...] = jnp.zeros_like(carry_ref)

    h = _rms(x_ref[0], g_ref[...]).astype(BF16)
    p = jnp.dot(h, w_ref[...], preferred_element_type=F32)
    cos = cos_ref[...]
    sin = sin_ref[...]
    tm = p.shape[0]

    o = 0
    q = _rope(p[:, o:o + NSA_WIDTH], cos, sin) * (HEAD_DIM ** -0.5 * LOG2_E)
    q_ref[0] = q.astype(BF16)
    o += NSA_WIDTH
    kc_ref[0] = p[:, o:o + LANES]; o += LANES
    vc_ref[0] = p[:, o:o + LANES]; o += LANES
    ks_ref[0] = _rope(p[:, o:o + LANES], cos, sin).astype(BF16); o += LANES
    vst_ref[0] = p[:, o:o + LANES].T.astype(BF16); o += LANES
    kw_ref[0] = _rope(p[:, o:o + LANES], cos, sin).astype(BF16); o += LANES
    vwt_ref[0] = p[:, o:o + LANES].T.astype(BF16); o += LANES
    gates_t = _sigmoid(p[:, o:o + LANES]).T
    gt_ref[0] = gates_t[:gt_ref.shape[1], :]
    o += LANES

    rw = p[:, o:]
    row = lax.broadcasted_iota(jnp.int32, rw.shape, 0)
    prev = jnp.where(row == 0, carry_ref[0:1, :], pltpu.roll(rw, 1, 0))
    carry_ref[...] = jnp.broadcast_to(rw[tm - 1:tm, :], carry_ref.shape)
    mixed = rw + (prev - rw) * mu_ref[...]
    r_ref[0] = mixed[:, 0:RWKV_WIDTH]
    k_ref[0] = mixed[:, RWKV_WIDTH:2 * RWKV_WIDTH]
    v_ref[0] = mixed[:, 2 * RWKV_WIDTH:3 * RWKV_WIDTH]
    lo_ref[0] = mixed[:, 3 * RWKV_WIDTH:RWKV_COLS]


def _in_proj(x3, g, w_cols, mu_cols, cos_t, sin_t, *, tm=512):
    b, t, d = x3.shape
    n = w_cols.shape[1]
    row = lambda bi, ti: (bi, ti, 0)
    col = lambda bi, ti: (bi, 0, ti)
    out_shapes = [
        jax.ShapeDtypeStruct((b, t, NSA_WIDTH), BF16),
        jax.ShapeDtypeStruct((b, t, LANES), F32),
        jax.ShapeDtypeStruct((b, t, LANES), F32),
        jax.ShapeDtypeStruct((b, t, LANES), BF16),
        jax.ShapeDtypeStruct((b, LANES, t), BF16),
        jax.ShapeDtypeStruct((b, t, LANES), BF16),
        jax.ShapeDtypeStruct((b, LANES, t), BF16),
        jax.ShapeDtypeStruct((b, GATE_ROWS, t), F32),
        jax.ShapeDtypeStruct((b, t, RWKV_WIDTH), F32),
        jax.ShapeDtypeStruct((b, t, RWKV_WIDTH), F32),
        jax.ShapeDtypeStruct((b, t, RWKV_WIDTH), F32),
        jax.ShapeDtypeStruct((b, t, LORA_COLS), F32),
    ]
    out_specs = [
        pl.BlockSpec((1, tm, NSA_WIDTH), row),
        pl.BlockSpec((1, tm, LANES), row),
        pl.BlockSpec((1, tm, LANES), row),
        pl.BlockSpec((1, tm, LANES), row),
        pl.BlockSpec((1, LANES, tm), col),
        pl.BlockSpec((1, tm, LANES), row),
        pl.BlockSpec((1, LANES, tm), col),
        pl.BlockSpec((1, GATE_ROWS, tm), col),
        pl.BlockSpec((1, tm, RWKV_WIDTH), row),
        pl.BlockSpec((1, tm, RWKV_WIDTH), row),
        pl.BlockSpec((1, tm, RWKV_WIDTH), row),
        pl.BlockSpec((1, tm, LORA_COLS), row),
    ]
    return pl.pallas_call(
        _in_proj_kernel,
        grid=(b, t // tm),
        in_specs=[
            pl.BlockSpec((1, tm, d), row),
            _const_spec((1, d)),
            _const_spec((d, n)),
            _const_spec((1, RWKV_COLS)),
            pl.BlockSpec((tm, LANES), lambda bi, ti: (ti, 0)),
            pl.BlockSpec((tm, LANES), lambda bi, ti: (ti, 0)),
        ],
        out_specs=out_specs,
        out_shape=out_shapes,
        scratch_shapes=[pltpu.VMEM((SUBLANES, RWKV_COLS), F32)],
        compiler_params=_params(("arbitrary", "arbitrary")),
        name="in_proj",
    )(x3, g.reshape(1, d), w_cols, mu_cols, cos_t, sin_t)


def _compress_kernel(kin_ref, vin_ref, pek_ref, w1k_ref, w1kc_ref, w2k_ref,
                     pev_ref, w1v_ref, w1vc_ref, w2v_ref, cos_ref, sin_ref,
                     kc_ref, vct_ref):
    def phi(tok_ref, pe, w1, w1cat_ref, w2bd):
        n = tok_ref.shape[1] // CMP_STRIDE
        pr = None
        for l in range(CMP_STRIDE):
            rows = tok_ref[0, pl.ds(l, n, stride=CMP_STRIDE), :].astype(BF16)
            term = jnp.dot(rows, w1cat_ref[l * LANES:(l + 1) * LANES, :], preferred_element_type=F32)
            pr = term if pr is None else pr + term
        bias = jnp.dot(pe, w1, preferred_element_type=F32)[0:1, :]
        hid = []
        for g in range(NSA_GROUPS):
            top = pr[:, g * CMP_HIDDEN:(g + 1) * CMP_HIDDEN]
            bot = pr[:, (NSA_GROUPS + g) * CMP_HIDDEN:(NSA_GROUPS + g + 1) * CMP_HIDDEN]
            hid.append(top + pltpu.roll(bot, n - 1, 0) + bias)
        act = _silu(jnp.concatenate(hid, axis=-1)).astype(BF16)
        return jnp.dot(act, w2bd, preferred_element_type=F32)

    kc = phi(kin_ref, pek_ref[...], w1k_ref[...], w1kc_ref, w2k_ref[...])
    kc_ref[0] = _rope(kc, cos_ref[...], sin_ref[...]).astype(BF16)
    vc = phi(vin_ref, pev_ref[...], w1v_ref[...], w1vc_ref, w2v_ref[...])
    vct_ref[0] = vc.T.astype(BF16)


def _compress(kin, vin, pek, w1k, w1kc, w2k, pev, w1v, w1vc, w2v, cos_c, sin_c):
    b, t, width = kin.shape
    ncp = t // CMP_STRIDE
    blk = pl.BlockSpec((1, t, width), lambda bi: (bi, 0, 0))
    cs = lambda a: _const_spec(a.shape)
    return pl.pallas_call(
        _compress_kernel,
        grid=(b,),
        in_specs=[blk, blk, cs(pek), cs(w1k), cs(w1kc), cs(w2k),
                  cs(pev), cs(w1v), cs(w1vc), cs(w2v), cs(cos_c), cs(sin_c)],
        out_specs=[pl.BlockSpec((1, ncp, LANES), lambda bi: (bi, 0, 0)),
                   pl.BlockSpec((1, LANES, ncp), lambda bi: (bi, 0, 0))],
        out_shape=[jax.ShapeDtypeStruct((b, ncp, LANES), BF16),
                   jax.ShapeDtypeStruct((b, LANES, ncp), BF16)],
        compiler_params=_params(("arbitrary",)),
        name="nsa_compress",
    )(kin, vin, pek, w1k, w1kc, w2k, pev, w1v, w1vc, w2v, cos_c, sin_c)


def _interleave(*programs):
    live = list(programs)
    while live:
        for gen in list(live):
            if next(gen, "done") == "done":
                live.remove(gen)


def _nsa_kernel(q_ref, qn_ref, gt_ref, kc_ref, vct_ref, ks_ref, vst_ref, kw_ref, vwt_ref, ovt_ref, ind_ref,
                o_ref, sa_ref, sb_ref, acc_ref, mstate_ref, qt2_ref, bias2_ref, oc2_ref, ow2_ref):
    i = pl.program_id(1)
    n_tiles = pl.num_programs(1)
    ncp = kc_ref.shape[1]
    ns = ovt_ref.shape[0]
    gw = NSA_GROUP_SIZE * Q_TILE
    width = NSA_GROUPS * gw
    sw_ = NSA_GROUPS * Q_TILE
    lanes = [slice(g * gw, (g + 1) * gw) for g in range(NSA_GROUPS)]
    feat = [slice(g * HEAD_DIM, (g + 1) * HEAD_DIM) for g in range(NSA_GROUPS)]
    lane_q = lax.broadcasted_iota(jnp.int32, (1, width), 1) % Q_TILE
    sel_keys = SEL_TRIP * KEY_BLOCK
    sel_rows = sel_keys // SEL_BLOCK
    win_keys = WINDOW + Q_TILE
    dot = lambda x, y: jnp.dot(x, y, preferred_element_type=F32)
    bufs = (sa_ref, sb_ref)
    q0 = i * Q_TILE
    t_row = q0 + lane_q
    n_full = q0 // sel_keys
    deep = n_full >= HEAD_TRIPS

    def v_ext(vt_ref, g, k0, n):
        return jnp.concatenate([vt_ref[0, feat[g], pl.ds(k0, n)], jnp.ones((BF16_ROWS, n), BF16)], axis=0)

    def prepare(tile, src_ref, slot, alone):
        p0 = tile * Q_TILE
        tp_row = p0 + lane_q
        qf = src_ref[0].astype(F32)
        zeros_half = jnp.zeros((HEAD_DIM, Q_TILE), F32)
        parts = []
        for g in range(NSA_GROUPS):
            for pair in range(NSA_GROUP_SIZE // 2):
                slab_t = qf[:, (2 * g + pair) * LANES:(2 * g + pair + 1) * LANES].T
                for half in range(2):
                    f = slab_t[half * HEAD_DIM:(half + 1) * HEAD_DIM, :]
                    parts.append(jnp.concatenate([f, zeros_half] if g == 0 else [zeros_half, f], axis=0))
        qt = jnp.concatenate(parts, axis=1).astype(BF16)
        qt2_ref[slot] = qt
        yield

        def compressed(rows):
            sc = dot(kc_ref[0, 0:rows, :], qt)
            c_end = lax.broadcasted_iota(jnp.int32, (rows, width), 0) * CMP_STRIDE + (CMP_BLOCK - 1)
            c_mask = c_end <= tp_row
            sc = jnp.where(c_mask, sc, NEG_INF)
            m_c = jnp.max(sc, axis=0, keepdims=True)
            e_c = jnp.where(c_mask, jnp.exp2(sc - m_c), 0.0)
            p_c = e_c / jnp.maximum(jnp.sum(e_c, axis=0, keepdims=True), 1e-30)
            outs, p_sum = [], []
            for g in range(NSA_GROUPS):
                outs.append(dot(vct_ref[0, feat[g], 0:rows], p_c[:, lanes[g]].astype(BF16)))
                acc = p_c[:, g * gw:g * gw + Q_TILE]
                for r in range(1, NSA_GROUP_SIZE):
                    acc = acc + p_c[:, g * gw + r * Q_TILE:g * gw + (r + 1) * Q_TILE]
                p_sum.append(acc)
            return outs + [_dot2_left(ovt_ref[:, 0:rows], jnp.concatenate(p_sum, axis=1))]

        if alone:
            n_vis = (p0 + Q_TILE - CMP_BLOCK) // CMP_STRIDE + 1
            blocks_needed = jnp.maximum(n_vis + KEY_BLOCK - 1, KEY_BLOCK) // KEY_BLOCK
            *o_c, imp = lax.switch(blocks_needed - 1, [functools.partial(compressed, nb * KEY_BLOCK)
                                                       for nb in range(1, ncp // KEY_BLOCK + 1)])
        else:
            *o_c, imp = compressed(ncp)
        for g in range(NSA_GROUPS):
            oc2_ref[slot, g] = o_c[g]
        yield

        w0 = pl.multiple_of(jnp.maximum(p0 - WINDOW, 0), KEY_BLOCK)
        sw = dot(kw_ref[0, pl.ds(w0, win_keys), :], qt)
        yield
        t_loc = tp_row - w0
        k_loc = lax.broadcasted_iota(jnp.int32, (win_keys, width), 0)
        old = lax.broadcasted_iota(jnp.int32, (KEY_BLOCK, width), 0) <= t_loc - WINDOW
        sw = jnp.where(k_loc <= t_loc, sw, NEG_INF)
        sw = jnp.concatenate([jnp.where(old, NEG_INF, sw[0:KEY_BLOCK]), sw[KEY_BLOCK:]], axis=0)
        m_w = jnp.max(sw, axis=0, keepdims=True)
        yield
        for g in range(NSA_GROUPS):
            p = jnp.exp2(sw[:, lanes[g]] - m_w[:, lanes[g]]).astype(BF16)
            pv = dot(v_ext(vwt_ref, g, w0, win_keys), p)
            ow2_ref[slot, g] = pv[0:HEAD_DIM, :] / pv[HEAD_DIM:HEAD_DIM + 1, :]
            yield

        s_id = lax.broadcasted_iota(jnp.int32, (ns, sw_), 0)
        cur = tp_row[:, 0:sw_] // SEL_BLOCK
        forced = (s_id == 0) | (s_id == cur) | (s_id == cur - 1)
        score = jnp.where(forced, -3e38, jnp.where(s_id <= cur, imp, -SEL_FORCE))
        bias = jnp.where(forced, 0.0, NEG_INF)
        for _ in range(min(SEL_TOPK, ns) - SEL_FORCED):
            mx = jnp.max(score, axis=0, keepdims=True)
            first = jnp.min(jnp.where(score == mx, s_id, ns), axis=0, keepdims=True)
            hit = s_id == first
            score = jnp.where(hit, -3e38, score)
            bias = jnp.where(hit, 0.0, bias)
            yield
        bias2_ref[slot] = bias

    def scores(j, s_ref, slot):
        k0 = pl.multiple_of(j * sel_keys, sel_keys)
        b0 = pl.multiple_of(j * sel_rows, sel_rows)
        rows = bias2_ref[slot, pl.ds(b0, sel_rows), :]
        rows = jnp.concatenate([rows[:, g * Q_TILE:(g + 1) * Q_TILE]
                                for g in range(NSA_GROUPS) for _ in range(NSA_GROUP_SIZE)], axis=1)
        rows = jnp.concatenate([rows, jnp.zeros_like(rows)], axis=0).astype(BF16)
        rhs = jnp.concatenate([qt2_ref[slot], rows,
                               jnp.zeros((LANES - rows.shape[0], width), BF16)], axis=0)
        lhs = jnp.concatenate([ks_ref[0, pl.ds(k0, sel_keys), :], ind_ref[...]], axis=1)
        s = dot(lhs, rhs)
        s_ref[...] = s
        return jnp.max(s, axis=0, keepdims=True)

    def softmax_pv(j, s_ref, mb, m_run):
        k0 = pl.multiple_of(j * sel_keys, sel_keys)
        m_new = jnp.maximum(m_run, mb)
        alpha = jnp.exp2(m_run - m_new)
        for g in range(NSA_GROUPS):
            p = jnp.exp2(s_ref[:, lanes[g]] - m_new[:, lanes[g]]).astype(BF16)
            pv = dot(v_ext(vst_ref, g, k0, sel_keys), p)
            acc_ref[g] = acc_ref[g] * alpha[:, lanes[g]] + pv
        return m_new

    def causal_tail(j, s_ref, m_run):
        k0 = pl.multiple_of(j * sel_keys, sel_keys)
        key = k0 + lax.broadcasted_iota(jnp.int32, (sel_keys, width), 0)
        s = jnp.where(key <= t_row, s_ref[...], NEG_INF)
        s_ref[...] = s
        softmax_pv(j, s_ref, jnp.max(s, axis=0, keepdims=True), m_run)

    def full_trips(j0, count, m_run, mb, slot):
        for t in range(count):
            mb_next = scores(j0 + t + 1, bufs[(t + 1) % 2], slot)
            m_run = softmax_pv(j0 + t, bufs[t % 2], mb, m_run)
            mb = mb_next
        return m_run, mb

    def head(slot):
        acc_ref[...] = jnp.zeros_like(acc_ref)
        m_run = jnp.full((1, width), NEG_INF, F32)
        mb = scores(0, sa_ref, slot)
        yield
        for j in range(HEAD_TRIPS):
            mb_next = scores(j + 1, bufs[(j + 1) % 2], slot)
            yield
            m_run = softmax_pv(j, bufs[j % 2], mb, m_run)
            mb = mb_next
            yield
        mstate_ref[0] = m_run
        mstate_ref[1] = mb

    nxt = jnp.minimum(i + 1, n_tiles - 1)

    @pl.when(i == 0)
    def _():
        _interleave(prepare(i, q_ref, 0, True))

    for par in range(2):
        @pl.when((i % 2 == par) & deep)
        def _(par=par):
            _interleave(prepare(nxt, qn_ref, 1 - par, False), head(par))

        @pl.when((i % 2 == par) & jnp.logical_not(deep))
        def _(par=par):
            _interleave(prepare(nxt, qn_ref, 1 - par, True))
            acc_ref[...] = jnp.zeros_like(acc_ref)
            mstate_ref[0] = jnp.full((1, width), NEG_INF, F32)
            mstate_ref[1] = scores(0, sa_ref, par)

    slot = i % 2
    j_head = jnp.where(deep, HEAD_TRIPS, 0)
    n_loops = (n_full - j_head) // LOOP_TRIPS
    m_run, mb_a = lax.fori_loop(
        0, n_loops, lambda kk, c: full_trips(j_head + LOOP_TRIPS * kk, LOOP_TRIPS, *c, slot),
        (mstate_ref[0], mstate_ref[1]))
    j_last = j_head + LOOP_TRIPS * n_loops

    for rem in range(LOOP_TRIPS):
        @pl.when(n_full - j_last == rem)
        def _(rem=rem):
            m_fin, _ = full_trips(j_last, rem, m_run, mb_a, slot)
            causal_tail(j_last + rem, bufs[rem % 2], m_fin)

    outs = []
    for g in range(NSA_GROUPS):
        o_s = acc_ref[g, 0:HEAD_DIM, :] / acc_ref[g, HEAD_DIM:HEAD_DIM + 1, :]
        o_c, o_w = oc2_ref[slot, g], ow2_ref[slot, g]
        heads = []
        for r in range(NSA_GROUP_SIZE):
            base = (g * NSA_GROUP_SIZE + r) * 3
            cols = slice(r * Q_TILE, (r + 1) * Q_TILE)
            heads.append(gt_ref[0, base:base + 1, :] * o_c[:, cols]
                         + gt_ref[0, base + 1:base + 2, :] * o_s[:, cols]
                         + gt_ref[0, base + 2:base + 3, :] * o_w[:, cols])
        for pair in range(NSA_GROUP_SIZE // 2):
            outs.append(jnp.concatenate(heads[2 * pair:2 * pair + 2], axis=0).T)
    o_ref[0] = jnp.concatenate(outs, axis=1)


def _nsa_attn(q, gates_t, kc, vct, ks, vst, kw, vwt, ovt):
    b, t, _ = q.shape
    ncp = kc.shape[1]
    ns = ovt.shape[0]
    n_tiles = t // Q_TILE
    sel_keys = SEL_TRIP * KEY_BLOCK
    width = NSA_GROUPS * NSA_GROUP_SIZE * Q_TILE
    assert t % sel_keys == 0 and sel_keys % Q_TILE == 0 and t >= WINDOW + Q_TILE and ncp % KEY_BLOCK == 0
    ind = (np.arange(sel_keys)[:, None] // SEL_BLOCK == np.arange(LANES)[None, :])
    ind = jnp.asarray(ind.astype(np.float32), dtype=BF16)
    full_rows = lambda bi, qi: (bi, 0, 0)
    return pl.pallas_call(
        _nsa_kernel,
        grid=(b, n_tiles),
        in_specs=[
            pl.BlockSpec((1, Q_TILE, NSA_WIDTH), lambda bi, qi: (bi, qi, 0)),
            pl.BlockSpec((1, Q_TILE, NSA_WIDTH), lambda bi, qi: (bi, jnp.minimum(qi + 1, n_tiles - 1), 0)),
            pl.BlockSpec((1, GATE_ROWS, Q_TILE), lambda bi, qi: (bi, 0, qi)),
            pl.BlockSpec((1, ncp, LANES), full_rows),
            pl.BlockSpec((1, LANES, ncp), full_rows),
            pl.BlockSpec((1, t, LANES), full_rows),
            pl.BlockSpec((1, LANES, t), full_rows),
            pl.BlockSpec((1, t, LANES), full_rows),
            pl.BlockSpec((1, LANES, t), full_rows),
            _const_spec((ns, ncp)),
            _const_spec((sel_keys, LANES)),
        ],
        out_specs=pl.BlockSpec((1, Q_TILE, NSA_WIDTH), lambda bi, qi: (bi, qi, 0)),
        out_shape=jax.ShapeDtypeStruct((b, t, NSA_WIDTH), F32),
        scratch_shapes=[pltpu.VMEM((sel_keys, width), F32),
                        pltpu.VMEM((sel_keys, width), F32),
                        pltpu.VMEM((NSA_GROUPS, HEAD_DIM + BF16_ROWS, NSA_GROUP_SIZE * Q_TILE), F32),
                        pltpu.VMEM((2, 1, width), F32),
                        pltpu.VMEM((2, LANES, width), BF16),
                        pltpu.VMEM((2, ns, NSA_GROUPS * Q_TILE), F32),
                        pltpu.VMEM((2, NSA_GROUPS, HEAD_DIM, NSA_GROUP_SIZE * Q_TILE), F32),
                        pltpu.VMEM((2, NSA_GROUPS, HEAD_DIM, NSA_GROUP_SIZE * Q_TILE), F32)],
        compiler_params=_params(("arbitrary", "arbitrary")),
        name="nsa_attn",
    )(q, q, gates_t, kc, vct, ks, vst, kw, vwt, ovt, ind)


def _same_head_mask():
    bi = lax.broadcasted_iota(jnp.int32, (HALF, HALF), 0) // HEAD_DIM
    bj = lax.broadcasted_iota(jnp.int32, (HALF, HALF), 1) // HEAD_DIM
    return bi == bj


def _bd_rows(x, same_head):
    xb = x.astype(BF16)
    tiled = jnp.concatenate([xb] * (HALF // x.shape[0]), axis=0)
    return jnp.where(same_head, tiled, jnp.zeros((), BF16))


def _rwkv_prep_stages(in_refs, par_refs, out_set, *, chunks):
    r_ref, k_ref, v_ref, lo_ref = in_refs
    w0_ref, w2_ref, a0_ref, a2_ref, g2_ref, kk_ref, ka_ref, rk_ref = par_refs
    wm_ref, zm_ref, arb_ref, rkv_ref, rt_ref, vb_ref, bkt_ref, gl_ref, bonus_ref, gate_ref = out_set
    c = RWKV_CHUNK
    nb = r_ref.shape[0]
    same_head = _same_head_mask()
    ones_bd = same_head.astype(BF16)
    n_tok = chunks * c
    row_i = lax.broadcasted_iota(jnp.int32, (n_tok, n_tok), 0)
    col_i = lax.broadcasted_iota(jnp.int32, (n_tok, n_tok), 1)
    tril_incl = ((row_i >= col_i) & (row_i // c == col_i // c)).astype(BF16)
    t_id = lax.broadcasted_iota(jnp.int32, (c, HALF), 0)
    j_id = lax.broadcasted_iota(jnp.int32, (c, HALF), 1) % HEAD_DIM
    strict_lower = t_id > j_id
    incl_lower = t_id >= j_id
    eye_all = (t_id == j_id).astype(F32)
    dot = lambda x, y: jnp.dot(x, y, preferred_element_type=F32)

    def bd_cols(xt):
        xb = xt.astype(BF16)
        return jnp.where(same_head, jnp.concatenate([xb, xb], axis=1), jnp.zeros((), BF16))

    halves = RWKV_WIDTH // HALF
    groups = [(b, ch) for b in range(nb) for ch in range(chunks)]
    chains = [(gi, hh) for gi in range(len(groups)) for hh in range(halves)]
    rows = [slice(ch * c, (ch + 1) * c) for ch in range(chunks)]
    lanes = [slice(hh * HALF, (hh + 1) * HALF) for hh in range(halves)]
    each = lambda fn: [fn(n, gi, lanes[hh]) for n, (gi, hh) in enumerate(chains)]

    lo = [lo_ref[b] for b in range(nb)]
    zs = [-(w0_ref[...] + _bdot(jnp.tanh(x[:, 0:DECAY_LORA]), w2_ref[...])) for x in lo]
    yield
    lr_b = [_sigmoid(a0_ref[...] + _bdot(x[:, DECAY_LORA:DECAY_LORA + AAA_LORA], a2_ref[...])) for x in lo]
    yield
    for b in range(nb):
        gate_ref[b] = _bdot(_sigmoid(lo[b][:, DECAY_LORA + AAA_LORA:]), g2_ref[...])
        vb_ref[b] = v_ref[b].astype(BF16)
    yield
    log_decay = []
    for z in zs:
        softplus = jnp.maximum(z, 0.0) + jnp.log(1.0 + jnp.exp(-jnp.abs(z)))
        log_decay.append(-jnp.exp(-softplus - 0.5))
    cum_b = [_dot3_left(tril_incl, x) for x in log_decay]
    yield
    r = [r_ref[b, rows[ch], :] for b, ch in groups]
    k = [k_ref[b, rows[ch], :] for b, ch in groups]
    v = [v_ref[b, rows[ch], :] for b, ch in groups]
    lr = [lr_b[b][rows[ch], :] for b, ch in groups]
    cum = [cum_b[b][rows[ch], :] for b, ch in groups]
    ld = [log_decay[b][rows[ch], :] for b, ch in groups]
    g_incl = [jnp.exp(x) for x in cum]
    g_excl = [jnp.exp(x - y) for x, y in zip(cum, ld)]
    g_inv = [jnp.exp(-x) for x in cum]
    for gi, (b, ch) in enumerate(groups):
        gl_ref[b, ch * SUBLANES:(ch + 1) * SUBLANES, :] = jnp.broadcast_to(
            g_incl[gi][c - 1:c, :], (SUBLANES, RWKV_WIDTH))

    kk = each(lambda n, gi, ln: k[gi][:, ln] * kk_ref[:, ln])
    k2 = each(lambda n, gi, ln: k[gi][:, ln] * (1.0 + (lr[gi][:, ln] - 1.0) * ka_ref[:, ln]))
    sums = each(lambda n, gi, ln: _dot2_right(
        jnp.concatenate([kk[n] * kk[n], r[gi][:, ln] * k2[n] * rk_ref[:, ln]], axis=0), ones_bd))
    ssq = [x[0:c] for x in sums]
    for n, (gi, hh) in enumerate(chains):
        b, ch = groups[gi]
        bonus_ref[b, rows[ch], lanes[hh]] = sums[n][c:] * v[gi][:, lanes[hh]]
    yield
    kk = each(lambda n, gi, ln: kk[n] / jnp.maximum(jnp.sqrt(ssq[n]), 1e-12))
    at = each(lambda n, gi, ln: -kk[n] * g_excl[gi][:, ln])
    bt = each(lambda n, gi, ln: kk[n] * lr[gi][:, ln] * g_inv[gi][:, ln])
    kt = each(lambda n, gi, ln: k2[n] * g_inv[gi][:, ln])
    rt = each(lambda n, gi, ln: r[gi][:, ln] * g_incl[gi][:, ln])
    for n, (gi, hh) in enumerate(chains):
        b, ch = groups[gi]
        rt_ref[b, rows[ch], lanes[hh]] = rt[n].astype(BF16)
        bkt_ref[b, (ch * halves + hh) * HALF:(ch * halves + hh + 1) * HALF, :] = (
            jnp.concatenate([bt[n], kt[n]], axis=0).T.astype(BF16))

    bt_bd = each(lambda n, gi, ln: bd_cols(jnp.concatenate([bt[n], bt[n]], axis=0).T))
    ar = each(lambda n, gi, ln: jnp.concatenate([at[n], rt[n]], axis=0).astype(BF16))
    ab = each(lambda n, gi, ln: dot(ar[n], bt_bd[n]))
    yield
    kt_bd = each(lambda n, gi, ln: bd_cols(jnp.concatenate([kt[n], kt[n]], axis=0).T))
    ak = each(lambda n, gi, ln: dot(ar[n], kt_bd[n]))
    yield
    a_ab = [jnp.where(strict_lower, x[0:c], 0.0) for x in ab]
    a_rb = [jnp.where(incl_lower, x[c:], 0.0) for x in ab]
    a_ak = [jnp.where(strict_lower, x[0:c], 0.0) for x in ak]
    a_rk = [jnp.where(incl_lower, x[c:], 0.0) for x in ak]
    for n, (gi, hh) in enumerate(chains):
        b, ch = groups[gi]
        arb_ref[b, rows[ch], lanes[hh]] = a_rb[n].astype(BF16)

    base = 4
    same_block = lambda s: (t_id // s) == (j_id // s)
    nd = [jnp.where(same_block(base), x, 0.0) for x in a_ab]
    nd2 = [dot(x.astype(BF16), _bd_rows(x, same_head)) for x in nd]
    yield
    inv = [eye_all + x for x in nd]
    inv = [x + dot(x.astype(BF16), _bd_rows(y, same_head)) for x, y in zip(inv, nd2)]
    yield
    s_blk = base
    while s_blk < c:
        lower_left = same_block(2 * s_blk) & ((t_id % (2 * s_blk)) >= s_blk) & ((j_id % (2 * s_blk)) < s_blk)
        te = [dot(x.astype(BF16), _bd_rows(jnp.where(lower_left, y, 0.0), same_head))
              for x, y in zip(inv, a_ab)]
        yield
        inv = [x + dot(y.astype(BF16), _bd_rows(x, same_head)) for x, y in zip(inv, te)]
        yield
        s_blk *= 2

    v_bd = each(lambda n, gi, ln: _bd_rows(v[gi][:, ln], same_head))
    inv_b = [x.astype(BF16) for x in inv]
    wm = each(lambda n, gi, ln: dot(inv_b[n], _bd_rows(at[n], same_head)))
    yield
    av = each(lambda n, gi, ln: dot(jnp.concatenate([a_ak[n], a_rk[n]], axis=0).astype(BF16), v_bd[n]))
    yield
    zm = each(lambda n, gi, ln: dot(inv_b[n], _bd_rows(av[n][0:c], same_head)))
    for n, (gi, hh) in enumerate(chains):
        b, ch = groups[gi]
        wm_ref[b, rows[ch], lanes[hh]] = wm[n].astype(BF16)
        zm_ref[b, rows[ch], lanes[hh]] = zm[n]
        rkv_ref[b, rows[ch], lanes[hh]] = av[n][c:]


def _rwkv_scan_stages(in_set, gng_ref, gnb_ref, o_ref, s_ref, *, chunks):
    wm_ref, zm_ref, arb_ref, rkv_ref, rt_ref, vb_ref, bkt_ref, gl_ref, bonus_ref, gate_ref = in_set
    c = RWKV_CHUNK
    halves = RWKV_WIDTH // HALF
    same_head = _same_head_mask()
    ones_bd = same_head.astype(BF16)
    chains = [(b, hh) for b in range(o_ref.shape[0]) for hh in range(halves)]
    lanes = [slice(hh * HALF, (hh + 1) * HALF) for hh in range(halves)]
    dot = lambda x, y: jnp.dot(x, y, preferred_element_type=F32)

    for ch in range(chunks):
        rw = slice(ch * c, (ch + 1) * c)
        each = lambda fn: [fn(n, b, lanes[hh]) for n, (b, hh) in enumerate(chains)]
        s0 = [s_ref[b, hh] for b, hh in chains]
        s0b = [x.astype(BF16) for x in s0]
        u = each(lambda n, b, ln: dot(wm_ref[b, rw, ln], s0b[n]) + zm_ref[b, rw, ln])
        yield
        y0 = each(lambda n, b, ln: dot(rt_ref[b, rw, ln], s0b[n]) + rkv_ref[b, rw, ln])
        yield
        y = each(lambda n, b, ln: y0[n] + dot(arb_ref[b, rw, ln], _bd_rows(u[n], same_head)))
        yield
        uv = each(lambda n, b, ln: jnp.concatenate([u[n].astype(BF16), vb_ref[b, rw, ln]], axis=0))
        upd = [dot(bkt_ref[b, (ch * halves + hh) * HALF:(ch * halves + hh + 1) * HALF, :], uv[n])
               for n, (b, hh) in enumerate(chains)]
        for n, (b, hh) in enumerate(chains):
            g_last = jnp.broadcast_to(gl_ref[b, ch * SUBLANES:ch * SUBLANES + 1, lanes[hh]], (LANES, HALF)).T
            g_col = jnp.concatenate([g_last, g_last], axis=1)
            s_ref[b, hh] = g_col * (s0[n] + jnp.where(same_head, upd[n], 0.0))
        yield
        mu = [_dot2_right(x, ones_bd) * (1.0 / HEAD_DIM) for x in y]
        yield
        yc = [x - m for x, m in zip(y, mu)]
        var = [_dot2_right(x * x, ones_bd) * (1.0 / HEAD_DIM) for x in yc]
        yield
        for n, (b, hh) in enumerate(chains):
            ln = lanes[hh]
            yn = yc[n] * lax.rsqrt(var[n] + RWKV_GN_EPS) * gng_ref[:, ln] + gnb_ref[:, ln]
            o_ref[b, rw, ln] = (yn + bonus_ref[b, rw, ln]) * gate_ref[b, rw, ln]


def _rwkv_kernel(*refs, chunks):
    in_refs, par_refs = refs[0:4], refs[4:12]
    gng_ref, gnb_ref, o_ref, s_ref = refs[12:16]
    sets = (refs[16:26], refs[26:36])
    step = pl.program_id(0)

    @pl.when(step == 0)
    def _():
        s_ref[...] = jnp.zeros_like(s_ref)
        for ref in sets[1]:
            ref[...] = jnp.zeros_like(ref)

    def run(write_set, read_set):
        prep = _rwkv_prep_stages(in_refs, par_refs, write_set, chunks=chunks)
        scan = _rwkv_scan_stages(read_set, gng_ref, gnb_ref, o_ref, s_ref, chunks=chunks)
        live = [prep, scan]
        while live:
            for gen in list(live):
                if next(gen, "done") == "done":
                    live.remove(gen)

    @pl.when(step % 2 == 0)
    def _():
        run(sets[0], sets[1])

    @pl.when(step % 2 == 1)
    def _():
        run(sets[1], sets[0])


def _rwkv(r, k, v, lo, w0, w2, a0, a2, g2, k_k, k_a, r_k, gn_g, gn_b, *, chunks=2):
    b, t, width = r.shape
    c = RWKV_CHUNK
    n_tok = chunks * c
    n_groups = t // n_tok
    halves = width // HALF
    vec = lambda a: a.reshape(1, width)
    cs = lambda a: _const_spec(a.shape)
    args = [vec(w0), w2.astype(BF16), vec(a0), a2.astype(BF16), g2.astype(BF16),
            vec(k_k), vec(k_a), vec(r_k), vec(gn_g), vec(gn_b)]
    ahead = lambda s: (0, jnp.minimum(s, n_groups - 1), 0)
    behind = lambda s: (0, jnp.maximum(s - 1, 0), 0)
    tok = lambda dt: pltpu.VMEM((b, n_tok, width), dt)
    scratch_set = [tok(BF16), tok(F32), tok(BF16), tok(F32), tok(BF16), tok(BF16),
                   pltpu.VMEM((b, chunks * halves * HALF, LANES), BF16),
                   pltpu.VMEM((b, chunks * SUBLANES, width), F32), tok(F32), tok(F32)]
    return pl.pallas_call(
        functools.partial(_rwkv_kernel, chunks=chunks),
        grid=(n_groups + 1,),
        in_specs=[pl.BlockSpec((b, n_tok, width), ahead)] * 3 + [pl.BlockSpec((b, n_tok, LORA_COLS), ahead)]
                 + [cs(a) for a in args],
        out_specs=pl.BlockSpec((b, n_tok, width), behind),
        out_shape=jax.ShapeDtypeStruct((b, t, width), F32),
        scratch_shapes=[pltpu.VMEM((b, halves, HALF, HALF), F32)] + scratch_set + scratch_set,
        compiler_params=_params(("arbitrary",)),
        name="rwkv7",
    )(r, k, v, lo, *args)


def _mem_kv_kernel(m_ref, g_ref, wk_ref, wv_ref, kt_ref, v_ref):
    m = _rms(m_ref[0], g_ref[...]).astype(BF16)
    kt_ref[0] = jnp.dot(m, wk_ref[...], preferred_element_type=F32).T.astype(BF16)
    v_ref[0] = jnp.dot(m, wv_ref[...], preferred_element_type=F32).astype(BF16)


def _mem_kv(mem, g, wk, wv):
    b, mt, d = mem.shape
    return pl.pallas_call(
        _mem_kv_kernel,
        grid=(b,),
        in_specs=[pl.BlockSpec((1, mt, d), lambda bi: (bi, 0, 0)), _const_spec((1, d)),
                  _const_spec((d, d)), _const_spec((d, d))],
        out_specs=[pl.BlockSpec((1, d, mt), lambda bi: (bi, 0, 0)),
                   pl.BlockSpec((1, mt, d), lambda bi: (bi, 0, 0))],
        out_shape=[jax.ShapeDtypeStruct((b, d, mt), BF16), jax.ShapeDtypeStruct((b, mt, d), BF16)],
        compiler_params=_params(("arbitrary",)),
        name="mem_kv",
    )(mem, g.reshape(1, d), wk.astype(BF16), wv.astype(BF16))


def _out_mem_kernel(x_ref, on_ref, or_ref, ng_ref, wo1_ref, wo2_ref, mpost_ref,
                    mpre_ref, wq_ref, kt_ref, v_ref, wo_ref, mempost_ref, o_ref, *, parts):
    tm = x_ref.shape[1] // parts
    rows = [slice(n * tm, (n + 1) * tm) for n in range(parts)]
    dot = lambda a, b: jnp.dot(a, b, preferred_element_type=F32)
    d = x_ref.shape[-1]
    hd = d // MEM_HEADS

    a = [_rms(on_ref[0, rw, :], ng_ref[...]).astype(BF16) for rw in rows]
    mixed = [dot(a[n], wo1_ref[...]) + dot(or_ref[0, rw, :].astype(BF16), wo2_ref[...])
             for n, rw in enumerate(rows)]
    x = [x_ref[0, rw, :] + _rms(mixed[n], mpost_ref[...]) for n, rw in enumerate(rows)]
    h = [_rms(xn, mpre_ref[...]).astype(BF16) for xn in x]
    q = [(dot(hn, wq_ref[...]) * (hd ** -0.5)).astype(BF16) for hn in h]
    heads = [[] for _ in rows]
    for hi in range(MEM_HEADS):
        cols = slice(hi * hd, (hi + 1) * hd)
        s = [dot(qn[:, cols], kt_ref[0, cols, :]) for qn in q]
        e = [jnp.exp(sn - jnp.max(sn, axis=-1, keepdims=True)) for sn in s]
        p = [(en / jnp.sum(en, axis=-1, keepdims=True)).astype(BF16) for en in e]
        for n in range(parts):
            heads[n].append(dot(p[n], v_ref[0, :, cols]))
    att = [dot(jnp.concatenate(hn, axis=-1).astype(BF16), wo_ref[...]) for hn in heads]
    for n, rw in enumerate(rows):
        o_ref[0, rw, :] = x[n] + _rms(att[n], mempost_ref[...])


def _out_mem(x3, o_nsa, o_rwkv, nsa_g, w_out, mix_post_g, mem_pre_g, wq, kt, vm, wo, mem_post_g,
             *, tm=1024, parts=4):
    b, t, d = x3.shape
    mt = vm.shape[1]
    row = lambda bi, ti: (bi, ti, 0)
    per_b = lambda bi, ti: (bi, 0, 0)
    w_out = w_out.astype(BF16)
    return pl.pallas_call(
        functools.partial(_out_mem_kernel, parts=parts),
        grid=(b, t // tm),
        in_specs=[
            pl.BlockSpec((1, tm, d), row),
            pl.BlockSpec((1, tm, NSA_WIDTH), row),
            pl.BlockSpec((1, tm, RWKV_WIDTH), row),
            _const_spec((1, NSA_WIDTH)),
            _const_spec((NSA_WIDTH, d)),
            _const_spec((RWKV_WIDTH, d)),
            _const_spec((1, d)),
            _const_spec((1, d)),
            _const_spec((d, d)),
            pl.BlockSpec((1, d, mt), per_b),
            pl.BlockSpec((1, mt, d), per_b),
            _const_spec((d, d)),
            _const_spec((1, d)),
        ],
        out_specs=pl.BlockSpec((1, tm, d), row),
        out_shape=jax.ShapeDtypeStruct((b, t, d), F32),
        compiler_params=_params(("arbitrary", "arbitrary")),
        name="out_mem",
    )(x3, o_nsa, o_rwkv, nsa_g.reshape(1, -1), w_out[:NSA_WIDTH], w_out[NSA_WIDTH:],
      mix_post_g.reshape(1, d), mem_pre_g.reshape(1, d), wq.astype(BF16), kt, vm,
      wo.astype(BF16), mem_post_g.reshape(1, d))


def _rope_tables(pos):
    half = HEAD_DIM // 2
    inv = ROPE_THETA ** (-jnp.arange(half, dtype=F32) / half)
    ang = pos.astype(F32)[:, None] * inv[None, :]
    cos, sin = jnp.cos(ang), jnp.sin(ang)
    cos_t = jnp.concatenate([cos, cos, cos, cos], axis=-1)
    sin_t = jnp.concatenate([-sin, sin, -sin, sin], axis=-1)
    return cos_t, sin_t


def _overlap_t(ns, ncp):
    c0 = np.arange(ncp)[None, :] * CMP_STRIDE
    s0 = np.arange(ns)[:, None] * SEL_BLOCK
    ov = (c0 < s0 + SEL_BLOCK) & (c0 + CMP_BLOCK > s0) & (np.arange(ncp)[None, :] < ncp - 1)
    return jnp.asarray(ov.astype(np.float32), dtype=BF16)


def _pad_cols(w, n):
    return jnp.pad(w, ((0, 0), (0, n - w.shape[1])))


def _cmp_weights(pe, w1, w2):
    per = CMP_STRIDE
    w1r = w1.reshape(CMP_BLOCK, HEAD_DIM, CMP_HIDDEN)
    blocks = []
    for part in range(CMP_BLOCK // per):
        for g in range(NSA_GROUPS):
            z = jnp.zeros((per, NSA_GROUPS, HEAD_DIM, CMP_HIDDEN), F32)
            z = z.at[:, g].set(w1r[part * per:(part + 1) * per])
            blocks.append(z.reshape(per * NSA_GROUPS * HEAD_DIM, CMP_HIDDEN))
    w1cat = jnp.concatenate(blocks, axis=1).astype(BF16)
    w2bd = jnp.zeros((NSA_GROUPS * CMP_HIDDEN, NSA_GROUPS * HEAD_DIM), F32)
    for g in range(NSA_GROUPS):
        w2bd = w2bd.at[g * CMP_HIDDEN:(g + 1) * CMP_HIDDEN, g * HEAD_DIM:(g + 1) * HEAD_DIM].set(w2)
    pe8 = jnp.broadcast_to(pe.reshape(1, CMP_BLOCK * HEAD_DIM), (SUBLANES, CMP_BLOCK * HEAD_DIM))
    return pe8.astype(BF16), w1.astype(BF16), w1cat, w2bd.astype(BF16)


def kernel(x, mem, ffn1_pre_g, ffn1_w_gate, ffn1_w_up, ffn1_w_down, ffn1_post_g, mix_pre_g, w_in, cmp_pe_k, cmp_w1_k, cmp_w2_k, cmp_pe_v, cmp_w1_v, cmp_w2_v, nsa_out_g, rwkv_mu, rwkv_w0, rwkv_w2, rwkv_a0, rwkv_a2, rwkv_g2, rwkv_k_k, rwkv_k_a, rwkv_r_k, rwkv_gn_g, rwkv_gn_b, w_out, mix_post_g, mem_pre_g, mem_kv_g, mem_wq, mem_wk, mem_wv, mem_wo, mem_post_g, ffn2_pre_g, ffn2_w_gate, ffn2_w_up, ffn2_w_down, ffn2_post_g):
    b, t, d = x.shape
    ncp = t // CMP_STRIDE
    ns = t // SEL_BLOCK
    cos_t, sin_t = _rope_tables(jnp.arange(t))
    cos_c, sin_c = _rope_tables(jnp.arange(ncp) * CMP_STRIDE + (CMP_BLOCK - 1))
    ovt = _overlap_t(ns, ncp)

    for l in range(ffn1_pre_g.shape[0]):
        x2 = _ffn_block(x.reshape(b * t, d), ffn1_pre_g[l], ffn1_w_gate[l], ffn1_w_up[l],
                        ffn1_w_down[l], ffn1_post_g[l])
        x3 = x2.reshape(b, t, d)

        wi = w_in[l]
        nsa_w = NSA_WIDTH + 6 * NSA_KV_WIDTH
        gate_w = wi[:, nsa_w:nsa_w + 3 * NSA_HEADS]
        w_cols = jnp.concatenate([wi[:, :nsa_w], _pad_cols(gate_w, LANES),
                                  wi[:, nsa_w + 3 * NSA_HEADS:]], axis=1).astype(BF16)
        (q, k_cmp, v_cmp, k_slc, v_slc_t, k_win, v_win_t, gates_t, r, k, v, lo) = _in_proj(
            x3, mix_pre_g[l], w_cols, rwkv_mu[l].reshape(1, -1), cos_t, sin_t)

        pek, w1k, w1kc, w2k = _cmp_weights(cmp_pe_k[l], cmp_w1_k[l], cmp_w2_k[l])
        pev, w1v, w1vc, w2v = _cmp_weights(cmp_pe_v[l], cmp_w1_v[l], cmp_w2_v[l])
        kc, vct = _compress(k_cmp, v_cmp,
                            pek, w1k, w1kc, w2k, pev, w1v, w1vc, w2v, cos_c, sin_c)
        o_nsa = _nsa_attn(q, gates_t, kc, vct, k_slc, v_slc_t, k_win, v_win_t, ovt)

        o_rwkv = _rwkv(r, k, v, lo, rwkv_w0[l], rwkv_w2[l], rwkv_a0[l], rwkv_a2[l], rwkv_g2[l],
                       rwkv_k_k[l], rwkv_k_a[l], rwkv_r_k[l], rwkv_gn_g[l], rwkv_gn_b[l])

        kt, vm = _mem_kv(mem, mem_kv_g[l], mem_wk[l], mem_wv[l])
        x4 = _out_mem(x3, o_nsa, o_rwkv, nsa_out_g[l], w_out[l], mix_post_g[l], mem_pre_g[l],
                      mem_wq[l], kt, vm, mem_wo[l], mem_post_g[l])

        x = _ffn_block(x4.reshape(b * t, d), ffn2_pre_g[l], ffn2_w_gate[l], ffn2_w_up[l],
                       ffn2_w_down[l], ffn2_post_g[l]).reshape(b, t, d)
    return x
```

```python
import functools

import numpy as np
import jax
import jax.numpy as jnp
from jax import lax
from jax.experimental import pallas as pl
from jax.experimental.pallas import tpu as pltpu

F32 = jnp.float32
BF16 = jnp.bfloat16

HEAD_DIM = 64
NSA_HEADS = 8
NSA_GROUPS = 2
NSA_GROUP_SIZE = 4
NSA_WIDTH = 512
NSA_KV_WIDTH = 128
CMP_BLOCK = 32
CMP_STRIDE = 16
CMP_HIDDEN = 256
SEL_BLOCK = 64
SEL_TOPK = 16
SEL_FORCE = 1e4
SEL_FORCED = 3
WINDOW = 512
RWKV_WIDTH = 512
DECAY_LORA = 64
AAA_LORA = 64
GATE_LORA = 128
RWKV_GN_EPS = 64e-5
MEM_HEADS = 4
ROPE_THETA = 10000.0
NORM_EPS = 1e-6
NEG_INF = -1e30
LOG2_E = 1.4426950408889634

LORA_COLS = DECAY_LORA + AAA_LORA + GATE_LORA
RWKV_COLS = 3 * RWKV_WIDTH + LORA_COLS

LANES = 128
SUBLANES = 8
BF16_ROWS = 16
GATE_ROWS = 32
KEY_BLOCK = 128
Q_TILE = 128
SEL_TRIP = 4
LOOP_TRIPS = 4
HEAD_TRIPS = 8
MID_TRIPS = 4
RWKV_CHUNK = 64
HALF = 256
VMEM_LIMIT = 56 * 1024 * 1024


def _bdot(a, b):
    return jnp.dot(a.astype(BF16), b.astype(BF16), preferred_element_type=F32)


def _split3(x):
    h1 = x.astype(BF16)
    r1 = x - h1.astype(F32)
    h2 = r1.astype(BF16)
    r2 = r1 - h2.astype(F32)
    return h1, h2, r2.astype(BF16)


def _split2(x):
    h1 = x.astype(BF16)
    return h1, (x - h1.astype(F32)).astype(BF16)


def _dot2_right(x, m):
    h1, h2 = _split2(x)
    return (jnp.dot(h1, m, preferred_element_type=F32) + jnp.dot(h2, m, preferred_element_type=F32))


def _dot3_right(x, m):
    h1, h2, h3 = _split3(x)
    d = lambda h: jnp.dot(h, m, preferred_element_type=F32)
    return d(h1) + d(h2) + d(h3)


def _dot2_left(m, x):
    h1, h2 = _split2(x)
    return (jnp.dot(m, h1, preferred_element_type=F32) + jnp.dot(m, h2, preferred_element_type=F32))


def _dot3_left(m, x):
    h1, h2, h3 = _split3(x)
    d = lambda h: jnp.dot(m, h, preferred_element_type=F32)
    return d(h1) + d(h2) + d(h3)


def _rms(x, g):
    return x * lax.rsqrt(jnp.mean(x * x, axis=-1, keepdims=True) + NORM_EPS) * g


def _silu(x):
    return x / (1.0 + jnp.exp(-x))


def _sigmoid(x):
    return 1.0 / (1.0 + jnp.exp(-x))


def _const_spec(shape):
    nd = len(shape)
    return pl.BlockSpec(shape, lambda *_: (0,) * nd)


def _params(sem):
    return pltpu.CompilerParams(dimension_semantics=sem, vmem_limit_bytes=VMEM_LIMIT)


def _ffn_kernel(x_ref, pre_ref, wg_ref, wu_ref, wd_ref, post_ref, o_ref, *, ff_chunk):
    x = x_ref[...]
    h = _rms(x, pre_ref[...]).astype(BF16)
    d_ff = wg_ref.shape[1]
    acc = jnp.zeros(x.shape, F32)
    for c0 in range(0, d_ff, ff_chunk):
        g = jnp.dot(h, wg_ref[:, c0:c0 + ff_chunk], preferred_element_type=F32)
        u = jnp.dot(h, wu_ref[:, c0:c0 + ff_chunk], preferred_element_type=F32)
        a = (_silu(g) * u).astype(BF16)
        acc = acc + jnp.dot(a, wd_ref[c0:c0 + ff_chunk, :], preferred_element_type=F32)
    o_ref[...] = x + 0.5 * _rms(acc, post_ref[...])


def _ffn_block(x2, pre_g, wg, wu, wd, post_g, *, tm=512, ff_chunk=256):
    m, d = x2.shape
    d_ff = wg.shape[1]
    return pl.pallas_call(
        functools.partial(_ffn_kernel, ff_chunk=ff_chunk),
        grid=(m // tm,),
        in_specs=[
            pl.BlockSpec((tm, d), lambda i: (i, 0)),
            _const_spec((1, d)),
            _const_spec((d, d_ff)),
            _const_spec((d, d_ff)),
            _const_spec((d_ff, d)),
            _const_spec((1, d)),
        ],
        out_specs=pl.BlockSpec((tm, d), lambda i: (i, 0)),
        out_shape=jax.ShapeDtypeStruct((m, d), F32),
        compiler_params=_params(("arbitrary",)),
        name="ffn_block",
    )(x2, pre_g.reshape(1, d), wg.astype(BF16), wu.astype(BF16), wd.astype(BF16),
      post_g.reshape(1, d))


def _swap_halves(x):
    n = x.shape[-1]
    lane = lax.broadcasted_iota(jnp.int32, x.shape, x.ndim - 1)
    fwd = pltpu.roll(x, n - HEAD_DIM // 2, x.ndim - 1)
    bwd = pltpu.roll(x, HEAD_DIM // 2, x.ndim - 1)
    return jnp.where((lane % HEAD_DIM) < HEAD_DIM // 2, fwd, bwd)


def _rope(x, cos, sin_signed):
    reps = x.shape[-1] // LANES
    c = jnp.concatenate([cos] * reps, axis=-1) if reps > 1 else cos
    s = jnp.concatenate([sin_signed] * reps, axis=-1) if reps > 1 else sin_signed
    return x * c + _swap_halves(x) * s


def _in_proj_kernel(x_ref, g_ref, w_ref, mu_ref, cos_ref, sin_ref,
                    q_ref, kc_ref, vc_ref, ks_ref, vst_ref, kw_ref, vwt_ref, gt_ref,
                    r_ref, k_ref, v_ref, lo_ref, carry_ref):
    @pl.when(pl.program_id(1) == 0)
    def _():
        carry_ref[...] = jnp.zeros_like(carry_ref)

    h = _rms(x_ref[0], g_ref[...]).astype(BF16)
    p = jnp.dot(h, w_ref[...], preferred_element_type=F32)
    cos = cos_ref[...]
    sin = sin_ref[...]
    tm = p.shape[0]

    o = 0
    q = _rope(p[:, o:o + NSA_WIDTH], cos, sin) * (HEAD_DIM ** -0.5 * LOG2_E)
    q_ref[0] = q.astype(BF16)
    o += NSA_WIDTH
    kc_ref[0] = p[:, o:o + LANES]; o += LANES
    vc_ref[0] = p[:, o:o + LANES]; o += LANES
    ks_ref[0] = _rope(p[:, o:o + LANES], cos, sin).astype(BF16); o += LANES
    vst_ref[0] = p[:, o:o + LANES].T.astype(BF16); o += LANES
    kw_ref[0] = _rope(p[:, o:o + LANES], cos, sin).astype(BF16); o += LANES
    vwt_ref[0] = p[:, o:o + LANES].T.astype(BF16); o += LANES
    gates_t = _sigmoid(p[:, o:o + LANES]).T
    gt_ref[0] = gates_t[:gt_ref.shape[1], :]
    o += LANES

    rw = p[:, o:]
    row = lax.broadcasted_iota(jnp.int32, rw.shape, 0)
    prev = jnp.where(row == 0, carry_ref[0:1, :], pltpu.roll(rw, 1, 0))
    carry_ref[...] = jnp.broadcast_to(rw[tm - 1:tm, :], carry_ref.shape)
    mixed = rw + (prev - rw) * mu_ref[...]
    r_ref[0] = mixed[:, 0:RWKV_WIDTH]
    k_ref[0] = mixed[:, RWKV_WIDTH:2 * RWKV_WIDTH]
    v_ref[0] = mixed[:, 2 * RWKV_WIDTH:3 * RWKV_WIDTH]
    lo_ref[0] = mixed[:, 3 * RWKV_WIDTH:RWKV_COLS]


def _in_proj(x3, g, w_cols, mu_cols, cos_t, sin_t, *, tm=512):
    b, t, d = x3.shape
    n = w_cols.shape[1]
    row = lambda bi, ti: (bi, ti, 0)
    col = lambda bi, ti: (bi, 0, ti)
    out_shapes = [
        jax.ShapeDtypeStruct((b, t, NSA_WIDTH), BF16),
        jax.ShapeDtypeStruct((b, t, LANES), F32),
        jax.ShapeDtypeStruct((b, t, LANES), F32),
        jax.ShapeDtypeStruct((b, t, LANES), BF16),
        jax.ShapeDtypeStruct((b, LANES, t), BF16),
        jax.ShapeDtypeStruct((b, t, LANES), BF16),
        jax.ShapeDtypeStruct((b, LANES, t), BF16),
        jax.ShapeDtypeStruct((b, GATE_ROWS, t), F32),
        jax.ShapeDtypeStruct((b, t, RWKV_WIDTH), F32),
        jax.ShapeDtypeStruct((b, t, RWKV_WIDTH), F32),
        jax.ShapeDtypeStruct((b, t, RWKV_WIDTH), F32),
        jax.ShapeDtypeStruct((b, t, LORA_COLS), F32),
    ]
    out_specs = [
        pl.BlockSpec((1, tm, NSA_WIDTH), row),
        pl.BlockSpec((1, tm, LANES), row),
        pl.BlockSpec((1, tm, LANES), row),
        pl.BlockSpec((1, tm, LANES), row),
        pl.BlockSpec((1, LANES, tm), col),
        pl.BlockSpec((1, tm, LANES), row),
        pl.BlockSpec((1, LANES, tm), col),
        pl.BlockSpec((1, GATE_ROWS, tm), col),
        pl.BlockSpec((1, tm, RWKV_WIDTH), row),
        pl.BlockSpec((1, tm, RWKV_WIDTH), row),
        pl.BlockSpec((1, tm, RWKV_WIDTH), row),
        pl.BlockSpec((1, tm, LORA_COLS), row),
    ]
    return pl.pallas_call(
        _in_proj_kernel,
        grid=(b, t // tm),
        in_specs=[
            pl.BlockSpec((1, tm, d), row),
            _const_spec((1, d)),
            _const_spec((d, n)),
            _const_spec((1, RWKV_COLS)),
            pl.BlockSpec((tm, LANES), lambda bi, ti: (ti, 0)),
            pl.BlockSpec((tm, LANES), lambda bi, ti: (ti, 0)),
        ],
        out_specs=out_specs,
        out_shape=out_shapes,
        scratch_shapes=[pltpu.VMEM((SUBLANES, RWKV_COLS), F32)],
        compiler_params=_params(("arbitrary", "arbitrary")),
        name="in_proj",
    )(x3, g.reshape(1, d), w_cols, mu_cols, cos_t, sin_t)


def _compress_kernel(kin_ref, vin_ref, pek_ref, w1k_ref, w1kc_ref, w2k_ref,
                     pev_ref, w1v_ref, w1vc_ref, w2v_ref, cos_ref, sin_ref,
                     kc_ref, vct_ref):
    def phi(tok_ref, pe, w1, w1cat_ref, w2bd):
        n = tok_ref.shape[1] // CMP_STRIDE
        pr = None
        for l in range(CMP_STRIDE):
            rows = tok_ref[0, pl.ds(l, n, stride=CMP_STRIDE), :].astype(BF16)
            term = jnp.dot(rows, w1cat_ref[l * LANES:(l + 1) * LANES, :], preferred_element_type=F32)
            pr = term if pr is None else pr + term
        bias = jnp.dot(pe, w1, preferred_element_type=F32)[0:1, :]
        hid = []
        for g in range(NSA_GROUPS):
            top = pr[:, g * CMP_HIDDEN:(g + 1) * CMP_HIDDEN]
            bot = pr[:, (NSA_GROUPS + g) * CMP_HIDDEN:(NSA_GROUPS + g + 1) * CMP_HIDDEN]
            hid.append(top + pltpu.roll(bot, n - 1, 0) + bias)
        act = _silu(jnp.concatenate(hid, axis=-1)).astype(BF16)
        return jnp.dot(act, w2bd, preferred_element_type=F32)

    kc = phi(kin_ref, pek_ref[...], w1k_ref[...], w1kc_ref, w2k_ref[...])
    kc_ref[0] = _rope(kc, cos_ref[...], sin_ref[...]).astype(BF16)
    vc = phi(vin_ref, pev_ref[...], w1v_ref[...], w1vc_ref, w2v_ref[...])
    vct_ref[0] = vc.T.astype(BF16)


def _compress(kin, vin, pek, w1k, w1kc, w2k, pev, w1v, w1vc, w2v, cos_c, sin_c):
    b, t, width = kin.shape
    ncp = t // CMP_STRIDE
    blk = pl.BlockSpec((1, t, width), lambda bi: (bi, 0, 0))
    cs = lambda a: _const_spec(a.shape)
    return pl.pallas_call(
        _compress_kernel,
        grid=(b,),
        in_specs=[blk, blk, cs(pek), cs(w1k), cs(w1kc), cs(w2k),
                  cs(pev), cs(w1v), cs(w1vc), cs(w2v), cs(cos_c), cs(sin_c)],
        out_specs=[pl.BlockSpec((1, ncp, LANES), lambda bi: (bi, 0, 0)),
                   pl.BlockSpec((1, LANES, ncp), lambda bi: (bi, 0, 0))],
        out_shape=[jax.ShapeDtypeStruct((b, ncp, LANES), BF16),
                   jax.ShapeDtypeStruct((b, LANES, ncp), BF16)],
        compiler_params=_params(("arbitrary",)),
        name="nsa_compress",
    )(kin, vin, pek, w1k, w1kc, w2k, pev, w1v, w1vc, w2v, cos_c, sin_c)


def _interleave(*programs):
    live = list(programs)
    while live:
        for gen in list(live):
            if next(gen, "done") == "done":
                live.remove(gen)


def _nsa_kernel(q_ref, qn_ref, gt_ref, kc_ref, vct_ref, ks_ref, vst_ref, kw_ref, vwt_ref, ovt_ref, ind_ref,
                o_ref, sa_ref, sb_ref, acc_ref, mstate_ref, qt2_ref, bias2_ref, oc2_ref, ow2_ref):
    i = pl.program_id(1)
    n_tiles = pl.num_programs(1)
    ncp = kc_ref.shape[1]
    ns = ovt_ref.shape[0]
    gw = NSA_GROUP_SIZE * Q_TILE
    width = NSA_GROUPS * gw
    sw_ = NSA_GROUPS * Q_TILE
    lanes = [slice(g * gw, (g + 1) * gw) for g in range(NSA_GROUPS)]
    feat = [slice(g * HEAD_DIM, (g + 1) * HEAD_DIM) for g in range(NSA_GROUPS)]
    lane_q = lax.broadcasted_iota(jnp.int32, (1, width), 1) % Q_TILE
    sel_keys = SEL_TRIP * KEY_BLOCK
    sel_rows = sel_keys // SEL_BLOCK
    win_keys = WINDOW + Q_TILE
    dot = lambda x, y: jnp.dot(x, y, preferred_element_type=F32)
    bufs = (sa_ref, sb_ref)
    q0 = i * Q_TILE
    t_row = q0 + lane_q
    n_full = q0 // sel_keys
    deep = n_full >= HEAD_TRIPS

    def v_ext(vt_ref, g, k0, n):
        return jnp.concatenate([vt_ref[0, feat[g], pl.ds(k0, n)], jnp.ones((BF16_ROWS, n), BF16)], axis=0)

    def prepare(tile, src_ref, slot, alone):
        p0 = tile * Q_TILE
        tp_row = p0 + lane_q
        qf = src_ref[0].astype(F32)
        zeros_half = jnp.zeros((HEAD_DIM, Q_TILE), F32)
        parts = []
        for g in range(NSA_GROUPS):
            for pair in range(NSA_GROUP_SIZE // 2):
                slab_t = qf[:, (2 * g + pair) * LANES:(2 * g + pair + 1) * LANES].T
                for half in range(2):
                    f = slab_t[half * HEAD_DIM:(half + 1) * HEAD_DIM, :]
                    parts.append(jnp.concatenate([f, zeros_half] if g == 0 else [zeros_half, f], axis=0))
        qt = jnp.concatenate(parts, axis=1).astype(BF16)
        qt2_ref[slot] = qt
        yield

        def compressed(rows):
            sc = dot(kc_ref[0, 0:rows, :], qt)
            c_end = lax.broadcasted_iota(jnp.int32, (rows, width), 0) * CMP_STRIDE + (CMP_BLOCK - 1)
            c_mask = c_end <= tp_row
            sc = jnp.where(c_mask, sc, NEG_INF)
            m_c = jnp.max(sc, axis=0, keepdims=True)
            e_c = jnp.where(c_mask, jnp.exp2(sc - m_c), 0.0)
            p_c = e_c / jnp.maximum(jnp.sum(e_c, axis=0, keepdims=True), 1e-30)
            outs, p_sum = [], []
            for g in range(NSA_GROUPS):
                outs.append(dot(vct_ref[0, feat[g], 0:rows], p_c[:, lanes[g]].astype(BF16)))
                acc = p_c[:, g * gw:g * gw + Q_TILE]
                for r in range(1, NSA_GROUP_SIZE):
                    acc = acc + p_c[:, g * gw + r * Q_TILE:g * gw + (r + 1) * Q_TILE]
                p_sum.append(acc)
            return outs + [_dot2_left(ovt_ref[:, 0:rows], jnp.concatenate(p_sum, axis=1))]

        if alone:
            n_vis = (p0 + Q_TILE - CMP_BLOCK) // CMP_STRIDE + 1
            blocks_needed = jnp.maximum(n_vis + KEY_BLOCK - 1, KEY_BLOCK) // KEY_BLOCK
            *o_c, imp = lax.switch(blocks_needed - 1, [functools.partial(compressed, nb * KEY_BLOCK)
                                                       for nb in range(1, ncp // KEY_BLOCK + 1)])
        else:
            *o_c, imp = compressed(ncp)
        for g in range(NSA_GROUPS):
            oc2_ref[slot, g] = o_c[g]
        yield

        w0 = pl.multiple_of(jnp.maximum(p0 - WINDOW, 0), KEY_BLOCK)
        sw = dot(kw_ref[0, pl.ds(w0, win_keys), :], qt)
        yield
        t_loc = tp_row - w0
        k_loc = lax.broadcasted_iota(jnp.int32, (win_keys, width), 0)
        old = lax.broadcasted_iota(jnp.int32, (KEY_BLOCK, width), 0) <= t_loc - WINDOW
        sw = jnp.where(k_loc <= t_loc, sw, NEG_INF)
        sw = jnp.concatenate([jnp.where(old, NEG_INF, sw[0:KEY_BLOCK]), sw[KEY_BLOCK:]], axis=0)
        m_w = jnp.max(sw, axis=0, keepdims=True)
        yield
        for g in range(NSA_GROUPS):
            p = jnp.exp2(sw[:, lanes[g]] - m_w[:, lanes[g]]).astype(BF16)
            pv = dot(v_ext(vwt_ref, g, w0, win_keys), p)
            ow2_ref[slot, g] = pv[0:HEAD_DIM, :] / pv[HEAD_DIM:HEAD_DIM + 1, :]
            yield

        s_id = lax.broadcasted_iota(jnp.int32, (ns, sw_), 0)
        cur = tp_row[:, 0:sw_] // SEL_BLOCK
        forced = (s_id == 0) | (s_id == cur) | (s_id == cur - 1)
        score = jnp.where(forced, -3e38, jnp.where(s_id <= cur, imp, -SEL_FORCE))
        bias = jnp.where(forced, 0.0, NEG_INF)
        for _ in range(min(SEL_TOPK, ns) - SEL_FORCED):
            mx = jnp.max(score, axis=0, keepdims=True)
            first = jnp.min(jnp.where(score == mx, s_id, ns), axis=0, keepdims=True)
            hit = s_id == first
            score = jnp.where(hit, -3e38, score)
            bias = jnp.where(hit, 0.0, bias)
            yield
        bias2_ref[slot] = bias

    def scores(j, s_ref, slot):
        k0 = pl.multiple_of(j * sel_keys, sel_keys)
        b0 = pl.multiple_of(j * sel_rows, sel_rows)
        rows = bias2_ref[slot, pl.ds(b0, sel_rows), :]
        rows = jnp.concatenate([rows[:, g * Q_TILE:(g + 1) * Q_TILE]
                                for g in range(NSA_GROUPS) for _ in range(NSA_GROUP_SIZE)], axis=1)
        rows = jnp.concatenate([rows, jnp.zeros_like(rows)], axis=0).astype(BF16)
        rhs = jnp.concatenate([qt2_ref[slot], rows,
                               jnp.zeros((LANES - rows.shape[0], width), BF16)], axis=0)
        lhs = jnp.concatenate([ks_ref[0, pl.ds(k0, sel_keys), :], ind_ref[...]], axis=1)
        s = dot(lhs, rhs)
        s_ref[...] = s
        return jnp.max(s, axis=0, keepdims=True)

    def softmax_pv(j, s_ref, mb, m_run):
        k0 = pl.multiple_of(j * sel_keys, sel_keys)
        m_new = jnp.maximum(m_run, mb)
        alpha = jnp.exp2(m_run - m_new)
        for g in range(NSA_GROUPS):
            p = jnp.exp2(s_ref[:, lanes[g]] - m_new[:, lanes[g]]).astype(BF16)
            pv = dot(v_ext(vst_ref, g, k0, sel_keys), p)
            acc_ref[g] = acc_ref[g] * alpha[:, lanes[g]] + pv
        return m_new

    def causal_tail(j, s_ref, m_run):
        k0 = pl.multiple_of(j * sel_keys, sel_keys)
        key = k0 + lax.broadcasted_iota(jnp.int32, (sel_keys, width), 0)
        s = jnp.where(key <= t_row, s_ref[...], NEG_INF)
        s_ref[...] = s
        softmax_pv(j, s_ref, jnp.max(s, axis=0, keepdims=True), m_run)

    def full_trips(j0, count, m_run, mb, slot):
        for t in range(count):
            mb_next = scores(j0 + t + 1, bufs[(t + 1) % 2], slot)
            m_run = softmax_pv(j0 + t, bufs[t % 2], mb, m_run)
            mb = mb_next
        return m_run, mb

    def head(slot, trips):
        acc_ref[...] = jnp.zeros_like(acc_ref)
        m_run = jnp.full((1, width), NEG_INF, F32)
        mb = scores(0, sa_ref, slot)
        yield
        for j in range(trips):
            mb_next = scores(j + 1, bufs[(j + 1) % 2], slot)
            yield
            m_run = softmax_pv(j, bufs[j % 2], mb, m_run)
            mb = mb_next
            yield
        mstate_ref[0] = m_run
        mstate_ref[1] = mb

    nxt = jnp.minimum(i + 1, n_tiles - 1)

    @pl.when(i == 0)
    def _():
        _interleave(prepare(i, q_ref, 0, True))

    mid = (n_full >= MID_TRIPS) & jnp.logical_not(deep)
    for par in range(2):
        @pl.when((i % 2 == par) & deep)
        def _(par=par):
            _interleave(prepare(nxt, qn_ref, 1 - par, False), head(par, HEAD_TRIPS))

        @pl.when((i % 2 == par) & mid)
        def _(par=par):
            _interleave(prepare(nxt, qn_ref, 1 - par, False), head(par, MID_TRIPS))

        @pl.when((i % 2 == par) & (n_full < MID_TRIPS))
        def _(par=par):
            _interleave(prepare(nxt, qn_ref, 1 - par, True))
            acc_ref[...] = jnp.zeros_like(acc_ref)
            mstate_ref[0] = jnp.full((1, width), NEG_INF, F32)
            mstate_ref[1] = scores(0, sa_ref, par)

    slot = i % 2
    j_head = jnp.where(deep, HEAD_TRIPS, jnp.where(mid, MID_TRIPS, 0))
    n_loops = (n_full - j_head) // LOOP_TRIPS
    m_run, mb_a = lax.fori_loop(
        0, n_loops, lambda kk, c: full_trips(j_head + LOOP_TRIPS * kk, LOOP_TRIPS, *c, slot),
        (mstate_ref[0], mstate_ref[1]))
    j_last = j_head + LOOP_TRIPS * n_loops

    for rem in range(LOOP_TRIPS):
        @pl.when(n_full - j_last == rem)
        def _(rem=rem):
            m_fin, _ = full_trips(j_last, rem, m_run, mb_a, slot)
            causal_tail(j_last + rem, bufs[rem % 2], m_fin)

    outs = []
    for g in range(NSA_GROUPS):
        o_s = acc_ref[g, 0:HEAD_DIM, :] / acc_ref[g, HEAD_DIM:HEAD_DIM + 1, :]
        o_c, o_w = oc2_ref[slot, g], ow2_ref[slot, g]
        heads = []
        for r in range(NSA_GROUP_SIZE):
            base = (g * NSA_GROUP_SIZE + r) * 3
            cols = slice(r * Q_TILE, (r + 1) * Q_TILE)
            heads.append(gt_ref[0, base:base + 1, :] * o_c[:, cols]
                         + gt_ref[0, base + 1:base + 2, :] * o_s[:, cols]
                         + gt_ref[0, base + 2:base + 3, :] * o_w[:, cols])
        for pair in range(NSA_GROUP_SIZE // 2):
            outs.append(jnp.concatenate(heads[2 * pair:2 * pair + 2], axis=0).T)
    o_ref[0] = jnp.concatenate(outs, axis=1)


def _nsa_attn(q, gates_t, kc, vct, ks, vst, kw, vwt, ovt):
    b, t, _ = q.shape
    ncp = kc.shape[1]
    ns = ovt.shape[0]
    n_tiles = t // Q_TILE
    sel_keys = SEL_TRIP * KEY_BLOCK
    width = NSA_GROUPS * NSA_GROUP_SIZE * Q_TILE
    assert t % sel_keys == 0 and sel_keys % Q_TILE == 0 and t >= WINDOW + Q_TILE and ncp % KEY_BLOCK == 0
    ind = (np.arange(sel_keys)[:, None] // SEL_BLOCK == np.arange(LANES)[None, :])
    ind = jnp.asarray(ind.astype(np.float32), dtype=BF16)
    full_rows = lambda bi, qi: (bi, 0, 0)
    return pl.pallas_call(
        _nsa_kernel,
        grid=(b, n_tiles),
        in_specs=[
            pl.BlockSpec((1, Q_TILE, NSA_WIDTH), lambda bi, qi: (bi, qi, 0)),
            pl.BlockSpec((1, Q_TILE, NSA_WIDTH), lambda bi, qi: (bi, jnp.minimum(qi + 1, n_tiles - 1), 0)),
            pl.BlockSpec((1, GATE_ROWS, Q_TILE), lambda bi, qi: (bi, 0, qi)),
            pl.BlockSpec((1, ncp, LANES), full_rows),
            pl.BlockSpec((1, LANES, ncp), full_rows),
            pl.BlockSpec((1, t, LANES), full_rows),
            pl.BlockSpec((1, LANES, t), full_rows),
            pl.BlockSpec((1, t, LANES), full_rows),
            pl.BlockSpec((1, LANES, t), full_rows),
            _const_spec((ns, ncp)),
            _const_spec((sel_keys, LANES)),
        ],
        out_specs=pl.BlockSpec((1, Q_TILE, NSA_WIDTH), lambda bi, qi: (bi, qi, 0)),
        out_shape=jax.ShapeDtypeStruct((b, t, NSA_WIDTH), F32),
        scratch_shapes=[pltpu.VMEM((sel_keys, width), F32),
                        pltpu.VMEM((sel_keys, width), F32),
                        pltpu.VMEM((NSA_GROUPS, HEAD_DIM + BF16_ROWS, NSA_GROUP_SIZE * Q_TILE), F32),
                        pltpu.VMEM((2, 1, width), F32),
                        pltpu.VMEM((2, LANES, width), BF16),
                        pltpu.VMEM((2, ns, NSA_GROUPS * Q_TILE), F32),
                        pltpu.VMEM((2, NSA_GROUPS, HEAD_DIM, NSA_GROUP_SIZE * Q_TILE), F32),
                        pltpu.VMEM((2, NSA_GROUPS, HEAD_DIM, NSA_GROUP_SIZE * Q_TILE), F32)],
        compiler_params=_params(("arbitrary", "arbitrary")),
        name="nsa_attn",
    )(q, q, gates_t, kc, vct, ks, vst, kw, vwt, ovt, ind)


def _same_head_mask():
    bi = lax.broadcasted_iota(jnp.int32, (HALF, HALF), 0) // HEAD_DIM
    bj = lax.broadcasted_iota(jnp.int32, (HALF, HALF), 1) // HEAD_DIM
    return bi == bj


def _bd_rows(x, same_head):
    xb = x.astype(BF16)
    tiled = jnp.concatenate([xb] * (HALF // x.shape[0]), axis=0)
    return jnp.where(same_head, tiled, jnp.zeros((), BF16))


def _rwkv_prep_stages(in_refs, par_refs, out_set, *, chunks):
    r_ref, k_ref, v_ref, lo_ref = in_refs
    w0_ref, w2_ref, a0_ref, a2_ref, g2_ref, kk_ref, ka_ref, rk_ref = par_refs
    wm_ref, zm_ref, arb_ref, rkv_ref, rt_ref, vb_ref, bkt_ref, gl_ref, bonus_ref, gate_ref = out_set
    c = RWKV_CHUNK
    nb = r_ref.shape[0]
    same_head = _same_head_mask()
    ones_bd = same_head.astype(BF16)
    n_tok = chunks * c
    row_i = lax.broadcasted_iota(jnp.int32, (n_tok, n_tok), 0)
    col_i = lax.broadcasted_iota(jnp.int32, (n_tok, n_tok), 1)
    tril_incl = ((row_i >= col_i) & (row_i // c == col_i // c)).astype(BF16)
    t_id = lax.broadcasted_iota(jnp.int32, (c, HALF), 0)
    j_id = lax.broadcasted_iota(jnp.int32, (c, HALF), 1) % HEAD_DIM
    strict_lower = t_id > j_id
    incl_lower = t_id >= j_id
    eye_all = (t_id == j_id).astype(F32)
    dot = lambda x, y: jnp.dot(x, y, preferred_element_type=F32)

    def bd_cols(xt):
        xb = xt.astype(BF16)
        return jnp.where(same_head, jnp.concatenate([xb, xb], axis=1), jnp.zeros((), BF16))

    halves = RWKV_WIDTH // HALF
    groups = [(b, ch) for b in range(nb) for ch in range(chunks)]
    chains = [(gi, hh) for gi in range(len(groups)) for hh in range(halves)]
    rows = [slice(ch * c, (ch + 1) * c) for ch in range(chunks)]
    lanes = [slice(hh * HALF, (hh + 1) * HALF) for hh in range(halves)]
    each = lambda fn: [fn(n, gi, lanes[hh]) for n, (gi, hh) in enumerate(chains)]

    lo = [lo_ref[b] for b in range(nb)]
    zs = [-(w0_ref[...] + _bdot(jnp.tanh(x[:, 0:DECAY_LORA]), w2_ref[...])) for x in lo]
    yield
    lr_b = [_sigmoid(a0_ref[...] + _bdot(x[:, DECAY_LORA:DECAY_LORA + AAA_LORA], a2_ref[...])) for x in lo]
    yield
    for b in range(nb):
        gate_ref[b] = _bdot(_sigmoid(lo[b][:, DECAY_LORA + AAA_LORA:]), g2_ref[...])
        vb_ref[b] = v_ref[b].astype(BF16)
    yield
    log_decay = []
    for z in zs:
        softplus = jnp.maximum(z, 0.0) + jnp.log(1.0 + jnp.exp(-jnp.abs(z)))
        log_decay.append(-jnp.exp(-softplus - 0.5))
    cum_b = [_dot3_left(tril_incl, x) for x in log_decay]
    yield
    r = [r_ref[b, rows[ch], :] for b, ch in groups]
    k = [k_ref[b, rows[ch], :] for b, ch in groups]
    v = [v_ref[b, rows[ch], :] for b, ch in groups]
    lr = [lr_b[b][rows[ch], :] for b, ch in groups]
    cum = [cum_b[b][rows[ch], :] for b, ch in groups]
    ld = [log_decay[b][rows[ch], :] for b, ch in groups]
    g_incl = [jnp.exp(x) for x in cum]
    g_excl = [jnp.exp(x - y) for x, y in zip(cum, ld)]
    g_inv = [jnp.exp(-x) for x in cum]
    for gi, (b, ch) in enumerate(groups):
        gl_ref[b, ch * SUBLANES:(ch + 1) * SUBLANES, :] = jnp.broadcast_to(
            g_incl[gi][c - 1:c, :], (SUBLANES, RWKV_WIDTH))

    kk = each(lambda n, gi, ln: k[gi][:, ln] * kk_ref[:, ln])
    k2 = each(lambda n, gi, ln: k[gi][:, ln] * (1.0 + (lr[gi][:, ln] - 1.0) * ka_ref[:, ln]))
    sums = each(lambda n, gi, ln: _dot2_right(
        jnp.concatenate([kk[n] * kk[n], r[gi][:, ln] * k2[n] * rk_ref[:, ln]], axis=0), ones_bd))
    ssq = [x[0:c] for x in sums]
    for n, (gi, hh) in enumerate(chains):
        b, ch = groups[gi]
        bonus_ref[b, rows[ch], lanes[hh]] = sums[n][c:] * v[gi][:, lanes[hh]]
    yield
    kk = each(lambda n, gi, ln: kk[n] / jnp.maximum(jnp.sqrt(ssq[n]), 1e-12))
    at = each(lambda n, gi, ln: -kk[n] * g_excl[gi][:, ln])
    bt = each(lambda n, gi, ln: kk[n] * lr[gi][:, ln] * g_inv[gi][:, ln])
    kt = each(lambda n, gi, ln: k2[n] * g_inv[gi][:, ln])
    rt = each(lambda n, gi, ln: r[gi][:, ln] * g_incl[gi][:, ln])
    for n, (gi, hh) in enumerate(chains):
        b, ch = groups[gi]
        rt_ref[b, rows[ch], lanes[hh]] = rt[n].astype(BF16)
        bkt_ref[b, (ch * halves + hh) * HALF:(ch * halves + hh + 1) * HALF, :] = (
            jnp.concatenate([bt[n], kt[n]], axis=0).T.astype(BF16))

    bt_bd = each(lambda n, gi, ln: bd_cols(jnp.concatenate([bt[n], bt[n]], axis=0).T))
    ar = each(lambda n, gi, ln: jnp.concatenate([at[n], rt[n]], axis=0).astype(BF16))
    ab = each(lambda n, gi, ln: dot(ar[n], bt_bd[n]))
    yield
    kt_bd = each(lambda n, gi, ln: bd_cols(jnp.concatenate([kt[n], kt[n]], axis=0).T))
    ak = each(lambda n, gi, ln: dot(ar[n], kt_bd[n]))
    yield
    a_ab = [jnp.where(strict_lower, x[0:c], 0.0) for x in ab]
    a_rb = [jnp.where(incl_lower, x[c:], 0.0) for x in ab]
    a_ak = [jnp.where(strict_lower, x[0:c], 0.0) for x in ak]
    a_rk = [jnp.where(incl_lower, x[c:], 0.0) for x in ak]
    for n, (gi, hh) in enumerate(chains):
        b, ch = groups[gi]
        arb_ref[b, rows[ch], lanes[hh]] = a_rb[n].astype(BF16)

    base = 4
    same_block = lambda s: (t_id // s) == (j_id // s)
    nd = [jnp.where(same_block(base), x, 0.0) for x in a_ab]
    nd2 = [dot(x.astype(BF16), _bd_rows(x, same_head)) for x in nd]
    yield
    inv = [eye_all + x for x in nd]
    inv = [x + dot(x.astype(BF16), _bd_rows(y, same_head)) for x, y in zip(inv, nd2)]
    yield
    s_blk = base
    while s_blk < c:
        lower_left = same_block(2 * s_blk) & ((t_id % (2 * s_blk)) >= s_blk) & ((j_id % (2 * s_blk)) < s_blk)
        te = [dot(x.astype(BF16), _bd_rows(jnp.where(lower_left, y, 0.0), same_head))
              for x, y in zip(inv, a_ab)]
        yield
        inv = [x + dot(y.astype(BF16), _bd_rows(x, same_head)) for x, y in zip(inv, te)]
        yield
        s_blk *= 2

    v_bd = each(lambda n, gi, ln: _bd_rows(v[gi][:, ln], same_head))
    inv_b = [x.astype(BF16) for x in inv]
    wm = each(lambda n, gi, ln: dot(inv_b[n], _bd_rows(at[n], same_head)))
    yield
    av = each(lambda n, gi, ln: dot(jnp.concatenate([a_ak[n], a_rk[n]], axis=0).astype(BF16), v_bd[n]))
    yield
    zm = each(lambda n, gi, ln: dot(inv_b[n], _bd_rows(av[n][0:c], same_head)))
    for n, (gi, hh) in enumerate(chains):
        b, ch = groups[gi]
        wm_ref[b, rows[ch], lanes[hh]] = wm[n].astype(BF16)
        zm_ref[b, rows[ch], lanes[hh]] = zm[n]
        rkv_ref[b, rows[ch], lanes[hh]] = av[n][c:]


def _rwkv_scan_stages(in_set, gng_ref, gnb_ref, o_ref, s_ref, *, chunks):
    wm_ref, zm_ref, arb_ref, rkv_ref, rt_ref, vb_ref, bkt_ref, gl_ref, bonus_ref, gate_ref = in_set
    c = RWKV_CHUNK
    halves = RWKV_WIDTH // HALF
    same_head = _same_head_mask()
    ones_bd = same_head.astype(BF16)
    chains = [(b, hh) for b in range(o_ref.shape[0]) for hh in range(halves)]
    lanes = [slice(hh * HALF, (hh + 1) * HALF) for hh in range(halves)]
    dot = lambda x, y: jnp.dot(x, y, preferred_element_type=F32)

    for ch in range(chunks):
        rw = slice(ch * c, (ch + 1) * c)
        each = lambda fn: [fn(n, b, lanes[hh]) for n, (b, hh) in enumerate(chains)]
        s0 = [s_ref[b, hh] for b, hh in chains]
        s0b = [x.astype(BF16) for x in s0]
        u = each(lambda n, b, ln: dot(wm_ref[b, rw, ln], s0b[n]) + zm_ref[b, rw, ln])
        yield
        y0 = each(lambda n, b, ln: dot(rt_ref[b, rw, ln], s0b[n]) + rkv_ref[b, rw, ln])
        yield
        y = each(lambda n, b, ln: y0[n] + dot(arb_ref[b, rw, ln], _bd_rows(u[n], same_head)))
        yield
        uv = each(lambda n, b, ln: jnp.concatenate([u[n].astype(BF16), vb_ref[b, rw, ln]], axis=0))
        upd = [dot(bkt_ref[b, (ch * halves + hh) * HALF:(ch * halves + hh + 1) * HALF, :], uv[n])
               for n, (b, hh) in enumerate(chains)]
        for n, (b, hh) in enumerate(chains):
            g_last = jnp.broadcast_to(gl_ref[b, ch * SUBLANES:ch * SUBLANES + 1, lanes[hh]], (LANES, HALF)).T
            g_col = jnp.concatenate([g_last, g_last], axis=1)
            s_ref[b, hh] = g_col * (s0[n] + jnp.where(same_head, upd[n], 0.0))
        yield
        mu = [_dot2_right(x, ones_bd) * (1.0 / HEAD_DIM) for x in y]
        yield
        yc = [x - m for x, m in zip(y, mu)]
        var = [_dot2_right(x * x, ones_bd) * (1.0 / HEAD_DIM) for x in yc]
        yield
        for n, (b, hh) in enumerate(chains):
            ln = lanes[hh]
            yn = yc[n] * lax.rsqrt(var[n] + RWKV_GN_EPS) * gng_ref[:, ln] + gnb_ref[:, ln]
            o_ref[b, rw, ln] = (yn + bonus_ref[b, rw, ln]) * gate_ref[b, rw, ln]


def _rwkv_kernel(*refs, chunks):
    in_refs, par_refs = refs[0:4], refs[4:12]
    gng_ref, gnb_ref, o_ref, s_ref = refs[12:16]
    sets = (refs[16:26], refs[26:36])
    step = pl.program_id(0)

    @pl.when(step == 0)
    def _():
        s_ref[...] = jnp.zeros_like(s_ref)
        for ref in sets[1]:
            ref[...] = jnp.zeros_like(ref)

    def run(write_set, read_set):
        prep = _rwkv_prep_stages(in_refs, par_refs, write_set, chunks=chunks)
        scan = _rwkv_scan_stages(read_set, gng_ref, gnb_ref, o_ref, s_ref, chunks=chunks)
        live = [prep, scan]
        while live:
            for gen in list(live):
                if next(gen, "done") == "done":
                    live.remove(gen)

    @pl.when(step % 2 == 0)
    def _():
        run(sets[0], sets[1])

    @pl.when(step % 2 == 1)
    def _():
        run(sets[1], sets[0])


def _rwkv(r, k, v, lo, w0, w2, a0, a2, g2, k_k, k_a, r_k, gn_g, gn_b, *, chunks=2):
    b, t, width = r.shape
    c = RWKV_CHUNK
    n_tok = chunks * c
    n_groups = t // n_tok
    halves = width // HALF
    vec = lambda a: a.reshape(1, width)
    cs = lambda a: _const_spec(a.shape)
    args = [vec(w0), w2.astype(BF16), vec(a0), a2.astype(BF16), g2.astype(BF16),
            vec(k_k), vec(k_a), vec(r_k), vec(gn_g), vec(gn_b)]
    ahead = lambda s: (0, jnp.minimum(s, n_groups - 1), 0)
    behind = lambda s: (0, jnp.maximum(s - 1, 0), 0)
    tok = lambda dt: pltpu.VMEM((b, n_tok, width), dt)
    scratch_set = [tok(BF16), tok(F32), tok(BF16), tok(F32), tok(BF16), tok(BF16),
                   pltpu.VMEM((b, chunks * halves * HALF, LANES), BF16),
                   pltpu.VMEM((b, chunks * SUBLANES, width), F32), tok(F32), tok(F32)]
    return pl.pallas_call(
        functools.partial(_rwkv_kernel, chunks=chunks),
        grid=(n_groups + 1,),
        in_specs=[pl.BlockSpec((b, n_tok, width), ahead)] * 3 + [pl.BlockSpec((b, n_tok, LORA_COLS), ahead)]
                 + [cs(a) for a in args],
        out_specs=pl.BlockSpec((b, n_tok, width), behind),
        out_shape=jax.ShapeDtypeStruct((b, t, width), F32),
        scratch_shapes=[pltpu.VMEM((b, halves, HALF, HALF), F32)] + scratch_set + scratch_set,
        compiler_params=_params(("arbitrary",)),
        name="rwkv7",
    )(r, k, v, lo, *args)


def _mem_kv_kernel(m_ref, g_ref, wk_ref, wv_ref, kt_ref, v_ref):
    m = _rms(m_ref[0], g_ref[...]).astype(BF16)
    kt_ref[0] = jnp.dot(m, wk_ref[...], preferred_element_type=F32).T.astype(BF16)
    v_ref[0] = jnp.dot(m, wv_ref[...], preferred_element_type=F32).astype(BF16)


def _mem_kv(mem, g, wk, wv):
    b, mt, d = mem.shape
    return pl.pallas_call(
        _mem_kv_kernel,
        grid=(b,),
        in_specs=[pl.BlockSpec((1, mt, d), lambda bi: (bi, 0, 0)), _const_spec((1, d)),
                  _const_spec((d, d)), _const_spec((d, d))],
        out_specs=[pl.BlockSpec((1, d, mt), lambda bi: (bi, 0, 0)),
                   pl.BlockSpec((1, mt, d), lambda bi: (bi, 0, 0))],
        out_shape=[jax.ShapeDtypeStruct((b, d, mt), BF16), jax.ShapeDtypeStruct((b, mt, d), BF16)],
        compiler_params=_params(("arbitrary",)),
        name="mem_kv",
    )(mem, g.reshape(1, d), wk.astype(BF16), wv.astype(BF16))


def _out_mem_kernel(x_ref, on_ref, or_ref, ng_ref, wo1_ref, wo2_ref, mpost_ref,
                    mpre_ref, wq_ref, kt_ref, v_ref, wo_ref, mempost_ref, o_ref, *, parts):
    tm = x_ref.shape[1] // parts
    rows = [slice(n * tm, (n + 1) * tm) for n in range(parts)]
    dot = lambda a, b: jnp.dot(a, b, preferred_element_type=F32)
    d = x_ref.shape[-1]
    hd = d // MEM_HEADS

    a = [_rms(on_ref[0, rw, :], ng_ref[...]).astype(BF16) for rw in rows]
    mixed = [dot(a[n], wo1_ref[...]) + dot(or_ref[0, rw, :].astype(BF16), wo2_ref[...])
             for n, rw in enumerate(rows)]
    x = [x_ref[0, rw, :] + _rms(mixed[n], mpost_ref[...]) for n, rw in enumerate(rows)]
    h = [_rms(xn, mpre_ref[...]).astype(BF16) for xn in x]
    q = [(dot(hn, wq_ref[...]) * (hd ** -0.5)).astype(BF16) for hn in h]
    heads = [[] for _ in rows]
    for hi in range(MEM_HEADS):
        cols = slice(hi * hd, (hi + 1) * hd)
        s = [dot(qn[:, cols], kt_ref[0, cols, :]) for qn in q]
        e = [jnp.exp(sn - jnp.max(sn, axis=-1, keepdims=True)) for sn in s]
        p = [(en / jnp.sum(en, axis=-1, keepdims=True)).astype(BF16) for en in e]
        for n in range(parts):
            heads[n].append(dot(p[n], v_ref[0, :, cols]))
    att = [dot(jnp.concatenate(hn, axis=-1).astype(BF16), wo_ref[...]) for hn in heads]
    for n, rw in enumerate(rows):
        o_ref[0, rw, :] = x[n] + _rms(att[n], mempost_ref[...])


def _out_mem(x3, o_nsa, o_rwkv, nsa_g, w_out, mix_post_g, mem_pre_g, wq, kt, vm, wo, mem_post_g,
             *, tm=1024, parts=4):
    b, t, d = x3.shape
    mt = vm.shape[1]
    row = lambda bi, ti: (bi, ti, 0)
    per_b = lambda bi, ti: (bi, 0, 0)
    w_out = w_out.astype(BF16)
    return pl.pallas_call(
        functools.partial(_out_mem_kernel, parts=parts),
        grid=(b, t // tm),
        in_specs=[
            pl.BlockSpec((1, tm, d), row),
            pl.BlockSpec((1, tm, NSA_WIDTH), row),
            pl.BlockSpec((1, tm, RWKV_WIDTH), row),
            _const_spec((1, NSA_WIDTH)),
            _const_spec((NSA_WIDTH, d)),
            _const_spec((RWKV_WIDTH, d)),
            _const_spec((1, d)),
            _const_spec((1, d)),
            _const_spec((d, d)),
            pl.BlockSpec((1, d, mt), per_b),
            pl.BlockSpec((1, mt, d), per_b),
            _const_spec((d, d)),
            _const_spec((1, d)),
        ],
        out_specs=pl.BlockSpec((1, tm, d), row),
        out_shape=jax.ShapeDtypeStruct((b, t, d), F32),
        compiler_params=_params(("arbitrary", "arbitrary")),
        name="out_mem",
    )(x3, o_nsa, o_rwkv, nsa_g.reshape(1, -1), w_out[:NSA_WIDTH], w_out[NSA_WIDTH:],
      mix_post_g.reshape(1, d), mem_pre_g.reshape(1, d), wq.astype(BF16), kt, vm,
      wo.astype(BF16), mem_post_g.reshape(1, d))


def _rope_tables(pos):
    half = HEAD_DIM // 2
    inv = ROPE_THETA ** (-jnp.arange(half, dtype=F32) / half)
    ang = pos.astype(F32)[:, None] * inv[None, :]
    cos, sin = jnp.cos(ang), jnp.sin(ang)
    cos_t = jnp.concatenate([cos, cos, cos, cos], axis=-1)
    sin_t = jnp.concatenate([-sin, sin, -sin, sin], axis=-1)
    return cos_t, sin_t


def _overlap_t(ns, ncp):
    c0 = np.arange(ncp)[None, :] * CMP_STRIDE
    s0 = np.arange(ns)[:, None] * SEL_BLOCK
    ov = (c0 < s0 + SEL_BLOCK) & (c0 + CMP_BLOCK > s0) & (np.arange(ncp)[None, :] < ncp - 1)
    return jnp.asarray(ov.astype(np.float32), dtype=BF16)


def _pad_cols(w, n):
    return jnp.pad(w, ((0, 0), (0, n - w.shape[1])))


def _cmp_weights(pe, w1, w2):
    per = CMP_STRIDE
    w1r = w1.reshape(CMP_BLOCK, HEAD_DIM, CMP_HIDDEN)
    blocks = []
    for part in range(CMP_BLOCK // per):
        for g in range(NSA_GROUPS):
            z = jnp.zeros((per, NSA_GROUPS, HEAD_DIM, CMP_HIDDEN), F32)
            z = z.at[:, g].set(w1r[part * per:(part + 1) * per])
            blocks.append(z.reshape(per * NSA_GROUPS * HEAD_DIM, CMP_HIDDEN))
    w1cat = jnp.concatenate(blocks, axis=1).astype(BF16)
    w2bd = jnp.zeros((NSA_GROUPS * CMP_HIDDEN, NSA_GROUPS * HEAD_DIM), F32)
    for g in range(NSA_GROUPS):
        w2bd = w2bd.at[g * CMP_HIDDEN:(g + 1) * CMP_HIDDEN, g * HEAD_DIM:(g + 1) * HEAD_DIM].set(w2)
    pe8 = jnp.broadcast_to(pe.reshape(1, CMP_BLOCK * HEAD_DIM), (SUBLANES, CMP_BLOCK * HEAD_DIM))
    return pe8.astype(BF16), w1.astype(BF16), w1cat, w2bd.astype(BF16)


def kernel(x, mem, ffn1_pre_g, ffn1_w_gate, ffn1_w_up, ffn1_w_down, ffn1_post_g, mix_pre_g, w_in, cmp_pe_k, cmp_w1_k, cmp_w2_k, cmp_pe_v, cmp_w1_v, cmp_w2_v, nsa_out_g, rwkv_mu, rwkv_w0, rwkv_w2, rwkv_a0, rwkv_a2, rwkv_g2, rwkv_k_k, rwkv_k_a, rwkv_r_k, rwkv_gn_g, rwkv_gn_b, w_out, mix_post_g, mem_pre_g, mem_kv_g, mem_wq, mem_wk, mem_wv, mem_wo, mem_post_g, ffn2_pre_g, ffn2_w_gate, ffn2_w_up, ffn2_w_down, ffn2_post_g):
    b, t, d = x.shape
    ncp = t // CMP_STRIDE
    ns = t // SEL_BLOCK
    cos_t, sin_t = _rope_tables(jnp.arange(t))
    cos_c, sin_c = _rope_tables(jnp.arange(ncp) * CMP_STRIDE + (CMP_BLOCK - 1))
    ovt = _overlap_t(ns, ncp)

    for l in range(ffn1_pre_g.shape[0]):
        x2 = _ffn_block(x.reshape(b * t, d), ffn1_pre_g[l], ffn1_w_gate[l], ffn1_w_up[l],
                        ffn1_w_down[l], ffn1_post_g[l])
        x3 = x2.reshape(b, t, d)

        wi = w_in[l]
        nsa_w = NSA_WIDTH + 6 * NSA_KV_WIDTH
        gate_w = wi[:, nsa_w:nsa_w + 3 * NSA_HEADS]
        w_cols = jnp.concatenate([wi[:, :nsa_w], _pad_cols(gate_w, LANES),
                                  wi[:, nsa_w + 3 * NSA_HEADS:]], axis=1).astype(BF16)
        (q, k_cmp, v_cmp, k_slc, v_slc_t, k_win, v_win_t, gates_t, r, k, v, lo) = _in_proj(
            x3, mix_pre_g[l], w_cols, rwkv_mu[l].reshape(1, -1), cos_t, sin_t)

        pek, w1k, w1kc, w2k = _cmp_weights(cmp_pe_k[l], cmp_w1_k[l], cmp_w2_k[l])
        pev, w1v, w1vc, w2v = _cmp_weights(cmp_pe_v[l], cmp_w1_v[l], cmp_w2_v[l])
        kc, vct = _compress(k_cmp, v_cmp,
                            pek, w1k, w1kc, w2k, pev, w1v, w1vc, w2v, cos_c, sin_c)
        o_nsa = _nsa_attn(q, gates_t, kc, vct, k_slc, v_slc_t, k_win, v_win_t, ovt)

        o_rwkv = _rwkv(r, k, v, lo, rwkv_w0[l], rwkv_w2[l], rwkv_a0[l], rwkv_a2[l], rwkv_g2[l],
                       rwkv_k_k[l], rwkv_k_a[l], rwkv_r_k[l], rwkv_gn_g[l], rwkv_gn_b[l])

        kt, vm = _mem_kv(mem, mem_kv_g[l], mem_wk[l], mem_wv[l])
        x4 = _out_mem(x3, o_nsa, o_rwkv, nsa_out_g[l], w_out[l], mix_post_g[l], mem_pre_g[l],
                      mem_wq[l], kt, vm, mem_wo[l], mem_post_g[l])

        x = _ffn_block(x4.reshape(b * t, d), ffn2_pre_g[l], ffn2_w_gate[l], ffn2_w_up[l],
                       ffn2_w_down[l], ffn2_post_g[l]).reshape(b, t, d)
    return x
```

```python
import functools

import numpy as np
import jax
import jax.numpy as jnp
from jax import lax
from jax.experimental import pallas as pl
from jax.experimental.pallas import tpu as pltpu

F32 = jnp.float32
BF16 = jnp.bfloat16

HEAD_DIM = 64
NSA_HEADS = 8
NSA_GROUPS = 2
NSA_GROUP_SIZE = 4
NSA_WIDTH = 512
NSA_KV_WIDTH = 128
CMP_BLOCK = 32
CMP_STRIDE = 16
CMP_HIDDEN = 256
SEL_BLOCK = 64
SEL_TOPK = 16
SEL_FORCE = 1e4
SEL_FORCED = 3
WINDOW = 512
RWKV_WIDTH = 512
DECAY_LORA = 64
AAA_LORA = 64
GATE_LORA = 128
RWKV_GN_EPS = 64e-5
MEM_HEADS = 4
ROPE_THETA = 10000.0
NORM_EPS = 1e-6
NEG_INF = -1e30
LOG2_E = 1.4426950408889634

LORA_COLS = DECAY_LORA + AAA_LORA + GATE_LORA
RWKV_COLS = 3 * RWKV_WIDTH + LORA_COLS

LANES = 128
SUBLANES = 8
BF16_ROWS = 16
GATE_ROWS = 32
KEY_BLOCK = 128
Q_TILE = 128
SEL_TRIP = 4
LOOP_TRIPS = 4
HEAD_TRIPS = 8
RWKV_CHUNK = 64
HALF = 256
VMEM_LIMIT = 56 * 1024 * 1024


def _bdot(a, b):
    return jnp.dot(a.astype(BF16), b.astype(BF16), preferred_element_type=F32)


def _split3(x):
    h1 = x.astype(BF16)
    r1 = x - h1.astype(F32)
    h2 = r1.astype(BF16)
    r2 = r1 - h2.astype(F32)
    return h1, h2, r2.astype(BF16)


def _split2(x):
    h1 = x.astype(BF16)
    return h1, (x - h1.astype(F32)).astype(BF16)


def _dot2_right(x, m):
    h1, h2 = _split2(x)
    return (jnp.dot(h1, m, preferred_element_type=F32) + jnp.dot(h2, m, preferred_element_type=F32))


def _dot3_right(x, m):
    h1, h2, h3 = _split3(x)
    d = lambda h: jnp.dot(h, m, preferred_element_type=F32)
    return d(h1) + d(h2) + d(h3)


def _dot2_left(m, x):
    h1, h2 = _split2(x)
    return (jnp.dot(m, h1, preferred_element_type=F32) + jnp.dot(m, h2, preferred_element_type=F32))


def _dot3_left(m, x):
    h1, h2, h3 = _split3(x)
    d = lambda h: jnp.dot(m, h, preferred_element_type=F32)
    return d(h1) + d(h2) + d(h3)


def _rms(x, g):
    return x * lax.rsqrt(jnp.mean(x * x, axis=-1, keepdims=True) + NORM_EPS) * g


def _silu(x):
    return x / (1.0 + jnp.exp(-x))


def _sigmoid(x):
    return 1.0 / (1.0 + jnp.exp(-x))


def _const_spec(shape):
    nd = len(shape)
    return pl.BlockSpec(shape, lambda *_: (0,) * nd)


def _params(sem):
    return pltpu.CompilerParams(dimension_semantics=sem, vmem_limit_bytes=VMEM_LIMIT)


def _ffn_kernel(x_ref, pre_ref, wg_ref, wu_ref, wd_ref, post_ref, o_ref, *, ff_chunk):
    x = x_ref[...]
    h = _rms(x, pre_ref[...]).astype(BF16)
    d_ff = wg_ref.shape[1]
    acc = jnp.zeros(x.shape, F32)
    for c0 in range(0, d_ff, ff_chunk):
        g = jnp.dot(h, wg_ref[:, c0:c0 + ff_chunk], preferred_element_type=F32)
        u = jnp.dot(h, wu_ref[:, c0:c0 + ff_chunk], preferred_element_type=F32)
        a = (_silu(g) * u).astype(BF16)
        acc = acc + jnp.dot(a, wd_ref[c0:c0 + ff_chunk, :], preferred_element_type=F32)
    o_ref[...] = x + 0.5 * _rms(acc, post_ref[...])


def _ffn_block(x2, pre_g, wg, wu, wd, post_g, *, tm=512, ff_chunk=256):
    m, d = x2.shape
    d_ff = wg.shape[1]
    return pl.pallas_call(
        functools.partial(_ffn_kernel, ff_chunk=ff_chunk),
        grid=(m // tm,),
        in_specs=[
            pl.BlockSpec((tm, d), lambda i: (i, 0)),
            _const_spec((1, d)),
            _const_spec((d, d_ff)),
            _const_spec((d, d_ff)),
            _const_spec((d_ff, d)),
            _const_spec((1, d)),
        ],
        out_specs=pl.BlockSpec((tm, d), lambda i: (i, 0)),
        out_shape=jax.ShapeDtypeStruct((m, d), F32),
        compiler_params=_params(("arbitrary",)),
        name="ffn_block",
    )(x2, pre_g.reshape(1, d), wg.astype(BF16), wu.astype(BF16), wd.astype(BF16),
      post_g.reshape(1, d))


def _swap_halves(x):
    n = x.shape[-1]
    lane = lax.broadcasted_iota(jnp.int32, x.shape, x.ndim - 1)
    fwd = pltpu.roll(x, n - HEAD_DIM // 2, x.ndim - 1)
    bwd = pltpu.roll(x, HEAD_DIM // 2, x.ndim - 1)
    return jnp.where((lane % HEAD_DIM) < HEAD_DIM // 2, fwd, bwd)


def _rope(x, cos, sin_signed):
    reps = x.shape[-1] // LANES
    c = jnp.concatenate([cos] * reps, axis=-1) if reps > 1 else cos
    s = jnp.concatenate([sin_signed] * reps, axis=-1) if reps > 1 else sin_signed
    return x * c + _swap_halves(x) * s


def _in_proj_kernel(x_ref, g_ref, w_ref, mu_ref, cos_ref, sin_ref,
                    q_ref, kc_ref, vc_ref, ks_ref, vst_ref, kw_ref, vwt_ref, gt_ref,
                    r_ref, k_ref, v_ref, lo_ref, carry_ref):
    @pl.when(pl.program_id(1) == 0)
    def _():
        carry_ref[...] = jnp.zeros_like(carry_ref)

    h = _rms(x_ref[0], g_ref[...]).astype(BF16)
    p = jnp.dot(h, w_ref[...], preferred_element_type=F32)
    cos = cos_ref[...]
    sin = sin_ref[...]
    tm = p.shape[0]

    o = 0
    q = _rope(p[:, o:o + NSA_WIDTH], cos, sin) * (HEAD_DIM ** -0.5 * LOG2_E)
    q_ref[0] = q.astype(BF16)
    o += NSA_WIDTH
    kc_ref[0] = p[:, o:o + LANES]; o += LANES
    vc_ref[0] = p[:, o:o + LANES]; o += LANES
    ks_ref[0] = _rope(p[:, o:o + LANES], cos, sin).astype(BF16); o += LANES
    vst_ref[0] = p[:, o:o + LANES].T.astype(BF16); o += LANES
    kw_ref[0] = _rope(p[:, o:o + LANES], cos, sin).astype(BF16); o += LANES
    vwt_ref[0] = p[:, o:o + LANES].T.astype(BF16); o += LANES
    gates_t = _sigmoid(p[:, o:o + LANES]).T
    gt_ref[0] = gates_t[:gt_ref.shape[1], :]
    o += LANES

    rw = p[:, o:]
    row = lax.broadcasted_iota(jnp.int32, rw.shape, 0)
    prev = jnp.where(row == 0, carry_ref[0:1, :], pltpu.roll(rw, 1, 0))
    carry_ref[...] = jnp.broadcast_to(rw[tm - 1:tm, :], carry_ref.shape)
    mixed = rw + (prev - rw) * mu_ref[...]
    r_ref[0] = mixed[:, 0:RWKV_WIDTH]
    k_ref[0] = mixed[:, RWKV_WIDTH:2 * RWKV_WIDTH]
    v_ref[0] = mixed[:, 2 * RWKV_WIDTH:3 * RWKV_WIDTH]
    lo_ref[0] = mixed[:, 3 * RWKV_WIDTH:RWKV_COLS]


def _in_proj(x3, g, w_cols, mu_cols, cos_t, sin_t, *, tm=512):
    b, t, d = x3.shape
    n = w_cols.shape[1]
    row = lambda bi, ti: (bi, ti, 0)
    col = lambda bi, ti: (bi, 0, ti)
    out_shapes = [
        jax.ShapeDtypeStruct((b, t, NSA_WIDTH), BF16),
        jax.ShapeDtypeStruct((b, t, LANES), F32),
        jax.ShapeDtypeStruct((b, t, LANES), F32),
        jax.ShapeDtypeStruct((b, t, LANES), BF16),
        jax.ShapeDtypeStruct((b, LANES, t), BF16),
        jax.ShapeDtypeStruct((b, t, LANES), BF16),
        jax.ShapeDtypeStruct((b, LANES, t), BF16),
        jax.ShapeDtypeStruct((b, GATE_ROWS, t), F32),
        jax.ShapeDtypeStruct((b, t, RWKV_WIDTH), F32),
        jax.ShapeDtypeStruct((b, t, RWKV_WIDTH), F32),
        jax.ShapeDtypeStruct((b, t, RWKV_WIDTH), F32),
        jax.ShapeDtypeStruct((b, t, LORA_COLS), F32),
    ]
    out_specs = [
        pl.BlockSpec((1, tm, NSA_WIDTH), row),
        pl.BlockSpec((1, tm, LANES), row),
        pl.BlockSpec((1, tm, LANES), row),
        pl.BlockSpec((1, tm, LANES), row),
        pl.BlockSpec((1, LANES, tm), col),
        pl.BlockSpec((1, tm, LANES), row),
        pl.BlockSpec((1, LANES, tm), col),
        pl.BlockSpec((1, GATE_ROWS, tm), col),
        pl.BlockSpec((1, tm, RWKV_WIDTH), row),
        pl.BlockSpec((1, tm, RWKV_WIDTH), row),
        pl.BlockSpec((1, tm, RWKV_WIDTH), row),
        pl.BlockSpec((1, tm, LORA_COLS), row),
    ]
    return pl.pallas_call(
        _in_proj_kernel,
        grid=(b, t // tm),
        in_specs=[
            pl.BlockSpec((1, tm, d), row),
            _const_spec((1, d)),
            _const_spec((d, n)),
            _const_spec((1, RWKV_COLS)),
            pl.BlockSpec((tm, LANES), lambda bi, ti: (ti, 0)),
            pl.BlockSpec((tm, LANES), lambda bi, ti: (ti, 0)),
        ],
        out_specs=out_specs,
        out_shape=out_shapes,
        scratch_shapes=[pltpu.VMEM((SUBLANES, RWKV_COLS), F32)],
        compiler_params=_params(("arbitrary", "arbitrary")),
        name="in_proj",
    )(x3, g.reshape(1, d), w_cols, mu_cols, cos_t, sin_t)


def _compress_kernel(kin_ref, vin_ref, pek_ref, w1k_ref, w1kc_ref, w2k_ref,
                     pev_ref, w1v_ref, w1vc_ref, w2v_ref, cos_ref, sin_ref,
                     kc_ref, vct_ref):
    def phi(tok_ref, pe, w1, w1cat_ref, w2bd):
        n = tok_ref.shape[1] // CMP_STRIDE
        pr = None
        for l in range(CMP_STRIDE):
            rows = tok_ref[0, pl.ds(l, n, stride=CMP_STRIDE), :].astype(BF16)
            term = jnp.dot(rows, w1cat_ref[l * LANES:(l + 1) * LANES, :], preferred_element_type=F32)
            pr = term if pr is None else pr + term
        bias = jnp.dot(pe, w1, preferred_element_type=F32)[0:1, :]
        hid = []
        for g in range(NSA_GROUPS):
            top = pr[:, g * CMP_HIDDEN:(g + 1) * CMP_HIDDEN]
            bot = pr[:, (NSA_GROUPS + g) * CMP_HIDDEN:(NSA_GROUPS + g + 1) * CMP_HIDDEN]
            hid.append(top + pltpu.roll(bot, n - 1, 0) + bias)
        act = _silu(jnp.concatenate(hid, axis=-1)).astype(BF16)
        return jnp.dot(act, w2bd, preferred_element_type=F32)

    kc = phi(kin_ref, pek_ref[...], w1k_ref[...], w1kc_ref, w2k_ref[...])
    kc_ref[0] = _rope(kc, cos_ref[...], sin_ref[...]).astype(BF16)
    vc = phi(vin_ref, pev_ref[...], w1v_ref[...], w1vc_ref, w2v_ref[...])
    vct_ref[0] = vc.T.astype(BF16)


def _compress(kin, vin, pek, w1k, w1kc, w2k, pev, w1v, w1vc, w2v, cos_c, sin_c):
    b, t, width = kin.shape
    ncp = t // CMP_STRIDE
    blk = pl.BlockSpec((1, t, width), lambda bi: (bi, 0, 0))
    cs = lambda a: _const_spec(a.shape)
    return pl.pallas_call(
        _compress_kernel,
        grid=(b,),
        in_specs=[blk, blk, cs(pek), cs(w1k), cs(w1kc), cs(w2k),
                  cs(pev), cs(w1v), cs(w1vc), cs(w2v), cs(cos_c), cs(sin_c)],
        out_specs=[pl.BlockSpec((1, ncp, LANES), lambda bi: (bi, 0, 0)),
                   pl.BlockSpec((1, LANES, ncp), lambda bi: (bi, 0, 0))],
        out_shape=[jax.ShapeDtypeStruct((b, ncp, LANES), BF16),
                   jax.ShapeDtypeStruct((b, LANES, ncp), BF16)],
        compiler_params=_params(("arbitrary",)),
        name="nsa_compress",
    )(kin, vin, pek, w1k, w1kc, w2k, pev, w1v, w1vc, w2v, cos_c, sin_c)


def _interleave(*programs):
    live = list(programs)
    while live:
        for gen in list(live):
            if next(gen, "done") == "done":
                live.remove(gen)


def _nsa_kernel(q_ref, qn_ref, gt_ref, kc_ref, vct_ref, ks_ref, vst_ref, kw_ref, vwt_ref, ovt_ref, ind_ref,
                o_ref, sa_ref, sb_ref, acc_ref, mstate_ref, qt2_ref, bias2_ref, oc2_ref, ow2_ref):
    i = pl.program_id(1)
    n_tiles = pl.num_programs(1)
    ncp = kc_ref.shape[1]
    ns = ovt_ref.shape[0]
    gw = NSA_GROUP_SIZE * Q_TILE
    width = NSA_GROUPS * gw
    sw_ = NSA_GROUPS * Q_TILE
    lanes = [slice(g * gw, (g + 1) * gw) for g in range(NSA_GROUPS)]
    feat = [slice(g * HEAD_DIM, (g + 1) * HEAD_DIM) for g in range(NSA_GROUPS)]
    lane_q = lax.broadcasted_iota(jnp.int32, (1, width), 1) % Q_TILE
    sel_keys = SEL_TRIP * KEY_BLOCK
    sel_rows = sel_keys // SEL_BLOCK
    win_keys = WINDOW + Q_TILE
    dot = lambda x, y: jnp.dot(x, y, preferred_element_type=F32)
    bufs = (sa_ref, sb_ref)
    q0 = i * Q_TILE
    t_row = q0 + lane_q
    n_full = q0 // sel_keys
    deep = n_full >= HEAD_TRIPS

    def v_ext(vt_ref, g, k0, n):
        return jnp.concatenate([vt_ref[0, feat[g], pl.ds(k0, n)], jnp.ones((BF16_ROWS, n), BF16)], axis=0)

    def prepare(tile, src_ref, slot, alone):
        p0 = tile * Q_TILE
        tp_row = p0 + lane_q
        qf = src_ref[0].astype(F32)
        zeros_half = jnp.zeros((HEAD_DIM, Q_TILE), F32)
        parts = []
        for g in range(NSA_GROUPS):
            for pair in range(NSA_GROUP_SIZE // 2):
                slab_t = qf[:, (2 * g + pair) * LANES:(2 * g + pair + 1) * LANES].T
                for half in range(2):
                    f = slab_t[half * HEAD_DIM:(half + 1) * HEAD_DIM, :]
                    parts.append(jnp.concatenate([f, zeros_half] if g == 0 else [zeros_half, f], axis=0))
        qt = jnp.concatenate(parts, axis=1).astype(BF16)
        qt2_ref[slot] = qt
        yield

        def compressed(rows):
            sc = dot(kc_ref[0, 0:rows, :], qt)
            c_end = lax.broadcasted_iota(jnp.int32, (rows, width), 0) * CMP_STRIDE + (CMP_BLOCK - 1)
            c_mask = c_end <= tp_row
            sc = jnp.where(c_mask, sc, NEG_INF)
            m_c = jnp.max(sc, axis=0, keepdims=True)
            e_c = jnp.where(c_mask, jnp.exp2(sc - m_c), 0.0)
            p_c = e_c / jnp.maximum(jnp.sum(e_c, axis=0, keepdims=True), 1e-30)
            outs, p_sum = [], []
            for g in range(NSA_GROUPS):
                outs.append(dot(vct_ref[0, feat[g], 0:rows], p_c[:, lanes[g]].astype(BF16)))
                acc = p_c[:, g * gw:g * gw + Q_TILE]
                for r in range(1, NSA_GROUP_SIZE):
                    acc = acc + p_c[:, g * gw + r * Q_TILE:g * gw + (r + 1) * Q_TILE]
                p_sum.append(acc)
            return outs + [_dot2_left(ovt_ref[:, 0:rows], jnp.concatenate(p_sum, axis=1))]

        if alone:
            n_vis = (p0 + Q_TILE - CMP_BLOCK) // CMP_STRIDE + 1
            blocks_needed = jnp.maximum(n_vis + KEY_BLOCK - 1, KEY_BLOCK) // KEY_BLOCK
            *o_c, imp = lax.switch(blocks_needed - 1, [functools.partial(compressed, nb * KEY_BLOCK)
                                                       for nb in range(1, ncp // KEY_BLOCK + 1)])
        else:
            *o_c, imp = compressed(ncp)
        for g in range(NSA_GROUPS):
            oc2_ref[slot, g] = o_c[g]
        yield

        w0 = pl.multiple_of(jnp.maximum(p0 - WINDOW, 0), KEY_BLOCK)
        sw = dot(kw_ref[0, pl.ds(w0, win_keys), :], qt)
        yield
        t_loc = tp_row - w0
        k_loc = lax.broadcasted_iota(jnp.int32, (win_keys, width), 0)
        old = lax.broadcasted_iota(jnp.int32, (KEY_BLOCK, width), 0) <= t_loc - WINDOW
        sw = jnp.where(k_loc <= t_loc, sw, NEG_INF)
        sw = jnp.concatenate([jnp.where(old, NEG_INF, sw[0:KEY_BLOCK]), sw[KEY_BLOCK:]], axis=0)
        m_w = jnp.max(sw, axis=0, keepdims=True)
        yield
        for g in range(NSA_GROUPS):
            p = jnp.exp2(sw[:, lanes[g]] - m_w[:, lanes[g]]).astype(BF16)
            pv = dot(v_ext(vwt_ref, g, w0, win_keys), p)
            ow2_ref[slot, g] = pv[0:HEAD_DIM, :] / pv[HEAD_DIM:HEAD_DIM + 1, :]
            yield

        s_id = lax.broadcasted_iota(jnp.int32, (ns, sw_), 0)
        cur = tp_row[:, 0:sw_] // SEL_BLOCK
        forced = (s_id == 0) | (s_id == cur) | (s_id == cur - 1)
        score = jnp.where(forced, -3e38, jnp.where(s_id <= cur, imp, -SEL_FORCE))
        bias = jnp.where(forced, 0.0, NEG_INF)
        for _ in range(min(SEL_TOPK, ns) - SEL_FORCED):
            mx = jnp.max(score, axis=0, keepdims=True)
            first = jnp.min(jnp.where(score == mx, s_id, ns), axis=0, keepdims=True)
            hit = s_id == first
            score = jnp.where(hit, -3e38, score)
            bias = jnp.where(hit, 0.0, bias)
            yield
        bias2_ref[slot] = bias

    def scores(j, s_ref, slot):
        k0 = pl.multiple_of(j * sel_keys, sel_keys)
        b0 = pl.multiple_of(j * sel_rows, sel_rows)
        rows = bias2_ref[slot, pl.ds(b0, sel_rows), :]
        rows = jnp.concatenate([rows[:, g * Q_TILE:(g + 1) * Q_TILE]
                                for g in range(NSA_GROUPS) for _ in range(NSA_GROUP_SIZE)], axis=1)
        rows = jnp.concatenate([rows, jnp.zeros_like(rows)], axis=0).astype(BF16)
        rhs = jnp.concatenate([qt2_ref[slot], rows,
                               jnp.zeros((LANES - rows.shape[0], width), BF16)], axis=0)
        lhs = jnp.concatenate([ks_ref[0, pl.ds(k0, sel_keys), :], ind_ref[...]], axis=1)
        s = dot(lhs, rhs)
        s_ref[...] = s
        return jnp.max(s, axis=0, keepdims=True)

    def softmax_pv(j, s_ref, mb, m_run):
        k0 = pl.multiple_of(j * sel_keys, sel_keys)
        m_new = jnp.maximum(m_run, mb)
        alpha = jnp.exp2(m_run - m_new)
        for g in range(NSA_GROUPS):
            p = jnp.exp2(s_ref[:, lanes[g]] - m_new[:, lanes[g]]).astype(BF16)
            pv = dot(v_ext(vst_ref, g, k0, sel_keys), p)
            acc_ref[g] = acc_ref[g] * alpha[:, lanes[g]] + pv
        return m_new

    def causal_tail(j, s_ref, m_run):
        k0 = pl.multiple_of(j * sel_keys, sel_keys)
        key = k0 + lax.broadcasted_iota(jnp.int32, (sel_keys, width), 0)
        s = jnp.where(key <= t_row, s_ref[...], NEG_INF)
        s_ref[...] = s
        softmax_pv(j, s_ref, jnp.max(s, axis=0, keepdims=True), m_run)

    def full_trips(j0, count, m_run, mb, slot):
        for t in range(count):
            mb_next = scores(j0 + t + 1, bufs[(t + 1) % 2], slot)
            m_run = softmax_pv(j0 + t, bufs[t % 2], mb, m_run)
            mb = mb_next
        return m_run, mb

    def head(slot):
        acc_ref[...] = jnp.zeros_like(acc_ref)
        m_run = jnp.full((1, width), NEG_INF, F32)
        mb = scores(0, sa_ref, slot)
        yield
        for j in range(HEAD_TRIPS):
            mb_next = scores(j + 1, bufs[(j + 1) % 2], slot)
            yield
            m_run = softmax_pv(j, bufs[j % 2], mb, m_run)
            mb = mb_next
            yield
        mstate_ref[0] = m_run
        mstate_ref[1] = mb

    nxt = jnp.minimum(i + 1, n_tiles - 1)

    @pl.when(i == 0)
    def _():
        _interleave(prepare(i, q_ref, 0, False))

    for par in range(2):
        @pl.when((i % 2 == par) & deep)
        def _(par=par):
            _interleave(prepare(nxt, qn_ref, 1 - par, False), head(par))

        @pl.when((i % 2 == par) & jnp.logical_not(deep))
        def _(par=par):
            _interleave(prepare(nxt, qn_ref, 1 - par, True))
            acc_ref[...] = jnp.zeros_like(acc_ref)
            mstate_ref[0] = jnp.full((1, width), NEG_INF, F32)
            mstate_ref[1] = scores(0, sa_ref, par)

    slot = i % 2
    j_head = jnp.where(deep, HEAD_TRIPS, 0)
    n_loops = (n_full - j_head) // LOOP_TRIPS
    m_run, mb_a = lax.fori_loop(
        0, n_loops, lambda kk, c: full_trips(j_head + LOOP_TRIPS * kk, LOOP_TRIPS, *c, slot),
        (mstate_ref[0], mstate_ref[1]))
    j_last = j_head + LOOP_TRIPS * n_loops

    for rem in range(LOOP_TRIPS):
        @pl.when(n_full - j_last == rem)
        def _(rem=rem):
            m_fin, _ = full_trips(j_last, rem, m_run, mb_a, slot)
            causal_tail(j_last + rem, bufs[rem % 2], m_fin)

    outs = []
    for g in range(NSA_GROUPS):
        o_s = acc_ref[g, 0:HEAD_DIM, :] / acc_ref[g, HEAD_DIM:HEAD_DIM + 1, :]
        o_c, o_w = oc2_ref[slot, g], ow2_ref[slot, g]
        heads = []
        for r in range(NSA_GROUP_SIZE):
            base = (g * NSA_GROUP_SIZE + r) * 3
            cols = slice(r * Q_TILE, (r + 1) * Q_TILE)
            heads.append(gt_ref[0, base:base + 1, :] * o_c[:, cols]
                         + gt_ref[0, base + 1:base + 2, :] * o_s[:, cols]
                         + gt_ref[0, base + 2:base + 3, :] * o_w[:, cols])
        for pair in range(NSA_GROUP_SIZE // 2):
            outs.append(jnp.concatenate(heads[2 * pair:2 * pair + 2], axis=0).T)
    o_ref[0] = jnp.concatenate(outs, axis=1)


def _nsa_attn(q, gates_t, kc, vct, ks, vst, kw, vwt, ovt):
    b, t, _ = q.shape
    ncp = kc.shape[1]
    ns = ovt.shape[0]
    n_tiles = t // Q_TILE
    sel_keys = SEL_TRIP * KEY_BLOCK
    width = NSA_GROUPS * NSA_GROUP_SIZE * Q_TILE
    assert t % sel_keys == 0 and sel_keys % Q_TILE == 0 and t >= WINDOW + Q_TILE and ncp % KEY_BLOCK == 0
    ind = (np.arange(sel_keys)[:, None] // SEL_BLOCK == np.arange(LANES)[None, :])
    ind = jnp.asarray(ind.astype(np.float32), dtype=BF16)
    full_rows = lambda bi, qi: (bi, 0, 0)
    return pl.pallas_call(
        _nsa_kernel,
        grid=(b, n_tiles),
        in_specs=[
            pl.BlockSpec((1, Q_TILE, NSA_WIDTH), lambda bi, qi: (bi, qi, 0)),
            pl.BlockSpec((1, Q_TILE, NSA_WIDTH), lambda bi, qi: (bi, jnp.minimum(qi + 1, n_tiles - 1), 0)),
            pl.BlockSpec((1, GATE_ROWS, Q_TILE), lambda bi, qi: (bi, 0, qi)),
            pl.BlockSpec((1, ncp, LANES), full_rows),
            pl.BlockSpec((1, LANES, ncp), full_rows),
            pl.BlockSpec((1, t, LANES), full_rows),
            pl.BlockSpec((1, LANES, t), full_rows),
            pl.BlockSpec((1, t, LANES), full_rows),
            pl.BlockSpec((1, LANES, t), full_rows),
            _const_spec((ns, ncp)),
            _const_spec((sel_keys, LANES)),
        ],
        out_specs=pl.BlockSpec((1, Q_TILE, NSA_WIDTH), lambda bi, qi: (bi, qi, 0)),
        out_shape=jax.ShapeDtypeStruct((b, t, NSA_WIDTH), F32),
        scratch_shapes=[pltpu.VMEM((sel_keys, width), F32),
                        pltpu.VMEM((sel_keys, width), F32),
                        pltpu.VMEM((NSA_GROUPS, HEAD_DIM + BF16_ROWS, NSA_GROUP_SIZE * Q_TILE), F32),
                        pltpu.VMEM((2, 1, width), F32),
                        pltpu.VMEM((2, LANES, width), BF16),
                        pltpu.VMEM((2, ns, NSA_GROUPS * Q_TILE), F32),
                        pltpu.VMEM((2, NSA_GROUPS, HEAD_DIM, NSA_GROUP_SIZE * Q_TILE), F32),
                        pltpu.VMEM((2, NSA_GROUPS, HEAD_DIM, NSA_GROUP_SIZE * Q_TILE), F32)],
        compiler_params=_params(("arbitrary", "arbitrary")),
        name="nsa_attn",
    )(q, q, gates_t, kc, vct, ks, vst, kw, vwt, ovt, ind)


def _same_head_mask():
    bi = lax.broadcasted_iota(jnp.int32, (HALF, HALF), 0) // HEAD_DIM
    bj = lax.broadcasted_iota(jnp.int32, (HALF, HALF), 1) // HEAD_DIM
    return bi == bj


def _bd_rows(x, same_head):
    xb = x.astype(BF16)
    tiled = jnp.concatenate([xb] * (HALF // x.shape[0]), axis=0)
    return jnp.where(same_head, tiled, jnp.zeros((), BF16))


def _rwkv_prep_stages(in_refs, par_refs, out_set, *, chunks):
    r_ref, k_ref, v_ref, lo_ref = in_refs
    w0_ref, w2_ref, a0_ref, a2_ref, g2_ref, kk_ref, ka_ref, rk_ref = par_refs
    wm_ref, zm_ref, arb_ref, rkv_ref, rt_ref, vb_ref, bkt_ref, gl_ref, bonus_ref, gate_ref = out_set
    c = RWKV_CHUNK
    nb = r_ref.shape[0]
    same_head = _same_head_mask()
    ones_bd = same_head.astype(BF16)
    n_tok = chunks * c
    row_i = lax.broadcasted_iota(jnp.int32, (n_tok, n_tok), 0)
    col_i = lax.broadcasted_iota(jnp.int32, (n_tok, n_tok), 1)
    tril_incl = ((row_i >= col_i) & (row_i // c == col_i // c)).astype(BF16)
    t_id = lax.broadcasted_iota(jnp.int32, (c, HALF), 0)
    j_id = lax.broadcasted_iota(jnp.int32, (c, HALF), 1) % HEAD_DIM
    strict_lower = t_id > j_id
    incl_lower = t_id >= j_id
    eye_all = (t_id == j_id).astype(F32)
    dot = lambda x, y: jnp.dot(x, y, preferred_element_type=F32)

    def bd_cols(xt):
        xb = xt.astype(BF16)
        return jnp.where(same_head, jnp.concatenate([xb, xb], axis=1), jnp.zeros((), BF16))

    halves = RWKV_WIDTH // HALF
    groups = [(b, ch) for b in range(nb) for ch in range(chunks)]
    chains = [(gi, hh) for gi in range(len(groups)) for hh in range(halves)]
    rows = [slice(ch * c, (ch + 1) * c) for ch in range(chunks)]
    lanes = [slice(hh * HALF, (hh + 1) * HALF) for hh in range(halves)]
    each = lambda fn: [fn(n, gi, lanes[hh]) for n, (gi, hh) in enumerate(chains)]

    lo = [lo_ref[b] for b in range(nb)]
    zs = [-(w0_ref[...] + _bdot(jnp.tanh(x[:, 0:DECAY_LORA]), w2_ref[...])) for x in lo]
    yield
    lr_b = [_sigmoid(a0_ref[...] + _bdot(x[:, DECAY_LORA:DECAY_LORA + AAA_LORA], a2_ref[...])) for x in lo]
    yield
    for b in range(nb):
        gate_ref[b] = _bdot(_sigmoid(lo[b][:, DECAY_LORA + AAA_LORA:]), g2_ref[...])
        vb_ref[b] = v_ref[b].astype(BF16)
    yield
    log_decay = []
    for z in zs:
        softplus = jnp.maximum(z, 0.0) + jnp.log(1.0 + jnp.exp(-jnp.abs(z)))
        log_decay.append(-jnp.exp(-softplus - 0.5))
    cum_b = [_dot3_left(tril_incl, x) for x in log_decay]
    yield
    r = [r_ref[b, rows[ch], :] for b, ch in groups]
    k = [k_ref[b, rows[ch], :] for b, ch in groups]
    v = [v_ref[b, rows[ch], :] for b, ch in groups]
    lr = [lr_b[b][rows[ch], :] for b, ch in groups]
    cum = [cum_b[b][rows[ch], :] for b, ch in groups]
    ld = [log_decay[b][rows[ch], :] for b, ch in groups]
    g_incl = [jnp.exp(x) for x in cum]
    g_excl = [jnp.exp(x - y) for x, y in zip(cum, ld)]
    g_inv = [jnp.exp(-x) for x in cum]
    for gi, (b, ch) in enumerate(groups):
        gl_ref[b, ch * SUBLANES:(ch + 1) * SUBLANES, :] = jnp.broadcast_to(
            g_incl[gi][c - 1:c, :], (SUBLANES, RWKV_WIDTH))

    kk = each(lambda n, gi, ln: k[gi][:, ln] * kk_ref[:, ln])
    k2 = each(lambda n, gi, ln: k[gi][:, ln] * (1.0 + (lr[gi][:, ln] - 1.0) * ka_ref[:, ln]))
    sums = each(lambda n, gi, ln: _dot2_right(
        jnp.concatenate([kk[n] * kk[n], r[gi][:, ln] * k2[n] * rk_ref[:, ln]], axis=0), ones_bd))
    ssq = [x[0:c] for x in sums]
    for n, (gi, hh) in enumerate(chains):
        b, ch = groups[gi]
        bonus_ref[b, rows[ch], lanes[hh]] = sums[n][c:] * v[gi][:, lanes[hh]]
    yield
    kk = each(lambda n, gi, ln: kk[n] / jnp.maximum(jnp.sqrt(ssq[n]), 1e-12))
    at = each(lambda n, gi, ln: -kk[n] * g_excl[gi][:, ln])
    bt = each(lambda n, gi, ln: kk[n] * lr[gi][:, ln] * g_inv[gi][:, ln])
    kt = each(lambda n, gi, ln: k2[n] * g_inv[gi][:, ln])
    rt = each(lambda n, gi, ln: r[gi][:, ln] * g_incl[gi][:, ln])
    for n, (gi, hh) in enumerate(chains):
        b, ch = groups[gi]
        rt_ref[b, rows[ch], lanes[hh]] = rt[n].astype(BF16)
        bkt_ref[b, (ch * halves + hh) * HALF:(ch * halves + hh + 1) * HALF, :] = (
            jnp.concatenate([bt[n], kt[n]], axis=0).T.astype(BF16))

    bt_bd = each(lambda n, gi, ln: bd_cols(jnp.concatenate([bt[n], bt[n]], axis=0).T))
    ar = each(lambda n, gi, ln: jnp.concatenate([at[n], rt[n]], axis=0).astype(BF16))
    ab = each(lambda n, gi, ln: dot(ar[n], bt_bd[n]))
    yield
    kt_bd = each(lambda n, gi, ln: bd_cols(jnp.concatenate([kt[n], kt[n]], axis=0).T))
    ak = each(lambda n, gi, ln: dot(ar[n], kt_bd[n]))
    yield
    a_ab = [jnp.where(strict_lower, x[0:c], 0.0) for x in ab]
    a_rb = [jnp.where(incl_lower, x[c:], 0.0) for x in ab]
    a_ak = [jnp.where(strict_lower, x[0:c], 0.0) for x in ak]
    a_rk = [jnp.where(incl_lower, x[c:], 0.0) for x in ak]
    for n, (gi, hh) in enumerate(chains):
        b, ch = groups[gi]
        arb_ref[b, rows[ch], lanes[hh]] = a_rb[n].astype(BF16)

    base = 4
    same_block = lambda s: (t_id // s) == (j_id // s)
    nd = [jnp.where(same_block(base), x, 0.0) for x in a_ab]
    nd2 = [dot(x.astype(BF16), _bd_rows(x, same_head)) for x in nd]
    yield
    inv = [eye_all + x for x in nd]
    inv = [x + dot(x.astype(BF16), _bd_rows(y, same_head)) for x, y in zip(inv, nd2)]
    yield
    s_blk = base
    while s_blk < c:
        lower_left = same_block(2 * s_blk) & ((t_id % (2 * s_blk)) >= s_blk) & ((j_id % (2 * s_blk)) < s_blk)
        te = [dot(x.astype(BF16), _bd_rows(jnp.where(lower_left, y, 0.0), same_head))
              for x, y in zip(inv, a_ab)]
        yield
        inv = [x + dot(y.astype(BF16), _bd_rows(x, same_head)) for x, y in zip(inv, te)]
        yield
        s_blk *= 2

    v_bd = each(lambda n, gi, ln: _bd_rows(v[gi][:, ln], same_head))
    inv_b = [x.astype(BF16) for x in inv]
    wm = each(lambda n, gi, ln: dot(inv_b[n], _bd_rows(at[n], same_head)))
    yield
    av = each(lambda n, gi, ln: dot(jnp.concatenate([a_ak[n], a_rk[n]], axis=0).astype(BF16), v_bd[n]))
    yield
    zm = each(lambda n, gi, ln: dot(inv_b[n], _bd_rows(av[n][0:c], same_head)))
    for n, (gi, hh) in enumerate(chains):
        b, ch = groups[gi]
        wm_ref[b, rows[ch], lanes[hh]] = wm[n].astype(BF16)
        zm_ref[b, rows[ch], lanes[hh]] = zm[n]
        rkv_ref[b, rows[ch], lanes[hh]] = av[n][c:]


def _rwkv_scan_stages(in_set, gng_ref, gnb_ref, o_ref, s_ref, *, chunks):
    wm_ref, zm_ref, arb_ref, rkv_ref, rt_ref, vb_ref, bkt_ref, gl_ref, bonus_ref, gate_ref = in_set
    c = RWKV_CHUNK
    halves = RWKV_WIDTH // HALF
    same_head = _same_head_mask()
    ones_bd = same_head.astype(BF16)
    chains = [(b, hh) for b in range(o_ref.shape[0]) for hh in range(halves)]
    lanes = [slice(hh * HALF, (hh + 1) * HALF) for hh in range(halves)]
    dot = lambda x, y: jnp.dot(x, y, preferred_element_type=F32)

    for ch in range(chunks):
        rw = slice(ch * c, (ch + 1) * c)
        each = lambda fn: [fn(n, b, lanes[hh]) for n, (b, hh) in enumerate(chains)]
        s0 = [s_ref[b, hh] for b, hh in chains]
        s0b = [x.astype(BF16) for x in s0]
        u = each(lambda n, b, ln: dot(wm_ref[b, rw, ln], s0b[n]) + zm_ref[b, rw, ln])
        yield
        y0 = each(lambda n, b, ln: dot(rt_ref[b, rw, ln], s0b[n]) + rkv_ref[b, rw, ln])
        yield
        y = each(lambda n, b, ln: y0[n] + dot(arb_ref[b, rw, ln], _bd_rows(u[n], same_head)))
        yield
        uv = each(lambda n, b, ln: jnp.concatenate([u[n].astype(BF16), vb_ref[b, rw, ln]], axis=0))
        upd = [dot(bkt_ref[b, (ch * halves + hh) * HALF:(ch * halves + hh + 1) * HALF, :], uv[n])
               for n, (b, hh) in enumerate(chains)]
        for n, (b, hh) in enumerate(chains):
            g_last = jnp.broadcast_to(gl_ref[b, ch * SUBLANES:ch * SUBLANES + 1, lanes[hh]], (LANES, HALF)).T
            g_col = jnp.concatenate([g_last, g_last], axis=1)
            s_ref[b, hh] = g_col * (s0[n] + jnp.where(same_head, upd[n], 0.0))
        yield
        mu = [_dot2_right(x, ones_bd) * (1.0 / HEAD_DIM) for x in y]
        yield
        yc = [x - m for x, m in zip(y, mu)]
        var = [_dot2_right(x * x, ones_bd) * (1.0 / HEAD_DIM) for x in yc]
        yield
        for n, (b, hh) in enumerate(chains):
            ln = lanes[hh]
            yn = yc[n] * lax.rsqrt(var[n] + RWKV_GN_EPS) * gng_ref[:, ln] + gnb_ref[:, ln]
            o_ref[b, rw, ln] = (yn + bonus_ref[b, rw, ln]) * gate_ref[b, rw, ln]


def _rwkv_kernel(*refs, chunks):
    in_refs, par_refs = refs[0:4], refs[4:12]
    gng_ref, gnb_ref, o_ref, s_ref = refs[12:16]
    sets = (refs[16:26], refs[26:36])
    step = pl.program_id(0)

    @pl.when(step == 0)
    def _():
        s_ref[...] = jnp.zeros_like(s_ref)
        for ref in sets[1]:
            ref[...] = jnp.zeros_like(ref)

    def run(write_set, read_set):
        prep = _rwkv_prep_stages(in_refs, par_refs, write_set, chunks=chunks)
        scan = _rwkv_scan_stages(read_set, gng_ref, gnb_ref, o_ref, s_ref, chunks=chunks)
        live = [prep, scan]
        while live:
            for gen in list(live):
                if next(gen, "done") == "done":
                    live.remove(gen)

    @pl.when(step % 2 == 0)
    def _():
        run(sets[0], sets[1])

    @pl.when(step % 2 == 1)
    def _():
        run(sets[1], sets[0])


def _rwkv(r, k, v, lo, w0, w2, a0, a2, g2, k_k, k_a, r_k, gn_g, gn_b, *, chunks=2):
    b, t, width = r.shape
    c = RWKV_CHUNK
    n_tok = chunks * c
    n_groups = t // n_tok
    halves = width // HALF
    vec = lambda a: a.reshape(1, width)
    cs = lambda a: _const_spec(a.shape)
    args = [vec(w0), w2.astype(BF16), vec(a0), a2.astype(BF16), g2.astype(BF16),
            vec(k_k), vec(k_a), vec(r_k), vec(gn_g), vec(gn_b)]
    ahead = lambda s: (0, jnp.minimum(s, n_groups - 1), 0)
    behind = lambda s: (0, jnp.maximum(s - 1, 0), 0)
    tok = lambda dt: pltpu.VMEM((b, n_tok, width), dt)
    scratch_set = [tok(BF16), tok(F32), tok(BF16), tok(F32), tok(BF16), tok(BF16),
                   pltpu.VMEM((b, chunks * halves * HALF, LANES), BF16),
                   pltpu.VMEM((b, chunks * SUBLANES, width), F32), tok(F32), tok(F32)]
    return pl.pallas_call(
        functools.partial(_rwkv_kernel, chunks=chunks),
        grid=(n_groups + 1,),
        in_specs=[pl.BlockSpec((b, n_tok, width), ahead)] * 3 + [pl.BlockSpec((b, n_tok, LORA_COLS), ahead)]
                 + [cs(a) for a in args],
        out_specs=pl.BlockSpec((b, n_tok, width), behind),
        out_shape=jax.ShapeDtypeStruct((b, t, width), F32),
        scratch_shapes=[pltpu.VMEM((b, halves, HALF, HALF), F32)] + scratch_set + scratch_set,
        compiler_params=_params(("arbitrary",)),
        name="rwkv7",
    )(r, k, v, lo, *args)


def _mem_kv_kernel(m_ref, g_ref, wk_ref, wv_ref, kt_ref, v_ref):
    m = _rms(m_ref[0], g_ref[...]).astype(BF16)
    kt_ref[0] = jnp.dot(m, wk_ref[...], preferred_element_type=F32).T.astype(BF16)
    v_ref[0] = jnp.dot(m, wv_ref[...], preferred_element_type=F32).astype(BF16)


def _mem_kv(mem, g, wk, wv):
    b, mt, d = mem.shape
    return pl.pallas_call(
        _mem_kv_kernel,
        grid=(b,),
        in_specs=[pl.BlockSpec((1, mt, d), lambda bi: (bi, 0, 0)), _const_spec((1, d)),
                  _const_spec((d, d)), _const_spec((d, d))],
        out_specs=[pl.BlockSpec((1, d, mt), lambda bi: (bi, 0, 0)),
                   pl.BlockSpec((1, mt, d), lambda bi: (bi, 0, 0))],
        out_shape=[jax.ShapeDtypeStruct((b, d, mt), BF16), jax.ShapeDtypeStruct((b, mt, d), BF16)],
        compiler_params=_params(("arbitrary",)),
        name="mem_kv",
    )(mem, g.reshape(1, d), wk.astype(BF16), wv.astype(BF16))


def _out_mem_kernel(x_ref, on_ref, or_ref, ng_ref, wo1_ref, wo2_ref, mpost_ref,
                    mpre_ref, wq_ref, kt_ref, v_ref, wo_ref, mempost_ref, o_ref, *, parts):
    tm = x_ref.shape[1] // parts
    rows = [slice(n * tm, (n + 1) * tm) for n in range(parts)]
    dot = lambda a, b: jnp.dot(a, b, preferred_element_type=F32)
    d = x_ref.shape[-1]
    hd = d // MEM_HEADS

    a = [_rms(on_ref[0, rw, :], ng_ref[...]).astype(BF16) for rw in rows]
    mixed = [dot(a[n], wo1_ref[...]) + dot(or_ref[0, rw, :].astype(BF16), wo2_ref[...])
             for n, rw in enumerate(rows)]
    x = [x_ref[0, rw, :] + _rms(mixed[n], mpost_ref[...]) for n, rw in enumerate(rows)]
    h = [_rms(xn, mpre_ref[...]).astype(BF16) for xn in x]
    q = [(dot(hn, wq_ref[...]) * (hd ** -0.5)).astype(BF16) for hn in h]
    heads = [[] for _ in rows]
    for hi in range(MEM_HEADS):
        cols = slice(hi * hd, (hi + 1) * hd)
        s = [dot(qn[:, cols], kt_ref[0, cols, :]) for qn in q]
        e = [jnp.exp(sn - jnp.max(sn, axis=-1, keepdims=True)) for sn in s]
        p = [(en / jnp.sum(en, axis=-1, keepdims=True)).astype(BF16) for en in e]
        for n in range(parts):
            heads[n].append(dot(p[n], v_ref[0, :, cols]))
    att = [dot(jnp.concatenate(hn, axis=-1).astype(BF16), wo_ref[...]) for hn in heads]
    for n, rw in enumerate(rows):
        o_ref[0, rw, :] = x[n] + _rms(att[n], mempost_ref[...])


def _out_mem(x3, o_nsa, o_rwkv, nsa_g, w_out, mix_post_g, mem_pre_g, wq, kt, vm, wo, mem_post_g,
             *, tm=1024, parts=4):
    b, t, d = x3.shape
    mt = vm.shape[1]
    row = lambda bi, ti: (bi, ti, 0)
    per_b = lambda bi, ti: (bi, 0, 0)
    w_out = w_out.astype(BF16)
    return pl.pallas_call(
        functools.partial(_out_mem_kernel, parts=parts),
        grid=(b, t // tm),
        in_specs=[
            pl.BlockSpec((1, tm, d), row),
            pl.BlockSpec((1, tm, NSA_WIDTH), row),
            pl.BlockSpec((1, tm, RWKV_WIDTH), row),
            _const_spec((1, NSA_WIDTH)),
            _const_spec((NSA_WIDTH, d)),
            _const_spec((RWKV_WIDTH, d)),
            _const_spec((1, d)),
            _const_spec((1, d)),
            _const_spec((d, d)),
            pl.BlockSpec((1, d, mt), per_b),
            pl.BlockSpec((1, mt, d), per_b),
            _const_spec((d, d)),
            _const_spec((1, d)),
        ],
        out_specs=pl.BlockSpec((1, tm, d), row),
        out_shape=jax.ShapeDtypeStruct((b, t, d), F32),
        compiler_params=_params(("arbitrary", "arbitrary")),
        name="out_mem",
    )(x3, o_nsa, o_rwkv, nsa_g.reshape(1, -1), w_out[:NSA_WIDTH], w_out[NSA_WIDTH:],
      mix_post_g.reshape(1, d), mem_pre_g.reshape(1, d), wq.astype(BF16), kt, vm,
      wo.astype(BF16), mem_post_g.reshape(1, d))


def _rope_tables(pos):
    half = HEAD_DIM // 2
    inv = ROPE_THETA ** (-jnp.arange(half, dtype=F32) / half)
    ang = pos.astype(F32)[:, None] * inv[None, :]
    cos, sin = jnp.cos(ang), jnp.sin(ang)
    cos_t = jnp.concatenate([cos, cos, cos, cos], axis=-1)
    sin_t = jnp.concatenate([-sin, sin, -sin, sin], axis=-1)
    return cos_t, sin_t


def _overlap_t(ns, ncp):
    c0 = np.arange(ncp)[None, :] * CMP_STRIDE
    s0 = np.arange(ns)[:, None] * SEL_BLOCK
    ov = (c0 < s0 + SEL_BLOCK) & (c0 + CMP_BLOCK > s0) & (np.arange(ncp)[None, :] < ncp - 1)
    return jnp.asarray(ov.astype(np.float32), dtype=BF16)


def _pad_cols(w, n):
    return jnp.pad(w, ((0, 0), (0, n - w.shape[1])))


def _cmp_weights(pe, w1, w2):
    per = CMP_STRIDE
    w1r = w1.reshape(CMP_BLOCK, HEAD_DIM, CMP_HIDDEN)
    blocks = []
    for part in range(CMP_BLOCK // per):
        for g in range(NSA_GROUPS):
            z = jnp.zeros((per, NSA_GROUPS, HEAD_DIM, CMP_HIDDEN), F32)
            z = z.at[:, g].set(w1r[part * per:(part + 1) * per])
            blocks.append(z.reshape(per * NSA_GROUPS * HEAD_DIM, CMP_HIDDEN))
    w1cat = jnp.concatenate(blocks, axis=1).astype(BF16)
    w2bd = jnp.zeros((NSA_GROUPS * CMP_HIDDEN, NSA_GROUPS * HEAD_DIM), F32)
    for g in range(NSA_GROUPS):
        w2bd = w2bd.at[g * CMP_HIDDEN:(g + 1) * CMP_HIDDEN, g * HEAD_DIM:(g + 1) * HEAD_DIM].set(w2)
    pe8 = jnp.broadcast_to(pe.reshape(1, CMP_BLOCK * HEAD_DIM), (SUBLANES, CMP_BLOCK * HEAD_DIM))
    return pe8.astype(BF16), w1.astype(BF16), w1cat, w2bd.astype(BF16)


def kernel(x, mem, ffn1_pre_g, ffn1_w_gate, ffn1_w_up, ffn1_w_down, ffn1_post_g, mix_pre_g, w_in, cmp_pe_k, cmp_w1_k, cmp_w2_k, cmp_pe_v, cmp_w1_v, cmp_w2_v, nsa_out_g, rwkv_mu, rwkv_w0, rwkv_w2, rwkv_a0, rwkv_a2, rwkv_g2, rwkv_k_k, rwkv_k_a, rwkv_r_k, rwkv_gn_g, rwkv_gn_b, w_out, mix_post_g, mem_pre_g, mem_kv_g, mem_wq, mem_wk, mem_wv, mem_wo, mem_post_g, ffn2_pre_g, ffn2_w_gate, ffn2_w_up, ffn2_w_down, ffn2_post_g):
    b, t, d = x.shape
    ncp = t // CMP_STRIDE
    ns = t // SEL_BLOCK
    cos_t, sin_t = _rope_tables(jnp.arange(t))
    cos_c, sin_c = _rope_tables(jnp.arange(ncp) * CMP_STRIDE + (CMP_BLOCK - 1))
    ovt = _overlap_t(ns, ncp)

    for l in range(ffn1_pre_g.shape[0]):
        x2 = _ffn_block(x.reshape(b * t, d), ffn1_pre_g[l], ffn1_w_gate[l], ffn1_w_up[l],
                        ffn1_w_down[l], ffn1_post_g[l])
        x3 = x2.reshape(b, t, d)

        wi = w_in[l]
        nsa_w = NSA_WIDTH + 6 * NSA_KV_WIDTH
        gate_w = wi[:, nsa_w:nsa_w + 3 * NSA_HEADS]
        w_cols = jnp.concatenate([wi[:, :nsa_w], _pad_cols(gate_w, LANES),
                                  wi[:, nsa_w + 3 * NSA_HEADS:]], axis=1).astype(BF16)
        (q, k_cmp, v_cmp, k_slc, v_slc_t, k_win, v_win_t, gates_t, r, k, v, lo) = _in_proj(
            x3, mix_pre_g[l], w_cols, rwkv_mu[l].reshape(1, -1), cos_t, sin_t)

        pek, w1k, w1kc, w2k = _cmp_weights(cmp_pe_k[l], cmp_w1_k[l], cmp_w2_k[l])
        pev, w1v, w1vc, w2v = _cmp_weights(cmp_pe_v[l], cmp_w1_v[l], cmp_w2_v[l])
        kc, vct = _compress(k_cmp, v_cmp,
                            pek, w1k, w1kc, w2k, pev, w1v, w1vc, w2v, cos_c, sin_c)
        o_nsa = _nsa_attn(q, gates_t, kc, vct, k_slc, v_slc_t, k_win, v_win_t, ovt)

        o_rwkv = _rwkv(r, k, v, lo, rwkv_w0[l], rwkv_w2[l], rwkv_a0[l], rwkv_a2[l], rwkv_g2[l],
                       rwkv_k_k[l], rwkv_k_a[l], rwkv_r_k[l], rwkv_gn_g[l], rwkv_gn_b[l])

        kt, vm = _mem_kv(mem, mem_kv_g[l], mem_wk[l], mem_wv[l])
        x4 = _out_mem(x3, o_nsa, o_rwkv, nsa_out_g[l], w_out[l], mix_post_g[l], mem_pre_g[l],
                      mem_wq[l], kt, vm, mem_wo[l], mem_post_g[l])

        x = _ffn_block(x4.reshape(b * t, d), ffn2_pre_g[l], ffn2_w_gate[l], ffn2_w_up[l],
                       ffn2_w_down[l], ffn2_post_g[l]).reshape(b, t, d)
    return x
```
